```python
import jax, jax.numpy as jnp
from jax import lax
import numpy as np

D_MODEL = 1024
BATCH = 2
SEQ = 8192
DEPTH = 1
DEC_BATCH = 32
DEC_SEQ = 8
PAST_LEN = 16384
PAGE_SIZE = 128

HEAD_DIM = 64
A_WIDTH = 3 * D_MODEL // 8
A_HEADS = A_WIDTH // HEAD_DIM
M_WIDTH = D_MODEL // 4
M_HEADS = 4
M_HEAD_DIM = M_WIDTH // M_HEADS
C_WIDTH = D_MODEL - A_WIDTH - M_WIDTH
D_MIX = A_WIDTH + C_WIDTH + M_WIDTH
ROT_DIM = HEAD_DIM // 4
ROPE_THETA = 500000.0
DILATED_PAIRS = ((128, 1), (512, 4), (2048, 16))
MAX_WINDOW = 2048
CONV_W = 31
N_MEM = 256
Q_BLOCK = 128
EPS = 1e-6
SPLIT_WIDTHS = (A_WIDTH, A_WIDTH, A_WIDTH, A_WIDTH, C_WIDTH, C_WIDTH, C_WIDTH, M_WIDTH, M_WIDTH)
D_IN = sum(SPLIT_WIDTHS)
SPLIT_POINTS = tuple(sum(SPLIT_WIDTHS[:i + 1]) for i in range(len(SPLIT_WIDTHS) - 1))

kernel_name = 'dilated_conformer_memory_hybrid'


def rmsnorm(x, g):
    xf = x.astype(jnp.float32)
    y = xf * lax.rsqrt(jnp.mean(xf * xf, axis=-1, keepdims=True) + EPS) * g.astype(jnp.float32)
    return y.astype(x.dtype)


def rope(x, pos):
    inv = ROPE_THETA ** (-jnp.arange(0, ROT_DIM, 2, dtype=jnp.float32) / ROT_DIM)
    ang = pos.astype(jnp.float32)[:, None] * inv[None, :]
    cos = jnp.cos(ang)[None, :, None, :]
    sin = jnp.sin(ang)[None, :, None, :]
    half = ROT_DIM // 2
    x1 = x[..., :half].astype(jnp.float32)
    x2 = x[..., half:ROT_DIM].astype(jnp.float32)
    rot = jnp.concatenate([x1 * cos - x2 * sin, x2 * cos + x1 * sin], axis=-1).astype(x.dtype)
    return jnp.concatenate([rot, x[..., ROT_DIM:]], axis=-1)


def dilated_mix(q, qrow, k, v):
    scale = HEAD_DIM ** -0.5
    outs, lses = [], []
    for window, dil in DILATED_PAIRS:
        steps = jnp.arange(window // dil + 1, dtype=jnp.int32)
        idx = qrow[:, None] - dil * steps[None, :]
        valid = idx >= 0
        idx = jnp.maximum(idx, 0)
        kg = jnp.take(k, idx, axis=1)
        vg = jnp.take(v, idx, axis=1)
        s = jnp.einsum('bqhd,bqmhd->bhqm', q, kg, preferred_element_type=jnp.float32) * scale
        s = jnp.where(valid[None, None], s, -jnp.inf)
        lse = jax.nn.logsumexp(s, axis=-1)
        p = jnp.exp(s - lse[..., None])
        o = jnp.einsum('bhqm,bqmhd->bqhd', p.astype(v.dtype), vg, preferred_element_type=jnp.float32)
        outs.append(o)
        lses.append(lse)
    wts = jax.nn.softmax(jnp.stack(lses, axis=0), axis=0)
    wts = jnp.transpose(wts, (0, 1, 3, 2))[..., None]
    out = wts[0] * outs[0]
    for i in range(1, len(outs)):
        out = out + wts[i] * outs[i]
    return out.astype(q.dtype)


def prompt_dilated(q, k, v):
    b, s, h, dh = q.shape
    nb = s // Q_BLOCK
    qb = jnp.transpose(q.reshape(b, nb, Q_BLOCK, h, dh), (1, 0, 2, 3, 4))

    def one_block(args):
        q_blk, n = args
        qrow = n * Q_BLOCK + jnp.arange(Q_BLOCK, dtype=jnp.int32)
        return dilated_mix(q_blk, qrow, k, v)

    o = lax.map(one_block, (qb, jnp.arange(nb, dtype=jnp.int32)))
    return jnp.transpose(o, (1, 0, 2, 3, 4)).reshape(b, s, h, dh)


def depthwise_causal(u_ext, w_dw):
    c = u_ext.shape[-1]
    return lax.conv_general_dilated(u_ext, w_dw[:, None, :].astype(u_ext.dtype), (1,), 'VALID',
                                    dimension_numbers=('NWC', 'WIO', 'NWC'), feature_group_count=c)


def conformer_tail(c, b_dw, ln_g, ln_b, w_pw2, b_pw2):
    cf = c.astype(jnp.float32) + b_dw.astype(jnp.float32)
    mu = jnp.mean(cf, axis=-1, keepdims=True)
    var = jnp.mean(jnp.square(cf - mu), axis=-1, keepdims=True)
    cn = (cf - mu) * lax.rsqrt(var + EPS) * ln_g.astype(jnp.float32) + ln_b.astype(jnp.float32)
    sw = jax.nn.silu(cn).astype(c.dtype)
    return sw @ w_pw2 + b_pw2


def mem_kv(mem, norm_mem, w_mem_kv):
    b, n, _ = mem.shape
    kv = (rmsnorm(mem, norm_mem) @ w_mem_kv).reshape(b, n, 2, M_HEADS, M_HEAD_DIM)
    return kv[:, :, 0], kv[:, :, 1]


def mem_attend(qm, mk, mv):
    s = jnp.einsum('bthd,bnhd->bhtn', qm, mk, preferred_element_type=jnp.float32) * (M_HEAD_DIM ** -0.5)
    p = jax.nn.softmax(s, axis=-1)
    return jnp.einsum('bhtn,bnhd->bthd', p.astype(mv.dtype), mv)


def project(h, w_in, pos):
    b, t, _ = h.shape
    qa, ka, va, ga, ab, bb, gb, qm, gm = jnp.split(h @ w_in, SPLIT_POINTS, axis=-1)
    qa = rope(qa.reshape(b, t, A_HEADS, HEAD_DIM), pos)
    ka = rope(ka.reshape(b, t, A_HEADS, HEAD_DIM), pos)
    va = va.reshape(b, t, A_HEADS, HEAD_DIM)
    u = ab * jax.nn.sigmoid(bb)
    qm = qm.reshape(b, t, M_HEADS, M_HEAD_DIM)
    return qa, ka, va, ga, u, gb, qm, gm


def combine(x, oa, ga, ob, gb, om, gm, w_out, norm_post):
    b, t, _ = x.shape
    mixed = jnp.concatenate([oa.reshape(b, t, A_WIDTH) * jax.nn.silu(ga),
                             ob * jax.nn.silu(gb),
                             om.reshape(b, t, M_WIDTH) * jax.nn.silu(gm)], axis=-1)
    return x + rmsnorm(mixed @ w_out, norm_post)


def setup_inputs(seed: int = 0) -> dict:
    key = jax.random.key(seed)
    ks = jax.random.split(key, 20)
    win_len = min(MAX_WINDOW, PAST_LEN)
    nrm = jax.random.normal
    f32 = jnp.float32
    return {
        'x_prompt': nrm(ks[0], (BATCH, SEQ, D_MODEL), f32),
        'x_sample': nrm(ks[1], (DEC_BATCH, DEC_SEQ, D_MODEL), f32),
        'cache_win_k': nrm(ks[2], (DEPTH, DEC_BATCH, win_len, A_HEADS, HEAD_DIM), f32),
        'cache_win_v': nrm(ks[3], (DEPTH, DEC_BATCH, win_len, A_HEADS, HEAD_DIM), f32),
        'state_conv': 0.5 * nrm(ks[4], (DEPTH, DEC_BATCH, CONV_W - 1, C_WIDTH), f32),
        'cache_mem_k': nrm(ks[5], (DEPTH, DEC_BATCH, N_MEM, M_HEADS, M_HEAD_DIM), f32),
        'cache_mem_v': nrm(ks[6], (DEPTH, DEC_BATCH, N_MEM, M_HEADS, M_HEAD_DIM), f32),
        'mem_prompt': nrm(ks[7], (BATCH, N_MEM, D_MODEL), f32),
        'norm_pre': 1.0 + 0.05 * nrm(ks[8], (DEPTH, D_MODEL), f32),
        'norm_post': 1.0 + 0.05 * nrm(ks[9], (DEPTH, D_MODEL), f32),
        'w_in': nrm(ks[10], (DEPTH, D_MODEL, D_IN), f32) * D_MODEL ** -0.5,
        'w_out': nrm(ks[11], (DEPTH, D_MIX, D_MODEL), f32) * D_MIX ** -0.5,
        'norm_mem': 1.0 + 0.05 * nrm(ks[12], (DEPTH, D_MODEL), f32),
        'w_mem_kv': nrm(ks[13], (DEPTH, D_MODEL, 2 * M_WIDTH), f32) * D_MODEL ** -0.5,
        'w_dw': nrm(ks[14], (DEPTH, CONV_W, C_WIDTH), f32) * CONV_W ** -0.5,
        'b_dw': 0.02 * nrm(ks[15], (DEPTH, C_WIDTH), f32),
        'ln_conv_g': 1.0 + 0.05 * nrm(ks[16], (DEPTH, C_WIDTH), f32),
        'ln_conv_b': 0.02 * nrm(ks[17], (DEPTH, C_WIDTH), f32),
        'w_pw2': nrm(ks[18], (DEPTH, C_WIDTH, C_WIDTH), f32) * C_WIDTH ** -0.5,
        'b_pw2': 0.02 * nrm(ks[19], (DEPTH, C_WIDTH), f32),
    }


def reference(x_prompt, x_sample, cache_win_k, cache_win_v, state_conv, cache_mem_k, cache_mem_v, mem_prompt,
              norm_pre, norm_post, w_in, w_out, norm_mem, w_mem_kv, w_dw, b_dw, ln_conv_g, ln_conv_b, w_pw2, b_pw2):
    s_p = x_prompt.shape[1]
    t_s = x_sample.shape[1]
    pos_p = jnp.arange(s_p, dtype=jnp.int32)
    pos_s = PAST_LEN + jnp.arange(t_s, dtype=jnp.int32)
    keep_p = min(MAX_WINDOW, s_p)
    keep_s = min(MAX_WINDOW, PAST_LEN + t_s)
    xp, xs = x_prompt, x_sample
    wk_p, wv_p, cv_p, mk_p, mv_p, wk_s, wv_s, cv_s = [], [], [], [], [], [], [], []
    for l in range(DEPTH):
        hp = rmsnorm(xp, norm_pre[l])
        qa, ka, va, ga, u, gb, qm, gm = project(hp, w_in[l], pos_p)
        oa = prompt_dilated(qa, ka, va)
        u_ext = jnp.pad(u, ((0, 0), (CONV_W - 1, 0), (0, 0)))
        ob = conformer_tail(depthwise_causal(u_ext, w_dw[l]), b_dw[l], ln_conv_g[l], ln_conv_b[l], w_pw2[l], b_pw2[l])
        mk, mv = mem_kv(mem_prompt, norm_mem[l], w_mem_kv[l])
        om = mem_attend(qm, mk, mv)
        xp = combine(xp, oa, ga, ob, gb, om, gm, w_out[l], norm_post[l])
        wk_p.append(ka[:, s_p - keep_p:])
        wv_p.append(va[:, s_p - keep_p:])
        cv_p.append(u[:, s_p - (CONV_W - 1):])
        mk_p.append(mk)
        mv_p.append(mv)
        hs = rmsnorm(xs, norm_pre[l])
        qa, ka, va, ga, u, gb, qm, gm = project(hs, w_in[l], pos_s)
        kb = jnp.concatenate([cache_win_k[l], ka], axis=1)
        vb = jnp.concatenate([cache_win_v[l], va], axis=1)
        win_len = cache_win_k.shape[2]
        qrow = win_len + jnp.arange(t_s, dtype=jnp.int32)
        oa = dilated_mix(qa, qrow, kb, vb)
        uc = jnp.concatenate([state_conv[l], u], axis=1)
        ob = conformer_tail(depthwise_causal(uc, w_dw[l]), b_dw[l], ln_conv_g[l], ln_conv_b[l], w_pw2[l], b_pw2[l])
        om = mem_attend(qm, cache_mem_k[l], cache_mem_v[l])
        xs = combine(xs, oa, ga, ob, gb, om, gm, w_out[l], norm_post[l])
        tot = kb.shape[1]
        wk_s.append(kb[:, tot - keep_s:])
        wv_s.append(vb[:, tot - keep_s:])
        cv_s.append(uc[:, uc.shape[1] - (CONV_W - 1):])
    return (xp, xs, jnp.stack(wk_p), jnp.stack(wv_p), jnp.stack(cv_p), jnp.stack(mk_p), jnp.stack(mv_p),
            jnp.stack(wk_s), jnp.stack(wv_s), jnp.stack(cv_s))
```

```python
import functools
import math

import jax
import jax.numpy as jnp
from jax import lax
from jax.experimental import pallas as pl
from jax.experimental.pallas import tpu as pltpu

F32 = jnp.float32
BF16 = jnp.bfloat16

D_MODEL = 1024
HEAD_DIM = 64
HEAD_SHIFT = 6
A_WIDTH = 384
A_HEADS = 6
M_WIDTH = 256
M_HEADS = 4
C_WIDTH = 384
ROT_DIM = 16
ROPE_THETA = 500000.0
CONV_W = 31
N_MEM = 256
MAX_WINDOW = 2048
PAST_LEN = 16384
EPS = 1e-6
SCALE = HEAD_DIM ** -0.5
NEG = -1e30
LN2 = math.log(2.0)

LANES = 128
QB = 128
O_QA, O_KA, O_VA, O_GA, O_AB, O_BB, O_GB, O_QM, O_GM, D_IN = 0, 384, 768, 1152, 1536, 1920, 2304, 2688, 2944, 3200

VMEM_LIMIT = 56 * 1024 * 1024


def _sigmoid(x):
    return 1.0 / (1.0 + jnp.exp(-x))


def _silu(x):
    return x * _sigmoid(x)


def _rmsnorm(x, g):
    return x * lax.rsqrt(jnp.mean(x * x, axis=-1, keepdims=True) + EPS) * g


def _dot(a, b):
    return jnp.dot(a, b, preferred_element_type=F32)


def _dot_t(a, b):
    return lax.dot_general(a, b, (((1,), (1,)), ((), ())), preferred_element_type=F32)


def _rope(xw, cos, s1, s2):
    outs = []
    for g in range(xw.shape[1] // LANES):
        xg = xw[:, g * LANES:(g + 1) * LANES]
        outs.append(xg * cos + pltpu.roll(xg, LANES - 8, 1) * s1 + pltpu.roll(xg, 8, 1) * s2)
    return jnp.concatenate(outs, axis=1)


def _pair_split(q):
    lane = lax.broadcasted_iota(jnp.int32, q.shape, 1)
    qa = jnp.where(lane < HEAD_DIM, q, 0.0)
    qb = jnp.where(lane >= HEAD_DIM, q, 0.0)
    return jnp.concatenate([qa, qb], axis=0)


def _pair_join(x):
    t = x.shape[0] // 2
    lane = lax.broadcasted_iota(jnp.int32, (t, LANES), 1)
    return jnp.where(lane < HEAD_DIM, x[:t], x[t:])


def _mem_attend(qm, mk_ref, mv_ref):
    outs = []
    for g in range(M_WIDTH // LANES):
        cols = slice(g * LANES, (g + 1) * LANES)
        q2 = _pair_split(qm[:, cols] * SCALE).astype(BF16)
        s = _dot_t(q2, mk_ref[0, :, cols])
        m = jnp.max(s, axis=-1, keepdims=True)
        p = jnp.exp(s - m)
        l = jnp.sum(p, axis=-1, keepdims=True)
        o = _dot(p.astype(BF16), mv_ref[0, :, cols]) * (1.0 / l)
        outs.append(_pair_join(o))
    return jnp.concatenate(outs, axis=1)


def _conformer_tail(c, gate_b, bdw, lng, lnb, wpw_ref, bpw):
    cf = c + bdw
    mu = jnp.mean(cf, axis=-1, keepdims=True)
    dev = cf - mu
    var = jnp.mean(dev * dev, axis=-1, keepdims=True)
    cn = dev * lax.rsqrt(var + EPS) * lng + lnb
    ob = _dot(_silu(cn).astype(BF16), wpw_ref[...]) + bpw
    return ob * gate_b


def _mem_kv_kernel(mem_ref, g_ref, w_ref, mk_ref, mv_ref, mkb_ref, mvb_ref):
    h = _rmsnorm(mem_ref[0], g_ref[...]).astype(BF16)
    kv = _dot(h, w_ref[...])
    mk_ref[0] = kv[:, :M_WIDTH]
    mv_ref[0] = kv[:, M_WIDTH:]
    mkb_ref[0] = kv[:, :M_WIDTH].astype(BF16)
    mvb_ref[0] = kv[:, M_WIDTH:].astype(BF16)


def _mem_kv(mem, g, w_bf):
    b = mem.shape[0]
    blk = pl.BlockSpec((1, N_MEM, M_WIDTH), lambda i: (i, 0, 0))
    return pl.pallas_call(
        _mem_kv_kernel,
        grid=(b,),
        in_specs=[pl.BlockSpec((1, N_MEM, D_MODEL), lambda i: (i, 0, 0)),
                  pl.BlockSpec((1, D_MODEL), lambda i: (0, 0)),
                  pl.BlockSpec((D_MODEL, 2 * M_WIDTH), lambda i: (0, 0))],
        out_specs=[blk, blk, blk, blk],
        out_shape=[jax.ShapeDtypeStruct((b, N_MEM, M_WIDTH), F32)] * 2
        + [jax.ShapeDtypeStruct((b, N_MEM, M_WIDTH), BF16)] * 2,
        name="mem_kv",
    )(mem, g, w_bf)


CONV_HIST = 32
CONV_CHUNK = 64


def _prompt_proj_kernel(x_ref, npre_ref, win_ref, cos_ref, s1_ref, s2_ref, mk_ref, mv_ref,
                        wdw_ref, bdw_ref, lng_ref, lnb_ref, wpw_ref, bpw_ref,
                        q_ref, k_ref, v_ref, kf_ref, vf_ref, ga_ref, mix_ref, ust_ref,
                        uext_ref, conv_ref):
    t = x_ref.shape[1]
    i = pl.program_id(1)
    h = _rmsnorm(x_ref[0], npre_ref[...]).astype(BF16)
    proj = _dot(h, win_ref[...])

    cos, s1, s2 = cos_ref[...], s1_ref[...], s2_ref[...]
    q = _rope(proj[:, O_QA:O_KA], cos, s1, s2) * SCALE
    k = _rope(proj[:, O_KA:O_VA], cos, s1, s2)
    v = proj[:, O_VA:O_GA]
    q_ref[0] = q.astype(BF16)
    k_ref[0] = k.astype(BF16)
    v_ref[0] = v.astype(BF16)
    kf_ref[0] = k
    vf_ref[0] = v
    ga_ref[0] = _silu(proj[:, O_GA:O_AB])

    u = proj[:, O_AB:O_BB] * _sigmoid(proj[:, O_BB:O_GB])

    @pl.when(i == 0)
    def _():
        uext_ref[0:CONV_HIST, :] = jnp.zeros((CONV_HIST, C_WIDTH), F32)

    uext_ref[CONV_HIST:CONV_HIST + t, :] = u
    for r0 in range(0, t, CONV_CHUNK):
        acc = jnp.zeros((CONV_CHUNK, C_WIDTH), F32)
        for kk in range(CONV_W):
            tap = wdw_ref[CONV_W - 1 - kk:CONV_W - kk, :]
            acc = acc + uext_ref[CONV_HIST + r0 - kk:CONV_HIST + r0 - kk + CONV_CHUNK, :] * tap
        conv_ref[r0:r0 + CONV_CHUNK, :] = acc
    tail = uext_ref[t:t + CONV_HIST, :]
    uext_ref[0:CONV_HIST, :] = tail
    ust_ref[0] = tail

    mixed_b = _conformer_tail(conv_ref[...], _silu(proj[:, O_GB:O_QM]), bdw_ref[...], lng_ref[...], lnb_ref[...],
                              wpw_ref, bpw_ref[...])
    mixed_m = _mem_attend(proj[:, O_QM:O_GM], mk_ref, mv_ref) * _silu(proj[:, O_GM:D_IN])
    mix_ref[0, :, 0:C_WIDTH] = mixed_b.astype(BF16)
    mix_ref[0, :, C_WIDTH:] = mixed_m.astype(BF16)


def _prompt_proj(x, npre, win_bf, cos, s1, s2, mkb, mvb, wdw, bdw, lng, lnb, wpw_bf, bpw, tile):
    b, s, _ = x.shape
    nt = s // tile
    keep = min(MAX_WINDOW, s)
    first_keep = (s - keep) // tile
    row = lambda w: pl.BlockSpec((1, tile, w), lambda bi, i: (bi, i, 0))
    full = lambda a: pl.BlockSpec(a.shape, lambda bi, i: (0,) * a.ndim)
    tab = pl.BlockSpec((tile, LANES), lambda bi, i: (i, 0))
    memb = pl.BlockSpec((1, N_MEM, M_WIDTH), lambda bi, i: (bi, 0, 0))
    keepb = pl.BlockSpec((1, tile, A_WIDTH), lambda bi, i: (bi, jnp.maximum(i - first_keep, 0), 0))
    return pl.pallas_call(
        _prompt_proj_kernel,
        grid=(b, nt),
        in_specs=[row(D_MODEL), full(npre), full(win_bf), tab, tab, tab, memb, memb,
                  full(wdw), full(bdw), full(lng), full(lnb), full(wpw_bf), full(bpw)],
        out_specs=[row(A_WIDTH), row(A_WIDTH), row(A_WIDTH), keepb, keepb, row(A_WIDTH),
                   row(C_WIDTH + M_WIDTH), pl.BlockSpec((1, CONV_HIST, C_WIDTH), lambda bi, i: (bi, 0, 0))],
        out_shape=[jax.ShapeDtypeStruct((b, s, A_WIDTH), BF16)] * 3
        + [jax.ShapeDtypeStruct((b, keep, A_WIDTH), F32)] * 2
        + [jax.ShapeDtypeStruct((b, s, A_WIDTH), F32),
           jax.ShapeDtypeStruct((b, s, C_WIDTH + M_WIDTH), BF16),
           jax.ShapeDtypeStruct((b, CONV_HIST, C_WIDTH), F32)],
        scratch_shapes=[pltpu.VMEM((CONV_HIST + tile, C_WIDTH), F32), pltpu.VMEM((tile, C_WIDTH), F32)],
        compiler_params=pltpu.CompilerParams(dimension_semantics=("arbitrary", "arbitrary"),
                                             vmem_limit_bytes=VMEM_LIMIT),
        name="prompt_proj",
    )(x, npre, win_bf, cos, s1, s2, mkb, mvb, wdw, bdw, lng, lnb, wpw_bf, bpw)


DIL16 = 16
STAT_L = 8


def _dilated16_kernel(q_ref, kp_ref, kc_ref, vp_ref, vc_ref, o_ref, st_ref):
    i = pl.program_id(1)
    a = lax.broadcasted_iota(jnp.int32, (QB, 2 * QB), 0)
    c = lax.broadcasted_iota(jnp.int32, (QB, 2 * QB), 1)
    dj = QB + a - c
    valid = (dj >= 0) & (dj <= QB) & ((c >= QB) | (i > 0))
    bias = jnp.where(valid, 0.0, NEG)
    bias2 = jnp.concatenate([bias, bias], axis=0)
    lane = lax.broadcasted_iota(jnp.int32, (QB, LANES), 1)
    st = jnp.zeros((QB, LANES), F32)
    for g in range(A_WIDTH // LANES):
        cols = slice(g * LANES, (g + 1) * LANES)
        q2 = _pair_split(q_ref[0, :, cols].astype(F32)).astype(BF16)
        s = jnp.concatenate([_dot_t(q2, kp_ref[0, :, cols]), _dot_t(q2, kc_ref[0, :, cols])], axis=1) + bias2
        m = jnp.max(s, axis=-1, keepdims=True)
        p = jnp.exp(s - m)
        l = jnp.sum(p, axis=-1, keepdims=True)
        pb = p.astype(BF16)
        o = _dot(pb[:, :QB], vp_ref[0, :, cols]) + _dot(pb[:, QB:], vc_ref[0, :, cols])
        o_ref[0, :, cols] = _pair_join(o)
        st = jnp.where(lane == 2 * g, m[:QB], st)
        st = jnp.where(lane == 2 * g + 1, m[QB:], st)
        st = jnp.where(lane == STAT_L + 2 * g, l[:QB], st)
        st = jnp.where(lane == STAT_L + 2 * g + 1, l[QB:], st)
    st_ref[0] = st


def _dilated16(q, k, v):
    b, s, w = q.shape
    nj = s // DIL16
    view = lambda a: a.reshape(b, nj, DIL16 * w)
    cur = pl.BlockSpec((1, QB, w), lambda bi, i, c: (bi, i, c))
    prev = pl.BlockSpec((1, QB, w), lambda bi, i, c: (bi, jnp.maximum(i - 1, 0), c))
    o16, st16 = pl.pallas_call(
        _dilated16_kernel,
        grid=(b, nj // QB, DIL16),
        in_specs=[cur, prev, cur, prev, cur],
        out_specs=[cur, pl.BlockSpec((1, QB, LANES), lambda bi, i, c: (bi, i, c))],
        out_shape=[jax.ShapeDtypeStruct((b, nj, DIL16 * w), F32),
                   jax.ShapeDtypeStruct((b, nj, DIL16 * LANES), F32)],
        compiler_params=pltpu.CompilerParams(dimension_semantics=("arbitrary",) * 3,
                                             vmem_limit_bytes=VMEM_LIMIT),
        name="dilated16",
    )(view(q), view(k), view(k), view(v), view(v))
    return o16.reshape(b, s, w), st16.reshape(b, s, LANES)


BACK = 512
KWIN = BACK + QB


def _prompt_attn_kernel(*refs, tq, nblk):
    q_ref = refs[0]
    k_refs = refs[1:1 + nblk]
    v_refs = refs[1 + nblk:1 + 2 * nblk]
    o16_ref, st_ref, ga_ref, mix_ref, x_ref, wout_ref, npost_ref, y_ref, mixa_ref = refs[1 + 2 * nblk:]
    i = pl.program_id(1)

    a = lax.broadcasted_iota(jnp.int32, (QB, KWIN), 0)
    c = lax.broadcasted_iota(jnp.int32, (QB, KWIN), 1)
    d = BACK + a - c
    near = (d >= 0) & (d <= QB)
    far = (d >= 0) & (d <= BACK) & ((d & 3) == 0)
    bias0 = jnp.where(near & far, LN2, jnp.where(near | far, 0.0, NEG))
    lane = lax.broadcasted_iota(jnp.int32, (QB, LANES), 1)
    head_a = lane < HEAD_DIM

    for sub in range(tq // QB):
        rows = slice(sub * QB, (sub + 1) * QB)
        first_valid = BACK - i * tq - sub * QB
        bias = jnp.where(c >= first_valid, bias0, NEG)
        bias2 = jnp.concatenate([bias, bias], axis=0)
        pieces = []
        w = sub * QB
        while w < sub * QB + KWIN:
            blk, off = divmod(w, tq)
            n = min(tq - off, sub * QB + KWIN - w)
            pieces.append((blk, off, n))
            w += n
        st = st_ref[0, rows, :]
        for g in range(A_WIDTH // LANES):
            cols = slice(g * LANES, (g + 1) * LANES)
            q2 = _pair_split(q_ref[0, rows, cols].astype(F32)).astype(BF16)
            s = jnp.concatenate([_dot_t(q2, k_refs[blk][0, off:off + n, cols]) for blk, off, n in pieces],
                                axis=1) + bias2
            m = jnp.max(s, axis=-1, keepdims=True)
            p = jnp.exp(s - m)
            l = jnp.sum(p, axis=-1, keepdims=True)
            pb = p.astype(BF16)
            o = None
            col = 0
            for blk, off, n in pieces:
                part = _dot(pb[:, col:col + n], v_refs[blk][0, off:off + n, cols])
                o = part if o is None else o + part
                col += n
            o_n = _pair_join(o)
            m_n = jnp.where(head_a, m[:QB], m[QB:])
            l_n = jnp.where(head_a, l[:QB], l[QB:])
            m_f = jnp.where(head_a, st[:, 2 * g:2 * g + 1], st[:, 2 * g + 1:2 * g + 2])
            l_f = jnp.where(head_a, st[:, STAT_L + 2 * g:STAT_L + 2 * g + 1],
                            st[:, STAT_L + 2 * g + 1:STAT_L + 2 * g + 2])
            mx = jnp.maximum(m_n, m_f)
            w_n = jnp.exp(m_n - mx)
            w_f = jnp.exp(m_f - mx)
            oa = (o_n * w_n + o16_ref[0, rows, cols] * w_f) / (l_n * w_n + l_f * w_f)
            mixa_ref[rows, cols] = (oa * ga_ref[0, rows, cols]).astype(BF16)

    z = _dot(mixa_ref[...], wout_ref[0:A_WIDTH, :]) + _dot(mix_ref[0], wout_ref[A_WIDTH:, :])
    y_ref[0] = x_ref[0] + _rmsnorm(z, npost_ref[...])


def _prompt_attn(q, k, v, o16, st16, ga, mix, x, wout_bf, npost, tq):
    b, s, _ = x.shape
    nblk = BACK // tq + 1
    row = lambda w: pl.BlockSpec((1, tq, w), lambda bi, i: (bi, i, 0))
    back = lambda j: pl.BlockSpec((1, tq, A_WIDTH), lambda bi, i: (bi, jnp.maximum(i - (nblk - 1 - j), 0), 0))
    full = lambda a: pl.BlockSpec(a.shape, lambda bi, i: (0,) * a.ndim)
    kv_specs = [back(j) for j in range(nblk)]
    return pl.pallas_call(
        functools.partial(_prompt_attn_kernel, tq=tq, nblk=nblk),
        grid=(b, s // tq),
        in_specs=[row(A_WIDTH)] + kv_specs + kv_specs
        + [row(A_WIDTH), row(LANES), row(A_WIDTH), row(C_WIDTH + M_WIDTH), row(D_MODEL), full(wout_bf), full(npost)],
        out_specs=row(D_MODEL),
        out_shape=jax.ShapeDtypeStruct((b, s, D_MODEL), F32),
        scratch_shapes=[pltpu.VMEM((tq, A_WIDTH), BF16)],
        compiler_params=pltpu.CompilerParams(dimension_semantics=("arbitrary", "arbitrary"),
                                             vmem_limit_bytes=VMEM_LIMIT),
        name="prompt_attn",
    )(q, *([k] * nblk), *([v] * nblk), o16, st16, ga, mix, x, wout_bf, npost)


def _sample_proj_kernel(x_ref, npre_ref, win_ref, cos_ref, s1_ref, s2_ref,
                        q_ref, k_ref, v_ref, ga_ref, u_ref, gb_ref, qm_ref, gm_ref):
    h = _rmsnorm(x_ref[...], npre_ref[...]).astype(BF16)
    proj = _dot(h, win_ref[...])
    cos, s1, s2 = cos_ref[...], s1_ref[...], s2_ref[...]
    q_ref[...] = _rope(proj[:, O_QA:O_KA], cos, s1, s2) * SCALE
    k_ref[...] = _rope(proj[:, O_KA:O_VA], cos, s1, s2)
    v_ref[...] = proj[:, O_VA:O_GA]
    ga_ref[...] = _silu(proj[:, O_GA:O_AB])
    u_ref[...] = proj[:, O_AB:O_BB] * _sigmoid(proj[:, O_BB:O_GB])
    gb_ref[...] = _silu(proj[:, O_GB:O_QM])
    qm_ref[...] = proj[:, O_QM:O_GM]
    gm_ref[...] = _silu(proj[:, O_GM:D_IN])


def _sample_proj(x, npre, win_bf, cos, s1, s2):
    n = x.shape[0]
    full = lambda a: pl.BlockSpec(a.shape, lambda i: (0,) * a.ndim)
    outw = [A_WIDTH, A_WIDTH, A_WIDTH, A_WIDTH, C_WIDTH, C_WIDTH, M_WIDTH, M_WIDTH]
    return pl.pallas_call(
        _sample_proj_kernel,
        grid=(1,),
        in_specs=[full(x), full(npre), full(win_bf), full(cos), full(s1), full(s2)],
        out_specs=[pl.BlockSpec((n, w), lambda i: (0, 0)) for w in outw],
        out_shape=[jax.ShapeDtypeStruct((n, w), F32) for w in outw],
        compiler_params=pltpu.CompilerParams(vmem_limit_bytes=VMEM_LIMIT),
        name="sample_proj",
    )(x, npre, win_bf, cos, s1, s2)


STATE_PAD = 2
NEWK = 128


def _sample_mix_kernel(q_ref, kn_ref, vn_ref, u_ref, qm_ref, ck_ref, cv_ref, st_ref, cmk_ref, cmv_ref, wdw_ref,
                       oa_ref, om_ref, cv_out_ref, wk_ref, wv_ref, ns_ref, uc_ref, *, t_s, win_len):
    kn, vn = kn_ref[...], vn_ref[...]
    wk_ref[0, 0:win_len - t_s, :] = ck_ref[0, t_s:win_len, :]
    wk_ref[0, win_len - t_s:win_len, :] = kn
    wv_ref[0, 0:win_len - t_s, :] = cv_ref[0, t_s:win_len, :]
    wv_ref[0, win_len - t_s:win_len, :] = vn

    nr = A_HEADS * t_s
    q = q_ref[...]
    head_of_lane = lax.broadcasted_iota(jnp.int32, (t_s, A_WIDTH), 1) >> HEAD_SHIFT
    q6 = jnp.concatenate([jnp.where(head_of_lane == h, q, 0.0) for h in range(A_HEADS)], axis=0).astype(BF16)
    pad = jnp.zeros((NEWK - t_s, A_WIDTH), F32)
    kn_pad = jnp.concatenate([kn, pad], axis=0).astype(BF16)
    vn_pad = jnp.concatenate([vn, pad], axis=0).astype(BF16)
    s_c = _dot_t(q6, ck_ref[0].astype(BF16))
    s_n = _dot_t(q6, kn_pad)

    def weights(shape, key0):
        tq = lax.broadcasted_iota(jnp.int32, shape, 0) & (t_s - 1)
        key = lax.broadcasted_iota(jnp.int32, shape, 1) + key0
        d = win_len + tq - key
        ok = (d >= 0) & (key < win_len + t_s)
        w = ((d <= 128).astype(F32) + ((d <= 512) & ((d & 3) == 0)).astype(F32)
             + ((d <= 2048) & ((d & 15) == 0)).astype(F32))
        return jnp.where(ok, w, 0.0)

    w_c = weights(s_c.shape, 0)
    w_n = weights(s_n.shape, win_len)
    s_c = jnp.where(w_c > 0.0, s_c, NEG)
    s_n = jnp.where(w_n > 0.0, s_n, NEG)
    m = jnp.maximum(jnp.max(s_c, axis=-1, keepdims=True), jnp.max(s_n, axis=-1, keepdims=True))
    p_c = jnp.exp(s_c - m) * w_c
    p_n = jnp.exp(s_n - m) * w_n
    l = jnp.sum(p_c, axis=-1, keepdims=True) + jnp.sum(p_n, axis=-1, keepdims=True)
    o = (_dot(p_c.astype(BF16), cv_ref[0].astype(BF16)) + _dot(p_n.astype(BF16), vn_pad)) * (1.0 / l)
    oa = jnp.zeros((t_s, A_WIDTH), F32)
    for h in range(A_HEADS):
        oa = oa + jnp.where(head_of_lane == h, o[h * t_s:(h + 1) * t_s], 0.0)
    oa_ref[...] = oa

    qm = qm_ref[...] * SCALE
    mhead = lax.broadcasted_iota(jnp.int32, (t_s, M_WIDTH), 1) >> HEAD_SHIFT
    q4 = jnp.concatenate([jnp.where(mhead == h, qm, 0.0) for h in range(M_HEADS)], axis=0).astype(BF16)
    sm = _dot_t(q4, cmk_ref[0].astype(BF16))
    mm = jnp.max(sm, axis=-1, keepdims=True)
    pm = jnp.exp(sm - mm)
    lm = jnp.sum(pm, axis=-1, keepdims=True)
    om4 = _dot(pm.astype(BF16), cmv_ref[0].astype(BF16)) * (1.0 / lm)
    om = jnp.zeros((t_s, M_WIDTH), F32)
    for h in range(M_HEADS):
        om = om + jnp.where(mhead == h, om4[h * t_s:(h + 1) * t_s], 0.0)
    om_ref[...] = om

    nst = st_ref.shape[1]
    uc_ref[0:nst, :] = st_ref[0]
    uc_ref[nst:nst + t_s, :] = u_ref[...]
    acc = jnp.zeros((t_s, C_WIDTH), F32)
    for w in range(CONV_W):
        acc = acc + uc_ref[STATE_PAD + w:STATE_PAD + w + t_s, :] * wdw_ref[w:w + 1, :]
    cv_out_ref[...] = acc
    ns_ref[0] = uc_ref[STATE_PAD + t_s:STATE_PAD + t_s + CONV_W - 1, :]


def _sample_mix(q, kn, vn, u, qm, ck, cv, st_pad, cmk, cmv, wdw, t_s):
    nb, win_len, _ = ck.shape
    rowb = lambda w: pl.BlockSpec((t_s, w), lambda i: (i, 0))
    per = lambda a: pl.BlockSpec((1,) + a.shape[1:], lambda i: (i, 0, 0))
    n = nb * t_s
    return pl.pallas_call(
        functools.partial(_sample_mix_kernel, t_s=t_s, win_len=win_len),
        grid=(nb,),
        in_specs=[rowb(A_WIDTH), rowb(A_WIDTH), rowb(A_WIDTH), rowb(C_WIDTH), rowb(M_WIDTH),
                  per(ck), per(cv), per(st_pad), per(cmk), per(cmv), pl.BlockSpec(wdw.shape, lambda i: (0, 0))],
        out_specs=[rowb(A_WIDTH), rowb(M_WIDTH), rowb(C_WIDTH), per(ck), per(cv),
                   pl.BlockSpec((1, CONV_W - 1, C_WIDTH), lambda i: (i, 0, 0))],
        out_shape=[jax.ShapeDtypeStruct((n, A_WIDTH), F32), jax.ShapeDtypeStruct((n, M_WIDTH), F32),
                   jax.ShapeDtypeStruct((n, C_WIDTH), F32),
                   jax.ShapeDtypeStruct(ck.shape, F32), jax.ShapeDtypeStruct(cv.shape, F32),
                   jax.ShapeDtypeStruct((nb, CONV_W - 1, C_WIDTH), F32)],
        scratch_shapes=[pltpu.VMEM((STATE_PAD + CONV_W - 1 + t_s, C_WIDTH), F32)],
        compiler_params=pltpu.CompilerParams(dimension_semantics=("arbitrary",), vmem_limit_bytes=VMEM_LIMIT),
        name="sample_mix",
    )(q, kn, vn, u, qm, ck, cv, st_pad, cmk, cmv, wdw)


def _sample_out_kernel(x_ref, oa_ref, ga_ref, c_ref, gb_ref, om_ref, gm_ref, bdw_ref, lng_ref, lnb_ref, wpw_ref,
                       bpw_ref, wout_ref, npost_ref, y_ref):
    mixed_a = (oa_ref[...] * ga_ref[...]).astype(BF16)
    mixed_b = _conformer_tail(c_ref[...], gb_ref[...], bdw_ref[...], lng_ref[...], lnb_ref[...], wpw_ref,
                              bpw_ref[...]).astype(BF16)
    mixed_m = (om_ref[...] * gm_ref[...]).astype(BF16)
    z = (_dot(mixed_a, wout_ref[0:A_WIDTH, :]) + _dot(mixed_b, wout_ref[A_WIDTH:A_WIDTH + C_WIDTH, :])
         + _dot(mixed_m, wout_ref[A_WIDTH + C_WIDTH:, :]))
    y_ref[...] = x_ref[...] + _rmsnorm(z, npost_ref[...])


def _sample_out(x, oa, ga, c, gb, om, gm, bdw, lng, lnb, wpw_bf, bpw, wout_bf, npost):
    args = (x, oa, ga, c, gb, om, gm, bdw, lng, lnb, wpw_bf, bpw, wout_bf, npost)
    return pl.pallas_call(
        _sample_out_kernel,
        grid=(1,),
        in_specs=[pl.BlockSpec(a.shape, lambda i: (0, 0)) for a in args],
        out_specs=pl.BlockSpec(x.shape, lambda i: (0, 0)),
        out_shape=jax.ShapeDtypeStruct(x.shape, F32),
        compiler_params=pltpu.CompilerParams(vmem_limit_bytes=VMEM_LIMIT),
        name="sample_out",
    )(*args)


def _rope_tables(pos):
    inv = ROPE_THETA ** (-jnp.arange(0, ROT_DIM, 2, dtype=F32) / ROT_DIM)
    ang = pos.astype(F32)[:, None] * inv[None, :]
    cos, sin = jnp.cos(ang), jnp.sin(ang)
    n = pos.shape[0]
    half = ROT_DIM // 2
    ones = jnp.ones((n, HEAD_DIM - ROT_DIM), F32)
    zeros = jnp.zeros((n, HEAD_DIM - ROT_DIM), F32)
    zh = jnp.zeros((n, half), F32)
    c = jnp.concatenate([cos, cos, ones], axis=1)
    s1 = jnp.concatenate([-sin, zh, zeros], axis=1)
    s2 = jnp.concatenate([zh, sin, zeros], axis=1)
    rep = lambda a: jnp.concatenate([a] * (LANES // HEAD_DIM), axis=1)
    return rep(c), rep(s1), rep(s2)


PROJ_TILE = 256
ATTN_TILE = 256


def kernel(x_prompt, x_sample, cache_win_k, cache_win_v, state_conv, cache_mem_k, cache_mem_v, mem_prompt,
           norm_pre, norm_post, w_in, w_out, norm_mem, w_mem_kv, w_dw, b_dw, ln_conv_g, ln_conv_b, w_pw2, b_pw2):
    depth = w_in.shape[0]
    assert depth == 1, "single-layer step"
    b, s, _ = x_prompt.shape
    nb, t_s, _ = x_sample.shape
    win_len = cache_win_k.shape[2]
    assert s % (DIL16 * QB) == 0 and s % PROJ_TILE == 0 and s % ATTN_TILE == 0 and BACK % ATTN_TILE == 0
    assert win_len == MAX_WINDOW and win_len + t_s >= MAX_WINDOW and t_s % 8 == 0 and t_s & (t_s - 1) == 0 and t_s <= NEWK
    l = 0
    row = lambda a: a[l][None, :]
    npre, npost, nmem = row(norm_pre), row(norm_post), row(norm_mem)
    bdw, lng, lnb, bpw = row(b_dw), row(ln_conv_g), row(ln_conv_b), row(b_pw2)
    win_bf, wout_bf = w_in[l].astype(BF16), w_out[l].astype(BF16)
    wmem_bf, wpw_bf = w_mem_kv[l].astype(BF16), w_pw2[l].astype(BF16)
    wdw = w_dw[l]

    mk, mv, mkb, mvb = _mem_kv(mem_prompt, nmem, wmem_bf)
    cos, s1, s2 = _rope_tables(jnp.arange(s, dtype=jnp.int32))
    q, k, v, kf, vf, ga, mix, ust = _prompt_proj(x_prompt, npre, win_bf, cos, s1, s2, mkb, mvb,
                                                 wdw, bdw, lng, lnb, wpw_bf, bpw, PROJ_TILE)
    o16, st16 = _dilated16(q, k, v)
    y_prompt = _prompt_attn(q, k, v, o16, st16, ga, mix, x_prompt, wout_bf, npost, ATTN_TILE)

    pos_s = PAST_LEN + jnp.arange(t_s, dtype=jnp.int32)
    cs, s1s, s2s = (jnp.tile(a, (nb, 1)) for a in _rope_tables(pos_s))
    xs = x_sample.reshape(nb * t_s, D_MODEL)
    qs, ks, vs, gas, us, gbs, qms, gms = _sample_proj(xs, npre, win_bf, cs, s1s, s2s)
    ck = cache_win_k[l].reshape(nb, win_len, A_WIDTH)
    cv = cache_win_v[l].reshape(nb, win_len, A_WIDTH)
    st_pad = jnp.pad(state_conv[l], ((0, 0), (STATE_PAD, 0), (0, 0)))
    cmk = cache_mem_k[l].reshape(nb, N_MEM, M_WIDTH)
    cmv = cache_mem_v[l].reshape(nb, N_MEM, M_WIDTH)
    oas, oms, cs_conv, wk, wv, nst = _sample_mix(qs, ks, vs, us, qms, ck, cv, st_pad, cmk, cmv, wdw, t_s)
    y_sample = _sample_out(xs, oas, gas, cs_conv, gbs, oms, gms, bdw, lng, lnb, wpw_bf, bpw, wout_bf, npost)

    keep_p = kf.shape[1]
    return (y_prompt,
            y_sample.reshape(nb, t_s, D_MODEL),
            kf.reshape(1, b, keep_p, A_HEADS, HEAD_DIM),
            vf.reshape(1, b, keep_p, A_HEADS, HEAD_DIM),
            ust[:, CONV_HIST - (CONV_W - 1):, :][None],
            mk.reshape(1, b, N_MEM, M_HEADS, HEAD_DIM),
            mv.reshape(1, b, N_MEM, M_HEADS, HEAD_DIM),
            wk.reshape(1, nb, win_len, A_HEADS, HEAD_DIM),
            wv.reshape(1, nb, win_len, A_HEADS, HEAD_DIM),
            nst[None])
```

```python
import functools
import math

import jax
import jax.numpy as jnp
from jax import lax
from jax.experimental import pallas as pl
from jax.experimental.pallas import tpu as pltpu

F32 = jnp.float32
BF16 = jnp.bfloat16

D_MODEL = 1024
HEAD_DIM = 64
HEAD_SHIFT = 6
A_WIDTH = 384
A_HEADS = 6
M_WIDTH = 256
M_HEADS = 4
C_WIDTH = 384
ROT_DIM = 16
ROPE_THETA = 500000.0
CONV_W = 31
N_MEM = 256
MAX_WINDOW = 2048
PAST_LEN = 16384
EPS = 1e-6
SCALE = HEAD_DIM ** -0.5
NEG = -1e30
LN2 = math.log(2.0)

LANES = 128
QB = 128
O_QA, O_KA, O_VA, O_GA, O_AB, O_BB, O_GB, O_QM, O_GM, D_IN = 0, 384, 768, 1152, 1536, 1920, 2304, 2688, 2944, 3200

VMEM_LIMIT = 56 * 1024 * 1024


def _sigmoid(x):
    return 1.0 / (1.0 + jnp.exp(-x))


def _silu(x):
    return x * _sigmoid(x)


def _rmsnorm(x, g):
    return x * lax.rsqrt(jnp.mean(x * x, axis=-1, keepdims=True) + EPS) * g


def _dot(a, b):
    return jnp.dot(a, b, preferred_element_type=F32)


def _dot_t(a, b):
    return lax.dot_general(a, b, (((1,), (1,)), ((), ())), preferred_element_type=F32)


def _rope(xw, cos, s1, s2):
    outs = []
    for g in range(xw.shape[1] // LANES):
        xg = xw[:, g * LANES:(g + 1) * LANES]
        outs.append(xg * cos + pltpu.roll(xg, LANES - 8, 1) * s1 + pltpu.roll(xg, 8, 1) * s2)
    return jnp.concatenate(outs, axis=1)


def _pair_split(q):
    lane = lax.broadcasted_iota(jnp.int32, q.shape, 1)
    qa = jnp.where(lane < HEAD_DIM, q, 0.0)
    qb = jnp.where(lane >= HEAD_DIM, q, 0.0)
    return jnp.concatenate([qa, qb], axis=0)


def _pair_join(x):
    t = x.shape[0] // 2
    lane = lax.broadcasted_iota(jnp.int32, (t, LANES), 1)
    return jnp.where(lane < HEAD_DIM, x[:t], x[t:])


def _mem_attend(qm, mk_ref, mv_ref):
    outs = []
    for g in range(M_WIDTH // LANES):
        cols = slice(g * LANES, (g + 1) * LANES)
        q2 = _pair_split(qm[:, cols] * SCALE).astype(BF16)
        s = _dot_t(q2, mk_ref[0, :, cols])
        m = jnp.max(s, axis=-1, keepdims=True)
        p = jnp.exp(s - m)
        l = jnp.sum(p, axis=-1, keepdims=True)
        o = _dot(p.astype(BF16), mv_ref[0, :, cols]) * (1.0 / l)
        outs.append(_pair_join(o))
    return jnp.concatenate(outs, axis=1)


def _conformer_tail(c, gate_b, bdw, lng, lnb, wpw_ref, bpw):
    cf = c + bdw
    mu = jnp.mean(cf, axis=-1, keepdims=True)
    dev = cf - mu
    var = jnp.mean(dev * dev, axis=-1, keepdims=True)
    cn = dev * lax.rsqrt(var + EPS) * lng + lnb
    ob = _dot(_silu(cn).astype(BF16), wpw_ref[...]) + bpw
    return ob * gate_b


def _mem_kv_kernel(mem_ref, g_ref, w_ref, mk_ref, mv_ref, mkb_ref, mvb_ref):
    h = _rmsnorm(mem_ref[0], g_ref[...]).astype(BF16)
    kv = _dot(h, w_ref[...])
    mk_ref[0] = kv[:, :M_WIDTH]
    mv_ref[0] = kv[:, M_WIDTH:]
    mkb_ref[0] = kv[:, :M_WIDTH].astype(BF16)
    mvb_ref[0] = kv[:, M_WIDTH:].astype(BF16)


def _mem_kv(mem, g, w_bf):
    b = mem.shape[0]
    blk = pl.BlockSpec((1, N_MEM, M_WIDTH), lambda i: (i, 0, 0))
    return pl.pallas_call(
        _mem_kv_kernel,
        grid=(b,),
        in_specs=[pl.BlockSpec((1, N_MEM, D_MODEL), lambda i: (i, 0, 0)),
                  pl.BlockSpec((1, D_MODEL), lambda i: (0, 0)),
                  pl.BlockSpec((D_MODEL, 2 * M_WIDTH), lambda i: (0, 0))],
        out_specs=[blk, blk, blk, blk],
        out_shape=[jax.ShapeDtypeStruct((b, N_MEM, M_WIDTH), F32)] * 2
        + [jax.ShapeDtypeStruct((b, N_MEM, M_WIDTH), BF16)] * 2,
        name="mem_kv",
    )(mem, g, w_bf)


CONV_HIST = 32
CONV_CHUNK = 64


def _prompt_proj_kernel(x_ref, npre_ref, win_ref, cos_ref, s1_ref, s2_ref, mk_ref, mv_ref,
                        wdw_ref, bdw_ref, lng_ref, lnb_ref, wpw_ref, bpw_ref,
                        q_ref, k_ref, v_ref, q16_ref, k16_ref, v16_ref, kf_ref, vf_ref, ga_ref, mix_ref, ust_ref,
                        uext_ref, ush_ref, conv_ref, qkv_ref):
    t = x_ref.shape[1]
    i = pl.program_id(1)
    h = _rmsnorm(x_ref[0], npre_ref[...]).astype(BF16)
    proj = _dot(h, win_ref[...])

    cos, s1, s2 = cos_ref[...], s1_ref[...], s2_ref[...]
    q = _rope(proj[:, O_QA:O_KA], cos, s1, s2) * SCALE
    k = _rope(proj[:, O_KA:O_VA], cos, s1, s2)
    v = proj[:, O_VA:O_GA]
    q_ref[0] = q.astype(BF16)
    k_ref[0] = k.astype(BF16)
    v_ref[0] = v.astype(BF16)
    kf_ref[0] = k
    vf_ref[0] = v
    ga_ref[0] = _silu(proj[:, O_GA:O_AB])

    ngrp = A_WIDTH // LANES
    for idx, val in enumerate((q, k, v)):
        for g in range(ngrp):
            qkv_ref[ngrp * idx + g] = val[:, g * LANES:(g + 1) * LANES]
    for c in range(DIL16):
        for idx, out16 in enumerate((q16_ref, k16_ref, v16_ref)):
            rows_c = [qkv_ref[ngrp * idx + g, pl.ds(c, t // DIL16, stride=DIL16), :] for g in range(ngrp)]
            out16[0, c] = jnp.concatenate(rows_c, axis=1).astype(BF16)

    u = proj[:, O_AB:O_BB] * _sigmoid(proj[:, O_BB:O_GB])

    @pl.when(i == 0)
    def _():
        uext_ref[0:CONV_HIST, :] = jnp.zeros((CONV_HIST, C_WIDTH), F32)

    uext_ref[CONV_HIST:CONV_HIST + t, :] = u
    for r in range(1, 8):
        ush_ref[r - 1, 8:CONV_HIST + t, :] = uext_ref[8 - r:CONV_HIST + t - r, :]
    for r0 in range(0, t, CONV_CHUNK):
        acc = jnp.zeros((CONV_CHUNK, C_WIDTH), F32)
        for kk in range(CONV_W):
            tap = wdw_ref[CONV_W - 1 - kk:CONV_W - kk, :]
            r, base = kk % 8, CONV_HIST + r0 - (kk - kk % 8)
            if r == 0:
                slab = uext_ref[base:base + CONV_CHUNK, :]
            else:
                slab = ush_ref[r - 1, base:base + CONV_CHUNK, :]
            acc = acc + slab * tap
        conv_ref[r0:r0 + CONV_CHUNK, :] = acc
    tail = uext_ref[t:t + CONV_HIST, :]
    uext_ref[0:CONV_HIST, :] = tail
    ust_ref[0] = tail

    mixed_b = _conformer_tail(conv_ref[...], _silu(proj[:, O_GB:O_QM]), bdw_ref[...], lng_ref[...], lnb_ref[...],
                              wpw_ref, bpw_ref[...])
    mixed_m = _mem_attend(proj[:, O_QM:O_GM], mk_ref, mv_ref) * _silu(proj[:, O_GM:D_IN])
    mix_ref[0, :, 0:C_WIDTH] = mixed_b.astype(BF16)
    mix_ref[0, :, C_WIDTH:] = mixed_m.astype(BF16)


def _prompt_proj(x, npre, win_bf, cos, s1, s2, mkb, mvb, wdw, bdw, lng, lnb, wpw_bf, bpw, tile):
    b, s, _ = x.shape
    nt = s // tile
    keep = min(MAX_WINDOW, s)
    first_keep = (s - keep) // tile
    row = lambda w: pl.BlockSpec((1, tile, w), lambda bi, i: (bi, i, 0))
    full = lambda a: pl.BlockSpec(a.shape, lambda bi, i: (0,) * a.ndim)
    tab = pl.BlockSpec((tile, LANES), lambda bi, i: (i, 0))
    memb = pl.BlockSpec((1, N_MEM, M_WIDTH), lambda bi, i: (bi, 0, 0))
    keepb = pl.BlockSpec((1, tile, A_WIDTH), lambda bi, i: (bi, jnp.maximum(i - first_keep, 0), 0))
    cls = pl.BlockSpec((1, DIL16, tile // DIL16, A_WIDTH), lambda bi, i: (bi, 0, i, 0))
    return pl.pallas_call(
        _prompt_proj_kernel,
        grid=(b, nt),
        in_specs=[row(D_MODEL), full(npre), full(win_bf), tab, tab, tab, memb, memb,
                  full(wdw), full(bdw), full(lng), full(lnb), full(wpw_bf), full(bpw)],
        out_specs=[row(A_WIDTH), row(A_WIDTH), row(A_WIDTH), cls, cls, cls, keepb, keepb, row(A_WIDTH),
                   row(C_WIDTH + M_WIDTH), pl.BlockSpec((1, CONV_HIST, C_WIDTH), lambda bi, i: (bi, 0, 0))],
        out_shape=[jax.ShapeDtypeStruct((b, s, A_WIDTH), BF16)] * 3
        + [jax.ShapeDtypeStruct((b, DIL16, s // DIL16, A_WIDTH), BF16)] * 3
        + [jax.ShapeDtypeStruct((b, keep, A_WIDTH), F32)] * 2
        + [jax.ShapeDtypeStruct((b, s, A_WIDTH), F32),
           jax.ShapeDtypeStruct((b, s, C_WIDTH + M_WIDTH), BF16),
           jax.ShapeDtypeStruct((b, CONV_HIST, C_WIDTH), F32)],
        scratch_shapes=[pltpu.VMEM((CONV_HIST + tile, C_WIDTH), F32),
                        pltpu.VMEM((7, CONV_HIST + tile, C_WIDTH), F32),
                        pltpu.VMEM((tile, C_WIDTH), F32),
                        pltpu.VMEM((3 * (A_WIDTH // LANES), tile, LANES), F32)],
        compiler_params=pltpu.CompilerParams(dimension_semantics=("arbitrary", "arbitrary"),
                                             vmem_limit_bytes=VMEM_LIMIT),
        name="prompt_proj",
    )(x, npre, win_bf, cos, s1, s2, mkb, mvb, wdw, bdw, lng, lnb, wpw_bf, bpw)


DIL16 = 16
STAT_L = 8


def _dilated16_kernel(q_ref, k_ref, v_ref, o_ref, st_ref):
    a = lax.broadcasted_iota(jnp.int32, (QB, QB), 0)
    c = lax.broadcasted_iota(jnp.int32, (QB, QB), 1)
    tri_prev = jnp.where(c >= a, 0.0, NEG)
    tri_cur = jnp.where(c <= a, 0.0, NEG)
    bias_cur = jnp.concatenate([tri_cur, tri_cur], axis=0)
    both = jnp.concatenate([tri_prev, tri_cur], axis=1)
    bias_both = jnp.concatenate([both, both], axis=0)
    lane = lax.broadcasted_iota(jnp.int32, (QB, LANES), 1)
    for sub in range(q_ref.shape[2] // QB):
        rows = slice(sub * QB, (sub + 1) * QB)
        prev = slice((sub - 1) * QB, sub * QB)
        st = jnp.zeros((QB, LANES), F32)
        for g in range(A_WIDTH // LANES):
            cols = slice(g * LANES, (g + 1) * LANES)
            q2 = _pair_split(q_ref[0, 0, rows, cols].astype(F32)).astype(BF16)
            s_cur = _dot_t(q2, k_ref[0, 0, rows, cols])
            if sub == 0:
                s = s_cur + bias_cur
            else:
                s = jnp.concatenate([_dot_t(q2, k_ref[0, 0, prev, cols]), s_cur], axis=1) + bias_both
            m = jnp.max(s, axis=-1, keepdims=True)
            p = jnp.exp(s - m)
            l = jnp.sum(p, axis=-1, keepdims=True)
            pb = p.astype(BF16)
            if sub == 0:
                o = _dot(pb, v_ref[0, 0, rows, cols])
            else:
                o = _dot(pb[:, :QB], v_ref[0, 0, prev, cols]) + _dot(pb[:, QB:], v_ref[0, 0, rows, cols])
            o_ref[0, 0, rows, cols] = _pair_join(o)
            st = jnp.where(lane == 2 * g, m[:QB], st)
            st = jnp.where(lane == 2 * g + 1, m[QB:], st)
            st = jnp.where(lane == STAT_L + 2 * g, l[:QB], st)
            st = jnp.where(lane == STAT_L + 2 * g + 1, l[QB:], st)
        st_ref[0, 0, rows, :] = st


def _dilated16(q16, k16, v16):
    b, ncls, nj, w = q16.shape
    blk = lambda width: pl.BlockSpec((1, 1, nj, width), lambda bi, c: (bi, c, 0, 0))
    return pl.pallas_call(
        _dilated16_kernel,
        grid=(b, ncls),
        in_specs=[blk(w), blk(w), blk(w)],
        out_specs=[blk(w), blk(LANES)],
        out_shape=[jax.ShapeDtypeStruct((b, ncls, nj, w), F32),
                   jax.ShapeDtypeStruct((b, ncls, nj, LANES), F32)],
        compiler_params=pltpu.CompilerParams(dimension_semantics=("arbitrary",) * 2,
                                             vmem_limit_bytes=VMEM_LIMIT),
        name="dilated16",
    )(q16, k16, v16)


BACK = 512
KWIN = BACK + QB


def _prompt_attn_kernel(*refs, tq, nblk):
    q_ref = refs[0]
    k_refs = refs[1:1 + nblk]
    v_refs = refs[1 + nblk:1 + 2 * nblk]
    o16_ref, st_ref, ga_ref, mix_ref, x_ref, wout_ref, npost_ref, y_ref, mixa_ref, nat_ref = refs[1 + 2 * nblk:]
    i = pl.program_id(1)

    ngrp = A_WIDTH // LANES
    for cl in range(DIL16):
        dst = pl.ds(cl, tq // DIL16, stride=DIL16)
        for g in range(ngrp):
            nat_ref[g, dst, :] = o16_ref[0, cl, :, g * LANES:(g + 1) * LANES]
        nat_ref[ngrp, dst, :] = st_ref[0, cl]

    a = lax.broadcasted_iota(jnp.int32, (QB, KWIN), 0)
    c = lax.broadcasted_iota(jnp.int32, (QB, KWIN), 1)
    d = BACK + a - c
    near = (d >= 0) & (d <= QB)
    far = (d >= 0) & (d <= BACK) & ((d & 3) == 0)
    bias0 = jnp.where(near & far, LN2, jnp.where(near | far, 0.0, NEG))
    lane = lax.broadcasted_iota(jnp.int32, (QB, LANES), 1)
    head_a = lane < HEAD_DIM

    for sub in range(tq // QB):
        rows = slice(sub * QB, (sub + 1) * QB)
        first_valid = BACK - i * tq - sub * QB
        bias = jnp.where(c >= first_valid, bias0, NEG)
        bias2 = jnp.concatenate([bias, bias], axis=0)
        pieces = []
        w = sub * QB
        while w < sub * QB + KWIN:
            blk, off = divmod(w, tq)
            n = min(tq - off, sub * QB + KWIN - w)
            pieces.append((blk, off, n))
            w += n
        st = nat_ref[ngrp, rows, :]
        for g in range(ngrp):
            cols = slice(g * LANES, (g + 1) * LANES)
            q2 = _pair_split(q_ref[0, rows, cols].astype(F32)).astype(BF16)
            s = jnp.concatenate([_dot_t(q2, k_refs[blk][0, off:off + n, cols]) for blk, off, n in pieces],
                                axis=1) + bias2
            m = jnp.max(s, axis=-1, keepdims=True)
            p = jnp.exp(s - m)
            l = jnp.sum(p, axis=-1, keepdims=True)
            pb = p.astype(BF16)
            o = None
            col = 0
            for blk, off, n in pieces:
                part = _dot(pb[:, col:col + n], v_refs[blk][0, off:off + n, cols])
                o = part if o is None else o + part
                col += n
            o_n = _pair_join(o)
            m_n = jnp.where(head_a, m[:QB], m[QB:])
            l_n = jnp.where(head_a, l[:QB], l[QB:])
            m_f = jnp.where(head_a, st[:, 2 * g:2 * g + 1], st[:, 2 * g + 1:2 * g + 2])
            l_f = jnp.where(head_a, st[:, STAT_L + 2 * g:STAT_L + 2 * g + 1],
                            st[:, STAT_L + 2 * g + 1:STAT_L + 2 * g + 2])
            mx = jnp.maximum(m_n, m_f)
            w_n = jnp.exp(m_n - mx)
            w_f = jnp.exp(m_f - mx)
            oa = (o_n * w_n + nat_ref[g, rows, :] * w_f) / (l_n * w_n + l_f * w_f)
            mixa_ref[rows, cols] = (oa * ga_ref[0, rows, cols]).astype(BF16)

    z = _dot(mixa_ref[...], wout_ref[0:A_WIDTH, :]) + _dot(mix_ref[0], wout_ref[A_WIDTH:, :])
    y_ref[0] = x_ref[0] + _rmsnorm(z, npost_ref[...])


def _prompt_attn(q, k, v, o16, st16, ga, mix, x, wout_bf, npost, tq):
    b, s, _ = x.shape
    nblk = BACK // tq + 1
    row = lambda w: pl.BlockSpec((1, tq, w), lambda bi, i: (bi, i, 0))
    back = lambda j: pl.BlockSpec((1, tq, A_WIDTH), lambda bi, i: (bi, jnp.maximum(i - (nblk - 1 - j), 0), 0))
    full = lambda a: pl.BlockSpec(a.shape, lambda bi, i: (0,) * a.ndim)
    kv_specs = [back(j) for j in range(nblk)]
    cls = lambda w: pl.BlockSpec((1, DIL16, tq // DIL16, w), lambda bi, i: (bi, 0, i, 0))
    return pl.pallas_call(
        functools.partial(_prompt_attn_kernel, tq=tq, nblk=nblk),
        grid=(b, s // tq),
        in_specs=[row(A_WIDTH)] + kv_specs + kv_specs
        + [cls(A_WIDTH), cls(LANES), row(A_WIDTH), row(C_WIDTH + M_WIDTH), row(D_MODEL), full(wout_bf),
           full(npost)],
        out_specs=row(D_MODEL),
        out_shape=jax.ShapeDtypeStruct((b, s, D_MODEL), F32),
        scratch_shapes=[pltpu.VMEM((tq, A_WIDTH), BF16),
                        pltpu.VMEM((A_WIDTH // LANES + 1, tq, LANES), F32)],
        compiler_params=pltpu.CompilerParams(dimension_semantics=("arbitrary", "arbitrary"),
                                             vmem_limit_bytes=VMEM_LIMIT),
        name="prompt_attn",
    )(q, *([k] * nblk), *([v] * nblk), o16, st16, ga, mix, x, wout_bf, npost)


def _sample_proj_kernel(x_ref, npre_ref, win_ref, cos_ref, s1_ref, s2_ref,
                        q_ref, k_ref, v_ref, ga_ref, u_ref, gb_ref, qm_ref, gm_ref):
    h = _rmsnorm(x_ref[...], npre_ref[...]).astype(BF16)
    proj = _dot(h, win_ref[...])
    cos, s1, s2 = cos_ref[...], s1_ref[...], s2_ref[...]
    q_ref[...] = _rope(proj[:, O_QA:O_KA], cos, s1, s2) * SCALE
    k_ref[...] = _rope(proj[:, O_KA:O_VA], cos, s1, s2)
    v_ref[...] = proj[:, O_VA:O_GA]
    ga_ref[...] = _silu(proj[:, O_GA:O_AB])
    u_ref[...] = proj[:, O_AB:O_BB] * _sigmoid(proj[:, O_BB:O_GB])
    gb_ref[...] = _silu(proj[:, O_GB:O_QM])
    qm_ref[...] = proj[:, O_QM:O_GM]
    gm_ref[...] = _silu(proj[:, O_GM:D_IN])


def _sample_proj(x, npre, win_bf, cos, s1, s2):
    n = x.shape[0]
    full = lambda a: pl.BlockSpec(a.shape, lambda i: (0,) * a.ndim)
    outw = [A_WIDTH, A_WIDTH, A_WIDTH, A_WIDTH, C_WIDTH, C_WIDTH, M_WIDTH, M_WIDTH]
    return pl.pallas_call(
        _sample_proj_kernel,
        grid=(1,),
        in_specs=[full(x), full(npre), full(win_bf), full(cos), full(s1), full(s2)],
        out_specs=[pl.BlockSpec((n, w), lambda i: (0, 0)) for w in outw],
        out_shape=[jax.ShapeDtypeStruct((n, w), F32) for w in outw],
        compiler_params=pltpu.CompilerParams(vmem_limit_bytes=VMEM_LIMIT),
        name="sample_proj",
    )(x, npre, win_bf, cos, s1, s2)


STATE_PAD = 2
def _sample_mix_kernel(q_ref, knt_ref, vnt_ref, u_ref, qm_ref, ck_ref, cv_ref, st_ref, cmk_ref, cmv_ref, wdw_ref,
                       oa_ref, om_ref, cv_out_ref, wk_ref, wv_ref, ns_ref, uc_ref, *, t_s, win_len):
    knt, vnt = knt_ref[0], vnt_ref[0]
    new0 = LANES - t_s

    keep = lax.broadcasted_iota(jnp.int32, (A_WIDTH, LANES), 1) < new0
    nch = win_len // LANES
    for src_ref, new, dst_ref in ((ck_ref, knt, wk_ref), (cv_ref, vnt, wv_ref)):
        nxt = pltpu.roll(src_ref[0, :, 0:LANES], new0, 1)
        for c in range(nch):
            cur = nxt
            nxt = pltpu.roll(src_ref[0, :, (c + 1) * LANES:(c + 2) * LANES], new0, 1) if c + 1 < nch else new
            dst_ref[0, :, c * LANES:(c + 1) * LANES] = jnp.where(keep, cur, nxt)

    q = q_ref[...]
    head_of_lane = lax.broadcasted_iota(jnp.int32, (t_s, A_WIDTH), 1) >> HEAD_SHIFT
    q6 = jnp.concatenate([jnp.where(head_of_lane == h, q, 0.0) for h in range(A_HEADS)], axis=0).astype(BF16)
    s_c = _dot(q6, ck_ref[0].astype(BF16))
    s_n = _dot(q6, knt.astype(BF16))

    def weights(shape, key0, lo):
        tq = lax.broadcasted_iota(jnp.int32, shape, 0) & (t_s - 1)
        key = lax.broadcasted_iota(jnp.int32, shape, 1) + key0
        d = win_len + tq - key
        ok = (d >= 0) & (key >= lo)
        w = ((d <= 128).astype(F32) + ((d <= 512) & ((d & 3) == 0)).astype(F32)
             + ((d <= 2048) & ((d & 15) == 0)).astype(F32))
        return jnp.where(ok, w, 0.0)

    w_c = weights(s_c.shape, 0, 0)
    w_n = weights(s_n.shape, win_len - new0, win_len)
    s_c = jnp.where(w_c > 0.0, s_c, NEG)
    s_n = jnp.where(w_n > 0.0, s_n, NEG)
    m = jnp.maximum(jnp.max(s_c, axis=-1, keepdims=True), jnp.max(s_n, axis=-1, keepdims=True))
    p_c = jnp.exp(s_c - m) * w_c
    p_n = jnp.exp(s_n - m) * w_n
    l = jnp.sum(p_c, axis=-1, keepdims=True) + jnp.sum(p_n, axis=-1, keepdims=True)
    o = (_dot_t(p_c.astype(BF16), cv_ref[0].astype(BF16)) + _dot_t(p_n.astype(BF16), vnt.astype(BF16))) * (1.0 / l)
    oa = jnp.zeros((t_s, A_WIDTH), F32)
    for h in range(A_HEADS):
        oa = oa + jnp.where(head_of_lane == h, o[h * t_s:(h + 1) * t_s], 0.0)
    oa_ref[...] = oa

    qm = qm_ref[...] * SCALE
    mhead = lax.broadcasted_iota(jnp.int32, (t_s, M_WIDTH), 1) >> HEAD_SHIFT
    q4 = jnp.concatenate([jnp.where(mhead == h, qm, 0.0) for h in range(M_HEADS)], axis=0).astype(BF16)
    sm = _dot(q4, cmk_ref[0].astype(BF16))
    mm = jnp.max(sm, axis=-1, keepdims=True)
    pm = jnp.exp(sm - mm)
    lm = jnp.sum(pm, axis=-1, keepdims=True)
    om4 = _dot_t(pm.astype(BF16), cmv_ref[0].astype(BF16)) * (1.0 / lm)
    om = jnp.zeros((t_s, M_WIDTH), F32)
    for h in range(M_HEADS):
        om = om + jnp.where(mhead == h, om4[h * t_s:(h + 1) * t_s], 0.0)
    om_ref[...] = om

    nst = st_ref.shape[1]
    uc_ref[0:nst, :] = st_ref[0]
    uc_ref[nst:nst + t_s, :] = u_ref[...]
    acc = jnp.zeros((t_s, C_WIDTH), F32)
    for w in range(CONV_W):
        acc = acc + uc_ref[STATE_PAD + w:STATE_PAD + w + t_s, :] * wdw_ref[w:w + 1, :]
    cv_out_ref[...] = acc
    ns_ref[0] = uc_ref[STATE_PAD + t_s:STATE_PAD + t_s + CONV_W - 1, :]


def _sample_mix(q, knt, vnt, u, qm, ck, cv, st_pad, cmk, cmv, wdw, t_s):
    nb, _, win_len = ck.shape
    rowb = lambda w: pl.BlockSpec((t_s, w), lambda i: (i, 0))
    per = lambda a: pl.BlockSpec((1,) + a.shape[1:], lambda i: (i, 0, 0))
    n = nb * t_s
    return pl.pallas_call(
        functools.partial(_sample_mix_kernel, t_s=t_s, win_len=win_len),
        grid=(nb,),
        in_specs=[rowb(A_WIDTH), per(knt), per(vnt), rowb(C_WIDTH), rowb(M_WIDTH),
                  per(ck), per(cv), per(st_pad), per(cmk), per(cmv), pl.BlockSpec(wdw.shape, lambda i: (0, 0))],
        out_specs=[rowb(A_WIDTH), rowb(M_WIDTH), rowb(C_WIDTH), per(ck), per(cv),
                   pl.BlockSpec((1, CONV_W - 1, C_WIDTH), lambda i: (i, 0, 0))],
        out_shape=[jax.ShapeDtypeStruct((n, A_WIDTH), F32), jax.ShapeDtypeStruct((n, M_WIDTH), F32),
                   jax.ShapeDtypeStruct((n, C_WIDTH), F32),
                   jax.ShapeDtypeStruct(ck.shape, F32), jax.ShapeDtypeStruct(cv.shape, F32),
                   jax.ShapeDtypeStruct((nb, CONV_W - 1, C_WIDTH), F32)],
        scratch_shapes=[pltpu.VMEM((STATE_PAD + CONV_W - 1 + t_s, C_WIDTH), F32)],
        compiler_params=pltpu.CompilerParams(dimension_semantics=("arbitrary",), vmem_limit_bytes=VMEM_LIMIT),
        name="sample_mix",
    )(q, knt, vnt, u, qm, ck, cv, st_pad, cmk, cmv, wdw)


def _sample_out_kernel(x_ref, oa_ref, ga_ref, c_ref, gb_ref, om_ref, gm_ref, bdw_ref, lng_ref, lnb_ref, wpw_ref,
                       bpw_ref, wout_ref, npost_ref, y_ref):
    mixed_a = (oa_ref[...] * ga_ref[...]).astype(BF16)
    mixed_b = _conformer_tail(c_ref[...], gb_ref[...], bdw_ref[...], lng_ref[...], lnb_ref[...], wpw_ref,
                              bpw_ref[...]).astype(BF16)
    mixed_m = (om_ref[...] * gm_ref[...]).astype(BF16)
    z = (_dot(mixed_a, wout_ref[0:A_WIDTH, :]) + _dot(mixed_b, wout_ref[A_WIDTH:A_WIDTH + C_WIDTH, :])
         + _dot(mixed_m, wout_ref[A_WIDTH + C_WIDTH:, :]))
    y_ref[...] = x_ref[...] + _rmsnorm(z, npost_ref[...])


def _sample_out(x, oa, ga, c, gb, om, gm, bdw, lng, lnb, wpw_bf, bpw, wout_bf, npost):
    args = (x, oa, ga, c, gb, om, gm, bdw, lng, lnb, wpw_bf, bpw, wout_bf, npost)
    return pl.pallas_call(
        _sample_out_kernel,
        grid=(1,),
        in_specs=[pl.BlockSpec(a.shape, lambda i: (0, 0)) for a in args],
        out_specs=pl.BlockSpec(x.shape, lambda i: (0, 0)),
        out_shape=jax.ShapeDtypeStruct(x.shape, F32),
        compiler_params=pltpu.CompilerParams(vmem_limit_bytes=VMEM_LIMIT),
        name="sample_out",
    )(*args)


def _rope_tables(pos):
    half = ROT_DIM // 2
    inv = ROPE_THETA ** (-jnp.arange(0, ROT_DIM, 2, dtype=F32) / ROT_DIM)
    lane = jnp.arange(LANES, dtype=jnp.int32)
    in_head = lane % HEAD_DIM
    inv_lane = jnp.tile(inv, LANES // half)
    ang = pos.astype(F32)[:, None] * inv_lane[None, :]
    cos, sin = jnp.cos(ang), jnp.sin(ang)
    rot = (in_head < ROT_DIM)[None, :]
    lo = (in_head < half)[None, :]
    c = jnp.where(rot, cos, 1.0)
    s1 = jnp.where(lo, -sin, 0.0)
    s2 = jnp.where(rot & ~lo, sin, 0.0)
    return c, s1, s2


def _new_rows_t(a, nb, t_s):
    at = jnp.transpose(a.reshape(nb, t_s, a.shape[1]), (0, 2, 1))
    return jnp.pad(at, ((0, 0), (0, 0), (LANES - t_s, 0)))


def _feature_major(cache):
    nb, rows, heads, dim = cache.shape
    return jnp.transpose(cache, (0, 2, 3, 1)).reshape(nb, heads * dim, rows)


def _row_major(cache_t, heads):
    nb, width, rows = cache_t.shape
    return jnp.transpose(cache_t.reshape(nb, heads, width // heads, rows), (0, 3, 1, 2))


PROJ_TILE = 256
ATTN_TILE = 256


def kernel(x_prompt, x_sample, cache_win_k, cache_win_v, state_conv, cache_mem_k, cache_mem_v, mem_prompt,
           norm_pre, norm_post, w_in, w_out, norm_mem, w_mem_kv, w_dw, b_dw, ln_conv_g, ln_conv_b, w_pw2, b_pw2):
    depth = w_in.shape[0]
    assert depth == 1, "single-layer step"
    b, s, _ = x_prompt.shape
    nb, t_s, _ = x_sample.shape
    win_len = cache_win_k.shape[2]
    assert s % (DIL16 * QB) == 0 and s % PROJ_TILE == 0 and s % ATTN_TILE == 0 and BACK % ATTN_TILE == 0
    assert win_len == MAX_WINDOW and win_len % LANES == 0
    assert t_s % 8 == 0 and t_s & (t_s - 1) == 0 and t_s < LANES
    l = 0
    row = lambda a: a[l][None, :]
    npre, npost, nmem = row(norm_pre), row(norm_post), row(norm_mem)
    bdw, lng, lnb, bpw = row(b_dw), row(ln_conv_g), row(ln_conv_b), row(b_pw2)
    win_bf, wout_bf = w_in[l].astype(BF16), w_out[l].astype(BF16)
    wmem_bf, wpw_bf = w_mem_kv[l].astype(BF16), w_pw2[l].astype(BF16)
    wdw = w_dw[l]

    mk, mv, mkb, mvb = _mem_kv(mem_prompt, nmem, wmem_bf)
    cos, s1, s2 = _rope_tables(jnp.arange(s, dtype=jnp.int32))
    q, k, v, q16, k16, v16, kf, vf, ga, mix, ust = _prompt_proj(x_prompt, npre, win_bf, cos, s1, s2, mkb, mvb,
                                                                wdw, bdw, lng, lnb, wpw_bf, bpw, PROJ_TILE)
    o16, st16 = _dilated16(q16, k16, v16)
    y_prompt = _prompt_attn(q, k, v, o16, st16, ga, mix, x_prompt, wout_bf, npost, ATTN_TILE)

    pos_s = PAST_LEN + jnp.arange(t_s, dtype=jnp.int32)
    cs, s1s, s2s = (jnp.tile(a, (nb, 1)) for a in _rope_tables(pos_s))
    xs = x_sample.reshape(nb * t_s, D_MODEL)
    qs, ks, vs, gas, us, gbs, qms, gms = _sample_proj(xs, npre, win_bf, cs, s1s, s2s)
    st_pad = jnp.pad(state_conv[l], ((0, 0), (STATE_PAD, 0), (0, 0)))
    oas, oms, cs_conv, wk, wv, nst = _sample_mix(
        qs, _new_rows_t(ks, nb, t_s), _new_rows_t(vs, nb, t_s), us, qms,
        _feature_major(cache_win_k[l]), _feature_major(cache_win_v[l]), st_pad,
        _feature_major(cache_mem_k[l]), _feature_major(cache_mem_v[l]), wdw, t_s)
    y_sample = _sample_out(xs, oas, gas, cs_conv, gbs, oms, gms, bdw, lng, lnb, wpw_bf, bpw, wout_bf, npost)

    keep_p = kf.shape[1]
    return (y_prompt,
            y_sample.reshape(nb, t_s, D_MODEL),
            kf.reshape(1, b, keep_p, A_HEADS, HEAD_DIM),
            vf.reshape(1, b, keep_p, A_HEADS, HEAD_DIM),
            ust[:, CONV_HIST - (CONV_W - 1):, :][None],
            mk.reshape(1, b, N_MEM, M_HEADS, HEAD_DIM),
            mv.reshape(1, b, N_MEM, M_HEADS, HEAD_DIM),
            _row_major(wk, A_HEADS)[None],
            _row_major(wv, A_HEADS)[None],
            nst[None])
```

```python
import functools
import math

import jax
import jax.numpy as jnp
from jax import lax
from jax.experimental import pallas as pl
from jax.experimental.pallas import tpu as pltpu

F32 = jnp.float32
BF16 = jnp.bfloat16

D_MODEL = 1024
HEAD_DIM = 64
HEAD_SHIFT = 6
A_WIDTH = 384
A_HEADS = 6
M_WIDTH = 256
M_HEADS = 4
C_WIDTH = 384
ROT_DIM = 16
ROPE_THETA = 500000.0
CONV_W = 31
N_MEM = 256
MAX_WINDOW = 2048
PAST_LEN = 16384
EPS = 1e-6
SCALE = HEAD_DIM ** -0.5
NEG = -1e30
LN2 = math.log(2.0)

LANES = 128
NGRP = A_WIDTH // LANES
QB = 128
DIL16 = 16
O_QA, O_KA, O_VA, O_GA, O_AB, O_BB, O_GB, O_QM, O_GM, D_IN = 0, 384, 768, 1152, 1536, 1920, 2304, 2688, 2944, 3200
P_K, P_V, P_Q, P_W = 0, A_WIDTH, 2 * A_WIDTH, 3 * A_WIDTH

VMEM_LIMIT = 60 * 1024 * 1024


def _sigmoid(x):
    return 1.0 / (1.0 + jnp.exp(-x))


def _silu(x):
    return x * _sigmoid(x)


def _rmsnorm(x, g):
    return x * lax.rsqrt(jnp.mean(x * x, axis=-1, keepdims=True) + EPS) * g


def _dot(a, b):
    return jnp.dot(a, b, preferred_element_type=F32)


def _dot_t(a, b):
    return lax.dot_general(a, b, (((1,), (1,)), ((), ())), preferred_element_type=F32)


def _rope(xw, tab_ref):
    cos, s1, s2 = tab_ref[:, 0:LANES], tab_ref[:, LANES:2 * LANES], tab_ref[:, 2 * LANES:3 * LANES]
    outs = []
    for g in range(xw.shape[1] // LANES):
        xg = xw[:, g * LANES:(g + 1) * LANES]
        outs.append(xg * cos + pltpu.roll(xg, LANES - 8, 1) * s1 + pltpu.roll(xg, 8, 1) * s2)
    return jnp.concatenate(outs, axis=1)


def _pair_split(q):
    lane = lax.broadcasted_iota(jnp.int32, q.shape, 1)
    qa = jnp.where(lane < HEAD_DIM, q, 0.0)
    qb = jnp.where(lane >= HEAD_DIM, q, 0.0)
    return jnp.concatenate([qa, qb], axis=0)


def _pair_join(x):
    t = x.shape[0] // 2
    lane = lax.broadcasted_iota(jnp.int32, (t, LANES), 1)
    return jnp.where(lane < HEAD_DIM, x[:t], x[t:])


def _mem_attend(qm, mk_ref, mv_ref):
    outs = []
    for g in range(M_WIDTH // LANES):
        cols = slice(g * LANES, (g + 1) * LANES)
        q2 = _pair_split(qm[:, cols] * SCALE).astype(BF16)
        s = _dot_t(q2, mk_ref[0, :, cols])
        m = jnp.max(s, axis=-1, keepdims=True)
        p = jnp.exp(s - m)
        l = jnp.sum(p, axis=-1, keepdims=True)
        o = _dot(p.astype(BF16), mv_ref[0, :, cols]) * (1.0 / l)
        outs.append(_pair_join(o))
    return jnp.concatenate(outs, axis=1)


def _conformer_tail(c, gate_b, bdw, lng, lnb, wpw_ref, bpw):
    cf = c + bdw
    mu = jnp.mean(cf, axis=-1, keepdims=True)
    dev = cf - mu
    var = jnp.mean(dev * dev, axis=-1, keepdims=True)
    cn = dev * lax.rsqrt(var + EPS) * lng + lnb
    ob = _dot(_silu(cn).astype(BF16), wpw_ref[...]) + bpw
    return ob * gate_b


def _full(a):
    return pl.BlockSpec(a.shape, lambda *_: (0,) * a.ndim)


def _mem_kv_kernel(mem_ref, g_ref, w_ref, mk_ref, mv_ref, mkb_ref, mvb_ref):
    h = _rmsnorm(mem_ref[0], g_ref[...]).astype(BF16)
    kv = _dot(h, w_ref[...])
    mk_ref[0] = kv[:, :M_WIDTH]
    mv_ref[0] = kv[:, M_WIDTH:]
    mkb_ref[0] = kv[:, :M_WIDTH].astype(BF16)
    mvb_ref[0] = kv[:, M_WIDTH:].astype(BF16)


def _mem_kv(mem, g, w_bf):
    b = mem.shape[0]
    blk = pl.BlockSpec((1, N_MEM, M_WIDTH), lambda i: (i, 0, 0))
    return pl.pallas_call(
        _mem_kv_kernel,
        grid=(b,),
        in_specs=[pl.BlockSpec((1, N_MEM, D_MODEL), lambda i: (i, 0, 0)), _full(g), _full(w_bf)],
        out_specs=[blk, blk, blk, blk],
        out_shape=[jax.ShapeDtypeStruct((b, N_MEM, M_WIDTH), F32)] * 2
        + [jax.ShapeDtypeStruct((b, N_MEM, M_WIDTH), BF16)] * 2,
        name="mem_kv",
    )(mem, g, w_bf)


CONV_HIST = 32
CONV_CHUNK = 64


def _prompt_proj_kernel(x_ref, npre_ref, win_ref, tab_ref, mk_ref, mv_ref,
                        wdw_ref, bdw_ref, lng_ref, lnb_ref, wpw_ref, bpw_ref,
                        kvq_ref, kvq16_ref, kvf_ref, ga_ref, mix_ref, ust_ref,
                        uext_ref, ush_ref, conv_ref, stage_ref):
    t = x_ref.shape[1]
    i = pl.program_id(1)
    h = _rmsnorm(x_ref[0], npre_ref[...]).astype(BF16)
    proj = _dot(h, win_ref[...])

    q = _rope(proj[:, O_QA:O_KA], tab_ref) * SCALE
    k = _rope(proj[:, O_KA:O_VA], tab_ref)
    v = proj[:, O_VA:O_GA]
    kvq_ref[0, :, P_K:P_V] = k.astype(BF16)
    kvq_ref[0, :, P_V:P_Q] = v.astype(BF16)
    kvq_ref[0, :, P_Q:P_W] = q.astype(BF16)
    kvf_ref[0, :, 0:A_WIDTH] = k
    kvf_ref[0, :, A_WIDTH:] = v
    ga_ref[0] = _silu(proj[:, O_GA:O_AB])

    for idx, val in enumerate((k, v, q)):
        for g in range(NGRP):
            stage_ref[NGRP * idx + g] = val[:, g * LANES:(g + 1) * LANES]
    for c in range(DIL16):
        rows_c = [stage_ref[s, pl.ds(c, t // DIL16, stride=DIL16), :] for s in range(3 * NGRP)]
        kvq16_ref[0, c] = jnp.concatenate(rows_c, axis=1).astype(BF16)

    u = proj[:, O_AB:O_BB] * _sigmoid(proj[:, O_BB:O_GB])

    @pl.when(i == 0)
    def _():
        uext_ref[0:CONV_HIST, :] = jnp.zeros((CONV_HIST, C_WIDTH), F32)

    uext_ref[CONV_HIST:CONV_HIST + t, :] = u
    for r in range(1, 8):
        ush_ref[r - 1, 8:CONV_HIST + t, :] = uext_ref[8 - r:CONV_HIST + t - r, :]
    for r0 in range(0, t, CONV_CHUNK):
        acc = jnp.zeros((CONV_CHUNK, C_WIDTH), F32)
        for kk in range(CONV_W):
            tap = wdw_ref[CONV_W - 1 - kk:CONV_W - kk, :]
            r, base = kk % 8, CONV_HIST + r0 - (kk - kk % 8)
            if r == 0:
                slab = uext_ref[base:base + CONV_CHUNK, :]
            else:
                slab = ush_ref[r - 1, base:base + CONV_CHUNK, :]
            acc = acc + slab * tap
        conv_ref[r0:r0 + CONV_CHUNK, :] = acc
    tail = uext_ref[t:t + CONV_HIST, :]
    uext_ref[0:CONV_HIST, :] = tail
    ust_ref[0] = tail

    mixed_b = _conformer_tail(conv_ref[...], _silu(proj[:, O_GB:O_QM]), bdw_ref[...], lng_ref[...], lnb_ref[...],
                              wpw_ref, bpw_ref[...])
    mixed_m = _mem_attend(proj[:, O_QM:O_GM], mk_ref, mv_ref) * _silu(proj[:, O_GM:D_IN])
    mix_ref[0, :, 0:C_WIDTH] = mixed_b.astype(BF16)
    mix_ref[0, :, C_WIDTH:] = mixed_m.astype(BF16)


def _prompt_proj(x, npre, win_bf, tab, mkb, mvb, wdw, bdw, lng, lnb, wpw_bf, bpw, tile):
    b, s, _ = x.shape
    nt = s // tile
    keep = min(MAX_WINDOW, s)
    first_keep = (s - keep) // tile
    row = lambda w: pl.BlockSpec((1, tile, w), lambda bi, i: (bi, i, 0))
    memb = pl.BlockSpec((1, N_MEM, M_WIDTH), lambda bi, i: (bi, 0, 0))
    keepb = pl.BlockSpec((1, tile, 2 * A_WIDTH), lambda bi, i: (bi, jnp.maximum(i - first_keep, 0), 0))
    cls = pl.BlockSpec((1, DIL16, tile // DIL16, P_W), lambda bi, i: (bi, 0, i, 0))
    return pl.pallas_call(
        _prompt_proj_kernel,
        grid=(b, nt),
        in_specs=[row(D_MODEL), _full(npre), _full(win_bf), pl.BlockSpec((tile, 3 * LANES), lambda bi, i: (i, 0)),
                  memb, memb, _full(wdw), _full(bdw), _full(lng), _full(lnb), _full(wpw_bf), _full(bpw)],
        out_specs=[row(P_W), cls, keepb, row(A_WIDTH), row(C_WIDTH + M_WIDTH),
                   pl.BlockSpec((1, CONV_HIST, C_WIDTH), lambda bi, i: (bi, 0, 0))],
        out_shape=[jax.ShapeDtypeStruct((b, s, P_W), BF16),
                   jax.ShapeDtypeStruct((b, DIL16, s // DIL16, P_W), BF16),
                   jax.ShapeDtypeStruct((b, keep, 2 * A_WIDTH), F32),
                   jax.ShapeDtypeStruct((b, s, A_WIDTH), F32),
                   jax.ShapeDtypeStruct((b, s, C_WIDTH + M_WIDTH), BF16),
                   jax.ShapeDtypeStruct((b, CONV_HIST, C_WIDTH), F32)],
        scratch_shapes=[pltpu.VMEM((CONV_HIST + tile, C_WIDTH), F32),
                        pltpu.VMEM((7, CONV_HIST + tile, C_WIDTH), F32),
                        pltpu.VMEM((tile, C_WIDTH), F32),
                        pltpu.VMEM((3 * NGRP, tile, LANES), F32)],
        compiler_params=pltpu.CompilerParams(dimension_semantics=("arbitrary", "arbitrary"),
                                             vmem_limit_bytes=VMEM_LIMIT),
        name="prompt_proj",
    )(x, npre, win_bf, tab, mkb, mvb, wdw, bdw, lng, lnb, wpw_bf, bpw)


STAT_L = 8
O16_W = A_WIDTH + LANES


def _dilated16_kernel(kvq_ref, o_ref):
    a = lax.broadcasted_iota(jnp.int32, (QB, QB), 0)
    c = lax.broadcasted_iota(jnp.int32, (QB, QB), 1)
    tri_prev = jnp.where(c >= a, 0.0, NEG)
    tri_cur = jnp.where(c <= a, 0.0, NEG)
    bias_cur = jnp.concatenate([tri_cur, tri_cur], axis=0)
    both = jnp.concatenate([tri_prev, tri_cur], axis=1)
    bias_both = jnp.concatenate([both, both], axis=0)
    lane = lax.broadcasted_iota(jnp.int32, (QB, LANES), 1)
    for sub in range(kvq_ref.shape[2] // QB):
        rows = slice(sub * QB, (sub + 1) * QB)
        prev = slice((sub - 1) * QB, sub * QB)
        st = jnp.zeros((QB, LANES), F32)
        for g in range(NGRP):
            kc = slice(P_K + g * LANES, P_K + (g + 1) * LANES)
            vc = slice(P_V + g * LANES, P_V + (g + 1) * LANES)
            qc = slice(P_Q + g * LANES, P_Q + (g + 1) * LANES)
            q2 = _pair_split(kvq_ref[0, 0, rows, qc].astype(F32)).astype(BF16)
            s_cur = _dot_t(q2, kvq_ref[0, 0, rows, kc])
            if sub == 0:
                s = s_cur + bias_cur
            else:
                s = jnp.concatenate([_dot_t(q2, kvq_ref[0, 0, prev, kc]), s_cur], axis=1) + bias_both
            m = jnp.max(s, axis=-1, keepdims=True)
            p = jnp.exp(s - m)
            l = jnp.sum(p, axis=-1, keepdims=True)
            pb = p.astype(BF16)
            if sub == 0:
                o = _dot(pb, kvq_ref[0, 0, rows, vc])
            else:
                o = _dot(pb[:, :QB], kvq_ref[0, 0, prev, vc]) + _dot(pb[:, QB:], kvq_ref[0, 0, rows, vc])
            o_ref[0, 0, rows, g * LANES:(g + 1) * LANES] = _pair_join(o)
            st = jnp.where(lane == 2 * g, m[:QB], st)
            st = jnp.where(lane == 2 * g + 1, m[QB:], st)
            st = jnp.where(lane == STAT_L + 2 * g, l[:QB], st)
            st = jnp.where(lane == STAT_L + 2 * g + 1, l[QB:], st)
        o_ref[0, 0, rows, A_WIDTH:] = st


def _dilated16(kvq16):
    b, ncls, nj, w = kvq16.shape
    return pl.pallas_call(
        _dilated16_kernel,
        grid=(b, ncls),
        in_specs=[pl.BlockSpec((1, 1, nj, w), lambda bi, c: (bi, c, 0, 0))],
        out_specs=pl.BlockSpec((1, 1, nj, O16_W), lambda bi, c: (bi, c, 0, 0)),
        out_shape=jax.ShapeDtypeStruct((b, ncls, nj, O16_W), F32),
        compiler_params=pltpu.CompilerParams(dimension_semantics=("arbitrary",) * 2,
                                             vmem_limit_bytes=VMEM_LIMIT),
        name="dilated16",
    )(kvq16)


STATE_PAD = 2
S_Q, S_U, S_QM, S_ROW = 0, A_WIDTH, A_WIDTH + C_WIDTH, A_WIDTH + C_WIDTH + M_WIDTH


def _sample_mix_body(srow_ref, kvnt_ref, ck_ref, cv_ref, st_ref, cmk_ref, cmv_ref, wdw_ref,
                     smix_ref, wk_ref, wv_ref, ns_ref, uc_ref, *, t_s, win_len):
    knt, vnt = kvnt_ref[0, 0:A_WIDTH, :], kvnt_ref[0, A_WIDTH:, :]
    new0 = LANES - t_s

    keep = lax.broadcasted_iota(jnp.int32, (A_WIDTH, LANES), 1) < new0
    nch = win_len // LANES
    for src_ref, new, dst_ref in ((ck_ref, knt, wk_ref), (cv_ref, vnt, wv_ref)):
        nxt = pltpu.roll(src_ref[0, :, 0:LANES], new0, 1)
        for c in range(nch):
            cur = nxt
            nxt = pltpu.roll(src_ref[0, :, (c + 1) * LANES:(c + 2) * LANES], new0, 1) if c + 1 < nch else new
            dst_ref[0, :, c * LANES:(c + 1) * LANES] = jnp.where(keep, cur, nxt)

    q = srow_ref[:, S_Q:S_U]
    head_of_lane = lax.broadcasted_iota(jnp.int32, (t_s, A_WIDTH), 1) >> HEAD_SHIFT
    q6 = jnp.concatenate([jnp.where(head_of_lane == h, q, 0.0) for h in range(A_HEADS)], axis=0).astype(BF16)
    s_c = _dot(q6, ck_ref[0].astype(BF16))
    s_n = _dot(q6, knt.astype(BF16))

    def weights(shape, key0, lo):
        tq = lax.broadcasted_iota(jnp.int32, shape, 0) & (t_s - 1)
        key = lax.broadcasted_iota(jnp.int32, shape, 1) + key0
        d = win_len + tq - key
        ok = (d >= 0) & (key >= lo)
        w = ((d <= 128).astype(F32) + ((d <= 512) & ((d & 3) == 0)).astype(F32)
             + ((d <= 2048) & ((d & 15) == 0)).astype(F32))
        return jnp.where(ok, w, 0.0)

    w_c = weights(s_c.shape, 0, 0)
    w_n = weights(s_n.shape, win_len - new0, win_len)
    s_c = jnp.where(w_c > 0.0, s_c, NEG)
    s_n = jnp.where(w_n > 0.0, s_n, NEG)
    m = jnp.maximum(jnp.max(s_c, axis=-1, keepdims=True), jnp.max(s_n, axis=-1, keepdims=True))
    p_c = jnp.exp(s_c - m) * w_c
    p_n = jnp.exp(s_n - m) * w_n
    l = jnp.sum(p_c, axis=-1, keepdims=True) + jnp.sum(p_n, axis=-1, keepdims=True)
    o = (_dot_t(p_c.astype(BF16), cv_ref[0].astype(BF16)) + _dot_t(p_n.astype(BF16), vnt.astype(BF16))) * (1.0 / l)
    oa = jnp.zeros((t_s, A_WIDTH), F32)
    for h in range(A_HEADS):
        oa = oa + jnp.where(head_of_lane == h, o[h * t_s:(h + 1) * t_s], 0.0)
    smix_ref[:, S_Q:S_U] = oa

    qm = srow_ref[:, S_QM:S_ROW] * SCALE
    mhead = lax.broadcasted_iota(jnp.int32, (t_s, M_WIDTH), 1) >> HEAD_SHIFT
    q4 = jnp.concatenate([jnp.where(mhead == h, qm, 0.0) for h in range(M_HEADS)], axis=0).astype(BF16)
    sm = _dot(q4, cmk_ref[0].astype(BF16))
    mm = jnp.max(sm, axis=-1, keepdims=True)
    pm = jnp.exp(sm - mm)
    lm = jnp.sum(pm, axis=-1, keepdims=True)
    om4 = _dot_t(pm.astype(BF16), cmv_ref[0].astype(BF16)) * (1.0 / lm)
    om = jnp.zeros((t_s, M_WIDTH), F32)
    for h in range(M_HEADS):
        om = om + jnp.where(mhead == h, om4[h * t_s:(h + 1) * t_s], 0.0)
    smix_ref[:, S_QM:S_ROW] = om

    nst = st_ref.shape[1]
    uc_ref[0:nst, :] = st_ref[0]
    uc_ref[nst:nst + t_s, :] = srow_ref[:, S_U:S_QM]
    acc = jnp.zeros((t_s, C_WIDTH), F32)
    for w in range(CONV_W):
        acc = acc + uc_ref[STATE_PAD + w:STATE_PAD + w + t_s, :] * wdw_ref[w:w + 1, :]
    smix_ref[:, S_U:S_QM] = acc
    ns_ref[0] = uc_ref[STATE_PAD + t_s:STATE_PAD + t_s + CONV_W - 1, :]


BACK = 512
KWIN = BACK + QB


def _prompt_attn_kernel(q_ref, kvp_ref, kvc_ref, o16_ref, ga_ref, mix_ref, x_ref, wout_ref, npost_ref,
                        srow_ref, kvnt_ref, ck_ref, cv_ref, sst_ref, cmk_ref, cmv_ref, wdw_ref,
                        y_ref, smix_ref, wk_ref, wv_ref, ns_ref,
                        mixa_ref, nat_ref, uc_ref, *, tq, t_s, win_len):
    i = pl.program_id(1)
    kv_refs = (kvp_ref, kvc_ref)

    for cl in range(DIL16):
        dst = pl.ds(cl, tq // DIL16, stride=DIL16)
        for g in range(NGRP + 1):
            nat_ref[g, dst, :] = o16_ref[0, cl, :, g * LANES:(g + 1) * LANES]

    a = lax.broadcasted_iota(jnp.int32, (QB, KWIN), 0)
    c = lax.broadcasted_iota(jnp.int32, (QB, KWIN), 1)
    d = BACK + a - c
    near = (d >= 0) & (d <= QB)
    far = (d >= 0) & (d <= BACK) & ((d & 3) == 0)
    bias0 = jnp.where(near & far, LN2, jnp.where(near | far, 0.0, NEG))
    lane = lax.broadcasted_iota(jnp.int32, (QB, LANES), 1)
    head_a = lane < HEAD_DIM

    for sub in range(tq // QB):
        rows = slice(sub * QB, (sub + 1) * QB)
        first_valid = BACK - i * tq - sub * QB
        bias = jnp.where(c >= first_valid, bias0, NEG)
        bias2 = jnp.concatenate([bias, bias], axis=0)
        pieces = []
        w = sub * QB
        while w < sub * QB + KWIN:
            blk, off = divmod(w, tq)
            n = min(tq - off, sub * QB + KWIN - w)
            pieces.append((blk, off, n))
            w += n
        st = nat_ref[NGRP, rows, :]
        for g in range(NGRP):
            cols = slice(g * LANES, (g + 1) * LANES)
            kc = slice(P_K + g * LANES, P_K + (g + 1) * LANES)
            vc = slice(P_V + g * LANES, P_V + (g + 1) * LANES)
            q2 = _pair_split(q_ref[0, rows, cols].astype(F32)).astype(BF16)
            s = jnp.concatenate([_dot_t(q2, kv_refs[blk][0, off:off + n, kc]) for blk, off, n in pieces],
                                axis=1) + bias2
            m = jnp.max(s, axis=-1, keepdims=True)
            p = jnp.exp(s - m)
            l = jnp.sum(p, axis=-1, keepdims=True)
            pb = p.astype(BF16)
            o = None
            col = 0
            for blk, off, n in pieces:
                part = _dot(pb[:, col:col + n], kv_refs[blk][0, off:off + n, vc])
                o = part if o is None else o + part
                col += n
            o_n = _pair_join(o)
            m_n = jnp.where(head_a, m[:QB], m[QB:])
            l_n = jnp.where(head_a, l[:QB], l[QB:])
            m_f = jnp.where(head_a, st[:, 2 * g:2 * g + 1], st[:, 2 * g + 1:2 * g + 2])
            l_f = jnp.where(head_a, st[:, STAT_L + 2 * g:STAT_L + 2 * g + 1],
                            st[:, STAT_L + 2 * g + 1:STAT_L + 2 * g + 2])
            mx = jnp.maximum(m_n, m_f)
            w_n = jnp.exp(m_n - mx)
            w_f = jnp.exp(m_f - mx)
            oa = (o_n * w_n + nat_ref[g, rows, :] * w_f) / (l_n * w_n + l_f * w_f)
            mixa_ref[rows, cols] = (oa * ga_ref[0, rows, cols]).astype(BF16)

    z = _dot(mixa_ref[...], wout_ref[0:A_WIDTH, :]) + _dot(mix_ref[0], wout_ref[A_WIDTH:, :])
    y_ref[0] = x_ref[0] + _rmsnorm(z, npost_ref[...])

    _sample_mix_body(srow_ref, kvnt_ref, ck_ref, cv_ref, sst_ref, cmk_ref, cmv_ref, wdw_ref,
                     smix_ref, wk_ref, wv_ref, ns_ref, uc_ref, t_s=t_s, win_len=win_len)


def _prompt_attn(kvq, o16, ga, mix, x, wout_bf, npost, srow, kvnt, ck, cv, st_pad, cmk, cmv, wdw, tq, t_s):
    b, s, _ = x.shape
    nsteps = s // tq
    nb, _, win_len = ck.shape
    assert tq == BACK and b * nsteps == nb, "one sample batch per grid step"
    row = lambda w: pl.BlockSpec((1, tq, w), lambda bi, i: (bi, i, 0))
    qspec = pl.BlockSpec((1, tq, A_WIDTH), lambda bi, i: (bi, i, P_Q // A_WIDTH))
    kvprev = pl.BlockSpec((1, tq, 2 * A_WIDTH), lambda bi, i: (bi, jnp.maximum(i - 1, 0), 0))
    kvcur = pl.BlockSpec((1, tq, 2 * A_WIDTH), lambda bi, i: (bi, i, 0))
    cls = pl.BlockSpec((1, DIL16, tq // DIL16, O16_W), lambda bi, i: (bi, 0, i, 0))
    srows = pl.BlockSpec((t_s, S_ROW), lambda bi, i: (bi * nsteps + i, 0))
    per = lambda a: pl.BlockSpec((1,) + a.shape[1:], lambda bi, i: (bi * nsteps + i, 0, 0))
    return pl.pallas_call(
        functools.partial(_prompt_attn_kernel, tq=tq, t_s=t_s, win_len=win_len),
        grid=(b, nsteps),
        in_specs=[qspec, kvprev, kvcur, cls, row(A_WIDTH), row(C_WIDTH + M_WIDTH), row(D_MODEL),
                  _full(wout_bf), _full(npost),
                  srows, per(kvnt), per(ck), per(cv), per(st_pad), per(cmk), per(cmv), _full(wdw)],
        out_specs=[row(D_MODEL), srows, per(ck), per(cv),
                   pl.BlockSpec((1, CONV_W - 1, C_WIDTH), lambda bi, i: (bi * nsteps + i, 0, 0))],
        out_shape=[jax.ShapeDtypeStruct((b, s, D_MODEL), F32),
                   jax.ShapeDtypeStruct((nb * t_s, S_ROW), F32),
                   jax.ShapeDtypeStruct(ck.shape, F32), jax.ShapeDtypeStruct(cv.shape, F32),
                   jax.ShapeDtypeStruct((nb, CONV_W - 1, C_WIDTH), F32)],
        scratch_shapes=[pltpu.VMEM((tq, A_WIDTH), BF16),
                        pltpu.VMEM((NGRP + 1, tq, LANES), F32),
                        pltpu.VMEM((STATE_PAD + CONV_W - 1 + t_s, C_WIDTH), F32)],
        compiler_params=pltpu.CompilerParams(dimension_semantics=("arbitrary", "arbitrary"),
                                             vmem_limit_bytes=VMEM_LIMIT),
        name="prompt_attn",
    )(kvq, kvq, kvq, o16, ga, mix, x, wout_bf, npost, srow, kvnt, ck, cv, st_pad, cmk, cmv, wdw)


def _sample_proj_kernel(x_ref, npre_ref, win_ref, tab_ref, srow_ref, kv_ref, gate_ref):
    h = _rmsnorm(x_ref[...], npre_ref[...]).astype(BF16)
    proj = _dot(h, win_ref[...])
    srow_ref[:, S_Q:S_U] = _rope(proj[:, O_QA:O_KA], tab_ref) * SCALE
    srow_ref[:, S_U:S_QM] = proj[:, O_AB:O_BB] * _sigmoid(proj[:, O_BB:O_GB])
    srow_ref[:, S_QM:S_ROW] = proj[:, O_QM:O_GM]
    kv_ref[:, 0:A_WIDTH] = _rope(proj[:, O_KA:O_VA], tab_ref)
    kv_ref[:, A_WIDTH:] = proj[:, O_VA:O_GA]
    gate_ref[:, S_Q:S_U] = _silu(proj[:, O_GA:O_AB])
    gate_ref[:, S_U:S_QM] = _silu(proj[:, O_GB:O_QM])
    gate_ref[:, S_QM:S_ROW] = _silu(proj[:, O_GM:D_IN])


def _sample_proj(x, npre, win_bf, tab):
    n = x.shape[0]
    outw = [S_ROW, 2 * A_WIDTH, S_ROW]
    return pl.pallas_call(
        _sample_proj_kernel,
        grid=(1,),
        in_specs=[_full(x), _full(npre), _full(win_bf), _full(tab)],
        out_specs=[pl.BlockSpec((n, w), lambda i: (0, 0)) for w in outw],
        out_shape=[jax.ShapeDtypeStruct((n, w), F32) for w in outw],
        compiler_params=pltpu.CompilerParams(vmem_limit_bytes=VMEM_LIMIT),
        name="sample_proj",
    )(x, npre, win_bf, tab)


def _sample_out_kernel(x_ref, smix_ref, gate_ref, bdw_ref, lng_ref, lnb_ref, wpw_ref, bpw_ref, wout_ref, npost_ref,
                       y_ref):
    mixed_a = (smix_ref[:, S_Q:S_U] * gate_ref[:, S_Q:S_U]).astype(BF16)
    mixed_b = _conformer_tail(smix_ref[:, S_U:S_QM], gate_ref[:, S_U:S_QM], bdw_ref[...], lng_ref[...], lnb_ref[...],
                              wpw_ref, bpw_ref[...]).astype(BF16)
    mixed_m = (smix_ref[:, S_QM:S_ROW] * gate_ref[:, S_QM:S_ROW]).astype(BF16)
    z = (_dot(mixed_a, wout_ref[0:A_WIDTH, :]) + _dot(mixed_b, wout_ref[A_WIDTH:A_WIDTH + C_WIDTH, :])
         + _dot(mixed_m, wout_ref[A_WIDTH + C_WIDTH:, :]))
    y_ref[...] = x_ref[...] + _rmsnorm(z, npost_ref[...])


def _sample_out(x, smix, gate, bdw, lng, lnb, wpw_bf, bpw, wout_bf, npost):
    args = (x, smix, gate, bdw, lng, lnb, wpw_bf, bpw, wout_bf, npost)
    return pl.pallas_call(
        _sample_out_kernel,
        grid=(1,),
        in_specs=[_full(a) for a in args],
        out_specs=_full(x),
        out_shape=jax.ShapeDtypeStruct(x.shape, F32),
        compiler_params=pltpu.CompilerParams(vmem_limit_bytes=VMEM_LIMIT),
        name="sample_out",
    )(*args)


def _rope_table(pos):
    half = ROT_DIM // 2
    inv = ROPE_THETA ** (-jnp.arange(0, ROT_DIM, 2, dtype=F32) / ROT_DIM)
    lane = jnp.arange(LANES, dtype=jnp.int32)
    in_head = lane % HEAD_DIM
    inv_lane = jnp.tile(inv, LANES // half)
    ang = pos.astype(F32)[:, None] * inv_lane[None, :]
    cos, sin = jnp.cos(ang), jnp.sin(ang)
    rot = (in_head < ROT_DIM)[None, :]
    lo = (in_head < half)[None, :]
    return jnp.concatenate([jnp.where(rot, cos, 1.0), jnp.where(lo, -sin, 0.0), jnp.where(rot & ~lo, sin, 0.0)],
                           axis=1)


def _new_rows_t(a, nb, t_s):
    at = jnp.transpose(a.reshape(nb, t_s, a.shape[1]), (0, 2, 1))
    return jnp.pad(at, ((0, 0), (0, 0), (LANES - t_s, 0)))


def _feature_major(cache):
    nb, rows, heads, dim = cache.shape
    return jnp.transpose(cache, (0, 2, 3, 1)).reshape(nb, heads * dim, rows)


def _row_major(cache_t, heads):
    nb, width, rows = cache_t.shape
    return jnp.transpose(cache_t.reshape(nb, heads, width // heads, rows), (0, 3, 1, 2))


PROJ_TILE = 512
ATTN_TILE = BACK


def kernel(x_prompt, x_sample, cache_win_k, cache_win_v, state_conv, cache_mem_k, cache_mem_v, mem_prompt,
           norm_pre, norm_post, w_in, w_out, norm_mem, w_mem_kv, w_dw, b_dw, ln_conv_g, ln_conv_b, w_pw2, b_pw2):
    depth = w_in.shape[0]
    assert depth == 1, "single-layer step"
    b, s, _ = x_prompt.shape
    nb, t_s, _ = x_sample.shape
    win_len = cache_win_k.shape[2]
    assert s % (DIL16 * QB) == 0 and s % PROJ_TILE == 0 and s % ATTN_TILE == 0
    assert win_len == MAX_WINDOW and win_len % LANES == 0
    assert t_s % 8 == 0 and t_s & (t_s - 1) == 0 and t_s < LANES
    l = 0
    row = lambda a: a[l][None, :]
    npre, npost, nmem = row(norm_pre), row(norm_post), row(norm_mem)
    bdw, lng, lnb, bpw = row(b_dw), row(ln_conv_g), row(ln_conv_b), row(b_pw2)
    win_bf, wout_bf = w_in[l].astype(BF16), w_out[l].astype(BF16)
    wmem_bf, wpw_bf = w_mem_kv[l].astype(BF16), w_pw2[l].astype(BF16)
    wdw = w_dw[l]

    pos_s = PAST_LEN + jnp.arange(t_s, dtype=jnp.int32)
    xs = x_sample.reshape(nb * t_s, D_MODEL)
    srow, kv_s, gate_s = _sample_proj(xs, npre, win_bf, jnp.tile(_rope_table(pos_s), (nb, 1)))
    kvnt = _new_rows_t(kv_s, nb, t_s)
    st_pad = jnp.pad(state_conv[l], ((0, 0), (STATE_PAD, 0), (0, 0)))

    mk, mv, mkb, mvb = _mem_kv(mem_prompt, nmem, wmem_bf)
    tab = _rope_table(jnp.arange(s, dtype=jnp.int32))
    kvq, kvq16, kvf, ga, mix, ust = _prompt_proj(x_prompt, npre, win_bf, tab, mkb, mvb,
                                                 wdw, bdw, lng, lnb, wpw_bf, bpw, PROJ_TILE)
    o16 = _dilated16(kvq16)
    y_prompt, smix, wk, wv, nst = _prompt_attn(
        kvq, o16, ga, mix, x_prompt, wout_bf, npost, srow, kvnt,
        _feature_major(cache_win_k[l]), _feature_major(cache_win_v[l]), st_pad,
        _feature_major(cache_mem_k[l]), _feature_major(cache_mem_v[l]), wdw, ATTN_TILE, t_s)
    y_sample = _sample_out(xs, smix, gate_s, bdw, lng, lnb, wpw_bf, bpw, wout_bf, npost)

    keep_p = kvf.shape[1]
    return (y_prompt,
            y_sample.reshape(nb, t_s, D_MODEL),
            kvf[:, :, :A_WIDTH].reshape(1, b, keep_p, A_HEADS, HEAD_DIM),
            kvf[:, :, A_WIDTH:].reshape(1, b, keep_p, A_HEADS, HEAD_DIM),
            ust[:, CONV_HIST - (CONV_W - 1):, :][None],
            mk.reshape(1, b, N_MEM, M_HEADS, HEAD_DIM),
            mv.reshape(1, b, N_MEM, M_HEADS, HEAD_DIM),
            _row_major(wk, A_HEADS)[None],
            _row_major(wv, A_HEADS)[None],
            nst[None])
```

```python
import functools
import math

import jax
import jax.numpy as jnp
from jax import lax
from jax.experimental import pallas as pl
from jax.experimental.pallas import tpu as pltpu

F32 = jnp.float32
BF16 = jnp.bfloat16

D_MODEL = 1024
HEAD_DIM = 64
HEAD_SHIFT = 6
A_WIDTH = 384
A_HEADS = 6
M_WIDTH = 256
M_HEADS = 4
C_WIDTH = 384
ROT_DIM = 16
ROPE_THETA = 500000.0
CONV_W = 31
N_MEM = 256
MAX_WINDOW = 2048
PAST_LEN = 16384
EPS = 1e-6
SCALE = HEAD_DIM ** -0.5
NEG = -1e30
LN2 = math.log(2.0)

LANES = 128
NGRP = A_WIDTH // LANES
QB = 128
DIL16 = 16
O_QA, O_KA, O_VA, O_GA, O_AB, O_BB, O_GB, O_QM, O_GM, D_IN = 0, 384, 768, 1152, 1536, 1920, 2304, 2688, 2944, 3200
P_K, P_V, P_Q, P_W = 0, A_WIDTH, 2 * A_WIDTH, 3 * A_WIDTH

VMEM_LIMIT = 60 * 1024 * 1024


def _sigmoid(x):
    return 1.0 / (1.0 + jnp.exp(-x))


def _silu(x):
    return x * _sigmoid(x)


def _rmsnorm(x, g):
    return x * lax.rsqrt(jnp.mean(x * x, axis=-1, keepdims=True) + EPS) * g


def _dot(a, b):
    return jnp.dot(a, b, preferred_element_type=F32)


def _dot_t(a, b):
    return lax.dot_general(a, b, (((1,), (1,)), ((), ())), preferred_element_type=F32)


def _rope_patterns(cos, sin):
    in_head = lax.broadcasted_iota(jnp.int32, cos.shape, 1) & (HEAD_DIM - 1)
    rot, lo = in_head < ROT_DIM, in_head < ROT_DIM // 2
    return jnp.where(rot, cos, 1.0), jnp.where(lo, -sin, 0.0), jnp.where(rot & ~lo, sin, 0.0)


def _rope(xw, pats):
    cos, s1, s2 = pats
    outs = []
    for g in range(xw.shape[1] // LANES):
        xg = xw[:, g * LANES:(g + 1) * LANES]
        outs.append(xg * cos + pltpu.roll(xg, LANES - 8, 1) * s1 + pltpu.roll(xg, 8, 1) * s2)
    return jnp.concatenate(outs, axis=1)


def _pair_split(q):
    lane = lax.broadcasted_iota(jnp.int32, q.shape, 1)
    qa = jnp.where(lane < HEAD_DIM, q, 0.0)
    qb = jnp.where(lane >= HEAD_DIM, q, 0.0)
    return jnp.concatenate([qa, qb], axis=0)


def _pair_join(x):
    t = x.shape[0] // 2
    lane = lax.broadcasted_iota(jnp.int32, (t, LANES), 1)
    return jnp.where(lane < HEAD_DIM, x[:t], x[t:])


def _mem_attend(qm, mk_ref, mv_ref):
    outs = []
    for g in range(M_WIDTH // LANES):
        cols = slice(g * LANES, (g + 1) * LANES)
        q2 = _pair_split(qm[:, cols] * SCALE).astype(BF16)
        s = _dot_t(q2, mk_ref[0, :, cols])
        m = jnp.max(s, axis=-1, keepdims=True)
        p = jnp.exp(s - m)
        l = jnp.sum(p, axis=-1, keepdims=True)
        o = _dot(p.astype(BF16), mv_ref[0, :, cols]) * (1.0 / l)
        outs.append(_pair_join(o))
    return jnp.concatenate(outs, axis=1)


def _conformer_tail(c, gate_b, bdw, lng, lnb, wpw_ref, bpw):
    cf = c + bdw
    mu = jnp.mean(cf, axis=-1, keepdims=True)
    dev = cf - mu
    var = jnp.mean(dev * dev, axis=-1, keepdims=True)
    cn = dev * lax.rsqrt(var + EPS) * lng + lnb
    ob = _dot(_silu(cn).astype(BF16), wpw_ref[...]) + bpw
    return ob * gate_b


def _full(a):
    return pl.BlockSpec(a.shape, lambda *_: (0,) * a.ndim)


def _mem_kv_kernel(mem_ref, g_ref, w_ref, mk_ref, mv_ref, mkb_ref, mvb_ref):
    h = _rmsnorm(mem_ref[0], g_ref[...]).astype(BF16)
    kv = _dot(h, w_ref[...])
    mk_ref[0] = kv[:, :M_WIDTH]
    mv_ref[0] = kv[:, M_WIDTH:]
    mkb_ref[0] = kv[:, :M_WIDTH].astype(BF16)
    mvb_ref[0] = kv[:, M_WIDTH:].astype(BF16)


def _mem_kv(mem, g, w_bf):
    b = mem.shape[0]
    blk = pl.BlockSpec((1, N_MEM, M_WIDTH), lambda i: (i, 0, 0))
    return pl.pallas_call(
        _mem_kv_kernel,
        grid=(b,),
        in_specs=[pl.BlockSpec((1, N_MEM, D_MODEL), lambda i: (i, 0, 0)), _full(g), _full(w_bf)],
        out_specs=[blk, blk, blk, blk],
        out_shape=[jax.ShapeDtypeStruct((b, N_MEM, M_WIDTH), F32)] * 2
        + [jax.ShapeDtypeStruct((b, N_MEM, M_WIDTH), BF16)] * 2,
        name="mem_kv",
    )(mem, g, w_bf)


CONV_HIST = 32
CONV_CHUNK = 64


def _prompt_proj_kernel(x_ref, npre_ref, win_ref, rin_ref, rbase_ref, mk_ref, mv_ref,
                        wdw_ref, bdw_ref, lng_ref, lnb_ref, wpw_ref, bpw_ref,
                        kvq_ref, kvq16_ref, kf_ref, vf_ref, ga_ref, mix_ref, ust_ref,
                        uext_ref, ush_ref, conv_ref, stage_ref):
    t = x_ref.shape[1]
    i = pl.program_id(1)
    h = _rmsnorm(x_ref[0], npre_ref[...]).astype(BF16)
    proj = _dot(h, win_ref[...])

    cr, sr = rin_ref[:, 0:LANES], rin_ref[:, LANES:]
    cb, sb = rbase_ref[i, :, 0:LANES], rbase_ref[i, :, LANES:]
    pats = _rope_patterns(cb * cr - sb * sr, sb * cr + cb * sr)
    q = _rope(proj[:, O_QA:O_KA], pats) * SCALE
    k = _rope(proj[:, O_KA:O_VA], pats)
    v = proj[:, O_VA:O_GA]
    kvq_ref[0, :, P_K:P_V] = k.astype(BF16)
    kvq_ref[0, :, P_V:P_Q] = v.astype(BF16)
    kvq_ref[0, :, P_Q:P_W] = q.astype(BF16)
    kf_ref[0] = k
    vf_ref[0] = v
    ga_ref[0] = _silu(proj[:, O_GA:O_AB])

    for idx, val in enumerate((k, v, q)):
        for g in range(NGRP):
            stage_ref[NGRP * idx + g] = val[:, g * LANES:(g + 1) * LANES]
    for c in range(DIL16):
        rows_c = [stage_ref[s, pl.ds(c, t // DIL16, stride=DIL16), :] for s in range(3 * NGRP)]
        kvq16_ref[0, c] = jnp.concatenate(rows_c, axis=1).astype(BF16)

    u = proj[:, O_AB:O_BB] * _sigmoid(proj[:, O_BB:O_GB])

    @pl.when(i == 0)
    def _():
        uext_ref[0:CONV_HIST, :] = jnp.zeros((CONV_HIST, C_WIDTH), F32)

    uext_ref[CONV_HIST:CONV_HIST + t, :] = u
    for r in range(1, 8):
        ush_ref[r - 1, 8:CONV_HIST + t, :] = uext_ref[8 - r:CONV_HIST + t - r, :]
    for r0 in range(0, t, CONV_CHUNK):
        acc = jnp.zeros((CONV_CHUNK, C_WIDTH), F32)
        for kk in range(CONV_W):
            tap = wdw_ref[CONV_W - 1 - kk:CONV_W - kk, :]
            r, base = kk % 8, CONV_HIST + r0 - (kk - kk % 8)
            if r == 0:
                slab = uext_ref[base:base + CONV_CHUNK, :]
            else:
                slab = ush_ref[r - 1, base:base + CONV_CHUNK, :]
            acc = acc + slab * tap
        conv_ref[r0:r0 + CONV_CHUNK, :] = acc
    tail = uext_ref[t:t + CONV_HIST, :]
    uext_ref[0:CONV_HIST, :] = tail
    ust_ref[0] = tail

    mixed_b = _conformer_tail(conv_ref[...], _silu(proj[:, O_GB:O_QM]), bdw_ref[...], lng_ref[...], lnb_ref[...],
                              wpw_ref, bpw_ref[...])
    mixed_m = _mem_attend(proj[:, O_QM:O_GM], mk_ref, mv_ref) * _silu(proj[:, O_GM:D_IN])
    mix_ref[0, :, 0:C_WIDTH] = mixed_b.astype(BF16)
    mix_ref[0, :, C_WIDTH:] = mixed_m.astype(BF16)


def _prompt_proj(x, npre, win_bf, rin, rbase, mkb, mvb, wdw, bdw, lng, lnb, wpw_bf, bpw, tile):
    b, s, _ = x.shape
    nt = s // tile
    keep = min(MAX_WINDOW, s)
    first_keep = (s - keep) // tile
    row = lambda w: pl.BlockSpec((1, tile, w), lambda bi, i: (bi, i, 0))
    memb = pl.BlockSpec((1, N_MEM, M_WIDTH), lambda bi, i: (bi, 0, 0))
    keepb = pl.BlockSpec((1, tile, A_WIDTH), lambda bi, i: (bi, jnp.maximum(i - first_keep, 0), 0))
    cls = pl.BlockSpec((1, DIL16, tile // DIL16, P_W), lambda bi, i: (bi, 0, i, 0))
    return pl.pallas_call(
        _prompt_proj_kernel,
        grid=(b, nt),
        in_specs=[row(D_MODEL), _full(npre), _full(win_bf), _full(rin), _full(rbase),
                  memb, memb, _full(wdw), _full(bdw), _full(lng), _full(lnb), _full(wpw_bf), _full(bpw)],
        out_specs=[row(P_W), cls, keepb, keepb, row(A_WIDTH), row(C_WIDTH + M_WIDTH),
                   pl.BlockSpec((1, CONV_HIST, C_WIDTH), lambda bi, i: (bi, 0, 0))],
        out_shape=[jax.ShapeDtypeStruct((b, s, P_W), BF16),
                   jax.ShapeDtypeStruct((b, DIL16, s // DIL16, P_W), BF16),
                   jax.ShapeDtypeStruct((b, keep, A_WIDTH), F32),
                   jax.ShapeDtypeStruct((b, keep, A_WIDTH), F32),
                   jax.ShapeDtypeStruct((b, s, A_WIDTH), F32),
                   jax.ShapeDtypeStruct((b, s, C_WIDTH + M_WIDTH), BF16),
                   jax.ShapeDtypeStruct((b, CONV_HIST, C_WIDTH), F32)],
        scratch_shapes=[pltpu.VMEM((CONV_HIST + tile, C_WIDTH), F32),
                        pltpu.VMEM((7, CONV_HIST + tile, C_WIDTH), F32),
                        pltpu.VMEM((tile, C_WIDTH), F32),
                        pltpu.VMEM((3 * NGRP, tile, LANES), F32)],
        compiler_params=pltpu.CompilerParams(dimension_semantics=("arbitrary", "arbitrary"),
                                             vmem_limit_bytes=VMEM_LIMIT),
        name="prompt_proj",
    )(x, npre, win_bf, rin, rbase, mkb, mvb, wdw, bdw, lng, lnb, wpw_bf, bpw)


STAT_L = 8
O16_W = A_WIDTH + LANES


def _dilated16_kernel(kvq_ref, o_ref):
    a = lax.broadcasted_iota(jnp.int32, (QB, QB), 0)
    c = lax.broadcasted_iota(jnp.int32, (QB, QB), 1)
    tri_prev = jnp.where(c >= a, 0.0, NEG)
    tri_cur = jnp.where(c <= a, 0.0, NEG)
    bias_cur = jnp.concatenate([tri_cur, tri_cur], axis=0)
    both = jnp.concatenate([tri_prev, tri_cur], axis=1)
    bias_both = jnp.concatenate([both, both], axis=0)
    lane = lax.broadcasted_iota(jnp.int32, (QB, LANES), 1)
    for sub in range(kvq_ref.shape[2] // QB):
        rows = slice(sub * QB, (sub + 1) * QB)
        prev = slice((sub - 1) * QB, sub * QB)
        st = jnp.zeros((QB, LANES), F32)
        for g in range(NGRP):
            kc = slice(P_K + g * LANES, P_K + (g + 1) * LANES)
            vc = slice(P_V + g * LANES, P_V + (g + 1) * LANES)
            qc = slice(P_Q + g * LANES, P_Q + (g + 1) * LANES)
            q2 = _pair_split(kvq_ref[0, 0, rows, qc].astype(F32)).astype(BF16)
            s_cur = _dot_t(q2, kvq_ref[0, 0, rows, kc])
            if sub == 0:
                s = s_cur + bias_cur
            else:
                s = jnp.concatenate([_dot_t(q2, kvq_ref[0, 0, prev, kc]), s_cur], axis=1) + bias_both
            m = jnp.max(s, axis=-1, keepdims=True)
            p = jnp.exp(s - m)
            l = jnp.sum(p, axis=-1, keepdims=True)
            pb = p.astype(BF16)
            if sub == 0:
                o = _dot(pb, kvq_ref[0, 0, rows, vc])
            else:
                o = _dot(pb[:, :QB], kvq_ref[0, 0, prev, vc]) + _dot(pb[:, QB:], kvq_ref[0, 0, rows, vc])
            o_ref[0, 0, rows, g * LANES:(g + 1) * LANES] = _pair_join(o)
            st = jnp.where(lane == 2 * g, m[:QB], st)
            st = jnp.where(lane == 2 * g + 1, m[QB:], st)
            st = jnp.where(lane == STAT_L + 2 * g, l[:QB], st)
            st = jnp.where(lane == STAT_L + 2 * g + 1, l[QB:], st)
        o_ref[0, 0, rows, A_WIDTH:] = st


def _dilated16(kvq16):
    b, ncls, nj, w = kvq16.shape
    return pl.pallas_call(
        _dilated16_kernel,
        grid=(b, ncls),
        in_specs=[pl.BlockSpec((1, 1, nj, w), lambda bi, c: (bi, c, 0, 0))],
        out_specs=pl.BlockSpec((1, 1, nj, O16_W), lambda bi, c: (bi, c, 0, 0)),
        out_shape=jax.ShapeDtypeStruct((b, ncls, nj, O16_W), F32),
        compiler_params=pltpu.CompilerParams(dimension_semantics=("arbitrary",) * 2,
                                             vmem_limit_bytes=VMEM_LIMIT),
        name="dilated16",
    )(kvq16)


STATE_PAD = 2
S_Q, S_U, S_QM, S_ROW = 0, A_WIDTH, A_WIDTH + C_WIDTH, A_WIDTH + C_WIDTH + M_WIDTH


def _sample_mix_body(batch, srow_ref, kvt_ref, ck_ref, cv_ref, st_ref, cmk_ref, cmv_ref, wdw_ref,
                     smix_ref, wk_ref, wv_ref, ns_ref, uc_ref, *, t_s, win_len):
    new0 = LANES - t_s
    per_slab = LANES // t_s
    slab, pos = batch // per_slab, batch % per_slab
    kvnew = pltpu.roll(kvt_ref[slab], new0 - pos * t_s, 1)
    knt, vnt = kvnew[0:A_WIDTH], kvnew[A_WIDTH:]

    keep = lax.broadcasted_iota(jnp.int32, (A_WIDTH, LANES), 1) < new0
    nch = win_len // LANES
    for src_ref, new, dst_ref in ((ck_ref, knt, wk_ref), (cv_ref, vnt, wv_ref)):
        nxt = pltpu.roll(src_ref[0, :, 0:LANES], new0, 1)
        for c in range(nch):
            cur = nxt
            nxt = pltpu.roll(src_ref[0, :, (c + 1) * LANES:(c + 2) * LANES], new0, 1) if c + 1 < nch else new
            dst_ref[0, :, c * LANES:(c + 1) * LANES] = jnp.where(keep, cur, nxt)

    q = srow_ref[:, S_Q:S_U]
    head_of_lane = lax.broadcasted_iota(jnp.int32, (t_s, A_WIDTH), 1) >> HEAD_SHIFT
    q6 = jnp.concatenate([jnp.where(head_of_lane == h, q, 0.0) for h in range(A_HEADS)], axis=0).astype(BF16)
    s_c = _dot(q6, ck_ref[0].astype(BF16))
    s_n = _dot(q6, knt.astype(BF16))

    def weights(shape, key0, lo):
        tq = lax.broadcasted_iota(jnp.int32, shape, 0) & (t_s - 1)
        key = lax.broadcasted_iota(jnp.int32, shape, 1) + key0
        d = win_len + tq - key
        ok = (d >= 0) & (key >= lo)
        w = ((d <= 128).astype(F32) + ((d <= 512) & ((d & 3) == 0)).astype(F32)
             + ((d <= 2048) & ((d & 15) == 0)).astype(F32))
        return jnp.where(ok, w, 0.0)

    w_c = weights(s_c.shape, 0, 0)
    w_n = weights(s_n.shape, win_len - new0, win_len)
    s_c = jnp.where(w_c > 0.0, s_c, NEG)
    s_n = jnp.where(w_n > 0.0, s_n, NEG)
    m = jnp.maximum(jnp.max(s_c, axis=-1, keepdims=True), jnp.max(s_n, axis=-1, keepdims=True))
    p_c = jnp.exp(s_c - m) * w_c
    p_n = jnp.exp(s_n - m) * w_n
    l = jnp.sum(p_c, axis=-1, keepdims=True) + jnp.sum(p_n, axis=-1, keepdims=True)
    o = (_dot_t(p_c.astype(BF16), cv_ref[0].astype(BF16)) + _dot_t(p_n.astype(BF16), vnt.astype(BF16))) * (1.0 / l)
    oa = jnp.zeros((t_s, A_WIDTH), F32)
    for h in range(A_HEADS):
        oa = oa + jnp.where(head_of_lane == h, o[h * t_s:(h + 1) * t_s], 0.0)
    smix_ref[:, S_Q:S_U] = oa

    qm = srow_ref[:, S_QM:S_ROW] * SCALE
    mhead = lax.broadcasted_iota(jnp.int32, (t_s, M_WIDTH), 1) >> HEAD_SHIFT
    q4 = jnp.concatenate([jnp.where(mhead == h, qm, 0.0) for h in range(M_HEADS)], axis=0).astype(BF16)
    sm = _dot(q4, cmk_ref[0].astype(BF16))
    mm = jnp.max(sm, axis=-1, keepdims=True)
    pm = jnp.exp(sm - mm)
    lm = jnp.sum(pm, axis=-1, keepdims=True)
    om4 = _dot_t(pm.astype(BF16), cmv_ref[0].astype(BF16)) * (1.0 / lm)
    om = jnp.zeros((t_s, M_WIDTH), F32)
    for h in range(M_HEADS):
        om = om + jnp.where(mhead == h, om4[h * t_s:(h + 1) * t_s], 0.0)
    smix_ref[:, S_QM:S_ROW] = om

    nst = st_ref.shape[1]
    uc_ref[0:nst, :] = st_ref[0]
    uc_ref[nst:nst + t_s, :] = srow_ref[:, S_U:S_QM]
    acc = jnp.zeros((t_s, C_WIDTH), F32)
    for w in range(CONV_W):
        acc = acc + uc_ref[STATE_PAD + w:STATE_PAD + w + t_s, :] * wdw_ref[w:w + 1, :]
    smix_ref[:, S_U:S_QM] = acc
    ns_ref[0] = uc_ref[STATE_PAD + t_s:STATE_PAD + t_s + CONV_W - 1, :]


BACK = 512
KWIN = BACK + QB


def _prompt_attn_kernel(q_ref, kvp_ref, kvc_ref, o16_ref, ga_ref, mix_ref, x_ref, wout_ref, npost_ref,
                        srow_ref, kvt_ref, ck_ref, cv_ref, sst_ref, cmk_ref, cmv_ref, wdw_ref,
                        y_ref, smix_ref, wk_ref, wv_ref, ns_ref,
                        mixa_ref, nat_ref, uc_ref, *, tq, t_s, win_len):
    i = pl.program_id(1)
    kv_refs = (kvp_ref, kvc_ref)

    for cl in range(DIL16):
        dst = pl.ds(cl, tq // DIL16, stride=DIL16)
        for g in range(NGRP + 1):
            nat_ref[g, dst, :] = o16_ref[0, cl, :, g * LANES:(g + 1) * LANES]

    a = lax.broadcasted_iota(jnp.int32, (QB, KWIN), 0)
    c = lax.broadcasted_iota(jnp.int32, (QB, KWIN), 1)
    d = BACK + a - c
    near = (d >= 0) & (d <= QB)
    far = (d >= 0) & (d <= BACK) & ((d & 3) == 0)
    bias0 = jnp.where(near & far, LN2, jnp.where(near | far, 0.0, NEG))
    lane = lax.broadcasted_iota(jnp.int32, (QB, LANES), 1)
    head_a = lane < HEAD_DIM

    for sub in range(tq // QB):
        rows = slice(sub * QB, (sub + 1) * QB)
        first_valid = BACK - i * tq - sub * QB
        bias = jnp.where(c >= first_valid, bias0, NEG)
        bias2 = jnp.concatenate([bias, bias], axis=0)
        pieces = []
        w = sub * QB
        while w < sub * QB + KWIN:
            blk, off = divmod(w, tq)
            n = min(tq - off, sub * QB + KWIN - w)
            pieces.append((blk, off, n))
            w += n
        st = nat_ref[NGRP, rows, :]
        for g in range(NGRP):
            cols = slice(g * LANES, (g + 1) * LANES)
            kc = slice(P_K + g * LANES, P_K + (g + 1) * LANES)
            vc = slice(P_V + g * LANES, P_V + (g + 1) * LANES)
            q2 = _pair_split(q_ref[0, rows, cols].astype(F32)).astype(BF16)
            s = jnp.concatenate([_dot_t(q2, kv_refs[blk][0, off:off + n, kc]) for blk, off, n in pieces],
                                axis=1) + bias2
            m = jnp.max(s, axis=-1, keepdims=True)
            p = jnp.exp(s - m)
            l = jnp.sum(p, axis=-1, keepdims=True)
            pb = p.astype(BF16)
            o = None
            col = 0
            for blk, off, n in pieces:
                part = _dot(pb[:, col:col + n], kv_refs[blk][0, off:off + n, vc])
                o = part if o is None else o + part
                col += n
            o_n = _pair_join(o)
            m_n = jnp.where(head_a, m[:QB], m[QB:])
            l_n = jnp.where(head_a, l[:QB], l[QB:])
            m_f = jnp.where(head_a, st[:, 2 * g:2 * g + 1], st[:, 2 * g + 1:2 * g + 2])
            l_f = jnp.where(head_a, st[:, STAT_L + 2 * g:STAT_L + 2 * g + 1],
                            st[:, STAT_L + 2 * g + 1:STAT_L + 2 * g + 2])
            mx = jnp.maximum(m_n, m_f)
            w_n = jnp.exp(m_n - mx)
            w_f = jnp.exp(m_f - mx)
            oa = (o_n * w_n + nat_ref[g, rows, :] * w_f) / (l_n * w_n + l_f * w_f)
            mixa_ref[rows, cols] = (oa * ga_ref[0, rows, cols]).astype(BF16)

    z = _dot(mixa_ref[...], wout_ref[0:A_WIDTH, :]) + _dot(mix_ref[0], wout_ref[A_WIDTH:, :])
    y_ref[0] = x_ref[0] + _rmsnorm(z, npost_ref[...])

    _sample_mix_body(pl.program_id(0) * pl.num_programs(1) + i, srow_ref, kvt_ref, ck_ref, cv_ref, sst_ref,
                     cmk_ref, cmv_ref, wdw_ref, smix_ref, wk_ref, wv_ref, ns_ref, uc_ref, t_s=t_s, win_len=win_len)


def _prompt_attn(kvq, o16, ga, mix, x, wout_bf, npost, srow, kvt, ck, cv, st_pad, cmk, cmv, wdw, tq, t_s):
    b, s, _ = x.shape
    nsteps = s // tq
    nb, _, win_len = ck.shape
    assert tq == BACK and b * nsteps == nb, "one sample batch per grid step"
    row = lambda w: pl.BlockSpec((1, tq, w), lambda bi, i: (bi, i, 0))
    qspec = pl.BlockSpec((1, tq, A_WIDTH), lambda bi, i: (bi, i, P_Q // A_WIDTH))
    kvprev = pl.BlockSpec((1, tq, 2 * A_WIDTH), lambda bi, i: (bi, jnp.maximum(i - 1, 0), 0))
    kvcur = pl.BlockSpec((1, tq, 2 * A_WIDTH), lambda bi, i: (bi, i, 0))
    cls = pl.BlockSpec((1, DIL16, tq // DIL16, O16_W), lambda bi, i: (bi, 0, i, 0))
    srows = pl.BlockSpec((t_s, S_ROW), lambda bi, i: (bi * nsteps + i, 0))
    per = lambda a: pl.BlockSpec((1,) + a.shape[1:], lambda bi, i: (bi * nsteps + i, 0, 0))
    return pl.pallas_call(
        functools.partial(_prompt_attn_kernel, tq=tq, t_s=t_s, win_len=win_len),
        grid=(b, nsteps),
        in_specs=[qspec, kvprev, kvcur, cls, row(A_WIDTH), row(C_WIDTH + M_WIDTH), row(D_MODEL),
                  _full(wout_bf), _full(npost),
                  srows, _full(kvt), per(ck), per(cv), per(st_pad), per(cmk), per(cmv), _full(wdw)],
        out_specs=[row(D_MODEL), srows, per(ck), per(cv),
                   pl.BlockSpec((1, CONV_W - 1, C_WIDTH), lambda bi, i: (bi * nsteps + i, 0, 0))],
        out_shape=[jax.ShapeDtypeStruct((b, s, D_MODEL), F32),
                   jax.ShapeDtypeStruct((nb * t_s, S_ROW), F32),
                   jax.ShapeDtypeStruct(ck.shape, F32), jax.ShapeDtypeStruct(cv.shape, F32),
                   jax.ShapeDtypeStruct((nb, CONV_W - 1, C_WIDTH), F32)],
        scratch_shapes=[pltpu.VMEM((tq, A_WIDTH), BF16),
                        pltpu.VMEM((NGRP + 1, tq, LANES), F32),
                        pltpu.VMEM((STATE_PAD + CONV_W - 1 + t_s, C_WIDTH), F32)],
        compiler_params=pltpu.CompilerParams(dimension_semantics=("arbitrary", "arbitrary"),
                                             vmem_limit_bytes=VMEM_LIMIT),
        name="prompt_attn",
    )(kvq, kvq, kvq, o16, ga, mix, x, wout_bf, npost, srow, kvt, ck, cv, st_pad, cmk, cmv, wdw)


def _sample_proj_kernel(x_ref, npre_ref, win_ref, ang_ref, srow_ref, kvt_ref, gate_ref):
    h = _rmsnorm(x_ref[...], npre_ref[...]).astype(BF16)
    proj = _dot(h, win_ref[...])
    pats = _rope_patterns(ang_ref[:, 0:LANES], ang_ref[:, LANES:])
    srow_ref[:, S_Q:S_U] = _rope(proj[:, O_QA:O_KA], pats) * SCALE
    srow_ref[:, S_U:S_QM] = proj[:, O_AB:O_BB] * _sigmoid(proj[:, O_BB:O_GB])
    srow_ref[:, S_QM:S_ROW] = proj[:, O_QM:O_GM]
    kv_t = jnp.concatenate([_rope(proj[:, O_KA:O_VA], pats), proj[:, O_VA:O_GA]], axis=1).T
    for slab in range(kvt_ref.shape[0]):
        kvt_ref[slab] = kv_t[:, slab * LANES:(slab + 1) * LANES]
    gate_ref[:, S_Q:S_U] = _silu(proj[:, O_GA:O_AB])
    gate_ref[:, S_U:S_QM] = _silu(proj[:, O_GB:O_QM])
    gate_ref[:, S_QM:S_ROW] = _silu(proj[:, O_GM:D_IN])


def _sample_proj(x, npre, win_bf, ang):
    n = x.shape[0]
    assert n % LANES == 0
    shapes = [(n, S_ROW), (n // LANES, 2 * A_WIDTH, LANES), (n, S_ROW)]
    return pl.pallas_call(
        _sample_proj_kernel,
        grid=(1,),
        in_specs=[_full(x), _full(npre), _full(win_bf), _full(ang)],
        out_specs=[pl.BlockSpec(sh, lambda i, nd=len(sh): (0,) * nd) for sh in shapes],
        out_shape=[jax.ShapeDtypeStruct(sh, F32) for sh in shapes],
        compiler_params=pltpu.CompilerParams(vmem_limit_bytes=VMEM_LIMIT),
        name="sample_proj",
    )(x, npre, win_bf, ang)


def _sample_out_kernel(x_ref, smix_ref, gate_ref, bdw_ref, lng_ref, lnb_ref, wpw_ref, bpw_ref, wout_ref, npost_ref,
                       y_ref):
    mixed_a = (smix_ref[:, S_Q:S_U] * gate_ref[:, S_Q:S_U]).astype(BF16)
    mixed_b = _conformer_tail(smix_ref[:, S_U:S_QM], gate_ref[:, S_U:S_QM], bdw_ref[...], lng_ref[...], lnb_ref[...],
                              wpw_ref, bpw_ref[...]).astype(BF16)
    mixed_m = (smix_ref[:, S_QM:S_ROW] * gate_ref[:, S_QM:S_ROW]).astype(BF16)
    z = (_dot(mixed_a, wout_ref[0:A_WIDTH, :]) + _dot(mixed_b, wout_ref[A_WIDTH:A_WIDTH + C_WIDTH, :])
         + _dot(mixed_m, wout_ref[A_WIDTH + C_WIDTH:, :]))
    y_ref[...] = x_ref[...] + _rmsnorm(z, npost_ref[...])


def _sample_out(x, smix, gate, bdw, lng, lnb, wpw_bf, bpw, wout_bf, npost):
    args = (x, smix, gate, bdw, lng, lnb, wpw_bf, bpw, wout_bf, npost)
    return pl.pallas_call(
        _sample_out_kernel,
        grid=(1,),
        in_specs=[_full(a) for a in args],
        out_specs=_full(x),
        out_shape=jax.ShapeDtypeStruct(x.shape, F32),
        compiler_params=pltpu.CompilerParams(vmem_limit_bytes=VMEM_LIMIT),
        name="sample_out",
    )(*args)


def _rope_cos_sin(pos):
    inv = ROPE_THETA ** (-jnp.arange(0, ROT_DIM, 2, dtype=F32) / ROT_DIM)
    ang = pos.astype(F32)[:, None] * jnp.tile(inv, 2 * LANES // ROT_DIM)[None, :]
    return jnp.concatenate([jnp.cos(ang), jnp.sin(ang)], axis=1)


def _feature_major(cache):
    nb, rows, heads, dim = cache.shape
    return jnp.transpose(cache, (0, 2, 3, 1)).reshape(nb, heads * dim, rows)


def _row_major(cache_t, heads):
    nb, width, rows = cache_t.shape
    return jnp.transpose(cache_t.reshape(nb, heads, width // heads, rows), (0, 3, 1, 2))


PROJ_TILE = 512
ATTN_TILE = BACK


def kernel(x_prompt, x_sample, cache_win_k, cache_win_v, state_conv, cache_mem_k, cache_mem_v, mem_prompt,
           norm_pre, norm_post, w_in, w_out, norm_mem, w_mem_kv, w_dw, b_dw, ln_conv_g, ln_conv_b, w_pw2, b_pw2):
    depth = w_in.shape[0]
    assert depth == 1, "single-layer step"
    b, s, _ = x_prompt.shape
    nb, t_s, _ = x_sample.shape
    win_len = cache_win_k.shape[2]
    assert s % (DIL16 * QB) == 0 and s % PROJ_TILE == 0 and s % ATTN_TILE == 0
    assert win_len == MAX_WINDOW and win_len % LANES == 0
    assert t_s % 8 == 0 and t_s & (t_s - 1) == 0 and t_s < LANES
    l = 0
    row = lambda a: a[l][None, :]
    npre, npost, nmem = row(norm_pre), row(norm_post), row(norm_mem)
    bdw, lng, lnb, bpw = row(b_dw), row(ln_conv_g), row(ln_conv_b), row(b_pw2)
    win_bf, wout_bf = w_in[l].astype(BF16), w_out[l].astype(BF16)
    wmem_bf, wpw_bf = w_mem_kv[l].astype(BF16), w_pw2[l].astype(BF16)
    wdw = w_dw[l]

    pos_s = PAST_LEN + jnp.arange(t_s, dtype=jnp.int32)
    xs = x_sample.reshape(nb * t_s, D_MODEL)
    srow, kvt, gate_s = _sample_proj(xs, npre, win_bf, jnp.tile(_rope_cos_sin(pos_s), (nb, 1)))
    st_pad = jnp.pad(state_conv[l], ((0, 0), (STATE_PAD, 0), (0, 0)))

    mk, mv, mkb, mvb = _mem_kv(mem_prompt, nmem, wmem_bf)
    rin = _rope_cos_sin(jnp.arange(PROJ_TILE, dtype=jnp.int32))
    rbase = _rope_cos_sin(jnp.arange(0, s, PROJ_TILE, dtype=jnp.int32))[:, None, :]
    kvq, kvq16, kf, vf, ga, mix, ust = _prompt_proj(x_prompt, npre, win_bf, rin, rbase, mkb, mvb,
                                                    wdw, bdw, lng, lnb, wpw_bf, bpw, PROJ_TILE)
    o16 = _dilated16(kvq16)
    y_prompt, smix, wk, wv, nst = _prompt_attn(
        kvq, o16, ga, mix, x_prompt, wout_bf, npost, srow, kvt,
        _feature_major(cache_win_k[l]), _feature_major(cache_win_v[l]), st_pad,
        _feature_major(cache_mem_k[l]), _feature_major(cache_mem_v[l]), wdw, ATTN_TILE, t_s)
    y_sample = _sample_out(xs, smix, gate_s, bdw, lng, lnb, wpw_bf, bpw, wout_bf, npost)

    keep_p = kf.shape[1]
    return (y_prompt,
            y_sample.reshape(nb, t_s, D_MODEL),
            kf.reshape(1, b, keep_p, A_HEADS, HEAD_DIM),
            vf.reshape(1, b, keep_p, A_HEADS, HEAD_DIM),
            ust[:, CONV_HIST - (CONV_W - 1):, :][None],
            mk.reshape(1, b, N_MEM, M_HEADS, HEAD_DIM),
            mv.reshape(1, b, N_MEM, M_HEADS, HEAD_DIM),
            _row_major(wk, A_HEADS)[None],
            _row_major(wv, A_HEADS)[None],
            nst[None])
```

```python
import functools
import math

import jax
import jax.numpy as jnp
from jax import lax
from jax.experimental import pallas as pl
from jax.experimental.pallas import tpu as pltpu

F32 = jnp.float32
BF16 = jnp.bfloat16

D_MODEL = 1024
HEAD_DIM = 64
HEAD_SHIFT = 6
A_WIDTH = 384
A_HEADS = 6
M_WIDTH = 256
M_HEADS = 4
C_WIDTH = 384
ROT_DIM = 16
ROPE_THETA = 500000.0
CONV_W = 31
N_MEM = 256
MAX_WINDOW = 2048
PAST_LEN = 16384
EPS = 1e-6
SCALE = HEAD_DIM ** -0.5
NEG = -1e30
LN2 = math.log(2.0)

LANES = 128
NGRP = A_WIDTH // LANES
QB = 128
DIL16 = 16
DIL4 = 4
CLS4 = QB // DIL4
CLS4_SHIFT = 5
O_QA, O_KA, O_VA, O_GA, O_AB, O_BB, O_GB, O_QM, O_GM, D_IN = 0, 384, 768, 1152, 1536, 1920, 2304, 2688, 2944, 3200
P_K, P_V, P_Q, P_W = 0, A_WIDTH, 2 * A_WIDTH, 3 * A_WIDTH

VMEM_LIMIT = 60 * 1024 * 1024


def _sigmoid(x):
    return 1.0 / (1.0 + jnp.exp(-x))


def _silu(x):
    return x * _sigmoid(x)


def _rmsnorm(x, g):
    return x * lax.rsqrt(jnp.mean(x * x, axis=-1, keepdims=True) + EPS) * g


def _dot(a, b):
    return jnp.dot(a, b, preferred_element_type=F32)


def _dot_t(a, b):
    return lax.dot_general(a, b, (((1,), (1,)), ((), ())), preferred_element_type=F32)


def _rope_patterns(cos, sin):
    in_head = lax.broadcasted_iota(jnp.int32, cos.shape, 1) & (HEAD_DIM - 1)
    rot, lo = in_head < ROT_DIM, in_head < ROT_DIM // 2
    return jnp.where(rot, cos, 1.0), jnp.where(lo, -sin, 0.0), jnp.where(rot & ~lo, sin, 0.0)


def _rope(xw, pats):
    cos, s1, s2 = pats
    outs = []
    for g in range(xw.shape[1] // LANES):
        xg = xw[:, g * LANES:(g + 1) * LANES]
        outs.append(xg * cos + pltpu.roll(xg, LANES - 8, 1) * s1 + pltpu.roll(xg, 8, 1) * s2)
    return jnp.concatenate(outs, axis=1)


def _pair_split(q):
    lane = lax.broadcasted_iota(jnp.int32, q.shape, 1)
    qa = jnp.where(lane < HEAD_DIM, q, 0.0)
    qb = jnp.where(lane >= HEAD_DIM, q, 0.0)
    return jnp.concatenate([qa, qb], axis=0)


def _pair_join(x):
    t = x.shape[0] // 2
    lane = lax.broadcasted_iota(jnp.int32, (t, LANES), 1)
    return jnp.where(lane < HEAD_DIM, x[:t], x[t:])


def _mem_attend(qm, mk_ref, mv_ref):
    outs = []
    for g in range(M_WIDTH // LANES):
        cols = slice(g * LANES, (g + 1) * LANES)
        q2 = _pair_split(qm[:, cols] * SCALE).astype(BF16)
        s = _dot_t(q2, mk_ref[0, :, cols])
        m = jnp.max(s, axis=-1, keepdims=True)
        p = jnp.exp(s - m)
        l = jnp.sum(p, axis=-1, keepdims=True)
        o = _dot(p.astype(BF16), mv_ref[0, :, cols]) * (1.0 / l)
        outs.append(_pair_join(o))
    return jnp.concatenate(outs, axis=1)


def _conformer_tail(c, gate_b, bdw, lng, lnb, wpw_ref, bpw):
    cf = c + bdw
    mu = jnp.mean(cf, axis=-1, keepdims=True)
    dev = cf - mu
    var = jnp.mean(dev * dev, axis=-1, keepdims=True)
    cn = dev * lax.rsqrt(var + EPS) * lng + lnb
    ob = _dot(_silu(cn).astype(BF16), wpw_ref[...]) + bpw
    return ob * gate_b


def _full(a):
    return pl.BlockSpec(a.shape, lambda *_: (0,) * a.ndim)


def _mem_kv_kernel(mem_ref, g_ref, w_ref, mk_ref, mv_ref, mkb_ref, mvb_ref):
    h = _rmsnorm(mem_ref[0], g_ref[...]).astype(BF16)
    kv = _dot(h, w_ref[...])
    mk_ref[0] = kv[:, :M_WIDTH]
    mv_ref[0] = kv[:, M_WIDTH:]
    mkb_ref[0] = kv[:, :M_WIDTH].astype(BF16)
    mvb_ref[0] = kv[:, M_WIDTH:].astype(BF16)


def _mem_kv(mem, g, w_bf):
    b = mem.shape[0]
    blk = pl.BlockSpec((1, N_MEM, M_WIDTH), lambda i: (i, 0, 0))
    return pl.pallas_call(
        _mem_kv_kernel,
        grid=(b,),
        in_specs=[pl.BlockSpec((1, N_MEM, D_MODEL), lambda i: (i, 0, 0)), _full(g), _full(w_bf)],
        out_specs=[blk, blk, blk, blk],
        out_shape=[jax.ShapeDtypeStruct((b, N_MEM, M_WIDTH), F32)] * 2
        + [jax.ShapeDtypeStruct((b, N_MEM, M_WIDTH), BF16)] * 2,
        name="mem_kv",
    )(mem, g, w_bf)


CONV_HIST = 32
CONV_CHUNK = 64


def _prompt_proj_kernel(x_ref, npre_ref, win_ref, rin_ref, rbase_ref, mk_ref, mv_ref,
                        wdw_ref, bdw_ref, lng_ref, lnb_ref, wpw_ref, bpw_ref,
                        kvq_ref, kvq16_ref, kf_ref, vf_ref, ga_ref, mix_ref, ust_ref,
                        uext_ref, ush_ref, conv_ref, stage_ref):
    t = x_ref.shape[1]
    i = pl.program_id(1)

    @pl.when(i == 0)
    def _():
        uext_ref[0:CONV_HIST, :] = jnp.zeros((CONV_HIST, C_WIDTH), F32)

    h = _rmsnorm(x_ref[0], npre_ref[...]).astype(BF16)

    def project(c0, c1):
        return _dot(h, win_ref[:, c0:c1])

    def stage(idx, val):
        for g in range(NGRP):
            stage_ref[NGRP * idx + g] = val[:, g * LANES:(g + 1) * LANES]

    p_qk = project(O_QA, O_VA)
    p_vg = project(O_VA, O_AB)

    cr, sr = rin_ref[:, 0:LANES], rin_ref[:, LANES:]
    cb, sb = rbase_ref[i, :, 0:LANES], rbase_ref[i, :, LANES:]
    pats = _rope_patterns(cb * cr - sb * sr, sb * cr + cb * sr)
    k = _rope(p_qk[:, O_KA:O_VA], pats)
    kf_ref[0] = k
    stage(0, k)
    stage(2, _rope(p_qk[:, O_QA:O_KA], pats) * SCALE)

    p_glu = project(O_AB, O_GB)

    v = p_vg[:, 0:A_WIDTH]
    vf_ref[0] = v
    stage(1, v)
    ga_ref[0] = _silu(p_vg[:, A_WIDTH:])

    for c in range(DIL16):
        rows_c = [stage_ref[s, pl.ds(c, t // DIL16, stride=DIL16), :] for s in range(3 * NGRP)]
        kvq16_ref[0, c] = jnp.concatenate(rows_c, axis=1).astype(BF16)
    for blk in range(t // QB):
        for c in range(DIL4):
            rows_c = [stage_ref[s, pl.ds(blk * QB + c, CLS4, stride=DIL4), :] for s in range(3 * NGRP)]
            kvq_ref[0, blk * QB + c * CLS4:blk * QB + (c + 1) * CLS4, :] = (
                jnp.concatenate(rows_c, axis=1).astype(BF16))

    p_rest = project(O_GB, D_IN)

    u = p_glu[:, 0:C_WIDTH] * _sigmoid(p_glu[:, C_WIDTH:])
    uext_ref[CONV_HIST:CONV_HIST + t, :] = u
    for r in range(1, 8):
        ush_ref[r - 1, 8:CONV_HIST + t, :] = uext_ref[8 - r:CONV_HIST + t - r, :]
    for r0 in range(0, t, CONV_CHUNK):
        acc = jnp.zeros((CONV_CHUNK, C_WIDTH), F32)
        for kk in range(CONV_W):
            tap = wdw_ref[CONV_W - 1 - kk:CONV_W - kk, :]
            r, base = kk % 8, CONV_HIST + r0 - (kk - kk % 8)
            if r == 0:
                slab = uext_ref[base:base + CONV_CHUNK, :]
            else:
                slab = ush_ref[r - 1, base:base + CONV_CHUNK, :]
            acc = acc + slab * tap
        conv_ref[r0:r0 + CONV_CHUNK, :] = acc
    tail = uext_ref[t:t + CONV_HIST, :]
    uext_ref[0:CONV_HIST, :] = tail
    ust_ref[0] = tail

    mixed_b = _conformer_tail(conv_ref[...], _silu(p_rest[:, 0:O_QM - O_GB]), bdw_ref[...], lng_ref[...],
                              lnb_ref[...], wpw_ref, bpw_ref[...])
    mixed_m = (_mem_attend(p_rest[:, O_QM - O_GB:O_GM - O_GB], mk_ref, mv_ref) * _silu(p_rest[:, O_GM - O_GB:]))
    mix_ref[0, :, 0:C_WIDTH] = mixed_b.astype(BF16)
    mix_ref[0, :, C_WIDTH:] = mixed_m.astype(BF16)


def _prompt_proj(x, npre, win_bf, rin, rbase, mkb, mvb, wdw, bdw, lng, lnb, wpw_bf, bpw, tile):
    b, s, _ = x.shape
    nt = s // tile
    keep = min(MAX_WINDOW, s)
    first_keep = (s - keep) // tile
    row = lambda w: pl.BlockSpec((1, tile, w), lambda bi, i: (bi, i, 0))
    memb = pl.BlockSpec((1, N_MEM, M_WIDTH), lambda bi, i: (bi, 0, 0))
    keepb = pl.BlockSpec((1, tile, A_WIDTH), lambda bi, i: (bi, jnp.maximum(i - first_keep, 0), 0))
    cls = pl.BlockSpec((1, DIL16, tile // DIL16, P_W), lambda bi, i: (bi, 0, i, 0))
    return pl.pallas_call(
        _prompt_proj_kernel,
        grid=(b, nt),
        in_specs=[row(D_MODEL), _full(npre), _full(win_bf), _full(rin), _full(rbase),
                  memb, memb, _full(wdw), _full(bdw), _full(lng), _full(lnb), _full(wpw_bf), _full(bpw)],
        out_specs=[row(P_W), cls, keepb, keepb, row(A_WIDTH), row(C_WIDTH + M_WIDTH),
                   pl.BlockSpec((1, CONV_HIST, C_WIDTH), lambda bi, i: (bi, 0, 0))],
        out_shape=[jax.ShapeDtypeStruct((b, s, P_W), BF16),
                   jax.ShapeDtypeStruct((b, DIL16, s // DIL16, P_W), BF16),
                   jax.ShapeDtypeStruct((b, keep, A_WIDTH), F32),
                   jax.ShapeDtypeStruct((b, keep, A_WIDTH), F32),
                   jax.ShapeDtypeStruct((b, s, A_WIDTH), F32),
                   jax.ShapeDtypeStruct((b, s, C_WIDTH + M_WIDTH), BF16),
                   jax.ShapeDtypeStruct((b, CONV_HIST, C_WIDTH), F32)],
        scratch_shapes=[pltpu.VMEM((CONV_HIST + tile, C_WIDTH), F32),
                        pltpu.VMEM((7, CONV_HIST + tile, C_WIDTH), F32),
                        pltpu.VMEM((tile, C_WIDTH), F32),
                        pltpu.VMEM((3 * NGRP, tile, LANES), F32)],
        compiler_params=pltpu.CompilerParams(dimension_semantics=("arbitrary", "arbitrary"),
                                             vmem_limit_bytes=VMEM_LIMIT),
        name="prompt_proj",
    )(x, npre, win_bf, rin, rbase, mkb, mvb, wdw, bdw, lng, lnb, wpw_bf, bpw)


STAT_L = 8
O16_W = A_WIDTH + LANES


def _dilated16_kernel(kvq_ref, o_ref):
    a = lax.broadcasted_iota(jnp.int32, (QB, QB), 0)
    c = lax.broadcasted_iota(jnp.int32, (QB, QB), 1)
    tri_prev = jnp.where(c >= a, 0.0, NEG)
    tri_cur = jnp.where(c <= a, 0.0, NEG)
    bias_cur = jnp.concatenate([tri_cur, tri_cur], axis=0)
    both = jnp.concatenate([tri_prev, tri_cur], axis=1)
    bias_both = jnp.concatenate([both, both], axis=0)
    lane = lax.broadcasted_iota(jnp.int32, (QB, LANES), 1)
    for sub in range(kvq_ref.shape[2] // QB):
        rows = slice(sub * QB, (sub + 1) * QB)
        prev = slice((sub - 1) * QB, sub * QB)
        st = jnp.zeros((QB, LANES), F32)
        for g in range(NGRP):
            kc = slice(P_K + g * LANES, P_K + (g + 1) * LANES)
            vc = slice(P_V + g * LANES, P_V + (g + 1) * LANES)
            qc = slice(P_Q + g * LANES, P_Q + (g + 1) * LANES)
            q2 = _pair_split(kvq_ref[0, 0, rows, qc].astype(F32)).astype(BF16)
            s_cur = _dot_t(q2, kvq_ref[0, 0, rows, kc])
            if sub == 0:
                s = s_cur + bias_cur
            else:
                s = jnp.concatenate([_dot_t(q2, kvq_ref[0, 0, prev, kc]), s_cur], axis=1) + bias_both
            m = jnp.max(s, axis=-1, keepdims=True)
            p = jnp.exp(s - m)
            l = jnp.sum(p, axis=-1, keepdims=True)
            pb = p.astype(BF16)
            if sub == 0:
                o = _dot(pb, kvq_ref[0, 0, rows, vc])
            else:
                o = _dot(pb[:, :QB], kvq_ref[0, 0, prev, vc]) + _dot(pb[:, QB:], kvq_ref[0, 0, rows, vc])
            o_ref[0, 0, rows, g * LANES:(g + 1) * LANES] = _pair_join(o)
            st = jnp.where(lane == 2 * g, m[:QB], st)
            st = jnp.where(lane == 2 * g + 1, m[QB:], st)
            st = jnp.where(lane == STAT_L + 2 * g, l[:QB], st)
            st = jnp.where(lane == STAT_L + 2 * g + 1, l[QB:], st)
        o_ref[0, 0, rows, A_WIDTH:] = st


def _dilated16(kvq16):
    b, ncls, nj, w = kvq16.shape
    return pl.pallas_call(
        _dilated16_kernel,
        grid=(b, ncls),
        in_specs=[pl.BlockSpec((1, 1, nj, w), lambda bi, c: (bi, c, 0, 0))],
        out_specs=pl.BlockSpec((1, 1, nj, O16_W), lambda bi, c: (bi, c, 0, 0)),
        out_shape=jax.ShapeDtypeStruct((b, ncls, nj, O16_W), F32),
        compiler_params=pltpu.CompilerParams(dimension_semantics=("arbitrary",) * 2,
                                             vmem_limit_bytes=VMEM_LIMIT),
        name="dilated16",
    )(kvq16)


STATE_PAD = 2
S_Q, S_U, S_QM, S_ROW = 0, A_WIDTH, A_WIDTH + C_WIDTH, A_WIDTH + C_WIDTH + M_WIDTH


def _sample_mix_stages(batch, srow_ref, kvt_ref, ck_ref, cv_ref, st_ref, cmk_ref, cmv_ref, wdw_ref,
                       smix_ref, wk_ref, wv_ref, ns_ref, uc_ref, *, t_s, win_len):
    new0 = LANES - t_s
    per_slab = LANES // t_s
    slab, pos = batch // per_slab, batch % per_slab

    def new_rows(lo, hi):
        return pltpu.roll(kvt_ref[slab, lo:hi, :], new0 - pos * t_s, 1)

    def slide(src_ref, lo, dst_ref):
        keep = lax.broadcasted_iota(jnp.int32, (A_WIDTH, LANES), 1) < new0
        nch = win_len // LANES
        nxt = pltpu.roll(src_ref[0, :, 0:LANES], new0, 1)
        for c in range(nch):
            cur = nxt
            nxt = (pltpu.roll(src_ref[0, :, (c + 1) * LANES:(c + 2) * LANES], new0, 1) if c + 1 < nch
                   else new_rows(lo, lo + A_WIDTH))
            dst_ref[0, :, c * LANES:(c + 1) * LANES] = jnp.where(keep, cur, nxt)

    return [functools.partial(slide, ck_ref, 0, wk_ref), functools.partial(slide, cv_ref, A_WIDTH, wv_ref),
            functools.partial(_sample_attend, new_rows, srow_ref, ck_ref, cv_ref, smix_ref, t_s=t_s, win_len=win_len),
            functools.partial(_sample_mem_conv, srow_ref, st_ref, cmk_ref, cmv_ref, wdw_ref, smix_ref, ns_ref,
                              uc_ref, t_s=t_s)]


def _sample_attend(new_rows, srow_ref, ck_ref, cv_ref, smix_ref, *, t_s, win_len):
    new0 = LANES - t_s
    knt, vnt = new_rows(0, A_WIDTH), new_rows(A_WIDTH, 2 * A_WIDTH)
    q = srow_ref[:, S_Q:S_U]
    head_of_lane = lax.broadcasted_iota(jnp.int32, (t_s, A_WIDTH), 1) >> HEAD_SHIFT
    q6 = jnp.concatenate([jnp.where(head_of_lane == h, q, 0.0) for h in range(A_HEADS)], axis=0).astype(BF16)
    s_c = _dot(q6, ck_ref[0].astype(BF16))
    s_n = _dot(q6, knt.astype(BF16))

    def weights(shape, key0, lo):
        tq = lax.broadcasted_iota(jnp.int32, shape, 0) & (t_s - 1)
        key = lax.broadcasted_iota(jnp.int32, shape, 1) + key0
        d = win_len + tq - key
        ok = (d >= 0) & (key >= lo)
        w = ((d <= 128).astype(F32) + ((d <= 512) & ((d & 3) == 0)).astype(F32)
             + ((d <= 2048) & ((d & 15) == 0)).astype(F32))
        return jnp.where(ok, w, 0.0)

    w_c = weights(s_c.shape, 0, 0)
    w_n = weights(s_n.shape, win_len - new0, win_len)
    s_c = jnp.where(w_c > 0.0, s_c, NEG)
    s_n = jnp.where(w_n > 0.0, s_n, NEG)
    m = jnp.maximum(jnp.max(s_c, axis=-1, keepdims=True), jnp.max(s_n, axis=-1, keepdims=True))
    p_c = jnp.exp(s_c - m) * w_c
    p_n = jnp.exp(s_n - m) * w_n
    l = jnp.sum(p_c, axis=-1, keepdims=True) + jnp.sum(p_n, axis=-1, keepdims=True)
    o = (_dot_t(p_c.astype(BF16), cv_ref[0].astype(BF16)) + _dot_t(p_n.astype(BF16), vnt.astype(BF16))) * (1.0 / l)
    oa = jnp.zeros((t_s, A_WIDTH), F32)
    for h in range(A_HEADS):
        oa = oa + jnp.where(head_of_lane == h, o[h * t_s:(h + 1) * t_s], 0.0)
    smix_ref[:, S_Q:S_U] = oa


def _sample_mem_conv(srow_ref, st_ref, cmk_ref, cmv_ref, wdw_ref, smix_ref, ns_ref, uc_ref, *, t_s):
    qm = srow_ref[:, S_QM:S_ROW] * SCALE
    mhead = lax.broadcasted_iota(jnp.int32, (t_s, M_WIDTH), 1) >> HEAD_SHIFT
    q4 = jnp.concatenate([jnp.where(mhead == h, qm, 0.0) for h in range(M_HEADS)], axis=0).astype(BF16)
    sm = _dot(q4, cmk_ref[0].astype(BF16))
    mm = jnp.max(sm, axis=-1, keepdims=True)
    pm = jnp.exp(sm - mm)
    lm = jnp.sum(pm, axis=-1, keepdims=True)
    om4 = _dot_t(pm.astype(BF16), cmv_ref[0].astype(BF16)) * (1.0 / lm)
    om = jnp.zeros((t_s, M_WIDTH), F32)
    for h in range(M_HEADS):
        om = om + jnp.where(mhead == h, om4[h * t_s:(h + 1) * t_s], 0.0)
    smix_ref[:, S_QM:S_ROW] = om

    nst = st_ref.shape[1]
    uc_ref[0:nst, :] = st_ref[0]
    uc_ref[nst:nst + t_s, :] = srow_ref[:, S_U:S_QM]
    acc = jnp.zeros((t_s, C_WIDTH), F32)
    for w in range(CONV_W):
        acc = acc + uc_ref[STATE_PAD + w:STATE_PAD + w + t_s, :] * wdw_ref[w:w + 1, :]
    smix_ref[:, S_U:S_QM] = acc
    ns_ref[0] = uc_ref[STATE_PAD + t_s:STATE_PAD + t_s + CONV_W - 1, :]


BACK = 512
NFAR = BACK // QB - 1

def _natural_offset(p):
    p = p & (QB - 1)
    return DIL4 * (p & (CLS4 - 1)) + (p >> CLS4_SHIFT)


def _prompt_attn_kernel(q_ref, kvp_ref, kvc_ref, o16_ref, ga_ref, mix_ref, x_ref, wout_ref, npost_ref,
                        srow_ref, kvt_ref, ck_ref, cv_ref, sst_ref, cmk_ref, cmv_ref, wdw_ref,
                        y_ref, smix_ref, wk_ref, wv_ref, ns_ref,
                        mixa_ref, nat_ref, unp_ref, uc_ref, *, tq, t_s, win_len):
    i = pl.program_id(1)
    kv_refs = (kvp_ref, kvc_ref)

    for cl in range(DIL16):
        dst = pl.ds(cl, tq // DIL16, stride=DIL16)
        for g in range(NGRP + 1):
            nat_ref[g, dst, :] = o16_ref[0, cl, :, g * LANES:(g + 1) * LANES]

    a_n = _natural_offset(lax.broadcasted_iota(jnp.int32, (QB, 2 * QB), 0))
    c_p = lax.broadcasted_iota(jnp.int32, (QB, 2 * QB), 1)
    d = jnp.where(c_p < QB, QB, 0) + a_n - _natural_offset(c_p)
    near = (d >= 0) & (d <= QB)
    far = (d >= 0) & ((d & (DIL4 - 1)) == 0)
    bias_near0 = jnp.where(near & far, LN2, jnp.where(near | far, 0.0, NEG))
    ia = lax.broadcasted_iota(jnp.int32, (CLS4, QB), 0)
    cf = lax.broadcasted_iota(jnp.int32, (CLS4, QB), 1)
    fblk, ic = cf >> CLS4_SHIFT, cf & (CLS4 - 1)
    bias_far0 = jnp.where((fblk < NFAR) & ((fblk > 0) | (ic >= ia)), 0.0, NEG)
    lane = lax.broadcasted_iota(jnp.int32, (QB, LANES), 1)
    head_a = lane < HEAD_DIM

    def rows_of(w):
        blk, off = divmod(w, tq)
        return kv_refs[blk], off

    def group(x, c):
        return jnp.concatenate([x[c * CLS4:(c + 1) * CLS4], x[QB + c * CLS4:QB + (c + 1) * CLS4]], axis=0)

    def ungroup(parts):
        return jnp.concatenate([p_[:CLS4] for p_ in parts] + [p_[CLS4:] for p_ in parts], axis=0)

    units = [(sub, g) for sub in range(tq // QB) for g in range(NGRP)]
    ctx = {}
    for sub in range(tq // QB):
        first_valid = BACK - i * tq - sub * QB
        bias_near = jnp.where((c_p >= QB) | (NFAR * QB >= first_valid), bias_near0, NEG)
        bias_far = jnp.where(fblk * QB >= first_valid, bias_far0, NEG)
        far_blocks = [rows_of((sub + j) * QB) for j in range(NFAR)]
        ctx[sub] = dict(bias_near=jnp.concatenate([bias_near, bias_near], axis=0),
                        bias_far=jnp.concatenate([bias_far, bias_far], axis=0),
                        prev=rows_of((sub + NFAR) * QB), cur=rows_of((sub + NFAR + 1) * QB),
                        far=far_blocks + far_blocks[:1])

    def far_rows(sub, c, colsl):
        return jnp.concatenate([r[0, off + c * CLS4:off + (c + 1) * CLS4, colsl] for r, off in ctx[sub]["far"]],
                               axis=0)

    def block(sub, which, colsl):
        r, off = ctx[sub][which]
        return r[0, off:off + QB, colsl]

    scores, probs, outs = {}, {}, {}

    def stage_scores(u):
        sub, g = u
        kc = slice(P_K + g * LANES, P_K + (g + 1) * LANES)
        q2 = _pair_split(q_ref[0, sub * QB:(sub + 1) * QB, g * LANES:(g + 1) * LANES].astype(F32)).astype(BF16)
        s_far = ungroup([_dot_t(group(q2, c), far_rows(sub, c, kc)) + ctx[sub]["bias_far"] for c in range(DIL4)])
        s_near = jnp.concatenate([_dot_t(q2, block(sub, "prev", kc)), _dot_t(q2, block(sub, "cur", kc))],
                                 axis=1) + ctx[sub]["bias_near"]
        scores[u] = jnp.concatenate([s_far, s_near], axis=1)

    def stage_softmax(u):
        s = scores.pop(u)
        m = jnp.max(s, axis=-1, keepdims=True)
        p = jnp.exp(s - m)
        probs[u] = (m, jnp.sum(p, axis=-1, keepdims=True), p.astype(BF16))

    def stage_values(u):
        sub, g = u
        vc = slice(P_V + g * LANES, P_V + (g + 1) * LANES)
        pb = probs[u][2]
        outs[u] = (_dot(pb[:, QB:2 * QB], block(sub, "prev", vc)) + _dot(pb[:, 2 * QB:], block(sub, "cur", vc))
                   + ungroup([_dot(group(pb[:, :QB], c), far_rows(sub, c, vc)) for c in range(DIL4)]))

    def stage_merge(u):
        sub, g = u
        rows = slice(sub * QB, (sub + 1) * QB)
        cols = slice(g * LANES, (g + 1) * LANES)
        m, l, _ = probs.pop(u)
        o_s = _pair_join(outs.pop(u))
        m_s = jnp.where(head_a, m[:QB], m[QB:])
        l_s = jnp.where(head_a, l[:QB], l[QB:])
        for c in range(DIL4):
            dst = pl.ds(c, CLS4, stride=DIL4)
            unp_ref[3 * g, dst, :] = o_s[c * CLS4:(c + 1) * CLS4]
            unp_ref[3 * g + 1, dst, :] = m_s[c * CLS4:(c + 1) * CLS4]
            unp_ref[3 * g + 2, dst, :] = l_s[c * CLS4:(c + 1) * CLS4]
        o_n, m_n, l_n = unp_ref[3 * g], unp_ref[3 * g + 1], unp_ref[3 * g + 2]
        st = nat_ref[NGRP, rows, :]
        m_f = jnp.where(head_a, st[:, 2 * g:2 * g + 1], st[:, 2 * g + 1:2 * g + 2])
        l_f = jnp.where(head_a, st[:, STAT_L + 2 * g:STAT_L + 2 * g + 1],
                        st[:, STAT_L + 2 * g + 1:STAT_L + 2 * g + 2])
        mx = jnp.maximum(m_n, m_f)
        w_n = jnp.exp(m_n - mx)
        w_f = jnp.exp(m_f - mx)
        oa = (o_n * w_n + nat_ref[g, rows, :] * w_f) / (l_n * w_n + l_f * w_f)
        mixa_ref[rows, cols] = (oa * ga_ref[0, rows, cols]).astype(BF16)

    stages = (stage_scores, stage_softmax, stage_values, stage_merge)
    for t in range(len(units) + len(stages) - 1):
        for depth, stage in enumerate(stages):
            if 0 <= t - depth < len(units):
                stage(units[t - depth])

    z = _dot(mixa_ref[...], wout_ref[0:A_WIDTH, :]) + _dot(mix_ref[0], wout_ref[A_WIDTH:, :])
    y_ref[0] = x_ref[0] + _rmsnorm(z, npost_ref[...])

    for part in _sample_mix_stages(pl.program_id(0) * pl.num_programs(1) + i, srow_ref, kvt_ref, ck_ref, cv_ref,
                                   sst_ref, cmk_ref, cmv_ref, wdw_ref, smix_ref, wk_ref, wv_ref, ns_ref, uc_ref,
                                   t_s=t_s, win_len=win_len):
        part()


def _prompt_attn(kvq, o16, ga, mix, x, wout_bf, npost, srow, kvt, ck, cv, st_pad, cmk, cmv, wdw, tq, t_s):
    b, s, _ = x.shape
    nsteps = s // tq
    nb, _, win_len = ck.shape
    assert tq == BACK and b * nsteps == nb, "one sample batch per grid step"
    row = lambda w: pl.BlockSpec((1, tq, w), lambda bi, i: (bi, i, 0))
    qspec = pl.BlockSpec((1, tq, A_WIDTH), lambda bi, i: (bi, i, P_Q // A_WIDTH))
    kvprev = pl.BlockSpec((1, tq, 2 * A_WIDTH), lambda bi, i: (bi, jnp.maximum(i - 1, 0), 0))
    kvcur = pl.BlockSpec((1, tq, 2 * A_WIDTH), lambda bi, i: (bi, i, 0))
    cls = pl.BlockSpec((1, DIL16, tq // DIL16, O16_W), lambda bi, i: (bi, 0, i, 0))
    srows = pl.BlockSpec((t_s, S_ROW), lambda bi, i: (bi * nsteps + i, 0))
    per = lambda a: pl.BlockSpec((1,) + a.shape[1:], lambda bi, i: (bi * nsteps + i, 0, 0))
    return pl.pallas_call(
        functools.partial(_prompt_attn_kernel, tq=tq, t_s=t_s, win_len=win_len),
        grid=(b, nsteps),
        in_specs=[qspec, kvprev, kvcur, cls, row(A_WIDTH), row(C_WIDTH + M_WIDTH), row(D_MODEL),
                  _full(wout_bf), _full(npost),
                  srows, _full(kvt), per(ck), per(cv), per(st_pad), per(cmk), per(cmv), _full(wdw)],
        out_specs=[row(D_MODEL), srows, per(ck), per(cv),
                   pl.BlockSpec((1, CONV_W - 1, C_WIDTH), lambda bi, i: (bi * nsteps + i, 0, 0))],
        out_shape=[jax.ShapeDtypeStruct((b, s, D_MODEL), F32),
                   jax.ShapeDtypeStruct((nb * t_s, S_ROW), F32),
                   jax.ShapeDtypeStruct(ck.shape, F32), jax.ShapeDtypeStruct(cv.shape, F32),
                   jax.ShapeDtypeStruct((nb, CONV_W - 1, C_WIDTH), F32)],
        scratch_shapes=[pltpu.VMEM((tq, A_WIDTH), BF16),
                        pltpu.VMEM((NGRP + 1, tq, LANES), F32),
                        pltpu.VMEM((3 * NGRP, QB, LANES), F32),
                        pltpu.VMEM((STATE_PAD + CONV_W - 1 + t_s, C_WIDTH), F32)],
        compiler_params=pltpu.CompilerParams(dimension_semantics=("arbitrary", "arbitrary"),
                                             vmem_limit_bytes=VMEM_LIMIT),
        name="prompt_attn",
    )(kvq, kvq, kvq, o16, ga, mix, x, wout_bf, npost, srow, kvt, ck, cv, st_pad, cmk, cmv, wdw)


def _sample_proj_kernel(x_ref, npre_ref, win_ref, ang_ref, srow_ref, kvt_ref, gate_ref):
    h = _rmsnorm(x_ref[...], npre_ref[...]).astype(BF16)
    proj = _dot(h, win_ref[...])
    pats = _rope_patterns(ang_ref[:, 0:LANES], ang_ref[:, LANES:])
    srow_ref[:, S_Q:S_U] = _rope(proj[:, O_QA:O_KA], pats) * SCALE
    srow_ref[:, S_U:S_QM] = proj[:, O_AB:O_BB] * _sigmoid(proj[:, O_BB:O_GB])
    srow_ref[:, S_QM:S_ROW] = proj[:, O_QM:O_GM]
    kv_t = jnp.concatenate([_rope(proj[:, O_KA:O_VA], pats), proj[:, O_VA:O_GA]], axis=1).T
    for slab in range(kvt_ref.shape[0]):
        kvt_ref[slab] = kv_t[:, slab * LANES:(slab + 1) * LANES]
    gate_ref[:, S_Q:S_U] = _silu(proj[:, O_GA:O_AB])
    gate_ref[:, S_U:S_QM] = _silu(proj[:, O_GB:O_QM])
    gate_ref[:, S_QM:S_ROW] = _silu(proj[:, O_GM:D_IN])


def _sample_proj(x, npre, win_bf, ang):
    n = x.shape[0]
    assert n % LANES == 0
    shapes = [(n, S_ROW), (n // LANES, 2 * A_WIDTH, LANES), (n, S_ROW)]
    return pl.pallas_call(
        _sample_proj_kernel,
        grid=(1,),
        in_specs=[_full(x), _full(npre), _full(win_bf), _full(ang)],
        out_specs=[pl.BlockSpec(sh, lambda i, nd=len(sh): (0,) * nd) for sh in shapes],
        out_shape=[jax.ShapeDtypeStruct(sh, F32) for sh in shapes],
        compiler_params=pltpu.CompilerParams(vmem_limit_bytes=VMEM_LIMIT),
        name="sample_proj",
    )(x, npre, win_bf, ang)


def _sample_out_kernel(x_ref, smix_ref, gate_ref, bdw_ref, lng_ref, lnb_ref, wpw_ref, bpw_ref, wout_ref, npost_ref,
                       y_ref):
    mixed_a = (smix_ref[:, S_Q:S_U] * gate_ref[:, S_Q:S_U]).astype(BF16)
    mixed_b = _conformer_tail(smix_ref[:, S_U:S_QM], gate_ref[:, S_U:S_QM], bdw_ref[...], lng_ref[...], lnb_ref[...],
                              wpw_ref, bpw_ref[...]).astype(BF16)
    mixed_m = (smix_ref[:, S_QM:S_ROW] * gate_ref[:, S_QM:S_ROW]).astype(BF16)
    z = (_dot(mixed_a, wout_ref[0:A_WIDTH, :]) + _dot(mixed_b, wout_ref[A_WIDTH:A_WIDTH + C_WIDTH, :])
         + _dot(mixed_m, wout_ref[A_WIDTH + C_WIDTH:, :]))
    y_ref[...] = x_ref[...] + _rmsnorm(z, npost_ref[...])


def _sample_out(x, smix, gate, bdw, lng, lnb, wpw_bf, bpw, wout_bf, npost):
    args = (x, smix, gate, bdw, lng, lnb, wpw_bf, bpw, wout_bf, npost)
    return pl.pallas_call(
        _sample_out_kernel,
        grid=(1,),
        in_specs=[_full(a) for a in args],
        out_specs=_full(x),
        out_shape=jax.ShapeDtypeStruct(x.shape, F32),
        compiler_params=pltpu.CompilerParams(vmem_limit_bytes=VMEM_LIMIT),
        name="sample_out",
    )(*args)


def _rope_cos_sin(pos):
    inv = ROPE_THETA ** (-jnp.arange(0, ROT_DIM, 2, dtype=F32) / ROT_DIM)
    ang = pos.astype(F32)[:, None] * jnp.tile(inv, 2 * LANES // ROT_DIM)[None, :]
    return jnp.concatenate([jnp.cos(ang), jnp.sin(ang)], axis=1)


def _feature_major(cache):
    nb, rows, heads, dim = cache.shape
    return jnp.transpose(cache, (0, 2, 3, 1)).reshape(nb, heads * dim, rows)


def _row_major(cache_t, heads):
    nb, width, rows = cache_t.shape
    return jnp.transpose(cache_t.reshape(nb, heads, width // heads, rows), (0, 3, 1, 2))


PROJ_TILE = 512
ATTN_TILE = BACK


def kernel(x_prompt, x_sample, cache_win_k, cache_win_v, state_conv, cache_mem_k, cache_mem_v, mem_prompt,
           norm_pre, norm_post, w_in, w_out, norm_mem, w_mem_kv, w_dw, b_dw, ln_conv_g, ln_conv_b, w_pw2, b_pw2):
    depth = w_in.shape[0]
    assert depth == 1, "single-layer step"
    b, s, _ = x_prompt.shape
    nb, t_s, _ = x_sample.shape
    win_len = cache_win_k.shape[2]
    assert s % (DIL16 * QB) == 0 and s % PROJ_TILE == 0 and s % ATTN_TILE == 0
    assert win_len == MAX_WINDOW and win_len % LANES == 0
    assert t_s % 8 == 0 and t_s & (t_s - 1) == 0 and t_s < LANES
    l = 0
    row = lambda a: a[l][None, :]
    npre, npost, nmem = row(norm_pre), row(norm_post), row(norm_mem)
    bdw, lng, lnb, bpw = row(b_dw), row(ln_conv_g), row(ln_conv_b), row(b_pw2)
    win_bf, wout_bf = w_in[l].astype(BF16), w_out[l].astype(BF16)
    wmem_bf, wpw_bf = w_mem_kv[l].astype(BF16), w_pw2[l].astype(BF16)
    wdw = w_dw[l]

    pos_s = PAST_LEN + jnp.arange(t_s, dtype=jnp.int32)
    xs = x_sample.reshape(nb * t_s, D_MODEL)
    srow, kvt, gate_s = _sample_proj(xs, npre, win_bf, jnp.tile(_rope_cos_sin(pos_s), (nb, 1)))
    st_pad = jnp.pad(state_conv[l], ((0, 0), (STATE_PAD, 0), (0, 0)))

    mk, mv, mkb, mvb = _mem_kv(mem_prompt, nmem, wmem_bf)
    rin = _rope_cos_sin(jnp.arange(PROJ_TILE, dtype=jnp.int32))
    rbase = _rope_cos_sin(jnp.arange(0, s, PROJ_TILE, dtype=jnp.int32))[:, None, :]
    kvq, kvq16, kf, vf, ga, mix, ust = _prompt_proj(x_prompt, npre, win_bf, rin, rbase, mkb, mvb,
                                                    wdw, bdw, lng, lnb, wpw_bf, bpw, PROJ_TILE)
    o16 = _dilated16(kvq16)
    y_prompt, smix, wk, wv, nst = _prompt_attn(
        kvq, o16, ga, mix, x_prompt, wout_bf, npost, srow, kvt,
        _feature_major(cache_win_k[l]), _feature_major(cache_win_v[l]), st_pad,
        _feature_major(cache_mem_k[l]), _feature_major(cache_mem_v[l]), wdw, ATTN_TILE, t_s)
    y_sample = _sample_out(xs, smix, gate_s, bdw, lng, lnb, wpw_bf, bpw, wout_bf, npost)

    keep_p = kf.shape[1]
    return (y_prompt,
            y_sample.reshape(nb, t_s, D_MODEL),
            kf.reshape(1, b, keep_p, A_HEADS, HEAD_DIM),
            vf.reshape(1, b, keep_p, A_HEADS, HEAD_DIM),
            ust[:, CONV_HIST - (CONV_W - 1):, :][None],
            mk.reshape(1, b, N_MEM, M_HEADS, HEAD_DIM),
            mv.reshape(1, b, N_MEM, M_HEADS, HEAD_DIM),
            _row_major(wk, A_HEADS)[None],
            _row_major(wv, A_HEADS)[None],
            nst[None])
```

```python
import functools
import math

import jax
import jax.numpy as jnp
from jax import lax
from jax.experimental import pallas as pl
from jax.experimental.pallas import tpu as pltpu

F32 = jnp.float32
BF16 = jnp.bfloat16

D_MODEL = 1024
HEAD_DIM = 64
HEAD_SHIFT = 6
A_WIDTH = 384
A_HEADS = 6
M_WIDTH = 256
M_HEADS = 4
C_WIDTH = 384
ROT_DIM = 16
ROPE_THETA = 500000.0
CONV_W = 31
N_MEM = 256
MAX_WINDOW = 2048
PAST_LEN = 16384
EPS = 1e-6
SCALE = HEAD_DIM ** -0.5
NEG = -1e30
LN2 = math.log(2.0)

LANES = 128
NGRP = A_WIDTH // LANES
QB = 128
DIL16 = 16
DIL4 = 4
CLS4 = QB // DIL4
CLS4_SHIFT = 5
O_QA, O_KA, O_VA, O_GA, O_AB, O_BB, O_GB, O_QM, O_GM, D_IN = 0, 384, 768, 1152, 1536, 1920, 2304, 2688, 2944, 3200
P_K, P_V, P_Q, P_W = 0, A_WIDTH, 2 * A_WIDTH, 3 * A_WIDTH

VMEM_LIMIT = 60 * 1024 * 1024


def _sigmoid(x):
    return 1.0 / (1.0 + jnp.exp(-x))


def _silu(x):
    return x * _sigmoid(x)


def _rmsnorm(x, g):
    return x * lax.rsqrt(jnp.mean(x * x, axis=-1, keepdims=True) + EPS) * g


def _dot(a, b):
    return jnp.dot(a, b, preferred_element_type=F32)


def _dot_t(a, b):
    return lax.dot_general(a, b, (((1,), (1,)), ((), ())), preferred_element_type=F32)


def _rope_patterns(cos, sin):
    in_head = lax.broadcasted_iota(jnp.int32, cos.shape, 1) & (HEAD_DIM - 1)
    rot, lo = in_head < ROT_DIM, in_head < ROT_DIM // 2
    return jnp.where(rot, cos, 1.0), jnp.where(lo, -sin, 0.0), jnp.where(rot & ~lo, sin, 0.0)


def _rope(xw, pats):
    cos, s1, s2 = pats
    outs = []
    for g in range(xw.shape[1] // LANES):
        xg = xw[:, g * LANES:(g + 1) * LANES]
        outs.append(xg * cos + pltpu.roll(xg, LANES - 8, 1) * s1 + pltpu.roll(xg, 8, 1) * s2)
    return jnp.concatenate(outs, axis=1)


def _pair_split(q):
    lane = lax.broadcasted_iota(jnp.int32, q.shape, 1)
    qa = jnp.where(lane < HEAD_DIM, q, 0.0)
    qb = jnp.where(lane >= HEAD_DIM, q, 0.0)
    return jnp.concatenate([qa, qb], axis=0)


def _pair_join(x):
    t = x.shape[0] // 2
    lane = lax.broadcasted_iota(jnp.int32, (t, LANES), 1)
    return jnp.where(lane < HEAD_DIM, x[:t], x[t:])


def _mem_attend(qm, mk_ref, mv_ref):
    outs = []
    for g in range(M_WIDTH // LANES):
        cols = slice(g * LANES, (g + 1) * LANES)
        q2 = _pair_split(qm[:, cols] * SCALE).astype(BF16)
        s = _dot_t(q2, mk_ref[0, :, cols])
        m = jnp.max(s, axis=-1, keepdims=True)
        p = jnp.exp(s - m)
        l = jnp.sum(p, axis=-1, keepdims=True)
        o = _dot(p.astype(BF16), mv_ref[0, :, cols]) * (1.0 / l)
        outs.append(_pair_join(o))
    return jnp.concatenate(outs, axis=1)


def _conformer_tail(c, gate_b, bdw, lng, lnb, wpw_ref, bpw):
    cf = c + bdw
    mu = jnp.mean(cf, axis=-1, keepdims=True)
    dev = cf - mu
    var = jnp.mean(dev * dev, axis=-1, keepdims=True)
    cn = dev * lax.rsqrt(var + EPS) * lng + lnb
    ob = _dot(_silu(cn).astype(BF16), wpw_ref[...]) + bpw
    return ob * gate_b


def _full(a):
    return pl.BlockSpec(a.shape, lambda *_: (0,) * a.ndim)


def _mem_kv_kernel(mem_ref, g_ref, w_ref, mk_ref, mv_ref, mkb_ref, mvb_ref):
    h = _rmsnorm(mem_ref[0], g_ref[...]).astype(BF16)
    kv = _dot(h, w_ref[...])
    mk_ref[0] = kv[:, :M_WIDTH]
    mv_ref[0] = kv[:, M_WIDTH:]
    mkb_ref[0] = kv[:, :M_WIDTH].astype(BF16)
    mvb_ref[0] = kv[:, M_WIDTH:].astype(BF16)


def _mem_kv(mem, g, w_bf):
    b = mem.shape[0]
    blk = pl.BlockSpec((1, N_MEM, M_WIDTH), lambda i: (i, 0, 0))
    return pl.pallas_call(
        _mem_kv_kernel,
        grid=(b,),
        in_specs=[pl.BlockSpec((1, N_MEM, D_MODEL), lambda i: (i, 0, 0)), _full(g), _full(w_bf)],
        out_specs=[blk, blk, blk, blk],
        out_shape=[jax.ShapeDtypeStruct((b, N_MEM, M_WIDTH), F32)] * 2
        + [jax.ShapeDtypeStruct((b, N_MEM, M_WIDTH), BF16)] * 2,
        name="mem_kv",
    )(mem, g, w_bf)


CONV_HIST = 32
CONV_CHUNK = 64


def _prompt_proj_kernel(x_ref, npre_ref, win_ref, rin_ref, rbase_ref, mk_ref, mv_ref,
                        wdw_ref, bdw_ref, lng_ref, lnb_ref, wpw_ref, bpw_ref,
                        kvq_ref, kvq16_ref, kf_ref, vf_ref, ga_ref, mix_ref, ust_ref,
                        uext_ref, ush_ref, conv_ref, stage_ref):
    t = x_ref.shape[1]
    i = pl.program_id(1)

    @pl.when(i == 0)
    def _():
        uext_ref[0:CONV_HIST, :] = jnp.zeros((CONV_HIST, C_WIDTH), F32)

    h = _rmsnorm(x_ref[0], npre_ref[...]).astype(BF16)
    proj = {}

    def project(name, c0, c1):
        def run():
            proj[name] = _dot(h, win_ref[:, c0:c1])
        return run

    def stage(idx, val):
        for g in range(NGRP):
            stage_ref[NGRP * idx + g] = val[:, g * LANES:(g + 1) * LANES]

    def rope_pats():
        cr, sr = rin_ref[:, 0:LANES], rin_ref[:, LANES:]
        cb, sb = rbase_ref[i, :, 0:LANES], rbase_ref[i, :, LANES:]
        return _rope_patterns(cb * cr - sb * sr, sb * cr + cb * sr)

    def do_k():
        k = _rope(proj.pop("k"), rope_pats())
        kf_ref[0] = k
        stage(0, k)

    def do_q():
        stage(2, _rope(proj.pop("q"), rope_pats()) * SCALE)

    def do_v():
        v = proj.pop("v")
        vf_ref[0] = v
        stage(1, v)

    def do_ga():
        ga_ref[0] = _silu(proj.pop("ga"))

    def do_copies():
        for c in range(DIL16):
            rows_c = [stage_ref[s, pl.ds(c, t // DIL16, stride=DIL16), :] for s in range(3 * NGRP)]
            kvq16_ref[0, c] = jnp.concatenate(rows_c, axis=1).astype(BF16)
        for blk in range(t // QB):
            for c in range(DIL4):
                rows_c = [stage_ref[s, pl.ds(blk * QB + c, CLS4, stride=DIL4), :] for s in range(3 * NGRP)]
                kvq_ref[0, blk * QB + c * CLS4:blk * QB + (c + 1) * CLS4, :] = (
                    jnp.concatenate(rows_c, axis=1).astype(BF16))

    def do_glu():
        uext_ref[CONV_HIST:CONV_HIST + t, :] = proj.pop("ab") * _sigmoid(proj.pop("bb"))
        for r in range(1, 8):
            ush_ref[r - 1, 8:CONV_HIST + t, :] = uext_ref[8 - r:CONV_HIST + t - r, :]

    def conv_rows(r0):
        def run():
            acc = jnp.zeros((CONV_CHUNK, C_WIDTH), F32)
            for kk in range(CONV_W):
                tap = wdw_ref[CONV_W - 1 - kk:CONV_W - kk, :]
                r, base = kk % 8, CONV_HIST + r0 - (kk - kk % 8)
                if r == 0:
                    slab = uext_ref[base:base + CONV_CHUNK, :]
                else:
                    slab = ush_ref[r - 1, base:base + CONV_CHUNK, :]
                acc = acc + slab * tap
            conv_ref[r0:r0 + CONV_CHUNK, :] = acc
        return run

    def do_history():
        tail = uext_ref[t:t + CONV_HIST, :]
        uext_ref[0:CONV_HIST, :] = tail
        ust_ref[0] = tail

    def do_tail():
        mixed_b = _conformer_tail(conv_ref[...], _silu(proj.pop("gb")), bdw_ref[...], lng_ref[...], lnb_ref[...],
                                  wpw_ref, bpw_ref[...])
        mix_ref[0, :, 0:C_WIDTH] = mixed_b.astype(BF16)

    def do_mem():
        mixed_m = _mem_attend(proj.pop("qm"), mk_ref, mv_ref) * _silu(proj.pop("gm"))
        mix_ref[0, :, C_WIDTH:] = mixed_m.astype(BF16)

    convs = [conv_rows(r0) for r0 in range(0, t, CONV_CHUNK)]
    riders = [([project("k", O_KA, O_VA)], do_k), ([project("q", O_QA, O_KA)], do_q),
              ([project("v", O_VA, O_GA)], do_v), ([project("ga", O_GA, O_AB)], do_ga),
              ([project("qm", O_QM, O_GM), project("gm", O_GM, D_IN)], do_mem),
              ([project("gb", O_GB, O_QM)], do_copies)]
    program = [project("ab", O_AB, O_BB), project("bb", O_BB, O_GB), do_glu]
    for n, conv in enumerate(convs):
        if n < len(riders):
            program += riders[n][0]
        program.append(conv)
        if n < len(riders) and riders[n][1] is not None:
            program.append(riders[n][1])
    assert len(convs) >= len(riders)
    program += [do_history, do_tail]
    for piece in program:
        piece()
    assert not proj


def _prompt_proj(x, npre, win_bf, rin, rbase, mkb, mvb, wdw, bdw, lng, lnb, wpw_bf, bpw, tile):
    b, s, _ = x.shape
    nt = s // tile
    keep = min(MAX_WINDOW, s)
    first_keep = (s - keep) // tile
    row = lambda w: pl.BlockSpec((1, tile, w), lambda bi, i: (bi, i, 0))
    memb = pl.BlockSpec((1, N_MEM, M_WIDTH), lambda bi, i: (bi, 0, 0))
    keepb = pl.BlockSpec((1, tile, A_WIDTH), lambda bi, i: (bi, jnp.maximum(i - first_keep, 0), 0))
    cls = pl.BlockSpec((1, DIL16, tile // DIL16, P_W), lambda bi, i: (bi, 0, i, 0))
    return pl.pallas_call(
        _prompt_proj_kernel,
        grid=(b, nt),
        in_specs=[row(D_MODEL), _full(npre), _full(win_bf), _full(rin), _full(rbase),
                  memb, memb, _full(wdw), _full(bdw), _full(lng), _full(lnb), _full(wpw_bf), _full(bpw)],
        out_specs=[row(P_W), cls, keepb, keepb, row(A_WIDTH), row(C_WIDTH + M_WIDTH),
                   pl.BlockSpec((1, CONV_HIST, C_WIDTH), lambda bi, i: (bi, 0, 0))],
        out_shape=[jax.ShapeDtypeStruct((b, s, P_W), BF16),
                   jax.ShapeDtypeStruct((b, DIL16, s // DIL16, P_W), BF16),
                   jax.ShapeDtypeStruct((b, keep, A_WIDTH), F32),
                   jax.ShapeDtypeStruct((b, keep, A_WIDTH), F32),
                   jax.ShapeDtypeStruct((b, s, A_WIDTH), F32),
                   jax.ShapeDtypeStruct((b, s, C_WIDTH + M_WIDTH), BF16),
                   jax.ShapeDtypeStruct((b, CONV_HIST, C_WIDTH), F32)],
        scratch_shapes=[pltpu.VMEM((CONV_HIST + tile, C_WIDTH), F32),
                        pltpu.VMEM((7, CONV_HIST + tile, C_WIDTH), F32),
                        pltpu.VMEM((tile, C_WIDTH), F32),
                        pltpu.VMEM((3 * NGRP, tile, LANES), F32)],
        compiler_params=pltpu.CompilerParams(dimension_semantics=("arbitrary", "arbitrary"),
                                             vmem_limit_bytes=VMEM_LIMIT),
        name="prompt_proj",
    )(x, npre, win_bf, rin, rbase, mkb, mvb, wdw, bdw, lng, lnb, wpw_bf, bpw)


STAT_L = 8
O16_W = A_WIDTH + LANES


def _dilated16_kernel(kvq_ref, o_ref):
    a = lax.broadcasted_iota(jnp.int32, (QB, QB), 0)
    c = lax.broadcasted_iota(jnp.int32, (QB, QB), 1)
    tri_prev = jnp.where(c >= a, 0.0, NEG)
    tri_cur = jnp.where(c <= a, 0.0, NEG)
    bias_cur = jnp.concatenate([tri_cur, tri_cur], axis=0)
    both = jnp.concatenate([tri_prev, tri_cur], axis=1)
    bias_both = jnp.concatenate([both, both], axis=0)
    lane = lax.broadcasted_iota(jnp.int32, (QB, LANES), 1)
    nsub = kvq_ref.shape[2] // QB
    units = [(sub, g) for sub in range(nsub) for g in range(NGRP)]
    scores, probs = {}, {}
    stats = {sub: jnp.zeros((QB, LANES), F32) for sub in range(nsub)}

    def rows_of(sub):
        return slice(sub * QB, (sub + 1) * QB)

    def stage_scores(u):
        sub, g = u
        kc = slice(P_K + g * LANES, P_K + (g + 1) * LANES)
        qc = slice(P_Q + g * LANES, P_Q + (g + 1) * LANES)
        q2 = _pair_split(kvq_ref[0, 0, rows_of(sub), qc].astype(F32)).astype(BF16)
        s_cur = _dot_t(q2, kvq_ref[0, 0, rows_of(sub), kc])
        if sub == 0:
            scores[u] = s_cur + bias_cur
        else:
            scores[u] = jnp.concatenate([_dot_t(q2, kvq_ref[0, 0, rows_of(sub - 1), kc]), s_cur], axis=1) + bias_both

    def stage_softmax(u):
        s = scores.pop(u)
        m = jnp.max(s, axis=-1, keepdims=True)
        p = jnp.exp(s - m)
        probs[u] = (m, jnp.sum(p, axis=-1, keepdims=True), p.astype(BF16))

    def stage_values(u):
        sub, g = u
        vc = slice(P_V + g * LANES, P_V + (g + 1) * LANES)
        m, l, pb = probs.pop(u)
        if sub == 0:
            o = _dot(pb, kvq_ref[0, 0, rows_of(sub), vc])
        else:
            o = _dot(pb[:, :QB], kvq_ref[0, 0, rows_of(sub - 1), vc]) + _dot(pb[:, QB:], kvq_ref[0, 0, rows_of(sub), vc])
        o_ref[0, 0, rows_of(sub), g * LANES:(g + 1) * LANES] = _pair_join(o)
        st = stats[sub]
        st = jnp.where(lane == 2 * g, m[:QB], st)
        st = jnp.where(lane == 2 * g + 1, m[QB:], st)
        st = jnp.where(lane == STAT_L + 2 * g, l[:QB], st)
        stats[sub] = jnp.where(lane == STAT_L + 2 * g + 1, l[QB:], st)
        if g == NGRP - 1:
            o_ref[0, 0, rows_of(sub), A_WIDTH:] = stats.pop(sub)

    stages = (stage_scores, stage_softmax, stage_values)
    for t in range(len(units) + len(stages) - 1):
        for depth, stage in enumerate(stages):
            if 0 <= t - depth < len(units):
                stage(units[t - depth])


def _dilated16(kvq16):
    b, ncls, nj, w = kvq16.shape
    return pl.pallas_call(
        _dilated16_kernel,
        grid=(b, ncls),
        in_specs=[pl.BlockSpec((1, 1, nj, w), lambda bi, c: (bi, c, 0, 0))],
        out_specs=pl.BlockSpec((1, 1, nj, O16_W), lambda bi, c: (bi, c, 0, 0)),
        out_shape=jax.ShapeDtypeStruct((b, ncls, nj, O16_W), F32),
        compiler_params=pltpu.CompilerParams(dimension_semantics=("arbitrary",) * 2,
                                             vmem_limit_bytes=VMEM_LIMIT),
        name="dilated16",
    )(kvq16)


STATE_PAD = 2
S_Q, S_U, S_QM, S_ROW = 0, A_WIDTH, A_WIDTH + C_WIDTH, A_WIDTH + C_WIDTH + M_WIDTH


def _sample_mix_stages(batch, srow_ref, kvt_ref, ck_ref, cv_ref, st_ref, cmk_ref, cmv_ref, wdw_ref,
                       smix_ref, wk_ref, wv_ref, ns_ref, uc_ref, *, t_s, win_len):
    new0 = LANES - t_s
    per_slab = LANES // t_s
    slab, pos = batch // per_slab, batch % per_slab

    def new_rows(lo, hi):
        return pltpu.roll(kvt_ref[slab, lo:hi, :], new0 - pos * t_s, 1)

    def slide(src_ref, lo, dst_ref):
        keep = lax.broadcasted_iota(jnp.int32, (A_WIDTH, LANES), 1) < new0
        nch = win_len // LANES
        nxt = pltpu.roll(src_ref[0, :, 0:LANES], new0, 1)
        for c in range(nch):
            cur = nxt
            nxt = (pltpu.roll(src_ref[0, :, (c + 1) * LANES:(c + 2) * LANES], new0, 1) if c + 1 < nch
                   else new_rows(lo, lo + A_WIDTH))
            dst_ref[0, :, c * LANES:(c + 1) * LANES] = jnp.where(keep, cur, nxt)

    return [functools.partial(slide, ck_ref, 0, wk_ref), functools.partial(slide, cv_ref, A_WIDTH, wv_ref),
            functools.partial(_sample_attend, new_rows, srow_ref, ck_ref, cv_ref, smix_ref, t_s=t_s, win_len=win_len),
            functools.partial(_sample_mem_conv, srow_ref, st_ref, cmk_ref, cmv_ref, wdw_ref, smix_ref, ns_ref,
                              uc_ref, t_s=t_s)]


def _sample_attend(new_rows, srow_ref, ck_ref, cv_ref, smix_ref, *, t_s, win_len):
    new0 = LANES - t_s
    knt, vnt = new_rows(0, A_WIDTH), new_rows(A_WIDTH, 2 * A_WIDTH)
    q = srow_ref[:, S_Q:S_U]
    head_of_lane = lax.broadcasted_iota(jnp.int32, (t_s, A_WIDTH), 1) >> HEAD_SHIFT
    q6 = jnp.concatenate([jnp.where(head_of_lane == h, q, 0.0) for h in range(A_HEADS)], axis=0).astype(BF16)
    s_c = _dot(q6, ck_ref[0].astype(BF16))
    s_n = _dot(q6, knt.astype(BF16))

    def weights(shape, key0, lo):
        tq = lax.broadcasted_iota(jnp.int32, shape, 0) & (t_s - 1)
        key = lax.broadcasted_iota(jnp.int32, shape, 1) + key0
        d = win_len + tq - key
        ok = (d >= 0) & (key >= lo)
        w = ((d <= 128).astype(F32) + ((d <= 512) & ((d & 3) == 0)).astype(F32)
             + ((d <= 2048) & ((d & 15) == 0)).astype(F32))
        return jnp.where(ok, w, 0.0)

    w_c = weights(s_c.shape, 0, 0)
    w_n = weights(s_n.shape, win_len - new0, win_len)
    s_c = jnp.where(w_c > 0.0, s_c, NEG)
    s_n = jnp.where(w_n > 0.0, s_n, NEG)
    m = jnp.maximum(jnp.max(s_c, axis=-1, keepdims=True), jnp.max(s_n, axis=-1, keepdims=True))
    p_c = jnp.exp(s_c - m) * w_c
    p_n = jnp.exp(s_n - m) * w_n
    l = jnp.sum(p_c, axis=-1, keepdims=True) + jnp.sum(p_n, axis=-1, keepdims=True)
    o = (_dot_t(p_c.astype(BF16), cv_ref[0].astype(BF16)) + _dot_t(p_n.astype(BF16), vnt.astype(BF16))) * (1.0 / l)
    oa = jnp.zeros((t_s, A_WIDTH), F32)
    for h in range(A_HEADS):
        oa = oa + jnp.where(head_of_lane == h, o[h * t_s:(h + 1) * t_s], 0.0)
    smix_ref[:, S_Q:S_U] = oa


def _sample_mem_conv(srow_ref, st_ref, cmk_ref, cmv_ref, wdw_ref, smix_ref, ns_ref, uc_ref, *, t_s):
    qm = srow_ref[:, S_QM:S_ROW] * SCALE
    mhead = lax.broadcasted_iota(jnp.int32, (t_s, M_WIDTH), 1) >> HEAD_SHIFT
    q4 = jnp.concatenate([jnp.where(mhead == h, qm, 0.0) for h in range(M_HEADS)], axis=0).astype(BF16)
    sm = _dot(q4, cmk_ref[0].astype(BF16))
    mm = jnp.max(sm, axis=-1, keepdims=True)
    pm = jnp.exp(sm - mm)
    lm = jnp.sum(pm, axis=-1, keepdims=True)
    om4 = _dot_t(pm.astype(BF16), cmv_ref[0].astype(BF16)) * (1.0 / lm)
    om = jnp.zeros((t_s, M_WIDTH), F32)
    for h in range(M_HEADS):
        om = om + jnp.where(mhead == h, om4[h * t_s:(h + 1) * t_s], 0.0)
    smix_ref[:, S_QM:S_ROW] = om

    nst = st_ref.shape[1]
    uc_ref[0:nst, :] = st_ref[0]
    uc_ref[nst:nst + t_s, :] = srow_ref[:, S_U:S_QM]
    acc = jnp.zeros((t_s, C_WIDTH), F32)
    for w in range(CONV_W):
        acc = acc + uc_ref[STATE_PAD + w:STATE_PAD + w + t_s, :] * wdw_ref[w:w + 1, :]
    smix_ref[:, S_U:S_QM] = acc
    ns_ref[0] = uc_ref[STATE_PAD + t_s:STATE_PAD + t_s + CONV_W - 1, :]


BACK = 512
NFAR = BACK // QB - 1

def _natural_offset(p):
    p = p & (QB - 1)
    return DIL4 * (p & (CLS4 - 1)) + (p >> CLS4_SHIFT)


def _prompt_attn_kernel(q_ref, kvp_ref, kvc_ref, o16_ref, ga_ref, mix_ref, x_ref, wout_ref, npost_ref,
                        srow_ref, kvt_ref, ck_ref, cv_ref, sst_ref, cmk_ref, cmv_ref, wdw_ref,
                        y_ref, smix_ref, wk_ref, wv_ref, ns_ref,
                        mixa_ref, nat_ref, unp_ref, uc_ref, *, tq, t_s, win_len):
    i = pl.program_id(1)
    kv_refs = (kvp_ref, kvc_ref)

    for cl in range(DIL16):
        dst = pl.ds(cl, tq // DIL16, stride=DIL16)
        for g in range(NGRP + 1):
            nat_ref[g, dst, :] = o16_ref[0, cl, :, g * LANES:(g + 1) * LANES]

    a_n = _natural_offset(lax.broadcasted_iota(jnp.int32, (QB, 2 * QB), 0))
    c_p = lax.broadcasted_iota(jnp.int32, (QB, 2 * QB), 1)
    d = jnp.where(c_p < QB, QB, 0) + a_n - _natural_offset(c_p)
    near = (d >= 0) & (d <= QB)
    far = (d >= 0) & ((d & (DIL4 - 1)) == 0)
    bias_near0 = jnp.where(near & far, LN2, jnp.where(near | far, 0.0, NEG))
    ia = lax.broadcasted_iota(jnp.int32, (CLS4, QB), 0)
    cf = lax.broadcasted_iota(jnp.int32, (CLS4, QB), 1)
    fblk, ic = cf >> CLS4_SHIFT, cf & (CLS4 - 1)
    bias_far0 = jnp.where((fblk < NFAR) & ((fblk > 0) | (ic >= ia)), 0.0, NEG)
    lane = lax.broadcasted_iota(jnp.int32, (QB, LANES), 1)
    head_a = lane < HEAD_DIM

    def rows_of(w):
        blk, off = divmod(w, tq)
        return kv_refs[blk], off

    def group(x, c):
        return jnp.concatenate([x[c * CLS4:(c + 1) * CLS4], x[QB + c * CLS4:QB + (c + 1) * CLS4]], axis=0)

    def ungroup(parts):
        return jnp.concatenate([p_[:CLS4] for p_ in parts] + [p_[CLS4:] for p_ in parts], axis=0)

    units = [(sub, g) for sub in range(tq // QB) for g in range(NGRP)]
    ctx = {}
    for sub in range(tq // QB):
        first_valid = BACK - i * tq - sub * QB
        bias_near = jnp.where((c_p >= QB) | (NFAR * QB >= first_valid), bias_near0, NEG)
        bias_far = jnp.where(fblk * QB >= first_valid, bias_far0, NEG)
        far_blocks = [rows_of((sub + j) * QB) for j in range(NFAR)]
        ctx[sub] = dict(bias_near=jnp.concatenate([bias_near, bias_near], axis=0),
                        bias_far=jnp.concatenate([bias_far, bias_far], axis=0),
                        prev=rows_of((sub + NFAR) * QB), cur=rows_of((sub + NFAR + 1) * QB),
                        far=far_blocks + far_blocks[:1])

    def far_rows(sub, c, colsl):
        return jnp.concatenate([r[0, off + c * CLS4:off + (c + 1) * CLS4, colsl] for r, off in ctx[sub]["far"]],
                               axis=0)

    def block(sub, which, colsl):
        r, off = ctx[sub][which]
        return r[0, off:off + QB, colsl]

    scores, probs, outs = {}, {}, {}

    def stage_scores(u):
        sub, g = u
        kc = slice(P_K + g * LANES, P_K + (g + 1) * LANES)
        q2 = _pair_split(q_ref[0, sub * QB:(sub + 1) * QB, g * LANES:(g + 1) * LANES].astype(F32)).astype(BF16)
        s_far = ungroup([_dot_t(group(q2, c), far_rows(sub, c, kc)) + ctx[sub]["bias_far"] for c in range(DIL4)])
        s_near = jnp.concatenate([_dot_t(q2, block(sub, "prev", kc)), _dot_t(q2, block(sub, "cur", kc))],
                                 axis=1) + ctx[sub]["bias_near"]
        scores[u] = jnp.concatenate([s_far, s_near], axis=1)

    def stage_softmax(u):
        s = scores.pop(u)
        m = jnp.max(s, axis=-1, keepdims=True)
        p = jnp.exp(s - m)
        probs[u] = (m, jnp.sum(p, axis=-1, keepdims=True), p.astype(BF16))

    def stage_values(u):
        sub, g = u
        vc = slice(P_V + g * LANES, P_V + (g + 1) * LANES)
        pb = probs[u][2]
        outs[u] = (_dot(pb[:, QB:2 * QB], block(sub, "prev", vc)) + _dot(pb[:, 2 * QB:], block(sub, "cur", vc))
                   + ungroup([_dot(group(pb[:, :QB], c), far_rows(sub, c, vc)) for c in range(DIL4)]))

    def stage_merge(u):
        sub, g = u
        rows = slice(sub * QB, (sub + 1) * QB)
        cols = slice(g * LANES, (g + 1) * LANES)
        m, l, _ = probs.pop(u)
        o_s = _pair_join(outs.pop(u))
        m_s = jnp.where(head_a, m[:QB], m[QB:])
        l_s = jnp.where(head_a, l[:QB], l[QB:])
        for c in range(DIL4):
            dst = pl.ds(c, CLS4, stride=DIL4)
            unp_ref[3 * g, dst, :] = o_s[c * CLS4:(c + 1) * CLS4]
            unp_ref[3 * g + 1, dst, :] = m_s[c * CLS4:(c + 1) * CLS4]
            unp_ref[3 * g + 2, dst, :] = l_s[c * CLS4:(c + 1) * CLS4]
        o_n, m_n, l_n = unp_ref[3 * g], unp_ref[3 * g + 1], unp_ref[3 * g + 2]
        st = nat_ref[NGRP, rows, :]
        m_f = jnp.where(head_a, st[:, 2 * g:2 * g + 1], st[:, 2 * g + 1:2 * g + 2])
        l_f = jnp.where(head_a, st[:, STAT_L + 2 * g:STAT_L + 2 * g + 1],
                        st[:, STAT_L + 2 * g + 1:STAT_L + 2 * g + 2])
        mx = jnp.maximum(m_n, m_f)
        w_n = jnp.exp(m_n - mx)
        w_f = jnp.exp(m_f - mx)
        oa = (o_n * w_n + nat_ref[g, rows, :] * w_f) / (l_n * w_n + l_f * w_f)
        mixa_ref[rows, cols] = (oa * ga_ref[0, rows, cols]).astype(BF16)

    def finish(rows):
        z = _dot(mixa_ref[rows, :], wout_ref[0:A_WIDTH, :]) + _dot(mix_ref[0, rows, :], wout_ref[A_WIDTH:, :])
        y_ref[0, rows, :] = x_ref[0, rows, :] + _rmsnorm(z, npost_ref[...])

    stages = (stage_scores, stage_softmax, stage_values, stage_merge)
    half = len(units) // 2
    for t in range(len(units) + len(stages) - 1):
        for depth, stage in enumerate(stages):
            if 0 <= t - depth < len(units):
                stage(units[t - depth])
        if t - (len(stages) - 1) == half - 1:
            finish(slice(0, tq // 2))
    finish(slice(tq // 2, tq))

    for part in _sample_mix_stages(pl.program_id(0) * pl.num_programs(1) + i, srow_ref, kvt_ref, ck_ref, cv_ref,
                                   sst_ref, cmk_ref, cmv_ref, wdw_ref, smix_ref, wk_ref, wv_ref, ns_ref, uc_ref,
                                   t_s=t_s, win_len=win_len):
        part()


def _prompt_attn(kvq, o16, ga, mix, x, wout_bf, npost, srow, kvt, ck, cv, st_pad, cmk, cmv, wdw, tq, t_s):
    b, s, _ = x.shape
    nsteps = s // tq
    nb, _, win_len = ck.shape
    assert tq == BACK and b * nsteps == nb, "one sample batch per grid step"
    row = lambda w: pl.BlockSpec((1, tq, w), lambda bi, i: (bi, i, 0))
    qspec = pl.BlockSpec((1, tq, A_WIDTH), lambda bi, i: (bi, i, P_Q // A_WIDTH))
    kvprev = pl.BlockSpec((1, tq, 2 * A_WIDTH), lambda bi, i: (bi, jnp.maximum(i - 1, 0), 0))
    kvcur = pl.BlockSpec((1, tq, 2 * A_WIDTH), lambda bi, i: (bi, i, 0))
    cls = pl.BlockSpec((1, DIL16, tq // DIL16, O16_W), lambda bi, i: (bi, 0, i, 0))
    srows = pl.BlockSpec((t_s, S_ROW), lambda bi, i: (bi * nsteps + i, 0))
    per = lambda a: pl.BlockSpec((1,) + a.shape[1:], lambda bi, i: (bi * nsteps + i, 0, 0))
    return pl.pallas_call(
        functools.partial(_prompt_attn_kernel, tq=tq, t_s=t_s, win_len=win_len),
        grid=(b, nsteps),
        in_specs=[qspec, kvprev, kvcur, cls, row(A_WIDTH), row(C_WIDTH + M_WIDTH), row(D_MODEL),
                  _full(wout_bf), _full(npost),
                  srows, _full(kvt), per(ck), per(cv), per(st_pad), per(cmk), per(cmv), _full(wdw)],
        out_specs=[row(D_MODEL), srows, per(ck), per(cv),
                   pl.BlockSpec((1, CONV_W - 1, C_WIDTH), lambda bi, i: (bi * nsteps + i, 0, 0))],
        out_shape=[jax.ShapeDtypeStruct((b, s, D_MODEL), F32),
                   jax.ShapeDtypeStruct((nb * t_s, S_ROW), F32),
                   jax.ShapeDtypeStruct(ck.shape, F32), jax.ShapeDtypeStruct(cv.shape, F32),
                   jax.ShapeDtypeStruct((nb, CONV_W - 1, C_WIDTH), F32)],
        scratch_shapes=[pltpu.VMEM((tq, A_WIDTH), BF16),
                        pltpu.VMEM((NGRP + 1, tq, LANES), F32),
                        pltpu.VMEM((3 * NGRP, QB, LANES), F32),
                        pltpu.VMEM((STATE_PAD + CONV_W - 1 + t_s, C_WIDTH), F32)],
        compiler_params=pltpu.CompilerParams(dimension_semantics=("arbitrary", "arbitrary"),
                                             vmem_limit_bytes=VMEM_LIMIT),
        name="prompt_attn",
    )(kvq, kvq, kvq, o16, ga, mix, x, wout_bf, npost, srow, kvt, ck, cv, st_pad, cmk, cmv, wdw)


def _sample_proj_kernel(x_ref, npre_ref, win_ref, ang_ref, srow_ref, kvt_ref, gate_ref):
    h = _rmsnorm(x_ref[...], npre_ref[...]).astype(BF16)
    proj = _dot(h, win_ref[...])
    pats = _rope_patterns(ang_ref[:, 0:LANES], ang_ref[:, LANES:])
    srow_ref[:, S_Q:S_U] = _rope(proj[:, O_QA:O_KA], pats) * SCALE
    srow_ref[:, S_U:S_QM] = proj[:, O_AB:O_BB] * _sigmoid(proj[:, O_BB:O_GB])
    srow_ref[:, S_QM:S_ROW] = proj[:, O_QM:O_GM]
    kv_t = jnp.concatenate([_rope(proj[:, O_KA:O_VA], pats), proj[:, O_VA:O_GA]], axis=1).T
    for slab in range(kvt_ref.shape[0]):
        kvt_ref[slab] = kv_t[:, slab * LANES:(slab + 1) * LANES]
    gate_ref[:, S_Q:S_U] = _silu(proj[:, O_GA:O_AB])
    gate_ref[:, S_U:S_QM] = _silu(proj[:, O_GB:O_QM])
    gate_ref[:, S_QM:S_ROW] = _silu(proj[:, O_GM:D_IN])


def _sample_proj(x, npre, win_bf, ang):
    n = x.shape[0]
    assert n % LANES == 0
    shapes = [(n, S_ROW), (n // LANES, 2 * A_WIDTH, LANES), (n, S_ROW)]
    return pl.pallas_call(
        _sample_proj_kernel,
        grid=(1,),
        in_specs=[_full(x), _full(npre), _full(win_bf), _full(ang)],
        out_specs=[pl.BlockSpec(sh, lambda i, nd=len(sh): (0,) * nd) for sh in shapes],
        out_shape=[jax.ShapeDtypeStruct(sh, F32) for sh in shapes],
        compiler_params=pltpu.CompilerParams(vmem_limit_bytes=VMEM_LIMIT),
        name="sample_proj",
    )(x, npre, win_bf, ang)


def _sample_out_kernel(x_ref, smix_ref, gate_ref, bdw_ref, lng_ref, lnb_ref, wpw_ref, bpw_ref, wout_ref, npost_ref,
                       y_ref):
    mixed_a = (smix_ref[:, S_Q:S_U] * gate_ref[:, S_Q:S_U]).astype(BF16)
    mixed_b = _conformer_tail(smix_ref[:, S_U:S_QM], gate_ref[:, S_U:S_QM], bdw_ref[...], lng_ref[...], lnb_ref[...],
                              wpw_ref, bpw_ref[...]).astype(BF16)
    mixed_m = (smix_ref[:, S_QM:S_ROW] * gate_ref[:, S_QM:S_ROW]).astype(BF16)
    z = (_dot(mixed_a, wout_ref[0:A_WIDTH, :]) + _dot(mixed_b, wout_ref[A_WIDTH:A_WIDTH + C_WIDTH, :])
         + _dot(mixed_m, wout_ref[A_WIDTH + C_WIDTH:, :]))
    y_ref[...] = x_ref[...] + _rmsnorm(z, npost_ref[...])


def _sample_out(x, smix, gate, bdw, lng, lnb, wpw_bf, bpw, wout_bf, npost):
    args = (x, smix, gate, bdw, lng, lnb, wpw_bf, bpw, wout_bf, npost)
    return pl.pallas_call(
        _sample_out_kernel,
        grid=(1,),
        in_specs=[_full(a) for a in args],
        out_specs=_full(x),
        out_shape=jax.ShapeDtypeStruct(x.shape, F32),
        compiler_params=pltpu.CompilerParams(vmem_limit_bytes=VMEM_LIMIT),
        name="sample_out",
    )(*args)


def _rope_cos_sin(pos):
    inv = ROPE_THETA ** (-jnp.arange(0, ROT_DIM, 2, dtype=F32) / ROT_DIM)
    ang = pos.astype(F32)[:, None] * jnp.tile(inv, 2 * LANES // ROT_DIM)[None, :]
    return jnp.concatenate([jnp.cos(ang), jnp.sin(ang)], axis=1)


def _feature_major(cache):
    nb, rows, heads, dim = cache.shape
    return jnp.transpose(cache, (0, 2, 3, 1)).reshape(nb, heads * dim, rows)


def _row_major(cache_t, heads):
    nb, width, rows = cache_t.shape
    return jnp.transpose(cache_t.reshape(nb, heads, width // heads, rows), (0, 3, 1, 2))


PROJ_TILE = 512
ATTN_TILE = BACK


def kernel(x_prompt, x_sample, cache_win_k, cache_win_v, state_conv, cache_mem_k, cache_mem_v, mem_prompt,
           norm_pre, norm_post, w_in, w_out, norm_mem, w_mem_kv, w_dw, b_dw, ln_conv_g, ln_conv_b, w_pw2, b_pw2):
    depth = w_in.shape[0]
    assert depth == 1, "single-layer step"
    b, s, _ = x_prompt.shape
    nb, t_s, _ = x_sample.shape
    win_len = cache_win_k.shape[2]
    assert s % (DIL16 * QB) == 0 and s % PROJ_TILE == 0 and s % ATTN_TILE == 0
    assert win_len == MAX_WINDOW and win_len % LANES == 0
    assert t_s % 8 == 0 and t_s & (t_s - 1) == 0 and t_s < LANES
    l = 0
    row = lambda a: a[l][None, :]
    npre, npost, nmem = row(norm_pre), row(norm_post), row(norm_mem)
    bdw, lng, lnb, bpw = row(b_dw), row(ln_conv_g), row(ln_conv_b), row(b_pw2)
    win_bf, wout_bf = w_in[l].astype(BF16), w_out[l].astype(BF16)
    wmem_bf, wpw_bf = w_mem_kv[l].astype(BF16), w_pw2[l].astype(BF16)
    wdw = w_dw[l]

    pos_s = PAST_LEN + jnp.arange(t_s, dtype=jnp.int32)
    xs = x_sample.reshape(nb * t_s, D_MODEL)
    srow, kvt, gate_s = _sample_proj(xs, npre, win_bf, jnp.tile(_rope_cos_sin(pos_s), (nb, 1)))
    st_pad = jnp.pad(state_conv[l], ((0, 0), (STATE_PAD, 0), (0, 0)))

    mk, mv, mkb, mvb = _mem_kv(mem_prompt, nmem, wmem_bf)
    rin = _rope_cos_sin(jnp.arange(PROJ_TILE, dtype=jnp.int32))
    rbase = _rope_cos_sin(jnp.arange(0, s, PROJ_TILE, dtype=jnp.int32))[:, None, :]
    kvq, kvq16, kf, vf, ga, mix, ust = _prompt_proj(x_prompt, npre, win_bf, rin, rbase, mkb, mvb,
                                                    wdw, bdw, lng, lnb, wpw_bf, bpw, PROJ_TILE)
    o16 = _dilated16(kvq16)
    y_prompt, smix, wk, wv, nst = _prompt_attn(
        kvq, o16, ga, mix, x_prompt, wout_bf, npost, srow, kvt,
        _feature_major(cache_win_k[l]), _feature_major(cache_win_v[l]), st_pad,
        _feature_major(cache_mem_k[l]), _feature_major(cache_mem_v[l]), wdw, ATTN_TILE, t_s)
    y_sample = _sample_out(xs, smix, gate_s, bdw, lng, lnb, wpw_bf, bpw, wout_bf, npost)

    keep_p = kf.shape[1]
    return (y_prompt,
            y_sample.reshape(nb, t_s, D_MODEL),
            kf.reshape(1, b, keep_p, A_HEADS, HEAD_DIM),
            vf.reshape(1, b, keep_p, A_HEADS, HEAD_DIM),
            ust[:, CONV_HIST - (CONV_W - 1):, :][None],
            mk.reshape(1, b, N_MEM, M_HEADS, HEAD_DIM),
            mv.reshape(1, b, N_MEM, M_HEADS, HEAD_DIM),
            _row_major(wk, A_HEADS)[None],
            _row_major(wv, A_HEADS)[None],
            nst[None])
```

```python
import functools
import math

import jax
import jax.numpy as jnp
from jax import lax
from jax.experimental import pallas as pl
from jax.experimental.pallas import tpu as pltpu

F32 = jnp.float32
BF16 = jnp.bfloat16

D_MODEL = 1024
HEAD_DIM = 64
HEAD_SHIFT = 6
A_WIDTH = 384
A_HEADS = 6
M_WIDTH = 256
M_HEADS = 4
C_WIDTH = 384
ROT_DIM = 16
ROPE_THETA = 500000.0
CONV_W = 31
N_MEM = 256
MAX_WINDOW = 2048
PAST_LEN = 16384
EPS = 1e-6
SCALE = HEAD_DIM ** -0.5
NEG = -1e30
LN2 = math.log(2.0)

LANES = 128
NGRP = A_WIDTH // LANES
QB = 128
DIL16 = 16
DIL4 = 4
CLS4 = QB // DIL4
CLS4_SHIFT = 5
O_QA, O_KA, O_VA, O_GA, O_AB, O_BB, O_GB, O_QM, O_GM, D_IN = 0, 384, 768, 1152, 1536, 1920, 2304, 2688, 2944, 3200
P_K, P_V, P_Q, P_W = 0, A_WIDTH, 2 * A_WIDTH, 3 * A_WIDTH

VMEM_LIMIT = 60 * 1024 * 1024


def _sigmoid(x):
    return 1.0 / (1.0 + jnp.exp(-x))


def _silu(x):
    return x * _sigmoid(x)


def _rmsnorm(x, g):
    return x * lax.rsqrt(jnp.mean(x * x, axis=-1, keepdims=True) + EPS) * g


def _dot(a, b):
    return jnp.dot(a, b, preferred_element_type=F32)


def _dot_t(a, b):
    return lax.dot_general(a, b, (((1,), (1,)), ((), ())), preferred_element_type=F32)


def _rope_patterns(cos, sin):
    in_head = lax.broadcasted_iota(jnp.int32, cos.shape, 1) & (HEAD_DIM - 1)
    rot, lo = in_head < ROT_DIM, in_head < ROT_DIM // 2
    return jnp.where(rot, cos, 1.0), jnp.where(lo, -sin, 0.0), jnp.where(rot & ~lo, sin, 0.0)


def _rope(xw, pats):
    cos, s1, s2 = pats
    outs = []
    for g in range(xw.shape[1] // LANES):
        xg = xw[:, g * LANES:(g + 1) * LANES]
        outs.append(xg * cos + pltpu.roll(xg, LANES - 8, 1) * s1 + pltpu.roll(xg, 8, 1) * s2)
    return jnp.concatenate(outs, axis=1)


def _pair_split(q):
    lane = lax.broadcasted_iota(jnp.int32, q.shape, 1)
    qa = jnp.where(lane < HEAD_DIM, q, 0.0)
    qb = jnp.where(lane >= HEAD_DIM, q, 0.0)
    return jnp.concatenate([qa, qb], axis=0)


def _pair_join(x):
    t = x.shape[0] // 2
    lane = lax.broadcasted_iota(jnp.int32, (t, LANES), 1)
    return jnp.where(lane < HEAD_DIM, x[:t], x[t:])


def _mem_attend(qm, mk_ref, mv_ref):
    outs = []
    for g in range(M_WIDTH // LANES):
        cols = slice(g * LANES, (g + 1) * LANES)
        q2 = _pair_split(qm[:, cols] * SCALE).astype(BF16)
        s = _dot_t(q2, mk_ref[0, :, cols])
        m = jnp.max(s, axis=-1, keepdims=True)
        p = jnp.exp(s - m)
        l = jnp.sum(p, axis=-1, keepdims=True)
        o = _dot(p.astype(BF16), mv_ref[0, :, cols]) * (1.0 / l)
        outs.append(_pair_join(o))
    return jnp.concatenate(outs, axis=1)


def _conformer_tail(c, gate_b, bdw, lng, lnb, wpw_ref, bpw):
    cf = c + bdw
    mu = jnp.mean(cf, axis=-1, keepdims=True)
    dev = cf - mu
    var = jnp.mean(dev * dev, axis=-1, keepdims=True)
    cn = dev * lax.rsqrt(var + EPS) * lng + lnb
    ob = _dot(_silu(cn).astype(BF16), wpw_ref[...].astype(BF16)) + bpw
    return ob * gate_b


def _full(a):
    return pl.BlockSpec(a.shape, lambda *_: (0,) * a.ndim)


def _resident(a):
    return pl.BlockSpec(a.shape, lambda *_: (0,) * a.ndim, pipeline_mode=pl.Buffered(1))


def _mem_kv_kernel(mem_ref, g_ref, w_ref, mk_ref, mv_ref, mkb_ref, mvb_ref):
    h = _rmsnorm(mem_ref[0], g_ref[...]).astype(BF16)
    kv = _dot(h, w_ref[...].astype(BF16))
    mk_ref[0] = kv[:, :M_WIDTH]
    mv_ref[0] = kv[:, M_WIDTH:]
    mkb_ref[0] = kv[:, :M_WIDTH].astype(BF16)
    mvb_ref[0] = kv[:, M_WIDTH:].astype(BF16)


def _mem_kv(mem, g, wmem):
    b = mem.shape[0]
    blk = pl.BlockSpec((1, N_MEM, M_WIDTH), lambda i: (i, 0, 0))
    return pl.pallas_call(
        _mem_kv_kernel,
        grid=(b,),
        in_specs=[pl.BlockSpec((1, N_MEM, D_MODEL), lambda i: (i, 0, 0)), _full(g), _full(wmem)],
        out_specs=[blk, blk, blk, blk],
        out_shape=[jax.ShapeDtypeStruct((b, N_MEM, M_WIDTH), F32)] * 2
        + [jax.ShapeDtypeStruct((b, N_MEM, M_WIDTH), BF16)] * 2,
        name="mem_kv",
    )(mem, g, wmem)


CONV_HIST = 32
CONV_CHUNK = 64


def _prompt_proj_kernel(x_ref, npre_ref, win_ref, rin_ref, rbase_ref, mk_ref, mv_ref,
                        wdw_ref, bdw_ref, lng_ref, lnb_ref, wpw_ref, bpw_ref,
                        kvq_ref, kvq16_ref, kf_ref, vf_ref, ga_ref, mix_ref, ust_ref,
                        uext_ref, ush_ref, conv_ref, stage_ref, wbf_ref):
    t = x_ref.shape[1]
    i = pl.program_id(1)

    @pl.when((pl.program_id(0) == 0) & (i == 0))
    def _():
        wbf_ref[...] = win_ref[...].astype(BF16)

    @pl.when(i == 0)
    def _():
        uext_ref[0:CONV_HIST, :] = jnp.zeros((CONV_HIST, C_WIDTH), F32)

    h = _rmsnorm(x_ref[0], npre_ref[...]).astype(BF16)
    proj = {}

    def project(name, c0, c1):
        def run():
            proj[name] = _dot(h, wbf_ref[:, c0:c1])
        return run

    def stage(idx, val):
        for g in range(NGRP):
            stage_ref[NGRP * idx + g] = val[:, g * LANES:(g + 1) * LANES]

    def rope_pats():
        cr, sr = rin_ref[:, 0:LANES], rin_ref[:, LANES:]
        cb, sb = rbase_ref[i, :, 0:LANES], rbase_ref[i, :, LANES:]
        return _rope_patterns(cb * cr - sb * sr, sb * cr + cb * sr)

    def do_k():
        k = _rope(proj.pop("k"), rope_pats())
        kf_ref[0] = k
        stage(0, k)

    def do_q():
        stage(2, _rope(proj.pop("q"), rope_pats()) * SCALE)

    def do_v():
        v = proj.pop("v")
        vf_ref[0] = v
        stage(1, v)

    def do_ga():
        ga_ref[0] = _silu(proj.pop("ga"))

    def do_copies():
        for c in range(DIL16):
            rows_c = [stage_ref[s, pl.ds(c, t // DIL16, stride=DIL16), :] for s in range(3 * NGRP)]
            kvq16_ref[0, c] = jnp.concatenate(rows_c, axis=1).astype(BF16)
        for blk in range(t // QB):
            for c in range(DIL4):
                rows_c = [stage_ref[s, pl.ds(blk * QB + c, CLS4, stride=DIL4), :] for s in range(3 * NGRP)]
                kvq_ref[0, blk * QB + c * CLS4:blk * QB + (c + 1) * CLS4, :] = (
                    jnp.concatenate(rows_c, axis=1).astype(BF16))

    def do_glu():
        uext_ref[CONV_HIST:CONV_HIST + t, :] = proj.pop("ab") * _sigmoid(proj.pop("bb"))
        for r in range(1, 8):
            ush_ref[r - 1, 8:CONV_HIST + t, :] = uext_ref[8 - r:CONV_HIST + t - r, :]

    def conv_rows(r0):
        def run():
            acc = jnp.zeros((CONV_CHUNK, C_WIDTH), F32)
            for kk in range(CONV_W):
                tap = wdw_ref[CONV_W - 1 - kk:CONV_W - kk, :]
                r, base = kk % 8, CONV_HIST + r0 - (kk - kk % 8)
                if r == 0:
                    slab = uext_ref[base:base + CONV_CHUNK, :]
                else:
                    slab = ush_ref[r - 1, base:base + CONV_CHUNK, :]
                acc = acc + slab * tap
            conv_ref[r0:r0 + CONV_CHUNK, :] = acc
        return run

    def do_history():
        tail = uext_ref[t:t + CONV_HIST, :]
        uext_ref[0:CONV_HIST, :] = tail
        ust_ref[0] = tail

    def do_tail():
        mixed_b = _conformer_tail(conv_ref[...], _silu(proj.pop("gb")), bdw_ref[...], lng_ref[...], lnb_ref[...],
                                  wpw_ref, bpw_ref[...])
        mix_ref[0, :, 0:C_WIDTH] = mixed_b.astype(BF16)

    def do_mem():
        mixed_m = _mem_attend(proj.pop("qm"), mk_ref, mv_ref) * _silu(proj.pop("gm"))
        mix_ref[0, :, C_WIDTH:] = mixed_m.astype(BF16)

    convs = [conv_rows(r0) for r0 in range(0, t, CONV_CHUNK)]
    riders = [([project("k", O_KA, O_VA)], do_k), ([project("q", O_QA, O_KA)], do_q),
              ([project("v", O_VA, O_GA)], do_v), ([project("ga", O_GA, O_AB)], do_ga),
              ([project("qm", O_QM, O_GM), project("gm", O_GM, D_IN)], do_mem),
              ([project("gb", O_GB, O_QM)], do_copies)]
    program = [project("ab", O_AB, O_BB), project("bb", O_BB, O_GB), do_glu]
    for n, conv in enumerate(convs):
        if n < len(riders):
            program += riders[n][0]
        program.append(conv)
        if n < len(riders) and riders[n][1] is not None:
            program.append(riders[n][1])
    assert len(convs) >= len(riders)
    program += [do_history, do_tail]
    for piece in program:
        piece()
    assert not proj


def _prompt_proj(x, npre, win, rin, rbase, mkb, mvb, wdw, bdw, lng, lnb, wpw, bpw, tile):
    b, s, _ = x.shape
    nt = s // tile
    keep = min(MAX_WINDOW, s)
    first_keep = (s - keep) // tile
    row = lambda w: pl.BlockSpec((1, tile, w), lambda bi, i: (bi, i, 0))
    memb = pl.BlockSpec((1, N_MEM, M_WIDTH), lambda bi, i: (bi, 0, 0))
    keepb = pl.BlockSpec((1, tile, A_WIDTH), lambda bi, i: (bi, jnp.maximum(i - first_keep, 0), 0))
    cls = pl.BlockSpec((1, DIL16, tile // DIL16, P_W), lambda bi, i: (bi, 0, i, 0))
    return pl.pallas_call(
        _prompt_proj_kernel,
        grid=(b, nt),
        in_specs=[row(D_MODEL), _full(npre), _resident(win), _full(rin), _full(rbase),
                  memb, memb, _full(wdw), _full(bdw), _full(lng), _full(lnb), _full(wpw), _full(bpw)],
        out_specs=[row(P_W), cls, keepb, keepb, row(A_WIDTH), row(C_WIDTH + M_WIDTH),
                   pl.BlockSpec((1, CONV_HIST, C_WIDTH), lambda bi, i: (bi, 0, 0))],
        out_shape=[jax.ShapeDtypeStruct((b, s, P_W), BF16),
                   jax.ShapeDtypeStruct((b, DIL16, s // DIL16, P_W), BF16),
                   jax.ShapeDtypeStruct((b, keep, A_WIDTH), F32),
                   jax.ShapeDtypeStruct((b, keep, A_WIDTH), F32),
                   jax.ShapeDtypeStruct((b, s, A_WIDTH), F32),
                   jax.ShapeDtypeStruct((b, s, C_WIDTH + M_WIDTH), BF16),
                   jax.ShapeDtypeStruct((b, CONV_HIST, C_WIDTH), F32)],
        scratch_shapes=[pltpu.VMEM((CONV_HIST + tile, C_WIDTH), F32),
                        pltpu.VMEM((7, CONV_HIST + tile, C_WIDTH), F32),
                        pltpu.VMEM((tile, C_WIDTH), F32),
                        pltpu.VMEM((3 * NGRP, tile, LANES), F32),
                        pltpu.VMEM(win.shape, BF16)],
        compiler_params=pltpu.CompilerParams(dimension_semantics=("arbitrary", "arbitrary"),
                                             vmem_limit_bytes=VMEM_LIMIT),
        name="prompt_proj",
    )(x, npre, win, rin, rbase, mkb, mvb, wdw, bdw, lng, lnb, wpw, bpw)


STAT_L = 8
O16_W = A_WIDTH + LANES


def _dilated16_kernel(kvq_ref, o_ref):
    a = lax.broadcasted_iota(jnp.int32, (QB, QB), 0)
    c = lax.broadcasted_iota(jnp.int32, (QB, QB), 1)
    tri_prev = jnp.where(c >= a, 0.0, NEG)
    tri_cur = jnp.where(c <= a, 0.0, NEG)
    bias_cur = jnp.concatenate([tri_cur, tri_cur], axis=0)
    both = jnp.concatenate([tri_prev, tri_cur], axis=1)
    bias_both = jnp.concatenate([both, both], axis=0)
    lane = lax.broadcasted_iota(jnp.int32, (QB, LANES), 1)
    nsub = kvq_ref.shape[2] // QB
    units = [(sub, g) for sub in range(nsub) for g in range(NGRP)]
    scores, probs = {}, {}
    stats = {sub: jnp.zeros((QB, LANES), F32) for sub in range(nsub)}

    def rows_of(sub):
        return slice(sub * QB, (sub + 1) * QB)

    def stage_scores(u):
        sub, g = u
        kc = slice(P_K + g * LANES, P_K + (g + 1) * LANES)
        qc = slice(P_Q + g * LANES, P_Q + (g + 1) * LANES)
        q2 = _pair_split(kvq_ref[0, 0, rows_of(sub), qc].astype(F32)).astype(BF16)
        s_cur = _dot_t(q2, kvq_ref[0, 0, rows_of(sub), kc])
        if sub == 0:
            scores[u] = s_cur + bias_cur
        else:
            scores[u] = jnp.concatenate([_dot_t(q2, kvq_ref[0, 0, rows_of(sub - 1), kc]), s_cur], axis=1) + bias_both

    def stage_softmax(u):
        s = scores.pop(u)
        m = jnp.max(s, axis=-1, keepdims=True)
        p = jnp.exp(s - m)
        probs[u] = (m, jnp.sum(p, axis=-1, keepdims=True), p.astype(BF16))

    def stage_values(u):
        sub, g = u
        vc = slice(P_V + g * LANES, P_V + (g + 1) * LANES)
        m, l, pb = probs.pop(u)
        if sub == 0:
            o = _dot(pb, kvq_ref[0, 0, rows_of(sub), vc])
        else:
            o = _dot(pb[:, :QB], kvq_ref[0, 0, rows_of(sub - 1), vc]) + _dot(pb[:, QB:], kvq_ref[0, 0, rows_of(sub), vc])
        o_ref[0, 0, rows_of(sub), g * LANES:(g + 1) * LANES] = _pair_join(o)
        st = stats[sub]
        st = jnp.where(lane == 2 * g, m[:QB], st)
        st = jnp.where(lane == 2 * g + 1, m[QB:], st)
        st = jnp.where(lane == STAT_L + 2 * g, l[:QB], st)
        stats[sub] = jnp.where(lane == STAT_L + 2 * g + 1, l[QB:], st)
        if g == NGRP - 1:
            o_ref[0, 0, rows_of(sub), A_WIDTH:] = stats.pop(sub)

    stages = (stage_scores, stage_softmax, stage_values)
    for t in range(len(units) + len(stages) - 1):
        for depth, stage in enumerate(stages):
            if 0 <= t - depth < len(units):
                stage(units[t - depth])


def _dilated16(kvq16):
    b, ncls, nj, w = kvq16.shape
    return pl.pallas_call(
        _dilated16_kernel,
        grid=(b, ncls),
        in_specs=[pl.BlockSpec((1, 1, nj, w), lambda bi, c: (bi, c, 0, 0))],
        out_specs=pl.BlockSpec((1, 1, nj, O16_W), lambda bi, c: (bi, c, 0, 0)),
        out_shape=jax.ShapeDtypeStruct((b, ncls, nj, O16_W), F32),
        compiler_params=pltpu.CompilerParams(dimension_semantics=("arbitrary",) * 2,
                                             vmem_limit_bytes=VMEM_LIMIT),
        name="dilated16",
    )(kvq16)


STATE_PAD = 2
S_Q, S_U, S_QM, S_ROW = 0, A_WIDTH, A_WIDTH + C_WIDTH, A_WIDTH + C_WIDTH + M_WIDTH


def _sample_mix_stages(batch, srow_ref, kvt_ref, ck_ref, cv_ref, st_ref, cmk_ref, cmv_ref, wdw_ref,
                       smix_ref, wk_ref, wv_ref, ns_ref, uc_ref, *, t_s, win_len):
    new0 = LANES - t_s
    per_slab = LANES // t_s
    slab, pos = batch // per_slab, batch % per_slab

    def new_rows(lo, hi):
        return pltpu.roll(kvt_ref[slab, lo:hi, :], new0 - pos * t_s, 1)

    def slide(src_ref, lo, dst_ref):
        keep = lax.broadcasted_iota(jnp.int32, (A_WIDTH, LANES), 1) < new0
        nch = win_len // LANES
        nxt = pltpu.roll(src_ref[0, :, 0:LANES], new0, 1)
        for c in range(nch):
            cur = nxt
            nxt = (pltpu.roll(src_ref[0, :, (c + 1) * LANES:(c + 2) * LANES], new0, 1) if c + 1 < nch
                   else new_rows(lo, lo + A_WIDTH))
            dst_ref[0, :, c * LANES:(c + 1) * LANES] = jnp.where(keep, cur, nxt)

    return [functools.partial(slide, ck_ref, 0, wk_ref), functools.partial(slide, cv_ref, A_WIDTH, wv_ref),
            functools.partial(_sample_attend, new_rows, srow_ref, ck_ref, cv_ref, smix_ref, t_s=t_s, win_len=win_len),
            functools.partial(_sample_mem_conv, srow_ref, st_ref, cmk_ref, cmv_ref, wdw_ref, smix_ref, ns_ref,
                              uc_ref, t_s=t_s)]


def _sample_attend(new_rows, srow_ref, ck_ref, cv_ref, smix_ref, *, t_s, win_len):
    new0 = LANES - t_s
    knt, vnt = new_rows(0, A_WIDTH), new_rows(A_WIDTH, 2 * A_WIDTH)
    q = srow_ref[:, S_Q:S_U]
    head_of_lane = lax.broadcasted_iota(jnp.int32, (t_s, A_WIDTH), 1) >> HEAD_SHIFT
    q6 = jnp.concatenate([jnp.where(head_of_lane == h, q, 0.0) for h in range(A_HEADS)], axis=0).astype(BF16)
    s_c = _dot(q6, ck_ref[0].astype(BF16))
    s_n = _dot(q6, knt.astype(BF16))

    def weights(shape, key0, lo):
        tq = lax.broadcasted_iota(jnp.int32, shape, 0) & (t_s - 1)
        key = lax.broadcasted_iota(jnp.int32, shape, 1) + key0
        d = win_len + tq - key
        ok = (d >= 0) & (key >= lo)
        w = ((d <= 128).astype(F32) + ((d <= 512) & ((d & 3) == 0)).astype(F32)
             + ((d <= 2048) & ((d & 15) == 0)).astype(F32))
        return jnp.where(ok, w, 0.0)

    w_c = weights(s_c.shape, 0, 0)
    w_n = weights(s_n.shape, win_len - new0, win_len)
    s_c = jnp.where(w_c > 0.0, s_c, NEG)
    s_n = jnp.where(w_n > 0.0, s_n, NEG)
    m = jnp.maximum(jnp.max(s_c, axis=-1, keepdims=True), jnp.max(s_n, axis=-1, keepdims=True))
    p_c = jnp.exp(s_c - m) * w_c
    p_n = jnp.exp(s_n - m) * w_n
    l = jnp.sum(p_c, axis=-1, keepdims=True) + jnp.sum(p_n, axis=-1, keepdims=True)
    o = (_dot_t(p_c.astype(BF16), cv_ref[0].astype(BF16)) + _dot_t(p_n.astype(BF16), vnt.astype(BF16))) * (1.0 / l)
    oa = jnp.zeros((t_s, A_WIDTH), F32)
    for h in range(A_HEADS):
        oa = oa + jnp.where(head_of_lane == h, o[h * t_s:(h + 1) * t_s], 0.0)
    smix_ref[:, S_Q:S_U] = oa


def _sample_mem_conv(srow_ref, st_ref, cmk_ref, cmv_ref, wdw_ref, smix_ref, ns_ref, uc_ref, *, t_s):
    qm = srow_ref[:, S_QM:S_ROW] * SCALE
    mhead = lax.broadcasted_iota(jnp.int32, (t_s, M_WIDTH), 1) >> HEAD_SHIFT
    q4 = jnp.concatenate([jnp.where(mhead == h, qm, 0.0) for h in range(M_HEADS)], axis=0).astype(BF16)
    sm = _dot(q4, cmk_ref[0].astype(BF16))
    mm = jnp.max(sm, axis=-1, keepdims=True)
    pm = jnp.exp(sm - mm)
    lm = jnp.sum(pm, axis=-1, keepdims=True)
    om4 = _dot_t(pm.astype(BF16), cmv_ref[0].astype(BF16)) * (1.0 / lm)
    om = jnp.zeros((t_s, M_WIDTH), F32)
    for h in range(M_HEADS):
        om = om + jnp.where(mhead == h, om4[h * t_s:(h + 1) * t_s], 0.0)
    smix_ref[:, S_QM:S_ROW] = om

    nst = st_ref.shape[1]
    uc_ref[0:nst, :] = st_ref[0]
    uc_ref[nst:nst + t_s, :] = srow_ref[:, S_U:S_QM]
    acc = jnp.zeros((t_s, C_WIDTH), F32)
    for w in range(CONV_W):
        acc = acc + uc_ref[STATE_PAD + w:STATE_PAD + w + t_s, :] * wdw_ref[w:w + 1, :]
    smix_ref[:, S_U:S_QM] = acc
    ns_ref[0] = uc_ref[STATE_PAD + t_s:STATE_PAD + t_s + CONV_W - 1, :]


BACK = 512
NFAR = BACK // QB - 1

def _natural_offset(p):
    p = p & (QB - 1)
    return DIL4 * (p & (CLS4 - 1)) + (p >> CLS4_SHIFT)


def _prompt_attn_kernel(q_ref, kvc_ref, o16_ref, ga_ref, mix_ref, x_ref, wout_ref, npost_ref,
                        srow_ref, kvt_ref, ck_ref, cv_ref, sst_ref, cmk_ref, cmv_ref, wdw_ref,
                        y_ref, smix_ref, wk_ref, wv_ref, ns_ref,
                        mixa_ref, nat_ref, unp_ref, uc_ref, kvp_ref, woutbf_ref, *, tq, t_s, win_len):
    i = pl.program_id(1)
    kv_refs = (kvp_ref, kvc_ref)

    @pl.when((pl.program_id(0) == 0) & (i == 0))
    def _():
        woutbf_ref[...] = wout_ref[...].astype(BF16)

    @pl.when(i == 0)
    def _():
        kvp_ref[...] = jnp.zeros(kvp_ref.shape, BF16)

    for cl in range(DIL16):
        dst = pl.ds(cl, tq // DIL16, stride=DIL16)
        for g in range(NGRP + 1):
            nat_ref[g, dst, :] = o16_ref[0, cl, :, g * LANES:(g + 1) * LANES]

    a_n = _natural_offset(lax.broadcasted_iota(jnp.int32, (QB, 2 * QB), 0))
    c_p = lax.broadcasted_iota(jnp.int32, (QB, 2 * QB), 1)
    d = jnp.where(c_p < QB, QB, 0) + a_n - _natural_offset(c_p)
    near = (d >= 0) & (d <= QB)
    far = (d >= 0) & ((d & (DIL4 - 1)) == 0)
    bias_near0 = jnp.where(near & far, LN2, jnp.where(near | far, 0.0, NEG))
    ia = lax.broadcasted_iota(jnp.int32, (CLS4, QB), 0)
    cf = lax.broadcasted_iota(jnp.int32, (CLS4, QB), 1)
    fblk, ic = cf >> CLS4_SHIFT, cf & (CLS4 - 1)
    bias_far0 = jnp.where((fblk < NFAR) & ((fblk > 0) | (ic >= ia)), 0.0, NEG)
    lane = lax.broadcasted_iota(jnp.int32, (QB, LANES), 1)
    head_a = lane < HEAD_DIM

    def rows_of(w):
        blk, off = divmod(w, tq)
        return kv_refs[blk], off

    def group(x, c):
        return jnp.concatenate([x[c * CLS4:(c + 1) * CLS4], x[QB + c * CLS4:QB + (c + 1) * CLS4]], axis=0)

    def ungroup(parts):
        return jnp.concatenate([p_[:CLS4] for p_ in parts] + [p_[CLS4:] for p_ in parts], axis=0)

    units = [(sub, g) for sub in range(tq // QB) for g in range(NGRP)]
    ctx = {}
    for sub in range(tq // QB):
        first_valid = BACK - i * tq - sub * QB
        bias_near = jnp.where((c_p >= QB) | (NFAR * QB >= first_valid), bias_near0, NEG)
        bias_far = jnp.where(fblk * QB >= first_valid, bias_far0, NEG)
        far_blocks = [rows_of((sub + j) * QB) for j in range(NFAR)]
        ctx[sub] = dict(bias_near=jnp.concatenate([bias_near, bias_near], axis=0),
                        bias_far=jnp.concatenate([bias_far, bias_far], axis=0),
                        prev=rows_of((sub + NFAR) * QB), cur=rows_of((sub + NFAR + 1) * QB),
                        far=far_blocks + far_blocks[:1])

    def far_rows(sub, c, colsl):
        return jnp.concatenate([r[0, off + c * CLS4:off + (c + 1) * CLS4, colsl] for r, off in ctx[sub]["far"]],
                               axis=0)

    def block(sub, which, colsl):
        r, off = ctx[sub][which]
        return r[0, off:off + QB, colsl]

    scores, probs, outs = {}, {}, {}

    def stage_scores(u):
        sub, g = u
        kc = slice(P_K + g * LANES, P_K + (g + 1) * LANES)
        q2 = _pair_split(q_ref[0, sub * QB:(sub + 1) * QB, g * LANES:(g + 1) * LANES].astype(F32)).astype(BF16)
        s_far = ungroup([_dot_t(group(q2, c), far_rows(sub, c, kc)) + ctx[sub]["bias_far"] for c in range(DIL4)])
        s_near = jnp.concatenate([_dot_t(q2, block(sub, "prev", kc)), _dot_t(q2, block(sub, "cur", kc))],
                                 axis=1) + ctx[sub]["bias_near"]
        scores[u] = jnp.concatenate([s_far, s_near], axis=1)

    def stage_softmax(u):
        s = scores.pop(u)
        m = jnp.max(s, axis=-1, keepdims=True)
        p = jnp.exp(s - m)
        probs[u] = (m, jnp.sum(p, axis=-1, keepdims=True), p.astype(BF16))

    def stage_values(u):
        sub, g = u
        vc = slice(P_V + g * LANES, P_V + (g + 1) * LANES)
        pb = probs[u][2]
        outs[u] = (_dot(pb[:, QB:2 * QB], block(sub, "prev", vc)) + _dot(pb[:, 2 * QB:], block(sub, "cur", vc))
                   + ungroup([_dot(group(pb[:, :QB], c), far_rows(sub, c, vc)) for c in range(DIL4)]))

    def stage_merge(u):
        sub, g = u
        rows = slice(sub * QB, (sub + 1) * QB)
        cols = slice(g * LANES, (g + 1) * LANES)
        m, l, _ = probs.pop(u)
        o_s = _pair_join(outs.pop(u))
        m_s = jnp.where(head_a, m[:QB], m[QB:])
        l_s = jnp.where(head_a, l[:QB], l[QB:])
        for c in range(DIL4):
            dst = pl.ds(c, CLS4, stride=DIL4)
            unp_ref[3 * g, dst, :] = o_s[c * CLS4:(c + 1) * CLS4]
            unp_ref[3 * g + 1, dst, :] = m_s[c * CLS4:(c + 1) * CLS4]
            unp_ref[3 * g + 2, dst, :] = l_s[c * CLS4:(c + 1) * CLS4]
        o_n, m_n, l_n = unp_ref[3 * g], unp_ref[3 * g + 1], unp_ref[3 * g + 2]
        st = nat_ref[NGRP, rows, :]
        m_f = jnp.where(head_a, st[:, 2 * g:2 * g + 1], st[:, 2 * g + 1:2 * g + 2])
        l_f = jnp.where(head_a, st[:, STAT_L + 2 * g:STAT_L + 2 * g + 1],
                        st[:, STAT_L + 2 * g + 1:STAT_L + 2 * g + 2])
        mx = jnp.maximum(m_n, m_f)
        w_n = jnp.exp(m_n - mx)
        w_f = jnp.exp(m_f - mx)
        oa = (o_n * w_n + nat_ref[g, rows, :] * w_f) / (l_n * w_n + l_f * w_f)
        mixa_ref[rows, cols] = (oa * ga_ref[0, rows, cols]).astype(BF16)

    def finish(rows):
        z = (_dot(mixa_ref[rows, :], woutbf_ref[0:A_WIDTH, :])
             + _dot(mix_ref[0, rows, :], woutbf_ref[A_WIDTH:, :]))
        y_ref[0, rows, :] = x_ref[0, rows, :] + _rmsnorm(z, npost_ref[...])

    stages = (stage_scores, stage_softmax, stage_values, stage_merge)
    half = len(units) // 2
    for t in range(len(units) + len(stages) - 1):
        for depth, stage in enumerate(stages):
            if 0 <= t - depth < len(units):
                stage(units[t - depth])
        if t - (len(stages) - 1) == half - 1:
            finish(slice(0, tq // 2))
    finish(slice(tq // 2, tq))
    kvp_ref[...] = kvc_ref[...]

    for part in _sample_mix_stages(pl.program_id(0) * pl.num_programs(1) + i, srow_ref, kvt_ref, ck_ref, cv_ref,
                                   sst_ref, cmk_ref, cmv_ref, wdw_ref, smix_ref, wk_ref, wv_ref, ns_ref, uc_ref,
                                   t_s=t_s, win_len=win_len):
        part()


def _prompt_attn(kvq, o16, ga, mix, x, wout, npost, srow, kvt, ck, cv, st_pad, cmk, cmv, wdw, tq, t_s):
    b, s, _ = x.shape
    nsteps = s // tq
    nb, _, win_len = ck.shape
    assert tq == BACK and b * nsteps == nb, "one sample batch per grid step"
    row = lambda w: pl.BlockSpec((1, tq, w), lambda bi, i: (bi, i, 0))
    qspec = pl.BlockSpec((1, tq, A_WIDTH), lambda bi, i: (bi, i, P_Q // A_WIDTH))
    kvcur = pl.BlockSpec((1, tq, 2 * A_WIDTH), lambda bi, i: (bi, i, 0))
    cls = pl.BlockSpec((1, DIL16, tq // DIL16, O16_W), lambda bi, i: (bi, 0, i, 0))
    srows = pl.BlockSpec((t_s, S_ROW), lambda bi, i: (bi * nsteps + i, 0))
    per = lambda a: pl.BlockSpec((1,) + a.shape[1:], lambda bi, i: (bi * nsteps + i, 0, 0))
    return pl.pallas_call(
        functools.partial(_prompt_attn_kernel, tq=tq, t_s=t_s, win_len=win_len),
        grid=(b, nsteps),
        in_specs=[qspec, kvcur, cls, row(A_WIDTH), row(C_WIDTH + M_WIDTH), row(D_MODEL),
                  _resident(wout), _full(npost),
                  srows, _full(kvt), per(ck), per(cv), per(st_pad), per(cmk), per(cmv), _full(wdw)],
        out_specs=[row(D_MODEL), srows, per(ck), per(cv),
                   pl.BlockSpec((1, CONV_W - 1, C_WIDTH), lambda bi, i: (bi * nsteps + i, 0, 0))],
        out_shape=[jax.ShapeDtypeStruct((b, s, D_MODEL), F32),
                   jax.ShapeDtypeStruct((nb * t_s, S_ROW), F32),
                   jax.ShapeDtypeStruct(ck.shape, F32), jax.ShapeDtypeStruct(cv.shape, F32),
                   jax.ShapeDtypeStruct((nb, CONV_W - 1, C_WIDTH), F32)],
        scratch_shapes=[pltpu.VMEM((tq, A_WIDTH), BF16),
                        pltpu.VMEM((NGRP + 1, tq, LANES), F32),
                        pltpu.VMEM((3 * NGRP, QB, LANES), F32),
                        pltpu.VMEM((STATE_PAD + CONV_W - 1 + t_s, C_WIDTH), F32),
                        pltpu.VMEM((1, tq, 2 * A_WIDTH), BF16),
                        pltpu.VMEM(wout.shape, BF16)],
        compiler_params=pltpu.CompilerParams(dimension_semantics=("arbitrary", "arbitrary"),
                                             vmem_limit_bytes=VMEM_LIMIT),
        name="prompt_attn",
    )(kvq, kvq, o16, ga, mix, x, wout, npost, srow, kvt, ck, cv, st_pad, cmk, cmv, wdw)


def _sample_proj_kernel(x_ref, npre_ref, win_ref, ang_ref, srow_ref, kvt_ref, gate_ref):
    h = _rmsnorm(x_ref[...], npre_ref[...]).astype(BF16)
    proj = _dot(h, win_ref[...].astype(BF16))
    pats = _rope_patterns(ang_ref[:, 0:LANES], ang_ref[:, LANES:])
    srow_ref[:, S_Q:S_U] = _rope(proj[:, O_QA:O_KA], pats) * SCALE
    srow_ref[:, S_U:S_QM] = proj[:, O_AB:O_BB] * _sigmoid(proj[:, O_BB:O_GB])
    srow_ref[:, S_QM:S_ROW] = proj[:, O_QM:O_GM]
    kv_t = jnp.concatenate([_rope(proj[:, O_KA:O_VA], pats), proj[:, O_VA:O_GA]], axis=1).T
    for slab in range(kvt_ref.shape[0]):
        kvt_ref[slab] = kv_t[:, slab * LANES:(slab + 1) * LANES]
    gate_ref[:, S_Q:S_U] = _silu(proj[:, O_GA:O_AB])
    gate_ref[:, S_U:S_QM] = _silu(proj[:, O_GB:O_QM])
    gate_ref[:, S_QM:S_ROW] = _silu(proj[:, O_GM:D_IN])


def _sample_proj(x, npre, win, ang):
    n = x.shape[0]
    assert n % LANES == 0
    shapes = [(n, S_ROW), (n // LANES, 2 * A_WIDTH, LANES), (n, S_ROW)]
    return pl.pallas_call(
        _sample_proj_kernel,
        grid=(1,),
        in_specs=[_full(x), _full(npre), _full(win), _full(ang)],
        out_specs=[pl.BlockSpec(sh, lambda i, nd=len(sh): (0,) * nd) for sh in shapes],
        out_shape=[jax.ShapeDtypeStruct(sh, F32) for sh in shapes],
        compiler_params=pltpu.CompilerParams(vmem_limit_bytes=VMEM_LIMIT),
        name="sample_proj",
    )(x, npre, win, ang)


def _sample_out_kernel(x_ref, smix_ref, gate_ref, bdw_ref, lng_ref, lnb_ref, wpw_ref, bpw_ref, wout_ref, npost_ref,
                       y_ref):
    mixed_a = (smix_ref[:, S_Q:S_U] * gate_ref[:, S_Q:S_U]).astype(BF16)
    mixed_b = _conformer_tail(smix_ref[:, S_U:S_QM], gate_ref[:, S_U:S_QM], bdw_ref[...], lng_ref[...], lnb_ref[...],
                              wpw_ref, bpw_ref[...]).astype(BF16)
    mixed_m = (smix_ref[:, S_QM:S_ROW] * gate_ref[:, S_QM:S_ROW]).astype(BF16)
    wout = wout_ref[...].astype(BF16)
    z = (_dot(mixed_a, wout[0:A_WIDTH]) + _dot(mixed_b, wout[A_WIDTH:A_WIDTH + C_WIDTH])
         + _dot(mixed_m, wout[A_WIDTH + C_WIDTH:]))
    y_ref[...] = x_ref[...] + _rmsnorm(z, npost_ref[...])


def _sample_out(x, smix, gate, bdw, lng, lnb, wpw, bpw, wout, npost):
    args = (x, smix, gate, bdw, lng, lnb, wpw, bpw, wout, npost)
    return pl.pallas_call(
        _sample_out_kernel,
        grid=(1,),
        in_specs=[_full(a) for a in args],
        out_specs=_full(x),
        out_shape=jax.ShapeDtypeStruct(x.shape, F32),
        compiler_params=pltpu.CompilerParams(vmem_limit_bytes=VMEM_LIMIT),
        name="sample_out",
    )(*args)


def _rope_cos_sin(pos):
    inv = ROPE_THETA ** (-jnp.arange(0, ROT_DIM, 2, dtype=F32) / ROT_DIM)
    ang = pos.astype(F32)[:, None] * jnp.tile(inv, 2 * LANES // ROT_DIM)[None, :]
    return jnp.concatenate([jnp.cos(ang), jnp.sin(ang)], axis=1)


def _feature_major(cache):
    nb, rows, heads, dim = cache.shape
    return jnp.transpose(cache, (0, 2, 3, 1)).reshape(nb, heads * dim, rows)


def _row_major(cache_t, heads):
    nb, width, rows = cache_t.shape
    return jnp.transpose(cache_t.reshape(nb, heads, width // heads, rows), (0, 3, 1, 2))


PROJ_TILE = 512
ATTN_TILE = BACK


def kernel(x_prompt, x_sample, cache_win_k, cache_win_v, state_conv, cache_mem_k, cache_mem_v, mem_prompt,
           norm_pre, norm_post, w_in, w_out, norm_mem, w_mem_kv, w_dw, b_dw, ln_conv_g, ln_conv_b, w_pw2, b_pw2):
    depth = w_in.shape[0]
    assert depth == 1, "single-layer step"
    b, s, _ = x_prompt.shape
    nb, t_s, _ = x_sample.shape
    win_len = cache_win_k.shape[2]
    assert s % (DIL16 * QB) == 0 and s % PROJ_TILE == 0 and s % ATTN_TILE == 0
    assert win_len == MAX_WINDOW and win_len % LANES == 0
    assert t_s % 8 == 0 and t_s & (t_s - 1) == 0 and t_s < LANES
    l = 0
    row = lambda a: a[l][None, :]
    npre, npost, nmem = row(norm_pre), row(norm_post), row(norm_mem)
    bdw, lng, lnb, bpw = row(b_dw), row(ln_conv_g), row(ln_conv_b), row(b_pw2)
    win, wout, wmem, wpw = w_in[l], w_out[l], w_mem_kv[l], w_pw2[l]
    wdw = w_dw[l]

    pos_s = PAST_LEN + jnp.arange(t_s, dtype=jnp.int32)
    xs = x_sample.reshape(nb * t_s, D_MODEL)
    srow, kvt, gate_s = _sample_proj(xs, npre, win, jnp.tile(_rope_cos_sin(pos_s), (nb, 1)))
    st_pad = jnp.pad(state_conv[l], ((0, 0), (STATE_PAD, 0), (0, 0)))

    mk, mv, mkb, mvb = _mem_kv(mem_prompt, nmem, wmem)
    rin = _rope_cos_sin(jnp.arange(PROJ_TILE, dtype=jnp.int32))
    rbase = _rope_cos_sin(jnp.arange(0, s, PROJ_TILE, dtype=jnp.int32))[:, None, :]
    kvq, kvq16, kf, vf, ga, mix, ust = _prompt_proj(x_prompt, npre, win, rin, rbase, mkb, mvb,
                                                    wdw, bdw, lng, lnb, wpw, bpw, PROJ_TILE)
    o16 = _dilated16(kvq16)
    y_prompt, smix, wk, wv, nst = _prompt_attn(
        kvq, o16, ga, mix, x_prompt, wout, npost, srow, kvt,
        _feature_major(cache_win_k[l]), _feature_major(cache_win_v[l]), st_pad,
        _feature_major(cache_mem_k[l]), _feature_major(cache_mem_v[l]), wdw, ATTN_TILE, t_s)
    y_sample = _sample_out(xs, smix, gate_s, bdw, lng, lnb, wpw, bpw, wout, npost)

    keep_p = kf.shape[1]
    return (y_prompt,
            y_sample.reshape(nb, t_s, D_MODEL),
            kf.reshape(1, b, keep_p, A_HEADS, HEAD_DIM),
            vf.reshape(1, b, keep_p, A_HEADS, HEAD_DIM),
            ust[:, CONV_HIST - (CONV_W - 1):, :][None],
            mk.reshape(1, b, N_MEM, M_HEADS, HEAD_DIM),
            mv.reshape(1, b, N_MEM, M_HEADS, HEAD_DIM),
            _row_major(wk, A_HEADS)[None],
            _row_major(wv, A_HEADS)[None],
            nst[None])
```

```python
import functools
import math

import jax
import jax.numpy as jnp
from jax import lax
from jax.experimental import pallas as pl
from jax.experimental.pallas import tpu as pltpu

F32 = jnp.float32
BF16 = jnp.bfloat16

D_MODEL = 1024
HEAD_DIM = 64
HEAD_SHIFT = 6
A_WIDTH = 384
A_HEADS = 6
M_WIDTH = 256
M_HEADS = 4
C_WIDTH = 384
ROT_DIM = 16
ROPE_THETA = 500000.0
CONV_W = 31
N_MEM = 256
MAX_WINDOW = 2048
PAST_LEN = 16384
EPS = 1e-6
SCALE = HEAD_DIM ** -0.5
NEG = -1e30
LN2 = math.log(2.0)

LANES = 128
NGRP = A_WIDTH // LANES
QB = 128
DIL16 = 16
DIL4 = 4
CLS4 = QB // DIL4
CLS4_SHIFT = 5
O_QA, O_KA, O_VA, O_GA, O_AB, O_BB, O_GB, O_QM, O_GM, D_IN = 0, 384, 768, 1152, 1536, 1920, 2304, 2688, 2944, 3200
P_K, P_V, P_Q, P_W = 0, A_WIDTH, 2 * A_WIDTH, 3 * A_WIDTH

VMEM_LIMIT = 60 * 1024 * 1024


def _sigmoid(x):
    return 1.0 / (1.0 + jnp.exp(-x))


def _silu(x):
    return x * _sigmoid(x)


def _rmsnorm(x, g):
    return x * lax.rsqrt(jnp.mean(x * x, axis=-1, keepdims=True) + EPS) * g


def _dot(a, b):
    return jnp.dot(a, b, preferred_element_type=F32)


def _dot_t(a, b):
    return lax.dot_general(a, b, (((1,), (1,)), ((), ())), preferred_element_type=F32)


def _rope_patterns(cos, sin):
    in_head = lax.broadcasted_iota(jnp.int32, cos.shape, 1) & (HEAD_DIM - 1)
    rot, lo = in_head < ROT_DIM, in_head < ROT_DIM // 2
    return jnp.where(rot, cos, 1.0), jnp.where(lo, -sin, 0.0), jnp.where(rot & ~lo, sin, 0.0)


def _rope(xw, pats):
    cos, s1, s2 = pats
    outs = []
    for g in range(xw.shape[1] // LANES):
        xg = xw[:, g * LANES:(g + 1) * LANES]
        outs.append(xg * cos + pltpu.roll(xg, LANES - 8, 1) * s1 + pltpu.roll(xg, 8, 1) * s2)
    return jnp.concatenate(outs, axis=1)


def _pair_split(q):
    lane = lax.broadcasted_iota(jnp.int32, q.shape, 1)
    qa = jnp.where(lane < HEAD_DIM, q, 0.0)
    qb = jnp.where(lane >= HEAD_DIM, q, 0.0)
    return jnp.concatenate([qa, qb], axis=0)


def _pair_join(x):
    t = x.shape[0] // 2
    lane = lax.broadcasted_iota(jnp.int32, (t, LANES), 1)
    return jnp.where(lane < HEAD_DIM, x[:t], x[t:])


def _mem_attend(qm, mk_ref, mv_ref):
    outs = []
    for g in range(M_WIDTH // LANES):
        cols = slice(g * LANES, (g + 1) * LANES)
        q2 = _pair_split(qm[:, cols] * SCALE).astype(BF16)
        s = _dot_t(q2, mk_ref[0, :, cols])
        m = jnp.max(s, axis=-1, keepdims=True)
        p = jnp.exp(s - m)
        l = jnp.sum(p, axis=-1, keepdims=True)
        o = _dot(p.astype(BF16), mv_ref[0, :, cols]) * (1.0 / l)
        outs.append(_pair_join(o))
    return jnp.concatenate(outs, axis=1)


def _conformer_tail(c, gate_b, bdw, lng, lnb, wpw_ref, bpw):
    cf = c + bdw
    mu = jnp.mean(cf, axis=-1, keepdims=True)
    dev = cf - mu
    var = jnp.mean(dev * dev, axis=-1, keepdims=True)
    cn = dev * lax.rsqrt(var + EPS) * lng + lnb
    ob = _dot(_silu(cn).astype(BF16), wpw_ref[...].astype(BF16)) + bpw
    return ob * gate_b


def _full(a):
    return pl.BlockSpec(a.shape, lambda *_: (0,) * a.ndim)


def _resident(a):
    return pl.BlockSpec(a.shape, lambda *_: (0,) * a.ndim, pipeline_mode=pl.Buffered(1))


def _mem_kv_kernel(mem_ref, g_ref, w_ref, mk_ref, mv_ref, mkb_ref, mvb_ref):
    h = _rmsnorm(mem_ref[0], g_ref[...]).astype(BF16)
    kv = _dot(h, w_ref[...].astype(BF16))
    mk_ref[0] = kv[:, :M_WIDTH]
    mv_ref[0] = kv[:, M_WIDTH:]
    mkb_ref[0] = kv[:, :M_WIDTH].astype(BF16)
    mvb_ref[0] = kv[:, M_WIDTH:].astype(BF16)


def _mem_kv(mem, g, wmem):
    b = mem.shape[0]
    blk = pl.BlockSpec((1, N_MEM, M_WIDTH), lambda i: (i, 0, 0))
    return pl.pallas_call(
        _mem_kv_kernel,
        grid=(b,),
        in_specs=[pl.BlockSpec((1, N_MEM, D_MODEL), lambda i: (i, 0, 0)), _full(g), _full(wmem)],
        out_specs=[blk, blk, blk, blk],
        out_shape=[jax.ShapeDtypeStruct((b, N_MEM, M_WIDTH), F32)] * 2
        + [jax.ShapeDtypeStruct((b, N_MEM, M_WIDTH), BF16)] * 2,
        name="mem_kv",
    )(mem, g, wmem)


CONV_HIST = 32
CONV_CHUNK = 64


def _prompt_proj_kernel(x_ref, npre_ref, win_ref, rin_ref, rbase_ref, mk_ref, mv_ref,
                        wdw_ref, bdw_ref, lng_ref, lnb_ref, wpw_ref, bpw_ref, kvt_ref, cv_ref,
                        kvq_ref, kvq16_ref, kf_ref, vf_ref, ga_ref, mix_ref, ust_ref, wv_ref,
                        uext_ref, ush_ref, conv_ref, stage_ref, wbf_ref, *, t_s, win_len):
    t = x_ref.shape[1]
    i = pl.program_id(1)

    @pl.when((pl.program_id(0) == 0) & (i == 0))
    def _():
        wbf_ref[...] = win_ref[...].astype(BF16)

    @pl.when(i == 0)
    def _():
        uext_ref[0:CONV_HIST, :] = jnp.zeros((CONV_HIST, C_WIDTH), F32)

    h = _rmsnorm(x_ref[0], npre_ref[...]).astype(BF16)
    proj = {}

    def project(name, c0, c1):
        def run():
            proj[name] = _dot(h, wbf_ref[:, c0:c1])
        return run

    def stage(idx, val):
        for g in range(NGRP):
            stage_ref[NGRP * idx + g] = val[:, g * LANES:(g + 1) * LANES]

    def rope_pats():
        cr, sr = rin_ref[:, 0:LANES], rin_ref[:, LANES:]
        cb, sb = rbase_ref[i, :, 0:LANES], rbase_ref[i, :, LANES:]
        return _rope_patterns(cb * cr - sb * sr, sb * cr + cb * sr)

    def do_k():
        k = _rope(proj.pop("k"), rope_pats())
        kf_ref[0] = k
        stage(0, k)

    def do_q():
        stage(2, _rope(proj.pop("q"), rope_pats()) * SCALE)

    def do_v():
        v = proj.pop("v")
        vf_ref[0] = v
        stage(1, v)

    def do_ga():
        ga_ref[0] = _silu(proj.pop("ga"))

    def do_copies():
        for c in range(DIL16):
            rows_c = [stage_ref[s, pl.ds(c, t // DIL16, stride=DIL16), :] for s in range(3 * NGRP)]
            kvq16_ref[0, c] = jnp.concatenate(rows_c, axis=1).astype(BF16)
        for blk in range(t // QB):
            for c in range(DIL4):
                rows_c = [stage_ref[s, pl.ds(blk * QB + c, CLS4, stride=DIL4), :] for s in range(3 * NGRP)]
                kvq_ref[0, blk * QB + c * CLS4:blk * QB + (c + 1) * CLS4, :] = (
                    jnp.concatenate(rows_c, axis=1).astype(BF16))

    def do_glu():
        uext_ref[CONV_HIST:CONV_HIST + t, :] = proj.pop("ab") * _sigmoid(proj.pop("bb"))
        for r in range(1, 8):
            ush_ref[r - 1, 8:CONV_HIST + t, :] = uext_ref[8 - r:CONV_HIST + t - r, :]

    def conv_rows(r0):
        def run():
            acc = jnp.zeros((CONV_CHUNK, C_WIDTH), F32)
            for kk in range(CONV_W):
                tap = wdw_ref[CONV_W - 1 - kk:CONV_W - kk, :]
                r, base = kk % 8, CONV_HIST + r0 - (kk - kk % 8)
                if r == 0:
                    slab = uext_ref[base:base + CONV_CHUNK, :]
                else:
                    slab = ush_ref[r - 1, base:base + CONV_CHUNK, :]
                acc = acc + slab * tap
            conv_ref[r0:r0 + CONV_CHUNK, :] = acc
        return run

    def do_history():
        tail = uext_ref[t:t + CONV_HIST, :]
        uext_ref[0:CONV_HIST, :] = tail
        ust_ref[0] = tail

    def do_tail():
        mixed_b = _conformer_tail(conv_ref[...], _silu(proj.pop("gb")), bdw_ref[...], lng_ref[...], lnb_ref[...],
                                  wpw_ref, bpw_ref[...])
        mix_ref[0, :, 0:C_WIDTH] = mixed_b.astype(BF16)

    def do_mem():
        mixed_m = _mem_attend(proj.pop("qm"), mk_ref, mv_ref) * _silu(proj.pop("gm"))
        mix_ref[0, :, C_WIDTH:] = mixed_m.astype(BF16)

    convs = [conv_rows(r0) for r0 in range(0, t, CONV_CHUNK)]
    riders = [([project("k", O_KA, O_VA)], do_k), ([project("q", O_QA, O_KA)], do_q),
              ([project("v", O_VA, O_GA)], do_v), ([project("ga", O_GA, O_AB)], do_ga),
              ([project("qm", O_QM, O_GM), project("gm", O_GM, D_IN)], do_mem),
              ([project("gb", O_GB, O_QM)], do_copies)]
    slide_v = functools.partial(_slide_cache, pl.program_id(0) * pl.num_programs(1) + i, kvt_ref, cv_ref, A_WIDTH,
                                wv_ref, t_s=t_s, win_len=win_len)
    program = [project("ab", O_AB, O_BB), project("bb", O_BB, O_GB), slide_v, do_glu]
    for n, conv in enumerate(convs):
        if n < len(riders):
            program += riders[n][0]
        program.append(conv)
        if n < len(riders) and riders[n][1] is not None:
            program.append(riders[n][1])
    assert len(convs) >= len(riders)
    program += [do_history, do_tail]
    for piece in program:
        piece()
    assert not proj


def _prompt_proj(x, npre, win, rin, rbase, mkb, mvb, wdw, bdw, lng, lnb, wpw, bpw, kvt, cv, tile, t_s):
    b, s, _ = x.shape
    nt = s // tile
    nb, _, win_len = cv.shape
    assert b * nt == nb, "one sample batch per grid step"
    per_batch = pl.BlockSpec((1,) + cv.shape[1:], lambda bi, i: (bi * nt + i, 0, 0))
    keep = min(MAX_WINDOW, s)
    first_keep = (s - keep) // tile
    row = lambda w: pl.BlockSpec((1, tile, w), lambda bi, i: (bi, i, 0))
    memb = pl.BlockSpec((1, N_MEM, M_WIDTH), lambda bi, i: (bi, 0, 0))
    keepb = pl.BlockSpec((1, tile, A_WIDTH), lambda bi, i: (bi, jnp.maximum(i - first_keep, 0), 0))
    cls = pl.BlockSpec((1, DIL16, tile // DIL16, P_W), lambda bi, i: (bi, 0, i, 0))
    return pl.pallas_call(
        functools.partial(_prompt_proj_kernel, t_s=t_s, win_len=win_len),
        grid=(b, nt),
        in_specs=[row(D_MODEL), _full(npre), _resident(win), _full(rin), _full(rbase),
                  memb, memb, _full(wdw), _full(bdw), _full(lng), _full(lnb), _full(wpw), _full(bpw),
                  _full(kvt), per_batch],
        out_specs=[row(P_W), cls, keepb, keepb, row(A_WIDTH), row(C_WIDTH + M_WIDTH),
                   pl.BlockSpec((1, CONV_HIST, C_WIDTH), lambda bi, i: (bi, 0, 0)), per_batch],
        out_shape=[jax.ShapeDtypeStruct((b, s, P_W), BF16),
                   jax.ShapeDtypeStruct((b, DIL16, s // DIL16, P_W), BF16),
                   jax.ShapeDtypeStruct((b, keep, A_WIDTH), F32),
                   jax.ShapeDtypeStruct((b, keep, A_WIDTH), F32),
                   jax.ShapeDtypeStruct((b, s, A_WIDTH), F32),
                   jax.ShapeDtypeStruct((b, s, C_WIDTH + M_WIDTH), BF16),
                   jax.ShapeDtypeStruct((b, CONV_HIST, C_WIDTH), F32),
                   jax.ShapeDtypeStruct(cv.shape, F32)],
        scratch_shapes=[pltpu.VMEM((CONV_HIST + tile, C_WIDTH), F32),
                        pltpu.VMEM((7, CONV_HIST + tile, C_WIDTH), F32),
                        pltpu.VMEM((tile, C_WIDTH), F32),
                        pltpu.VMEM((3 * NGRP, tile, LANES), F32),
                        pltpu.VMEM(win.shape, BF16)],
        compiler_params=pltpu.CompilerParams(dimension_semantics=("arbitrary", "arbitrary"),
                                             vmem_limit_bytes=VMEM_LIMIT),
        name="prompt_proj",
    )(x, npre, win, rin, rbase, mkb, mvb, wdw, bdw, lng, lnb, wpw, bpw, kvt, cv)


STAT_L = 8
O16_W = A_WIDTH + LANES


def _dilated16_kernel(kvq_ref, o_ref):
    a = lax.broadcasted_iota(jnp.int32, (QB, QB), 0)
    c = lax.broadcasted_iota(jnp.int32, (QB, QB), 1)
    tri_prev = jnp.where(c >= a, 0.0, NEG)
    tri_cur = jnp.where(c <= a, 0.0, NEG)
    bias_cur = jnp.concatenate([tri_cur, tri_cur], axis=0)
    both = jnp.concatenate([tri_prev, tri_cur], axis=1)
    bias_both = jnp.concatenate([both, both], axis=0)
    lane = lax.broadcasted_iota(jnp.int32, (QB, LANES), 1)
    nsub = kvq_ref.shape[2] // QB
    units = [(sub, g) for sub in range(nsub) for g in range(NGRP)]
    scores, probs = {}, {}
    stats = {sub: jnp.zeros((QB, LANES), F32) for sub in range(nsub)}

    def rows_of(sub):
        return slice(sub * QB, (sub + 1) * QB)

    def stage_scores(u):
        sub, g = u
        kc = slice(P_K + g * LANES, P_K + (g + 1) * LANES)
        qc = slice(P_Q + g * LANES, P_Q + (g + 1) * LANES)
        q2 = _pair_split(kvq_ref[0, 0, rows_of(sub), qc].astype(F32)).astype(BF16)
        s_cur = _dot_t(q2, kvq_ref[0, 0, rows_of(sub), kc])
        if sub == 0:
            scores[u] = s_cur + bias_cur
        else:
            scores[u] = jnp.concatenate([_dot_t(q2, kvq_ref[0, 0, rows_of(sub - 1), kc]), s_cur], axis=1) + bias_both

    def stage_softmax(u):
        s = scores.pop(u)
        m = jnp.max(s, axis=-1, keepdims=True)
        p = jnp.exp(s - m)
        probs[u] = (m, jnp.sum(p, axis=-1, keepdims=True), p.astype(BF16))

    def stage_values(u):
        sub, g = u
        vc = slice(P_V + g * LANES, P_V + (g + 1) * LANES)
        m, l, pb = probs.pop(u)
        if sub == 0:
            o = _dot(pb, kvq_ref[0, 0, rows_of(sub), vc])
        else:
            o = _dot(pb[:, :QB], kvq_ref[0, 0, rows_of(sub - 1), vc]) + _dot(pb[:, QB:], kvq_ref[0, 0, rows_of(sub), vc])
        o_ref[0, 0, rows_of(sub), g * LANES:(g + 1) * LANES] = _pair_join(o)
        st = stats[sub]
        st = jnp.where(lane == 2 * g, m[:QB], st)
        st = jnp.where(lane == 2 * g + 1, m[QB:], st)
        st = jnp.where(lane == STAT_L + 2 * g, l[:QB], st)
        stats[sub] = jnp.where(lane == STAT_L + 2 * g + 1, l[QB:], st)
        if g == NGRP - 1:
            o_ref[0, 0, rows_of(sub), A_WIDTH:] = stats.pop(sub)

    stages = (stage_scores, stage_softmax, stage_values)
    for t in range(len(units) + len(stages) - 1):
        for depth, stage in enumerate(stages):
            if 0 <= t - depth < len(units):
                stage(units[t - depth])


def _dilated16(kvq16):
    b, ncls, nj, w = kvq16.shape
    return pl.pallas_call(
        _dilated16_kernel,
        grid=(b, ncls),
        in_specs=[pl.BlockSpec((1, 1, nj, w), lambda bi, c: (bi, c, 0, 0))],
        out_specs=pl.BlockSpec((1, 1, nj, O16_W), lambda bi, c: (bi, c, 0, 0)),
        out_shape=jax.ShapeDtypeStruct((b, ncls, nj, O16_W), F32),
        compiler_params=pltpu.CompilerParams(dimension_semantics=("arbitrary",) * 2,
                                             vmem_limit_bytes=VMEM_LIMIT),
        name="dilated16",
    )(kvq16)


STATE_PAD = 2
S_Q, S_U, S_QM, S_ROW = 0, A_WIDTH, A_WIDTH + C_WIDTH, A_WIDTH + C_WIDTH + M_WIDTH


def _new_rows(batch, kvt_ref, lo, hi, t_s):
    per_slab = LANES // t_s
    slab, pos = batch // per_slab, batch % per_slab
    return pltpu.roll(kvt_ref[slab, lo:hi, :], LANES - t_s - pos * t_s, 1)


def _slide_cache(batch, kvt_ref, src_ref, lo, dst_ref, *, t_s, win_len):
    new0 = LANES - t_s
    keep = lax.broadcasted_iota(jnp.int32, (A_WIDTH, LANES), 1) < new0
    nch = win_len // LANES
    nxt = pltpu.roll(src_ref[0, :, 0:LANES], new0, 1)
    for c in range(nch):
        cur = nxt
        nxt = (pltpu.roll(src_ref[0, :, (c + 1) * LANES:(c + 2) * LANES], new0, 1) if c + 1 < nch
               else _new_rows(batch, kvt_ref, lo, lo + A_WIDTH, t_s))
        dst_ref[0, :, c * LANES:(c + 1) * LANES] = jnp.where(keep, cur, nxt)


def _sample_mix_stages(batch, srow_ref, kvt_ref, ck_ref, cv_ref, st_ref, cmk_ref, cmv_ref, wdw_ref,
                       smix_ref, wk_ref, ns_ref, uc_ref, *, t_s, win_len):
    new_rows = functools.partial(_new_rows, batch, kvt_ref, t_s=t_s)
    return [functools.partial(_slide_cache, batch, kvt_ref, ck_ref, 0, wk_ref, t_s=t_s, win_len=win_len),
            functools.partial(_sample_attend, new_rows, srow_ref, ck_ref, cv_ref, smix_ref, t_s=t_s, win_len=win_len),
            functools.partial(_sample_mem_conv, srow_ref, st_ref, cmk_ref, cmv_ref, wdw_ref, smix_ref, ns_ref,
                              uc_ref, t_s=t_s)]


def _sample_attend(new_rows, srow_ref, ck_ref, cv_ref, smix_ref, *, t_s, win_len):
    new0 = LANES - t_s
    knt, vnt = new_rows(0, A_WIDTH), new_rows(A_WIDTH, 2 * A_WIDTH)
    q = srow_ref[:, S_Q:S_U]
    head_of_lane = lax.broadcasted_iota(jnp.int32, (t_s, A_WIDTH), 1) >> HEAD_SHIFT
    q6 = jnp.concatenate([jnp.where(head_of_lane == h, q, 0.0) for h in range(A_HEADS)], axis=0).astype(BF16)
    s_c = _dot(q6, ck_ref[0].astype(BF16))
    s_n = _dot(q6, knt.astype(BF16))

    def weights(shape, key0, lo):
        tq = lax.broadcasted_iota(jnp.int32, shape, 0) & (t_s - 1)
        key = lax.broadcasted_iota(jnp.int32, shape, 1) + key0
        d = win_len + tq - key
        ok = (d >= 0) & (key >= lo)
        w = ((d <= 128).astype(F32) + ((d <= 512) & ((d & 3) == 0)).astype(F32)
             + ((d <= 2048) & ((d & 15) == 0)).astype(F32))
        return jnp.where(ok, w, 0.0)

    w_c = weights(s_c.shape, 0, 0)
    w_n = weights(s_n.shape, win_len - new0, win_len)
    s_c = jnp.where(w_c > 0.0, s_c, NEG)
    s_n = jnp.where(w_n > 0.0, s_n, NEG)
    m = jnp.maximum(jnp.max(s_c, axis=-1, keepdims=True), jnp.max(s_n, axis=-1, keepdims=True))
    p_c = jnp.exp(s_c - m) * w_c
    p_n = jnp.exp(s_n - m) * w_n
    l = jnp.sum(p_c, axis=-1, keepdims=True) + jnp.sum(p_n, axis=-1, keepdims=True)
    o = (_dot_t(p_c.astype(BF16), cv_ref[0].astype(BF16)) + _dot_t(p_n.astype(BF16), vnt.astype(BF16))) * (1.0 / l)
    oa = jnp.zeros((t_s, A_WIDTH), F32)
    for h in range(A_HEADS):
        oa = oa + jnp.where(head_of_lane == h, o[h * t_s:(h + 1) * t_s], 0.0)
    smix_ref[:, S_Q:S_U] = oa


def _sample_mem_conv(srow_ref, st_ref, cmk_ref, cmv_ref, wdw_ref, smix_ref, ns_ref, uc_ref, *, t_s):
    qm = srow_ref[:, S_QM:S_ROW] * SCALE
    mhead = lax.broadcasted_iota(jnp.int32, (t_s, M_WIDTH), 1) >> HEAD_SHIFT
    q4 = jnp.concatenate([jnp.where(mhead == h, qm, 0.0) for h in range(M_HEADS)], axis=0).astype(BF16)
    sm = _dot(q4, cmk_ref[0].astype(BF16))
    mm = jnp.max(sm, axis=-1, keepdims=True)
    pm = jnp.exp(sm - mm)
    lm = jnp.sum(pm, axis=-1, keepdims=True)
    om4 = _dot_t(pm.astype(BF16), cmv_ref[0].astype(BF16)) * (1.0 / lm)
    om = jnp.zeros((t_s, M_WIDTH), F32)
    for h in range(M_HEADS):
        om = om + jnp.where(mhead == h, om4[h * t_s:(h + 1) * t_s], 0.0)
    smix_ref[:, S_QM:S_ROW] = om

    nst = st_ref.shape[1]
    uc_ref[0:nst, :] = st_ref[0]
    uc_ref[nst:nst + t_s, :] = srow_ref[:, S_U:S_QM]
    acc = jnp.zeros((t_s, C_WIDTH), F32)
    for w in range(CONV_W):
        acc = acc + uc_ref[STATE_PAD + w:STATE_PAD + w + t_s, :] * wdw_ref[w:w + 1, :]
    smix_ref[:, S_U:S_QM] = acc
    ns_ref[0] = uc_ref[STATE_PAD + t_s:STATE_PAD + t_s + CONV_W - 1, :]


BACK = 512
NFAR = BACK // QB - 1

def _natural_offset(p):
    p = p & (QB - 1)
    return DIL4 * (p & (CLS4 - 1)) + (p >> CLS4_SHIFT)


def _prompt_attn_kernel(q_ref, kvc_ref, o16_ref, ga_ref, mix_ref, x_ref, wout_ref, npost_ref,
                        srow_ref, kvt_ref, ck_ref, cv_ref, sst_ref, cmk_ref, cmv_ref, wdw_ref,
                        y_ref, smix_ref, wk_ref, ns_ref,
                        mixa_ref, nat_ref, unp_ref, uc_ref, kvp_ref, woutbf_ref, *, tq, t_s, win_len):
    i = pl.program_id(1)
    kv_refs = (kvp_ref, kvc_ref)

    @pl.when((pl.program_id(0) == 0) & (i == 0))
    def _():
        woutbf_ref[...] = wout_ref[...].astype(BF16)

    @pl.when(i == 0)
    def _():
        kvp_ref[...] = jnp.zeros(kvp_ref.shape, BF16)

    for cl in range(DIL16):
        dst = pl.ds(cl, tq // DIL16, stride=DIL16)
        for g in range(NGRP + 1):
            nat_ref[g, dst, :] = o16_ref[0, cl, :, g * LANES:(g + 1) * LANES]

    a_n = _natural_offset(lax.broadcasted_iota(jnp.int32, (QB, 2 * QB), 0))
    c_p = lax.broadcasted_iota(jnp.int32, (QB, 2 * QB), 1)
    d = jnp.where(c_p < QB, QB, 0) + a_n - _natural_offset(c_p)
    near = (d >= 0) & (d <= QB)
    far = (d >= 0) & ((d & (DIL4 - 1)) == 0)
    bias_near0 = jnp.where(near & far, LN2, jnp.where(near | far, 0.0, NEG))
    ia = lax.broadcasted_iota(jnp.int32, (CLS4, QB), 0)
    cf = lax.broadcasted_iota(jnp.int32, (CLS4, QB), 1)
    fblk, ic = cf >> CLS4_SHIFT, cf & (CLS4 - 1)
    bias_far0 = jnp.where((fblk < NFAR) & ((fblk > 0) | (ic >= ia)), 0.0, NEG)
    lane = lax.broadcasted_iota(jnp.int32, (QB, LANES), 1)
    head_a = lane < HEAD_DIM

    def rows_of(w):
        blk, off = divmod(w, tq)
        return kv_refs[blk], off

    def group(x, c):
        return jnp.concatenate([x[c * CLS4:(c + 1) * CLS4], x[QB + c * CLS4:QB + (c + 1) * CLS4]], axis=0)

    def ungroup(parts):
        return jnp.concatenate([p_[:CLS4] for p_ in parts] + [p_[CLS4:] for p_ in parts], axis=0)

    units = [(sub, g) for sub in range(tq // QB) for g in range(NGRP)]
    ctx = {}
    for sub in range(tq // QB):
        first_valid = BACK - i * tq - sub * QB
        bias_near = jnp.where((c_p >= QB) | (NFAR * QB >= first_valid), bias_near0, NEG)
        bias_far = jnp.where(fblk * QB >= first_valid, bias_far0, NEG)
        far_blocks = [rows_of((sub + j) * QB) for j in range(NFAR)]
        ctx[sub] = dict(bias_near=jnp.concatenate([bias_near, bias_near], axis=0),
                        bias_far=jnp.concatenate([bias_far, bias_far], axis=0),
                        prev=rows_of((sub + NFAR) * QB), cur=rows_of((sub + NFAR + 1) * QB),
                        far=far_blocks + far_blocks[:1])

    def far_rows(sub, c, colsl):
        return jnp.concatenate([r[0, off + c * CLS4:off + (c + 1) * CLS4, colsl] for r, off in ctx[sub]["far"]],
                               axis=0)

    def block(sub, which, colsl):
        r, off = ctx[sub][which]
        return r[0, off:off + QB, colsl]

    scores, probs, outs = {}, {}, {}

    def stage_scores(u):
        sub, g = u
        kc = slice(P_K + g * LANES, P_K + (g + 1) * LANES)
        q2 = _pair_split(q_ref[0, sub * QB:(sub + 1) * QB, g * LANES:(g + 1) * LANES].astype(F32)).astype(BF16)
        s_far = ungroup([_dot_t(group(q2, c), far_rows(sub, c, kc)) + ctx[sub]["bias_far"] for c in range(DIL4)])
        s_near = jnp.concatenate([_dot_t(q2, block(sub, "prev", kc)), _dot_t(q2, block(sub, "cur", kc))],
                                 axis=1) + ctx[sub]["bias_near"]
        scores[u] = jnp.concatenate([s_far, s_near], axis=1)

    def stage_softmax(u):
        s = scores.pop(u)
        m = jnp.max(s, axis=-1, keepdims=True)
        p = jnp.exp(s - m)
        probs[u] = (m, jnp.sum(p, axis=-1, keepdims=True), p.astype(BF16))

    def stage_values(u):
        sub, g = u
        vc = slice(P_V + g * LANES, P_V + (g + 1) * LANES)
        pb = probs[u][2]
        outs[u] = (_dot(pb[:, QB:2 * QB], block(sub, "prev", vc)) + _dot(pb[:, 2 * QB:], block(sub, "cur", vc))
                   + ungroup([_dot(group(pb[:, :QB], c), far_rows(sub, c, vc)) for c in range(DIL4)]))

    def stage_merge(u):
        sub, g = u
        rows = slice(sub * QB, (sub + 1) * QB)
        cols = slice(g * LANES, (g + 1) * LANES)
        m, l, _ = probs.pop(u)
        o_s = _pair_join(outs.pop(u))
        m_s = jnp.where(head_a, m[:QB], m[QB:])
        l_s = jnp.where(head_a, l[:QB], l[QB:])
        for c in range(DIL4):
            dst = pl.ds(c, CLS4, stride=DIL4)
            unp_ref[3 * g, dst, :] = o_s[c * CLS4:(c + 1) * CLS4]
            unp_ref[3 * g + 1, dst, :] = m_s[c * CLS4:(c + 1) * CLS4]
            unp_ref[3 * g + 2, dst, :] = l_s[c * CLS4:(c + 1) * CLS4]
        o_n, m_n, l_n = unp_ref[3 * g], unp_ref[3 * g + 1], unp_ref[3 * g + 2]
        st = nat_ref[NGRP, rows, :]
        m_f = jnp.where(head_a, st[:, 2 * g:2 * g + 1], st[:, 2 * g + 1:2 * g + 2])
        l_f = jnp.where(head_a, st[:, STAT_L + 2 * g:STAT_L + 2 * g + 1],
                        st[:, STAT_L + 2 * g + 1:STAT_L + 2 * g + 2])
        mx = jnp.maximum(m_n, m_f)
        w_n = jnp.exp(m_n - mx)
        w_f = jnp.exp(m_f - mx)
        oa = (o_n * w_n + nat_ref[g, rows, :] * w_f) / (l_n * w_n + l_f * w_f)
        mixa_ref[rows, cols] = (oa * ga_ref[0, rows, cols]).astype(BF16)

    def finish(rows):
        z = (_dot(mixa_ref[rows, :], woutbf_ref[0:A_WIDTH, :])
             + _dot(mix_ref[0, rows, :], woutbf_ref[A_WIDTH:, :]))
        y_ref[0, rows, :] = x_ref[0, rows, :] + _rmsnorm(z, npost_ref[...])

    stages = (stage_scores, stage_softmax, stage_values, stage_merge)
    half = len(units) // 2
    for t in range(len(units) + len(stages) - 1):
        for depth, stage in enumerate(stages):
            if 0 <= t - depth < len(units):
                stage(units[t - depth])
        if t - (len(stages) - 1) == half - 1:
            finish(slice(0, tq // 2))
    finish(slice(tq // 2, tq))
    kvp_ref[...] = kvc_ref[...]

    for part in _sample_mix_stages(pl.program_id(0) * pl.num_programs(1) + i, srow_ref, kvt_ref, ck_ref, cv_ref,
                                   sst_ref, cmk_ref, cmv_ref, wdw_ref, smix_ref, wk_ref, ns_ref, uc_ref,
                                   t_s=t_s, win_len=win_len):
        part()


def _prompt_attn(kvq, o16, ga, mix, x, wout, npost, srow, kvt, ck, cv, st_pad, cmk, cmv, wdw, tq, t_s):
    b, s, _ = x.shape
    nsteps = s // tq
    nb, _, win_len = ck.shape
    assert tq == BACK and b * nsteps == nb, "one sample batch per grid step"
    row = lambda w: pl.BlockSpec((1, tq, w), lambda bi, i: (bi, i, 0))
    qspec = pl.BlockSpec((1, tq, A_WIDTH), lambda bi, i: (bi, i, P_Q // A_WIDTH))
    kvcur = pl.BlockSpec((1, tq, 2 * A_WIDTH), lambda bi, i: (bi, i, 0))
    cls = pl.BlockSpec((1, DIL16, tq // DIL16, O16_W), lambda bi, i: (bi, 0, i, 0))
    srows = pl.BlockSpec((t_s, S_ROW), lambda bi, i: (bi * nsteps + i, 0))
    per = lambda a: pl.BlockSpec((1,) + a.shape[1:], lambda bi, i: (bi * nsteps + i, 0, 0))
    return pl.pallas_call(
        functools.partial(_prompt_attn_kernel, tq=tq, t_s=t_s, win_len=win_len),
        grid=(b, nsteps),
        in_specs=[qspec, kvcur, cls, row(A_WIDTH), row(C_WIDTH + M_WIDTH), row(D_MODEL),
                  _resident(wout), _full(npost),
                  srows, _full(kvt), per(ck), per(cv), per(st_pad), per(cmk), per(cmv), _full(wdw)],
        out_specs=[row(D_MODEL), srows, per(ck),
                   pl.BlockSpec((1, CONV_W - 1, C_WIDTH), lambda bi, i: (bi * nsteps + i, 0, 0))],
        out_shape=[jax.ShapeDtypeStruct((b, s, D_MODEL), F32),
                   jax.ShapeDtypeStruct((nb * t_s, S_ROW), F32),
                   jax.ShapeDtypeStruct(ck.shape, F32),
                   jax.ShapeDtypeStruct((nb, CONV_W - 1, C_WIDTH), F32)],
        scratch_shapes=[pltpu.VMEM((tq, A_WIDTH), BF16),
                        pltpu.VMEM((NGRP + 1, tq, LANES), F32),
                        pltpu.VMEM((3 * NGRP, QB, LANES), F32),
                        pltpu.VMEM((STATE_PAD + CONV_W - 1 + t_s, C_WIDTH), F32),
                        pltpu.VMEM((1, tq, 2 * A_WIDTH), BF16),
                        pltpu.VMEM(wout.shape, BF16)],
        compiler_params=pltpu.CompilerParams(dimension_semantics=("arbitrary", "arbitrary"),
                                             vmem_limit_bytes=VMEM_LIMIT),
        name="prompt_attn",
    )(kvq, kvq, o16, ga, mix, x, wout, npost, srow, kvt, ck, cv, st_pad, cmk, cmv, wdw)


def _sample_proj_kernel(x_ref, npre_ref, win_ref, ang_ref, srow_ref, kvt_ref, gate_ref):
    h = _rmsnorm(x_ref[...], npre_ref[...]).astype(BF16)
    proj = _dot(h, win_ref[...].astype(BF16))
    pats = _rope_patterns(ang_ref[:, 0:LANES], ang_ref[:, LANES:])
    srow_ref[:, S_Q:S_U] = _rope(proj[:, O_QA:O_KA], pats) * SCALE
    srow_ref[:, S_U:S_QM] = proj[:, O_AB:O_BB] * _sigmoid(proj[:, O_BB:O_GB])
    srow_ref[:, S_QM:S_ROW] = proj[:, O_QM:O_GM]
    kv_t = jnp.concatenate([_rope(proj[:, O_KA:O_VA], pats), proj[:, O_VA:O_GA]], axis=1).T
    for slab in range(kvt_ref.shape[0]):
        kvt_ref[slab] = kv_t[:, slab * LANES:(slab + 1) * LANES]
    gate_ref[:, S_Q:S_U] = _silu(proj[:, O_GA:O_AB])
    gate_ref[:, S_U:S_QM] = _silu(proj[:, O_GB:O_QM])
    gate_ref[:, S_QM:S_ROW] = _silu(proj[:, O_GM:D_IN])


def _sample_proj(x, npre, win, ang):
    n = x.shape[0]
    assert n % LANES == 0
    shapes = [(n, S_ROW), (n // LANES, 2 * A_WIDTH, LANES), (n, S_ROW)]
    return pl.pallas_call(
        _sample_proj_kernel,
        grid=(1,),
        in_specs=[_full(x), _full(npre), _full(win), _full(ang)],
        out_specs=[pl.BlockSpec(sh, lambda i, nd=len(sh): (0,) * nd) for sh in shapes],
        out_shape=[jax.ShapeDtypeStruct(sh, F32) for sh in shapes],
        compiler_params=pltpu.CompilerParams(vmem_limit_bytes=VMEM_LIMIT),
        name="sample_proj",
    )(x, npre, win, ang)


def _sample_out_kernel(x_ref, smix_ref, gate_ref, bdw_ref, lng_ref, lnb_ref, wpw_ref, bpw_ref, wout_ref, npost_ref,
                       y_ref):
    mixed_a = (smix_ref[:, S_Q:S_U] * gate_ref[:, S_Q:S_U]).astype(BF16)
    mixed_b = _conformer_tail(smix_ref[:, S_U:S_QM], gate_ref[:, S_U:S_QM], bdw_ref[...], lng_ref[...], lnb_ref[...],
                              wpw_ref, bpw_ref[...]).astype(BF16)
    mixed_m = (smix_ref[:, S_QM:S_ROW] * gate_ref[:, S_QM:S_ROW]).astype(BF16)
    wout = wout_ref[...].astype(BF16)
    z = (_dot(mixed_a, wout[0:A_WIDTH]) + _dot(mixed_b, wout[A_WIDTH:A_WIDTH + C_WIDTH])
         + _dot(mixed_m, wout[A_WIDTH + C_WIDTH:]))
    y_ref[...] = x_ref[...] + _rmsnorm(z, npost_ref[...])


def _sample_out(x, smix, gate, bdw, lng, lnb, wpw, bpw, wout, npost):
    args = (x, smix, gate, bdw, lng, lnb, wpw, bpw, wout, npost)
    return pl.pallas_call(
        _sample_out_kernel,
        grid=(1,),
        in_specs=[_full(a) for a in args],
        out_specs=_full(x),
        out_shape=jax.ShapeDtypeStruct(x.shape, F32),
        compiler_params=pltpu.CompilerParams(vmem_limit_bytes=VMEM_LIMIT),
        name="sample_out",
    )(*args)


def _rope_cos_sin(pos):
    inv = ROPE_THETA ** (-jnp.arange(0, ROT_DIM, 2, dtype=F32) / ROT_DIM)
    ang = pos.astype(F32)[:, None] * jnp.tile(inv, 2 * LANES // ROT_DIM)[None, :]
    return jnp.concatenate([jnp.cos(ang), jnp.sin(ang)], axis=1)


def _feature_major(cache):
    nb, rows, heads, dim = cache.shape
    return jnp.transpose(cache, (0, 2, 3, 1)).reshape(nb, heads * dim, rows)


def _row_major(cache_t, heads):
    nb, width, rows = cache_t.shape
    return jnp.transpose(cache_t.reshape(nb, heads, width // heads, rows), (0, 3, 1, 2))


PROJ_TILE = 512
ATTN_TILE = BACK


def kernel(x_prompt, x_sample, cache_win_k, cache_win_v, state_conv, cache_mem_k, cache_mem_v, mem_prompt,
           norm_pre, norm_post, w_in, w_out, norm_mem, w_mem_kv, w_dw, b_dw, ln_conv_g, ln_conv_b, w_pw2, b_pw2):
    depth = w_in.shape[0]
    assert depth == 1, "single-layer step"
    b, s, _ = x_prompt.shape
    nb, t_s, _ = x_sample.shape
    win_len = cache_win_k.shape[2]
    assert s % (DIL16 * QB) == 0 and s % PROJ_TILE == 0 and s % ATTN_TILE == 0
    assert win_len == MAX_WINDOW and win_len % LANES == 0
    assert t_s % 8 == 0 and t_s & (t_s - 1) == 0 and t_s < LANES
    l = 0
    row = lambda a: a[l][None, :]
    npre, npost, nmem = row(norm_pre), row(norm_post), row(norm_mem)
    bdw, lng, lnb, bpw = row(b_dw), row(ln_conv_g), row(ln_conv_b), row(b_pw2)
    win, wout, wmem, wpw = w_in[l], w_out[l], w_mem_kv[l], w_pw2[l]
    wdw = w_dw[l]

    pos_s = PAST_LEN + jnp.arange(t_s, dtype=jnp.int32)
    xs = x_sample.reshape(nb * t_s, D_MODEL)
    srow, kvt, gate_s = _sample_proj(xs, npre, win, jnp.tile(_rope_cos_sin(pos_s), (nb, 1)))
    st_pad = jnp.pad(state_conv[l], ((0, 0), (STATE_PAD, 0), (0, 0)))

    mk, mv, mkb, mvb = _mem_kv(mem_prompt, nmem, wmem)
    rin = _rope_cos_sin(jnp.arange(PROJ_TILE, dtype=jnp.int32))
    rbase = _rope_cos_sin(jnp.arange(0, s, PROJ_TILE, dtype=jnp.int32))[:, None, :]
    cv = _feature_major(cache_win_v[l])
    kvq, kvq16, kf, vf, ga, mix, ust, wv = _prompt_proj(x_prompt, npre, win, rin, rbase, mkb, mvb,
                                                        wdw, bdw, lng, lnb, wpw, bpw, kvt, cv, PROJ_TILE, t_s)
    o16 = _dilated16(kvq16)
    y_prompt, smix, wk, nst = _prompt_attn(
        kvq, o16, ga, mix, x_prompt, wout, npost, srow, kvt,
        _feature_major(cache_win_k[l]), cv, st_pad,
        _feature_major(cache_mem_k[l]), _feature_major(cache_mem_v[l]), wdw, ATTN_TILE, t_s)
    y_sample = _sample_out(xs, smix, gate_s, bdw, lng, lnb, wpw, bpw, wout, npost)

    keep_p = kf.shape[1]
    return (y_prompt,
            y_sample.reshape(nb, t_s, D_MODEL),
            kf.reshape(1, b, keep_p, A_HEADS, HEAD_DIM),
            vf.reshape(1, b, keep_p, A_HEADS, HEAD_DIM),
            ust[:, CONV_HIST - (CONV_W - 1):, :][None],
            mk.reshape(1, b, N_MEM, M_HEADS, HEAD_DIM),
            mv.reshape(1, b, N_MEM, M_HEADS, HEAD_DIM),
            _row_major(wk, A_HEADS)[None],
            _row_major(wv, A_HEADS)[None],
            nst[None])
```

```python
import functools
import math

import jax
import jax.numpy as jnp
from jax import lax
from jax.experimental import pallas as pl
from jax.experimental.pallas import tpu as pltpu

F32 = jnp.float32
BF16 = jnp.bfloat16

D_MODEL = 1024
HEAD_DIM = 64
HEAD_SHIFT = 6
A_WIDTH = 384
A_HEADS = 6
M_WIDTH = 256
M_HEADS = 4
C_WIDTH = 384
ROT_DIM = 16
ROPE_THETA = 500000.0
CONV_W = 31
N_MEM = 256
MAX_WINDOW = 2048
PAST_LEN = 16384
EPS = 1e-6
SCALE = HEAD_DIM ** -0.5
NEG = -1e30
LN2 = math.log(2.0)

LANES = 128
NGRP = A_WIDTH // LANES
QB = 128
DIL16 = 16
DIL4 = 4
CLS4 = QB // DIL4
CLS4_SHIFT = 5
O_QA, O_KA, O_VA, O_GA, O_AB, O_BB, O_GB, O_QM, O_GM, D_IN = 0, 384, 768, 1152, 1536, 1920, 2304, 2688, 2944, 3200
P_K, P_V, P_Q, P_W = 0, A_WIDTH, 2 * A_WIDTH, 3 * A_WIDTH

VMEM_LIMIT = 60 * 1024 * 1024


def _sigmoid(x):
    return 1.0 / (1.0 + jnp.exp(-x))


def _silu(x):
    return x * _sigmoid(x)


def _rmsnorm(x, g):
    return x * lax.rsqrt(jnp.mean(x * x, axis=-1, keepdims=True) + EPS) * g


def _dot(a, b):
    return jnp.dot(a, b, preferred_element_type=F32)


def _dot_t(a, b):
    return lax.dot_general(a, b, (((1,), (1,)), ((), ())), preferred_element_type=F32)


def _rope_patterns(cos, sin):
    in_head = lax.broadcasted_iota(jnp.int32, cos.shape, 1) & (HEAD_DIM - 1)
    rot, lo = in_head < ROT_DIM, in_head < ROT_DIM // 2
    return jnp.where(rot, cos, 1.0), jnp.where(lo, -sin, 0.0), jnp.where(rot & ~lo, sin, 0.0)


def _rope(xw, pats):
    cos, s1, s2 = pats
    outs = []
    for g in range(xw.shape[1] // LANES):
        xg = xw[:, g * LANES:(g + 1) * LANES]
        outs.append(xg * cos + pltpu.roll(xg, LANES - 8, 1) * s1 + pltpu.roll(xg, 8, 1) * s2)
    return jnp.concatenate(outs, axis=1)


def _pair_split(q):
    lane = lax.broadcasted_iota(jnp.int32, q.shape, 1)
    qa = jnp.where(lane < HEAD_DIM, q, 0.0)
    qb = jnp.where(lane >= HEAD_DIM, q, 0.0)
    return jnp.concatenate([qa, qb], axis=0)


def _pair_join(x):
    t = x.shape[0] // 2
    lane = lax.broadcasted_iota(jnp.int32, (t, LANES), 1)
    return jnp.where(lane < HEAD_DIM, x[:t], x[t:])


def _mem_attend(qm, mk_ref, mv_ref):
    outs = []
    for g in range(M_WIDTH // LANES):
        cols = slice(g * LANES, (g + 1) * LANES)
        q2 = _pair_split(qm[:, cols] * SCALE).astype(BF16)
        s = _dot_t(q2, mk_ref[0, :, cols])
        m = jnp.max(s, axis=-1, keepdims=True)
        p = jnp.exp(s - m)
        l = jnp.sum(p, axis=-1, keepdims=True)
        o = _dot(p.astype(BF16), mv_ref[0, :, cols]) * (1.0 / l)
        outs.append(_pair_join(o))
    return jnp.concatenate(outs, axis=1)


def _conformer_tail(c, gate_b, bdw, lng, lnb, wpw_ref, bpw):
    cf = c + bdw
    mu = jnp.mean(cf, axis=-1, keepdims=True)
    dev = cf - mu
    var = jnp.mean(dev * dev, axis=-1, keepdims=True)
    cn = dev * lax.rsqrt(var + EPS) * lng + lnb
    ob = _dot(_silu(cn).astype(BF16), wpw_ref[...].astype(BF16)) + bpw
    return ob * gate_b


def _full(a):
    return pl.BlockSpec(a.shape, lambda *_: (0,) * a.ndim)


def _resident(a):
    return pl.BlockSpec(a.shape, lambda *_: (0,) * a.ndim, pipeline_mode=pl.Buffered(1))


def _mem_kv_kernel(mem_ref, g_ref, w_ref, mk_ref, mv_ref, mkb_ref, mvb_ref):
    h = _rmsnorm(mem_ref[0], g_ref[...]).astype(BF16)
    kv = _dot(h, w_ref[...].astype(BF16))
    mk_ref[0] = kv[:, :M_WIDTH]
    mv_ref[0] = kv[:, M_WIDTH:]
    mkb_ref[0] = kv[:, :M_WIDTH].astype(BF16)
    mvb_ref[0] = kv[:, M_WIDTH:].astype(BF16)


def _mem_kv(mem, g, wmem):
    b = mem.shape[0]
    blk = pl.BlockSpec((1, N_MEM, M_WIDTH), lambda i: (i, 0, 0))
    return pl.pallas_call(
        _mem_kv_kernel,
        grid=(b,),
        in_specs=[pl.BlockSpec((1, N_MEM, D_MODEL), lambda i: (i, 0, 0)), _full(g), _full(wmem)],
        out_specs=[blk, blk, blk, blk],
        out_shape=[jax.ShapeDtypeStruct((b, N_MEM, M_WIDTH), F32)] * 2
        + [jax.ShapeDtypeStruct((b, N_MEM, M_WIDTH), BF16)] * 2,
        name="mem_kv",
    )(mem, g, wmem)


CONV_HIST = 32
CONV_CHUNK = 64


def _prompt_proj_kernel(x_ref, npre_ref, win_ref, rin_ref, rbase_ref, mk_ref, mv_ref,
                        wdw_ref, bdw_ref, lng_ref, lnb_ref, wpw_ref, bpw_ref, kvt_ref, cv_ref,
                        kvq_ref, kvq16_ref, kf_ref, vf_ref, ga_ref, mix_ref, ust_ref, wv_ref,
                        uext_ref, ush_ref, conv_ref, stage_ref, *, t_s, win_len):
    t = x_ref.shape[1]
    i = pl.program_id(1)

    @pl.when(i == 0)
    def _():
        uext_ref[0:CONV_HIST, :] = jnp.zeros((CONV_HIST, C_WIDTH), F32)

    h = _rmsnorm(x_ref[0], npre_ref[...]).astype(BF16)
    proj = {}

    def project(name, c0, c1):
        def run():
            proj[name] = _dot(h, win_ref[:, c0:c1])
        return run

    def stage(idx, val):
        for g in range(NGRP):
            stage_ref[NGRP * idx + g] = val[:, g * LANES:(g + 1) * LANES]

    pats = []

    def rope_pats():
        if not pats:
            cr, sr = rin_ref[:, 0:LANES], rin_ref[:, LANES:]
            cb, sb = rbase_ref[i, :, 0:LANES], rbase_ref[i, :, LANES:]
            pats.append(_rope_patterns(cb * cr - sb * sr, sb * cr + cb * sr))
        return pats[0]

    def do_k():
        k = _rope(proj.pop("k"), rope_pats())
        kf_ref[0] = k
        stage(0, k)

    def do_q():
        stage(2, _rope(proj.pop("q"), rope_pats()) * SCALE)

    def do_v():
        v = proj.pop("v")
        vf_ref[0] = v
        stage(1, v)

    def do_ga():
        ga_ref[0] = _silu(proj.pop("ga"))

    def do_copies():
        for c in range(DIL16):
            rows_c = [stage_ref[s, pl.ds(c, t // DIL16, stride=DIL16), :] for s in range(3 * NGRP)]
            kvq16_ref[0, c] = jnp.concatenate(rows_c, axis=1).astype(BF16)
        for blk in range(t // QB):
            for c in range(DIL4):
                rows_c = [stage_ref[s, pl.ds(blk * QB + c, CLS4, stride=DIL4), :] for s in range(3 * NGRP)]
                kvq_ref[0, blk * QB + c * CLS4:blk * QB + (c + 1) * CLS4, :] = (
                    jnp.concatenate(rows_c, axis=1).astype(BF16))

    def do_glu():
        uext_ref[CONV_HIST:CONV_HIST + t, :] = proj.pop("ab") * _sigmoid(proj.pop("bb"))
        uext = uext_ref[...]
        for r in range(1, 8):
            ush_ref[r - 1] = pltpu.roll(uext, r, 0)

    def conv_rows(r0):
        def run():
            acc = jnp.zeros((CONV_CHUNK, C_WIDTH), F32)
            for kk in range(CONV_W):
                tap = wdw_ref[CONV_W - 1 - kk:CONV_W - kk, :]
                r, base = kk % 8, CONV_HIST + r0 - (kk - kk % 8)
                if r == 0:
                    slab = uext_ref[base:base + CONV_CHUNK, :]
                else:
                    slab = ush_ref[r - 1, base:base + CONV_CHUNK, :]
                acc = acc + slab * tap
            conv_ref[r0:r0 + CONV_CHUNK, :] = acc
        return run

    def do_history():
        tail = uext_ref[t:t + CONV_HIST, :]
        uext_ref[0:CONV_HIST, :] = tail
        ust_ref[0] = tail

    def do_tail():
        mixed_b = _conformer_tail(conv_ref[...], _silu(proj.pop("gb")), bdw_ref[...], lng_ref[...], lnb_ref[...],
                                  wpw_ref, bpw_ref[...])
        mix_ref[0, :, 0:C_WIDTH] = mixed_b.astype(BF16)

    def do_mem():
        mixed_m = _mem_attend(proj.pop("qm"), mk_ref, mv_ref) * _silu(proj.pop("gm"))
        mix_ref[0, :, C_WIDTH:] = mixed_m.astype(BF16)

    convs = [conv_rows(r0) for r0 in range(0, t, CONV_CHUNK)]
    riders = [([project("k", O_KA, O_VA)], do_k), ([project("q", O_QA, O_KA)], do_q),
              ([project("v", O_VA, O_GA)], do_v), ([project("ga", O_GA, O_AB)], do_ga),
              ([project("qm", O_QM, O_GM), project("gm", O_GM, D_IN)], do_mem),
              ([project("gb", O_GB, O_QM)], do_copies)]
    slide_v = functools.partial(_slide_cache, pl.program_id(0) * pl.num_programs(1) + i, kvt_ref, cv_ref, A_WIDTH,
                                wv_ref, t_s=t_s, win_len=win_len)
    program = [project("ab", O_AB, O_BB), project("bb", O_BB, O_GB), slide_v, do_glu]
    for n, conv in enumerate(convs):
        if n < len(riders):
            program += riders[n][0]
        program.append(conv)
        if n < len(riders) and riders[n][1] is not None:
            program.append(riders[n][1])
    assert len(convs) >= len(riders)
    program += [do_history, do_tail]
    for piece in program:
        piece()
    assert not proj


def _prompt_proj(x, npre, win, rin, rbase, mkb, mvb, wdw, bdw, lng, lnb, wpw, bpw, kvt, cv, tile, t_s):
    b, s, _ = x.shape
    nt = s // tile
    nb, _, win_len = cv.shape
    assert b * nt == nb, "one sample batch per grid step"
    per_batch = pl.BlockSpec((1,) + cv.shape[1:], lambda bi, i: (bi * nt + i, 0, 0))
    keep = min(MAX_WINDOW, s)
    first_keep = (s - keep) // tile
    row = lambda w: pl.BlockSpec((1, tile, w), lambda bi, i: (bi, i, 0))
    memb = pl.BlockSpec((1, N_MEM, M_WIDTH), lambda bi, i: (bi, 0, 0))
    keepb = pl.BlockSpec((1, tile, A_WIDTH), lambda bi, i: (bi, jnp.maximum(i - first_keep, 0), 0))
    cls = pl.BlockSpec((1, DIL16, tile // DIL16, P_W), lambda bi, i: (bi, 0, i, 0))
    return pl.pallas_call(
        functools.partial(_prompt_proj_kernel, t_s=t_s, win_len=win_len),
        grid=(b, nt),
        in_specs=[row(D_MODEL), _full(npre), _resident(win), _full(rin), _full(rbase),
                  memb, memb, _full(wdw), _full(bdw), _full(lng), _full(lnb), _full(wpw), _full(bpw),
                  _full(kvt), per_batch],
        out_specs=[row(P_W), cls, keepb, keepb, row(A_WIDTH), row(C_WIDTH + M_WIDTH),
                   pl.BlockSpec((1, CONV_HIST, C_WIDTH), lambda bi, i: (bi, 0, 0)), per_batch],
        out_shape=[jax.ShapeDtypeStruct((b, s, P_W), BF16),
                   jax.ShapeDtypeStruct((b, DIL16, s // DIL16, P_W), BF16),
                   jax.ShapeDtypeStruct((b, keep, A_WIDTH), F32),
                   jax.ShapeDtypeStruct((b, keep, A_WIDTH), F32),
                   jax.ShapeDtypeStruct((b, s, A_WIDTH), F32),
                   jax.ShapeDtypeStruct((b, s, C_WIDTH + M_WIDTH), BF16),
                   jax.ShapeDtypeStruct((b, CONV_HIST, C_WIDTH), F32),
                   jax.ShapeDtypeStruct(cv.shape, F32)],
        scratch_shapes=[pltpu.VMEM((CONV_HIST + tile, C_WIDTH), F32),
                        pltpu.VMEM((7, CONV_HIST + tile, C_WIDTH), F32),
                        pltpu.VMEM((tile, C_WIDTH), F32),
                        pltpu.VMEM((3 * NGRP, tile, LANES), F32)],
        compiler_params=pltpu.CompilerParams(dimension_semantics=("arbitrary", "arbitrary"),
                                             vmem_limit_bytes=VMEM_LIMIT),
        name="prompt_proj",
    )(x, npre, win, rin, rbase, mkb, mvb, wdw, bdw, lng, lnb, wpw, bpw, kvt, cv)


STAT_L = 8
O16_W = A_WIDTH + LANES


def _dilated16_kernel(kvq_ref, o_ref):
    a = lax.broadcasted_iota(jnp.int32, (QB, QB), 0)
    c = lax.broadcasted_iota(jnp.int32, (QB, QB), 1)
    tri_prev = jnp.where(c >= a, 0.0, NEG)
    tri_cur = jnp.where(c <= a, 0.0, NEG)
    bias_cur = jnp.concatenate([tri_cur, tri_cur], axis=0)
    both = jnp.concatenate([tri_prev, tri_cur], axis=1)
    bias_both = jnp.concatenate([both, both], axis=0)
    lane = lax.broadcasted_iota(jnp.int32, (QB, LANES), 1)
    nsub = kvq_ref.shape[2] // QB
    units = [(sub, g) for sub in range(nsub) for g in range(NGRP)]
    scores, probs = {}, {}
    stats = {sub: jnp.zeros((QB, LANES), F32) for sub in range(nsub)}

    def rows_of(sub):
        return slice(sub * QB, (sub + 1) * QB)

    def stage_scores(u):
        sub, g = u
        kc = slice(P_K + g * LANES, P_K + (g + 1) * LANES)
        qc = slice(P_Q + g * LANES, P_Q + (g + 1) * LANES)
        q2 = _pair_split(kvq_ref[0, 0, rows_of(sub), qc].astype(F32)).astype(BF16)
        s_cur = _dot_t(q2, kvq_ref[0, 0, rows_of(sub), kc])
        if sub == 0:
            scores[u] = s_cur + bias_cur
        else:
            scores[u] = jnp.concatenate([_dot_t(q2, kvq_ref[0, 0, rows_of(sub - 1), kc]), s_cur], axis=1) + bias_both

    def stage_softmax(u):
        s = scores.pop(u)
        m = jnp.max(s, axis=-1, keepdims=True)
        p = jnp.exp(s - m)
        probs[u] = (m, jnp.sum(p, axis=-1, keepdims=True), p.astype(BF16))

    def stage_values(u):
        sub, g = u
        vc = slice(P_V + g * LANES, P_V + (g + 1) * LANES)
        m, l, pb = probs.pop(u)
        if sub == 0:
            o = _dot(pb, kvq_ref[0, 0, rows_of(sub), vc])
        else:
            o = _dot(pb[:, :QB], kvq_ref[0, 0, rows_of(sub - 1), vc]) + _dot(pb[:, QB:], kvq_ref[0, 0, rows_of(sub), vc])
        o_ref[0, 0, rows_of(sub), g * LANES:(g + 1) * LANES] = _pair_join(o)
        st = stats[sub]
        st = jnp.where(lane == 2 * g, m[:QB], st)
        st = jnp.where(lane == 2 * g + 1, m[QB:], st)
        st = jnp.where(lane == STAT_L + 2 * g, l[:QB], st)
        stats[sub] = jnp.where(lane == STAT_L + 2 * g + 1, l[QB:], st)
        if g == NGRP - 1:
            o_ref[0, 0, rows_of(sub), A_WIDTH:] = stats.pop(sub)

    stages = (stage_scores, stage_softmax, stage_values)
    for t in range(len(units) + len(stages) - 1):
        for depth, stage in enumerate(stages):
            if 0 <= t - depth < len(units):
                stage(units[t - depth])


def _dilated16(kvq16):
    b, ncls, nj, w = kvq16.shape
    return pl.pallas_call(
        _dilated16_kernel,
        grid=(b, ncls),
        in_specs=[pl.BlockSpec((1, 1, nj, w), lambda bi, c: (bi, c, 0, 0))],
        out_specs=pl.BlockSpec((1, 1, nj, O16_W), lambda bi, c: (bi, c, 0, 0)),
        out_shape=jax.ShapeDtypeStruct((b, ncls, nj, O16_W), F32),
        compiler_params=pltpu.CompilerParams(dimension_semantics=("arbitrary",) * 2,
                                             vmem_limit_bytes=VMEM_LIMIT),
        name="dilated16",
    )(kvq16)


STATE_PAD = 2
S_Q, S_U, S_QM, S_ROW = 0, A_WIDTH, A_WIDTH + C_WIDTH, A_WIDTH + C_WIDTH + M_WIDTH


def _new_rows(batch, kvt_ref, lo, hi, t_s):
    per_slab = LANES // t_s
    slab, pos = batch // per_slab, batch % per_slab
    return pltpu.roll(kvt_ref[slab, lo:hi, :], LANES - t_s - pos * t_s, 1)


def _slide_cache(batch, kvt_ref, src_ref, lo, dst_ref, *, t_s, win_len):
    new0 = LANES - t_s
    keep = lax.broadcasted_iota(jnp.int32, (A_WIDTH, LANES), 1) < new0
    nch = win_len // LANES
    nxt = pltpu.roll(src_ref[0, :, 0:LANES], new0, 1)
    for c in range(nch):
        cur = nxt
        nxt = (pltpu.roll(src_ref[0, :, (c + 1) * LANES:(c + 2) * LANES], new0, 1) if c + 1 < nch
               else _new_rows(batch, kvt_ref, lo, lo + A_WIDTH, t_s))
        dst_ref[0, :, c * LANES:(c + 1) * LANES] = jnp.where(keep, cur, nxt)


def _sample_mix_stages(batch, srow_ref, kvt_ref, ck_ref, cv_ref, st_ref, cmk_ref, cmv_ref, wdw_ref,
                       smix_ref, wk_ref, ns_ref, uc_ref, *, t_s, win_len):
    new_rows = functools.partial(_new_rows, batch, kvt_ref, t_s=t_s)
    return [functools.partial(_slide_cache, batch, kvt_ref, ck_ref, 0, wk_ref, t_s=t_s, win_len=win_len),
            functools.partial(_sample_attend, new_rows, srow_ref, ck_ref, cv_ref, smix_ref, t_s=t_s, win_len=win_len),
            functools.partial(_sample_mem_conv, srow_ref, st_ref, cmk_ref, cmv_ref, wdw_ref, smix_ref, ns_ref,
                              uc_ref, t_s=t_s)]


def _sample_attend(new_rows, srow_ref, ck_ref, cv_ref, smix_ref, *, t_s, win_len):
    new0 = LANES - t_s
    knt, vnt = new_rows(0, A_WIDTH), new_rows(A_WIDTH, 2 * A_WIDTH)
    q = srow_ref[:, S_Q:S_U]
    head_of_lane = lax.broadcasted_iota(jnp.int32, (t_s, A_WIDTH), 1) >> HEAD_SHIFT
    q6 = jnp.concatenate([jnp.where(head_of_lane == h, q, 0.0) for h in range(A_HEADS)], axis=0).astype(BF16)
    s_c = _dot(q6, ck_ref[0].astype(BF16))
    s_n = _dot(q6, knt.astype(BF16))

    def weights(shape, key0, lo):
        tq = lax.broadcasted_iota(jnp.int32, shape, 0) & (t_s - 1)
        key = lax.broadcasted_iota(jnp.int32, shape, 1) + key0
        d = win_len + tq - key
        ok = (d >= 0) & (key >= lo)
        w = ((d <= 128).astype(F32) + ((d <= 512) & ((d & 3) == 0)).astype(F32)
             + ((d <= 2048) & ((d & 15) == 0)).astype(F32))
        return jnp.where(ok, w, 0.0)

    w_c = weights(s_c.shape, 0, 0)
    w_n = weights(s_n.shape, win_len - new0, win_len)
    s_c = jnp.where(w_c > 0.0, s_c, NEG)
    s_n = jnp.where(w_n > 0.0, s_n, NEG)
    m = jnp.maximum(jnp.max(s_c, axis=-1, keepdims=True), jnp.max(s_n, axis=-1, keepdims=True))
    p_c = jnp.exp(s_c - m) * w_c
    p_n = jnp.exp(s_n - m) * w_n
    l = jnp.sum(p_c, axis=-1, keepdims=True) + jnp.sum(p_n, axis=-1, keepdims=True)
    o = (_dot_t(p_c.astype(BF16), cv_ref[0].astype(BF16)) + _dot_t(p_n.astype(BF16), vnt.astype(BF16))) * (1.0 / l)
    oa = jnp.zeros((t_s, A_WIDTH), F32)
    for h in range(A_HEADS):
        oa = oa + jnp.where(head_of_lane == h, o[h * t_s:(h + 1) * t_s], 0.0)
    smix_ref[:, S_Q:S_U] = oa


def _sample_mem_conv(srow_ref, st_ref, cmk_ref, cmv_ref, wdw_ref, smix_ref, ns_ref, uc_ref, *, t_s):
    qm = srow_ref[:, S_QM:S_ROW] * SCALE
    mhead = lax.broadcasted_iota(jnp.int32, (t_s, M_WIDTH), 1) >> HEAD_SHIFT
    q4 = jnp.concatenate([jnp.where(mhead == h, qm, 0.0) for h in range(M_HEADS)], axis=0).astype(BF16)
    sm = _dot(q4, cmk_ref[0].astype(BF16))
    mm = jnp.max(sm, axis=-1, keepdims=True)
    pm = jnp.exp(sm - mm)
    lm = jnp.sum(pm, axis=-1, keepdims=True)
    om4 = _dot_t(pm.astype(BF16), cmv_ref[0].astype(BF16)) * (1.0 / lm)
    om = jnp.zeros((t_s, M_WIDTH), F32)
    for h in range(M_HEADS):
        om = om + jnp.where(mhead == h, om4[h * t_s:(h + 1) * t_s], 0.0)
    smix_ref[:, S_QM:S_ROW] = om

    nst = st_ref.shape[1]
    uc_ref[0:nst, :] = st_ref[0]
    uc_ref[nst:nst + t_s, :] = srow_ref[:, S_U:S_QM]
    acc = jnp.zeros((t_s, C_WIDTH), F32)
    for w in range(CONV_W):
        acc = acc + uc_ref[STATE_PAD + w:STATE_PAD + w + t_s, :] * wdw_ref[w:w + 1, :]
    smix_ref[:, S_U:S_QM] = acc
    ns_ref[0] = uc_ref[STATE_PAD + t_s:STATE_PAD + t_s + CONV_W - 1, :]


BACK = 512
NFAR = BACK // QB - 1

def _natural_offset(p):
    p = p & (QB - 1)
    return DIL4 * (p & (CLS4 - 1)) + (p >> CLS4_SHIFT)


def _prompt_attn_kernel(q_ref, kvc_ref, o16_ref, ga_ref, mix_ref, x_ref, wout_ref, npost_ref,
                        srow_ref, kvt_ref, ck_ref, cv_ref, sst_ref, cmk_ref, cmv_ref, wdw_ref,
                        y_ref, smix_ref, wk_ref, ns_ref,
                        mixa_ref, nat_ref, unp_ref, uc_ref, kvp_ref, woutbf_ref, *, tq, t_s, win_len):
    i = pl.program_id(1)
    kv_refs = (kvp_ref, kvc_ref)

    @pl.when((pl.program_id(0) == 0) & (i == 0))
    def _():
        woutbf_ref[...] = wout_ref[...].astype(BF16)

    @pl.when(i == 0)
    def _():
        kvp_ref[...] = jnp.zeros(kvp_ref.shape, BF16)

    for cl in range(DIL16):
        dst = pl.ds(cl, tq // DIL16, stride=DIL16)
        for g in range(NGRP + 1):
            nat_ref[g, dst, :] = o16_ref[0, cl, :, g * LANES:(g + 1) * LANES]

    a_n = _natural_offset(lax.broadcasted_iota(jnp.int32, (QB, 2 * QB), 0))
    c_p = lax.broadcasted_iota(jnp.int32, (QB, 2 * QB), 1)
    d = jnp.where(c_p < QB, QB, 0) + a_n - _natural_offset(c_p)
    near = (d >= 0) & (d <= QB)
    far = (d >= 0) & ((d & (DIL4 - 1)) == 0)
    bias_near0 = jnp.where(near & far, LN2, jnp.where(near | far, 0.0, NEG))
    ia = lax.broadcasted_iota(jnp.int32, (CLS4, QB), 0)
    cf = lax.broadcasted_iota(jnp.int32, (CLS4, QB), 1)
    fblk, ic = cf >> CLS4_SHIFT, cf & (CLS4 - 1)
    bias_far0 = jnp.where((fblk < NFAR) & ((fblk > 0) | (ic >= ia)), 0.0, NEG)
    lane = lax.broadcasted_iota(jnp.int32, (QB, LANES), 1)
    head_a = lane < HEAD_DIM

    def rows_of(w):
        blk, off = divmod(w, tq)
        return kv_refs[blk], off

    def group(x, c):
        return jnp.concatenate([x[c * CLS4:(c + 1) * CLS4], x[QB + c * CLS4:QB + (c + 1) * CLS4]], axis=0)

    def ungroup(parts):
        return jnp.concatenate([p_[:CLS4] for p_ in parts] + [p_[CLS4:] for p_ in parts], axis=0)

    units = [(sub, g) for sub in range(tq // QB) for g in range(NGRP)]
    ctx = {}
    for sub in range(tq // QB):
        first_valid = BACK - i * tq - sub * QB
        bias_near = jnp.where((c_p >= QB) | (NFAR * QB >= first_valid), bias_near0, NEG)
        bias_far = jnp.where(fblk * QB >= first_valid, bias_far0, NEG)
        far_blocks = [rows_of((sub + j) * QB) for j in range(NFAR)]
        ctx[sub] = dict(bias_near=jnp.concatenate([bias_near, bias_near], axis=0),
                        bias_far=jnp.concatenate([bias_far, bias_far], axis=0),
                        prev=rows_of((sub + NFAR) * QB), cur=rows_of((sub + NFAR + 1) * QB),
                        far=far_blocks + far_blocks[:1])

    def far_rows(sub, c, colsl):
        return jnp.concatenate([r[0, off + c * CLS4:off + (c + 1) * CLS4, colsl] for r, off in ctx[sub]["far"]],
                               axis=0)

    def block(sub, which, colsl):
        r, off = ctx[sub][which]
        return r[0, off:off + QB, colsl]

    scores, probs, outs = {}, {}, {}

    def stage_scores(u):
        sub, g = u
        kc = slice(P_K + g * LANES, P_K + (g + 1) * LANES)
        q2 = _pair_split(q_ref[0, sub * QB:(sub + 1) * QB, g * LANES:(g + 1) * LANES].astype(F32)).astype(BF16)
        s_far = ungroup([_dot_t(group(q2, c), far_rows(sub, c, kc)) + ctx[sub]["bias_far"] for c in range(DIL4)])
        s_near = jnp.concatenate([_dot_t(q2, block(sub, "prev", kc)), _dot_t(q2, block(sub, "cur", kc))],
                                 axis=1) + ctx[sub]["bias_near"]
        scores[u] = jnp.concatenate([s_far, s_near], axis=1)

    def stage_softmax(u):
        s = scores.pop(u)
        m = jnp.max(s, axis=-1, keepdims=True)
        p = jnp.exp(s - m)
        probs[u] = (m, jnp.sum(p, axis=-1, keepdims=True), p.astype(BF16))

    def stage_values(u):
        sub, g = u
        vc = slice(P_V + g * LANES, P_V + (g + 1) * LANES)
        pb = probs[u][2]
        outs[u] = (_dot(pb[:, QB:2 * QB], block(sub, "prev", vc)) + _dot(pb[:, 2 * QB:], block(sub, "cur", vc))
                   + ungroup([_dot(group(pb[:, :QB], c), far_rows(sub, c, vc)) for c in range(DIL4)]))

    def stage_merge(u):
        sub, g = u
        rows = slice(sub * QB, (sub + 1) * QB)
        cols = slice(g * LANES, (g + 1) * LANES)
        m, l, _ = probs.pop(u)
        o_s = _pair_join(outs.pop(u))
        m_s = jnp.where(head_a, m[:QB], m[QB:])
        l_s = jnp.where(head_a, l[:QB], l[QB:])
        for c in range(DIL4):
            dst = pl.ds(c, CLS4, stride=DIL4)
            unp_ref[3 * g, dst, :] = o_s[c * CLS4:(c + 1) * CLS4]
            unp_ref[3 * g + 1, dst, :] = m_s[c * CLS4:(c + 1) * CLS4]
            unp_ref[3 * g + 2, dst, :] = l_s[c * CLS4:(c + 1) * CLS4]
        o_n, m_n, l_n = unp_ref[3 * g], unp_ref[3 * g + 1], unp_ref[3 * g + 2]
        st = nat_ref[NGRP, rows, :]
        m_f = jnp.where(head_a, st[:, 2 * g:2 * g + 1], st[:, 2 * g + 1:2 * g + 2])
        l_f = jnp.where(head_a, st[:, STAT_L + 2 * g:STAT_L + 2 * g + 1],
                        st[:, STAT_L + 2 * g + 1:STAT_L + 2 * g + 2])
        mx = jnp.maximum(m_n, m_f)
        w_n = jnp.exp(m_n - mx)
        w_f = jnp.exp(m_f - mx)
        oa = (o_n * w_n + nat_ref[g, rows, :] * w_f) / (l_n * w_n + l_f * w_f)
        mixa_ref[rows, cols] = (oa * ga_ref[0, rows, cols]).astype(BF16)

    def finish(rows):
        z = (_dot(mixa_ref[rows, :], woutbf_ref[0:A_WIDTH, :])
             + _dot(mix_ref[0, rows, :], woutbf_ref[A_WIDTH:, :]))
        y_ref[0, rows, :] = x_ref[0, rows, :] + _rmsnorm(z, npost_ref[...])

    stages = (stage_scores, stage_softmax, stage_values, stage_merge)
    half = len(units) // 2
    for t in range(len(units) + len(stages) - 1):
        for depth, stage in enumerate(stages):
            if 0 <= t - depth < len(units):
                stage(units[t - depth])
        if t - (len(stages) - 1) == half - 1:
            finish(slice(0, tq // 2))
    finish(slice(tq // 2, tq))
    kvp_ref[...] = kvc_ref[...]

    for part in _sample_mix_stages(pl.program_id(0) * pl.num_programs(1) + i, srow_ref, kvt_ref, ck_ref, cv_ref,
                                   sst_ref, cmk_ref, cmv_ref, wdw_ref, smix_ref, wk_ref, ns_ref, uc_ref,
                                   t_s=t_s, win_len=win_len):
        part()


def _prompt_attn(kvq, o16, ga, mix, x, wout, npost, srow, kvt, ck, cv, st_pad, cmk, cmv, wdw, tq, t_s):
    b, s, _ = x.shape
    nsteps = s // tq
    nb, _, win_len = ck.shape
    assert tq == BACK and b * nsteps == nb, "one sample batch per grid step"
    row = lambda w: pl.BlockSpec((1, tq, w), lambda bi, i: (bi, i, 0))
    qspec = pl.BlockSpec((1, tq, A_WIDTH), lambda bi, i: (bi, i, P_Q // A_WIDTH))
    kvcur = pl.BlockSpec((1, tq, 2 * A_WIDTH), lambda bi, i: (bi, i, 0))
    cls = pl.BlockSpec((1, DIL16, tq // DIL16, O16_W), lambda bi, i: (bi, 0, i, 0))
    srows = pl.BlockSpec((t_s, S_ROW), lambda bi, i: (bi * nsteps + i, 0))
    per = lambda a: pl.BlockSpec((1,) + a.shape[1:], lambda bi, i: (bi * nsteps + i, 0, 0))
    return pl.pallas_call(
        functools.partial(_prompt_attn_kernel, tq=tq, t_s=t_s, win_len=win_len),
        grid=(b, nsteps),
        in_specs=[qspec, kvcur, cls, row(A_WIDTH), row(C_WIDTH + M_WIDTH), row(D_MODEL),
                  _resident(wout), _full(npost),
                  srows, _full(kvt), per(ck), per(cv), per(st_pad), per(cmk), per(cmv), _full(wdw)],
        out_specs=[row(D_MODEL), srows, per(ck),
                   pl.BlockSpec((1, CONV_W - 1, C_WIDTH), lambda bi, i: (bi * nsteps + i, 0, 0))],
        out_shape=[jax.ShapeDtypeStruct((b, s, D_MODEL), F32),
                   jax.ShapeDtypeStruct((nb * t_s, S_ROW), F32),
                   jax.ShapeDtypeStruct(ck.shape, F32),
                   jax.ShapeDtypeStruct((nb, CONV_W - 1, C_WIDTH), F32)],
        scratch_shapes=[pltpu.VMEM((tq, A_WIDTH), BF16),
                        pltpu.VMEM((NGRP + 1, tq, LANES), F32),
                        pltpu.VMEM((3 * NGRP, QB, LANES), F32),
                        pltpu.VMEM((STATE_PAD + CONV_W - 1 + t_s, C_WIDTH), F32),
                        pltpu.VMEM((1, tq, 2 * A_WIDTH), BF16),
                        pltpu.VMEM(wout.shape, BF16)],
        compiler_params=pltpu.CompilerParams(dimension_semantics=("arbitrary", "arbitrary"),
                                             vmem_limit_bytes=VMEM_LIMIT),
        name="prompt_attn",
    )(kvq, kvq, o16, ga, mix, x, wout, npost, srow, kvt, ck, cv, st_pad, cmk, cmv, wdw)


def _sample_proj_kernel(x_ref, npre_ref, win_ref, ang_ref, srow_ref, kvt_ref, gate_ref, wbf_ref):
    h = _rmsnorm(x_ref[...], npre_ref[...]).astype(BF16)
    wbf_ref[...] = win_ref[...].astype(BF16)
    proj = _dot(h, wbf_ref[...])
    pats = _rope_patterns(ang_ref[:, 0:LANES], ang_ref[:, LANES:])
    srow_ref[:, S_Q:S_U] = _rope(proj[:, O_QA:O_KA], pats) * SCALE
    srow_ref[:, S_U:S_QM] = proj[:, O_AB:O_BB] * _sigmoid(proj[:, O_BB:O_GB])
    srow_ref[:, S_QM:S_ROW] = proj[:, O_QM:O_GM]
    kv_t = jnp.concatenate([_rope(proj[:, O_KA:O_VA], pats), proj[:, O_VA:O_GA]], axis=1).T
    for slab in range(kvt_ref.shape[0]):
        kvt_ref[slab] = kv_t[:, slab * LANES:(slab + 1) * LANES]
    gate_ref[:, S_Q:S_U] = _silu(proj[:, O_GA:O_AB])
    gate_ref[:, S_U:S_QM] = _silu(proj[:, O_GB:O_QM])
    gate_ref[:, S_QM:S_ROW] = _silu(proj[:, O_GM:D_IN])


def _sample_proj(x, npre, win, ang):
    n = x.shape[0]
    assert n % LANES == 0
    shapes = [(n, S_ROW), (n // LANES, 2 * A_WIDTH, LANES), (n, S_ROW), win.shape]
    dtypes = [F32, F32, F32, BF16]
    return pl.pallas_call(
        _sample_proj_kernel,
        grid=(1,),
        in_specs=[_full(x), _full(npre), _full(win), _full(ang)],
        out_specs=[pl.BlockSpec(sh, lambda i, nd=len(sh): (0,) * nd) for sh in shapes],
        out_shape=[jax.ShapeDtypeStruct(sh, dt) for sh, dt in zip(shapes, dtypes)],
        compiler_params=pltpu.CompilerParams(vmem_limit_bytes=VMEM_LIMIT),
        name="sample_proj",
    )(x, npre, win, ang)


def _sample_out_kernel(x_ref, smix_ref, gate_ref, bdw_ref, lng_ref, lnb_ref, wpw_ref, bpw_ref, wout_ref, npost_ref,
                       y_ref):
    mixed_a = (smix_ref[:, S_Q:S_U] * gate_ref[:, S_Q:S_U]).astype(BF16)
    mixed_b = _conformer_tail(smix_ref[:, S_U:S_QM], gate_ref[:, S_U:S_QM], bdw_ref[...], lng_ref[...], lnb_ref[...],
                              wpw_ref, bpw_ref[...]).astype(BF16)
    mixed_m = (smix_ref[:, S_QM:S_ROW] * gate_ref[:, S_QM:S_ROW]).astype(BF16)
    wout = wout_ref[...].astype(BF16)
    z = (_dot(mixed_a, wout[0:A_WIDTH]) + _dot(mixed_b, wout[A_WIDTH:A_WIDTH + C_WIDTH])
         + _dot(mixed_m, wout[A_WIDTH + C_WIDTH:]))
    y_ref[...] = x_ref[...] + _rmsnorm(z, npost_ref[...])


def _sample_out(x, smix, gate, bdw, lng, lnb, wpw, bpw, wout, npost):
    args = (x, smix, gate, bdw, lng, lnb, wpw, bpw, wout, npost)
    return pl.pallas_call(
        _sample_out_kernel,
        grid=(1,),
        in_specs=[_full(a) for a in args],
        out_specs=_full(x),
        out_shape=jax.ShapeDtypeStruct(x.shape, F32),
        compiler_params=pltpu.CompilerParams(vmem_limit_bytes=VMEM_LIMIT),
        name="sample_out",
    )(*args)


def _rope_cos_sin(pos):
    inv = ROPE_THETA ** (-jnp.arange(0, ROT_DIM, 2, dtype=F32) / ROT_DIM)
    ang = pos.astype(F32)[:, None] * jnp.tile(inv, 2 * LANES // ROT_DIM)[None, :]
    return jnp.concatenate([jnp.cos(ang), jnp.sin(ang)], axis=1)


def _feature_major(cache):
    nb, rows, heads, dim = cache.shape
    return jnp.transpose(cache, (0, 2, 3, 1)).reshape(nb, heads * dim, rows)


def _row_major(cache_t, heads):
    nb, width, rows = cache_t.shape
    return jnp.transpose(cache_t.reshape(nb, heads, width // heads, rows), (0, 3, 1, 2))


PROJ_TILE = 512
ATTN_TILE = BACK


def kernel(x_prompt, x_sample, cache_win_k, cache_win_v, state_conv, cache_mem_k, cache_mem_v, mem_prompt,
           norm_pre, norm_post, w_in, w_out, norm_mem, w_mem_kv, w_dw, b_dw, ln_conv_g, ln_conv_b, w_pw2, b_pw2):
    depth = w_in.shape[0]
    assert depth == 1, "single-layer step"
    b, s, _ = x_prompt.shape
    nb, t_s, _ = x_sample.shape
    win_len = cache_win_k.shape[2]
    assert s % (DIL16 * QB) == 0 and s % PROJ_TILE == 0 and s % ATTN_TILE == 0
    assert win_len == MAX_WINDOW and win_len % LANES == 0
    assert t_s % 8 == 0 and t_s & (t_s - 1) == 0 and t_s < LANES
    l = 0
    row = lambda a: a[l][None, :]
    npre, npost, nmem = row(norm_pre), row(norm_post), row(norm_mem)
    bdw, lng, lnb, bpw = row(b_dw), row(ln_conv_g), row(ln_conv_b), row(b_pw2)
    win, wout, wmem, wpw = w_in[l], w_out[l], w_mem_kv[l], w_pw2[l]
    wdw = w_dw[l]

    pos_s = PAST_LEN + jnp.arange(t_s, dtype=jnp.int32)
    xs = x_sample.reshape(nb * t_s, D_MODEL)
    srow, kvt, gate_s, win_bf = _sample_proj(xs, npre, win, jnp.tile(_rope_cos_sin(pos_s), (nb, 1)))
    st_pad = jnp.pad(state_conv[l], ((0, 0), (STATE_PAD, 0), (0, 0)))

    mk, mv, mkb, mvb = _mem_kv(mem_prompt, nmem, wmem)
    rin = _rope_cos_sin(jnp.arange(PROJ_TILE, dtype=jnp.int32))
    rbase = _rope_cos_sin(jnp.arange(0, s, PROJ_TILE, dtype=jnp.int32))[:, None, :]
    ck, cv = _feature_major(cache_win_k[l]), _feature_major(cache_win_v[l])
    kvq, kvq16, kf, vf, ga, mix, ust, wv = _prompt_proj(x_prompt, npre, win_bf, rin, rbase, mkb, mvb, wdw, bdw,
                                                        lng, lnb, wpw, bpw, kvt, cv, PROJ_TILE, t_s)
    o16 = _dilated16(kvq16)
    y_prompt, smix, wk, nst = _prompt_attn(
        kvq, o16, ga, mix, x_prompt, wout, npost, srow, kvt, ck, cv, st_pad,
        _feature_major(cache_mem_k[l]), _feature_major(cache_mem_v[l]), wdw, ATTN_TILE, t_s)
    y_sample = _sample_out(xs, smix, gate_s, bdw, lng, lnb, wpw, bpw, wout, npost)

    keep_p = kf.shape[1]
    return (y_prompt,
            y_sample.reshape(nb, t_s, D_MODEL),
            kf.reshape(1, b, keep_p, A_HEADS, HEAD_DIM),
            vf.reshape(1, b, keep_p, A_HEADS, HEAD_DIM),
            ust[:, CONV_HIST - (CONV_W - 1):, :][None],
            mk.reshape(1, b, N_MEM, M_HEADS, HEAD_DIM),
            mv.reshape(1, b, N_MEM, M_HEADS, HEAD_DIM),
            _row_major(wk, A_HEADS)[None],
            _row_major(wv, A_HEADS)[None],
            nst[None])
```

```python
import functools
import math

import jax
import jax.numpy as jnp
from jax import lax
from jax.experimental import pallas as pl
from jax.experimental.pallas import tpu as pltpu

F32 = jnp.float32
BF16 = jnp.bfloat16

D_MODEL = 1024
HEAD_DIM = 64
HEAD_SHIFT = 6
A_WIDTH = 384
A_HEADS = 6
M_WIDTH = 256
M_HEADS = 4
C_WIDTH = 384
ROT_DIM = 16
ROPE_THETA = 500000.0
CONV_W = 31
N_MEM = 256
MAX_WINDOW = 2048
PAST_LEN = 16384
EPS = 1e-6
SCALE = HEAD_DIM ** -0.5
NEG = -1e30
LN2 = math.log(2.0)

LANES = 128
NGRP = A_WIDTH // LANES
QB = 128
DIL16 = 16
DIL4 = 4
CLS4 = QB // DIL4
CLS4_SHIFT = 5
O_QA, O_KA, O_VA, O_GA, O_AB, O_BB, O_GB, O_QM, O_GM, D_IN = 0, 384, 768, 1152, 1536, 1920, 2304, 2688, 2944, 3200
P_K, P_V, P_Q, P_W = 0, A_WIDTH, 2 * A_WIDTH, 3 * A_WIDTH

VMEM_LIMIT = 60 * 1024 * 1024


def _sigmoid(x):
    return 1.0 / (1.0 + jnp.exp(-x))


def _silu(x):
    return x * _sigmoid(x)


def _rmsnorm(x, g):
    return x * lax.rsqrt(jnp.mean(x * x, axis=-1, keepdims=True) + EPS) * g


def _dot(a, b):
    return jnp.dot(a, b, preferred_element_type=F32)


def _dot_t(a, b):
    return lax.dot_general(a, b, (((1,), (1,)), ((), ())), preferred_element_type=F32)


def _rope_patterns(cos, sin):
    in_head = lax.broadcasted_iota(jnp.int32, cos.shape, 1) & (HEAD_DIM - 1)
    rot, lo = in_head < ROT_DIM, in_head < ROT_DIM // 2
    return jnp.where(rot, cos, 1.0), jnp.where(lo, -sin, 0.0), jnp.where(rot & ~lo, sin, 0.0)


def _rope(xw, pats):
    cos, s1, s2 = pats
    outs = []
    for g in range(xw.shape[1] // LANES):
        xg = xw[:, g * LANES:(g + 1) * LANES]
        outs.append(xg * cos + pltpu.roll(xg, LANES - 8, 1) * s1 + pltpu.roll(xg, 8, 1) * s2)
    return jnp.concatenate(outs, axis=1)


def _pair_split(q):
    lane = lax.broadcasted_iota(jnp.int32, q.shape, 1)
    qa = jnp.where(lane < HEAD_DIM, q, 0.0)
    qb = jnp.where(lane >= HEAD_DIM, q, 0.0)
    return jnp.concatenate([qa, qb], axis=0)


def _pair_join(x):
    t = x.shape[0] // 2
    lane = lax.broadcasted_iota(jnp.int32, (t, LANES), 1)
    return jnp.where(lane < HEAD_DIM, x[:t], x[t:])


def _mem_attend(qm, mk_ref, mv_ref):
    outs = []
    for g in range(M_WIDTH // LANES):
        cols = slice(g * LANES, (g + 1) * LANES)
        q2 = _pair_split(qm[:, cols] * SCALE).astype(BF16)
        s = _dot_t(q2, mk_ref[0, :, cols])
        m = jnp.max(s, axis=-1, keepdims=True)
        p = jnp.exp(s - m)
        l = jnp.sum(p, axis=-1, keepdims=True)
        o = _dot(p.astype(BF16), mv_ref[0, :, cols]) * (1.0 / l)
        outs.append(_pair_join(o))
    return jnp.concatenate(outs, axis=1)


def _conformer_tail(c, gate_b, bdw, lng, lnb, wpw_ref, bpw):
    cf = c + bdw
    mu = jnp.mean(cf, axis=-1, keepdims=True)
    dev = cf - mu
    var = jnp.mean(dev * dev, axis=-1, keepdims=True)
    cn = dev * lax.rsqrt(var + EPS) * lng + lnb
    ob = _dot(_silu(cn).astype(BF16), wpw_ref[...].astype(BF16)) + bpw
    return ob * gate_b


def _full(a):
    return pl.BlockSpec(a.shape, lambda *_: (0,) * a.ndim)


def _resident(a):
    return pl.BlockSpec(a.shape, lambda *_: (0,) * a.ndim, pipeline_mode=pl.Buffered(1))


def _mem_kv_kernel(mem_ref, g_ref, w_ref, mk_ref, mv_ref, mkb_ref, mvb_ref):
    h = _rmsnorm(mem_ref[0], g_ref[...]).astype(BF16)
    kv = _dot(h, w_ref[...].astype(BF16))
    mk_ref[0] = kv[:, :M_WIDTH]
    mv_ref[0] = kv[:, M_WIDTH:]
    mkb_ref[0] = kv[:, :M_WIDTH].astype(BF16)
    mvb_ref[0] = kv[:, M_WIDTH:].astype(BF16)


def _mem_kv(mem, g, wmem):
    b = mem.shape[0]
    blk = pl.BlockSpec((1, N_MEM, M_WIDTH), lambda i: (i, 0, 0))
    return pl.pallas_call(
        _mem_kv_kernel,
        grid=(b,),
        in_specs=[pl.BlockSpec((1, N_MEM, D_MODEL), lambda i: (i, 0, 0)), _full(g), _full(wmem)],
        out_specs=[blk, blk, blk, blk],
        out_shape=[jax.ShapeDtypeStruct((b, N_MEM, M_WIDTH), F32)] * 2
        + [jax.ShapeDtypeStruct((b, N_MEM, M_WIDTH), BF16)] * 2,
        name="mem_kv",
    )(mem, g, wmem)


CONV_HIST = 32
CONV_CHUNK = 64


def _prompt_proj_kernel(x_ref, npre_ref, win_ref, rin_ref, rbase_ref, mk_ref, mv_ref,
                        wdw_ref, bdw_ref, lng_ref, lnb_ref, wpw_ref, bpw_ref, kvt_ref, cv_ref,
                        kvq_ref, kvq16_ref, kf_ref, vf_ref, ga_ref, mix_ref, ust_ref, wv_ref,
                        uext_ref, ush_ref, conv_ref, stage_ref, stage4_ref, *, t_s, win_len):
    t = x_ref.shape[1]
    i = pl.program_id(1)

    @pl.when(i == 0)
    def _():
        uext_ref[0:CONV_HIST, :] = jnp.zeros((CONV_HIST, C_WIDTH), F32)

    h = _rmsnorm(x_ref[0], npre_ref[...]).astype(BF16)
    proj = {}

    def project(name, c0, c1):
        def run():
            proj[name] = _dot(h, win_ref[:, c0:c1])
        return run

    def stage(idx, val):
        for g in range(NGRP):
            stage_ref[NGRP * idx + g] = val[:, g * LANES:(g + 1) * LANES]

    pats = []

    def rope_pats():
        if not pats:
            cr, sr = rin_ref[:, 0:LANES], rin_ref[:, LANES:]
            cb, sb = rbase_ref[i, :, 0:LANES], rbase_ref[i, :, LANES:]
            pats.append(_rope_patterns(cb * cr - sb * sr, sb * cr + cb * sr))
        return pats[0]

    def do_k():
        k = _rope(proj.pop("k"), rope_pats())
        kf_ref[0] = k
        stage(0, k)

    def do_q():
        stage(2, _rope(proj.pop("q"), rope_pats()) * SCALE)

    def do_v():
        v = proj.pop("v")
        vf_ref[0] = v
        stage(1, v)

    def do_ga():
        ga_ref[0] = _silu(proj.pop("ga"))

    def do_copies():
        quarter = t // DIL4
        for c4 in range(DIL4):
            for s in range(3 * NGRP):
                stage4_ref[s, c4 * quarter:(c4 + 1) * quarter, :] = stage_ref[s, pl.ds(c4, quarter, stride=DIL4), :]
        for c in range(DIL16):
            c4, c2 = c % DIL4, c // DIL4
            rows_c = [stage4_ref[s, pl.ds(c4 * quarter + c2, t // DIL16, stride=DIL4), :] for s in range(3 * NGRP)]
            kvq16_ref[0, c] = jnp.concatenate(rows_c, axis=1).astype(BF16)
        for blk in range(t // QB):
            for c4 in range(DIL4):
                src = slice(c4 * quarter + blk * CLS4, c4 * quarter + (blk + 1) * CLS4)
                rows_c = [stage4_ref[s, src, :] for s in range(3 * NGRP)]
                kvq_ref[0, blk * QB + c4 * CLS4:blk * QB + (c4 + 1) * CLS4, :] = (
                    jnp.concatenate(rows_c, axis=1).astype(BF16))

    def do_glu():
        uext_ref[CONV_HIST:CONV_HIST + t, :] = proj.pop("ab") * _sigmoid(proj.pop("bb"))
        uext = uext_ref[...]
        for r in range(1, 8):
            ush_ref[r - 1] = pltpu.roll(uext, r, 0)

    def conv_rows(r0):
        def run():
            acc = jnp.zeros((CONV_CHUNK, C_WIDTH), F32)
            for kk in range(CONV_W):
                tap = wdw_ref[CONV_W - 1 - kk:CONV_W - kk, :]
                r, base = kk % 8, CONV_HIST + r0 - (kk - kk % 8)
                if r == 0:
                    slab = uext_ref[base:base + CONV_CHUNK, :]
                else:
                    slab = ush_ref[r - 1, base:base + CONV_CHUNK, :]
                acc = acc + slab * tap
            conv_ref[r0:r0 + CONV_CHUNK, :] = acc
        return run

    def do_history():
        tail = uext_ref[t:t + CONV_HIST, :]
        uext_ref[0:CONV_HIST, :] = tail
        ust_ref[0] = tail

    def do_tail():
        mixed_b = _conformer_tail(conv_ref[...], _silu(proj.pop("gb")), bdw_ref[...], lng_ref[...], lnb_ref[...],
                                  wpw_ref, bpw_ref[...])
        mix_ref[0, :, 0:C_WIDTH] = mixed_b.astype(BF16)

    def do_mem():
        mixed_m = _mem_attend(proj.pop("qm"), mk_ref, mv_ref) * _silu(proj.pop("gm"))
        mix_ref[0, :, C_WIDTH:] = mixed_m.astype(BF16)

    convs = [conv_rows(r0) for r0 in range(0, t, CONV_CHUNK)]
    riders = [([project("k", O_KA, O_VA)], do_k), ([project("q", O_QA, O_KA)], do_q),
              ([project("v", O_VA, O_GA)], do_v), ([project("ga", O_GA, O_AB)], do_ga),
              ([project("qm", O_QM, O_GM), project("gm", O_GM, D_IN)], do_mem),
              ([project("gb", O_GB, O_QM)], do_copies)]
    slide_v = functools.partial(_slide_cache, pl.program_id(0) * pl.num_programs(1) + i, kvt_ref, cv_ref, A_WIDTH,
                                wv_ref, t_s=t_s, win_len=win_len)
    program = [project("ab", O_AB, O_BB), project("bb", O_BB, O_GB), slide_v, do_glu]
    for n, conv in enumerate(convs):
        if n < len(riders):
            program += riders[n][0]
        program.append(conv)
        if n < len(riders) and riders[n][1] is not None:
            program.append(riders[n][1])
    assert len(convs) >= len(riders)
    program += [do_history, do_tail]
    for piece in program:
        piece()
    assert not proj


def _prompt_proj(x, npre, win, rin, rbase, mkb, mvb, wdw, bdw, lng, lnb, wpw, bpw, kvt, cv, tile, t_s):
    b, s, _ = x.shape
    nt = s // tile
    nb, _, win_len = cv.shape
    assert b * nt == nb, "one sample batch per grid step"
    per_batch = pl.BlockSpec((1,) + cv.shape[1:], lambda bi, i: (bi * nt + i, 0, 0))
    keep = min(MAX_WINDOW, s)
    first_keep = (s - keep) // tile
    row = lambda w: pl.BlockSpec((1, tile, w), lambda bi, i: (bi, i, 0))
    memb = pl.BlockSpec((1, N_MEM, M_WIDTH), lambda bi, i: (bi, 0, 0))
    keepb = pl.BlockSpec((1, tile, A_WIDTH), lambda bi, i: (bi, jnp.maximum(i - first_keep, 0), 0))
    cls = pl.BlockSpec((1, DIL16, tile // DIL16, P_W), lambda bi, i: (bi, 0, i, 0))
    return pl.pallas_call(
        functools.partial(_prompt_proj_kernel, t_s=t_s, win_len=win_len),
        grid=(b, nt),
        in_specs=[row(D_MODEL), _full(npre), _resident(win), _full(rin), _full(rbase),
                  memb, memb, _full(wdw), _full(bdw), _full(lng), _full(lnb), _full(wpw), _full(bpw),
                  _full(kvt), per_batch],
        out_specs=[row(P_W), cls, keepb, keepb, row(A_WIDTH), row(C_WIDTH + M_WIDTH),
                   pl.BlockSpec((1, CONV_HIST, C_WIDTH), lambda bi, i: (bi, 0, 0)), per_batch],
        out_shape=[jax.ShapeDtypeStruct((b, s, P_W), BF16),
                   jax.ShapeDtypeStruct((b, DIL16, s // DIL16, P_W), BF16),
                   jax.ShapeDtypeStruct((b, keep, A_WIDTH), F32),
                   jax.ShapeDtypeStruct((b, keep, A_WIDTH), F32),
                   jax.ShapeDtypeStruct((b, s, A_WIDTH), F32),
                   jax.ShapeDtypeStruct((b, s, C_WIDTH + M_WIDTH), BF16),
                   jax.ShapeDtypeStruct((b, CONV_HIST, C_WIDTH), F32),
                   jax.ShapeDtypeStruct(cv.shape, F32)],
        scratch_shapes=[pltpu.VMEM((CONV_HIST + tile, C_WIDTH), F32),
                        pltpu.VMEM((7, CONV_HIST + tile, C_WIDTH), F32),
                        pltpu.VMEM((tile, C_WIDTH), F32),
                        pltpu.VMEM((3 * NGRP, tile, LANES), F32),
                        pltpu.VMEM((3 * NGRP, tile, LANES), F32)],
        compiler_params=pltpu.CompilerParams(dimension_semantics=("arbitrary", "arbitrary"),
                                             vmem_limit_bytes=VMEM_LIMIT),
        name="prompt_proj",
    )(x, npre, win, rin, rbase, mkb, mvb, wdw, bdw, lng, lnb, wpw, bpw, kvt, cv)


STAT_L = 8
O16_W = A_WIDTH + LANES


CLS_PER_STEP = 4


def _dilated16_kernel(kvq_ref, o_ref):
    a = lax.broadcasted_iota(jnp.int32, (QB, QB), 0)
    c = lax.broadcasted_iota(jnp.int32, (QB, QB), 1)
    tri_prev = jnp.where(c >= a, 0.0, NEG)
    tri_cur = jnp.where(c <= a, 0.0, NEG)
    bias_cur = jnp.concatenate([tri_cur, tri_cur], axis=0)
    both = jnp.concatenate([tri_prev, tri_cur], axis=1)
    bias_both = jnp.concatenate([both, both], axis=0)
    lane = lax.broadcasted_iota(jnp.int32, (QB, LANES), 1)
    ncls, nsub = kvq_ref.shape[1], kvq_ref.shape[2] // QB
    units = [(cl, sub, g) for cl in range(ncls) for sub in range(nsub) for g in range(NGRP)]
    scores, probs = {}, {}
    stats = {(cl, sub): jnp.zeros((QB, LANES), F32) for cl in range(ncls) for sub in range(nsub)}

    def rows_of(sub):
        return slice(sub * QB, (sub + 1) * QB)

    def stage_scores(u):
        cl, sub, g = u
        kc = slice(P_K + g * LANES, P_K + (g + 1) * LANES)
        qc = slice(P_Q + g * LANES, P_Q + (g + 1) * LANES)
        q2 = _pair_split(kvq_ref[0, cl, rows_of(sub), qc].astype(F32)).astype(BF16)
        s_cur = _dot_t(q2, kvq_ref[0, cl, rows_of(sub), kc])
        if sub == 0:
            scores[u] = s_cur + bias_cur
        else:
            scores[u] = jnp.concatenate([_dot_t(q2, kvq_ref[0, cl, rows_of(sub - 1), kc]), s_cur], axis=1) + bias_both

    def stage_softmax(u):
        s = scores.pop(u)
        m = jnp.max(s, axis=-1, keepdims=True)
        p = jnp.exp(s - m)
        probs[u] = (m, jnp.sum(p, axis=-1, keepdims=True), p.astype(BF16))

    def stage_values(u):
        cl, sub, g = u
        vc = slice(P_V + g * LANES, P_V + (g + 1) * LANES)
        m, l, pb = probs.pop(u)
        if sub == 0:
            o = _dot(pb, kvq_ref[0, cl, rows_of(sub), vc])
        else:
            o = (_dot(pb[:, :QB], kvq_ref[0, cl, rows_of(sub - 1), vc])
                 + _dot(pb[:, QB:], kvq_ref[0, cl, rows_of(sub), vc]))
        o_ref[0, cl, rows_of(sub), g * LANES:(g + 1) * LANES] = _pair_join(o)
        st = stats[cl, sub]
        st = jnp.where(lane == 2 * g, m[:QB], st)
        st = jnp.where(lane == 2 * g + 1, m[QB:], st)
        st = jnp.where(lane == STAT_L + 2 * g, l[:QB], st)
        stats[cl, sub] = jnp.where(lane == STAT_L + 2 * g + 1, l[QB:], st)
        if g == NGRP - 1:
            o_ref[0, cl, rows_of(sub), A_WIDTH:] = stats.pop((cl, sub))

    stages = (stage_scores, stage_softmax, stage_values)
    for t in range(len(units) + len(stages) - 1):
        for depth, stage in enumerate(stages):
            if 0 <= t - depth < len(units):
                stage(units[t - depth])


def _dilated16(kvq16):
    b, ncls, nj, w = kvq16.shape
    return pl.pallas_call(
        _dilated16_kernel,
        grid=(b, ncls // CLS_PER_STEP),
        in_specs=[pl.BlockSpec((1, CLS_PER_STEP, nj, w), lambda bi, c: (bi, c, 0, 0))],
        out_specs=pl.BlockSpec((1, CLS_PER_STEP, nj, O16_W), lambda bi, c: (bi, c, 0, 0)),
        out_shape=jax.ShapeDtypeStruct((b, ncls, nj, O16_W), F32),
        compiler_params=pltpu.CompilerParams(dimension_semantics=("arbitrary",) * 2,
                                             vmem_limit_bytes=VMEM_LIMIT),
        name="dilated16",
    )(kvq16)


STATE_PAD = 2
S_Q, S_U, S_QM, S_ROW = 0, A_WIDTH, A_WIDTH + C_WIDTH, A_WIDTH + C_WIDTH + M_WIDTH


def _new_rows(batch, kvt_ref, lo, hi, t_s):
    per_slab = LANES // t_s
    slab, pos = batch // per_slab, batch % per_slab
    return pltpu.roll(kvt_ref[slab, lo:hi, :], LANES - t_s - pos * t_s, 1)


def _slide_cache(batch, kvt_ref, src_ref, lo, dst_ref, *, t_s, win_len):
    new0 = LANES - t_s
    keep = lax.broadcasted_iota(jnp.int32, (A_WIDTH, LANES), 1) < new0
    nch = win_len // LANES
    nxt = pltpu.roll(src_ref[0, :, 0:LANES], new0, 1)
    for c in range(nch):
        cur = nxt
        nxt = (pltpu.roll(src_ref[0, :, (c + 1) * LANES:(c + 2) * LANES], new0, 1) if c + 1 < nch
               else _new_rows(batch, kvt_ref, lo, lo + A_WIDTH, t_s))
        dst_ref[0, :, c * LANES:(c + 1) * LANES] = jnp.where(keep, cur, nxt)


def _sample_mix_stages(batch, srow_ref, kvt_ref, ck_ref, cv_ref, st_ref, cmk_ref, cmv_ref, wdw_ref,
                       smix_ref, wk_ref, ns_ref, uc_ref, *, t_s, win_len):
    new_rows = functools.partial(_new_rows, batch, kvt_ref, t_s=t_s)
    return [functools.partial(_slide_cache, batch, kvt_ref, ck_ref, 0, wk_ref, t_s=t_s, win_len=win_len),
            functools.partial(_sample_attend, new_rows, srow_ref, ck_ref, cv_ref, smix_ref, t_s=t_s, win_len=win_len),
            functools.partial(_sample_mem_conv, srow_ref, st_ref, cmk_ref, cmv_ref, wdw_ref, smix_ref, ns_ref,
                              uc_ref, t_s=t_s)]


def _sample_attend(new_rows, srow_ref, ck_ref, cv_ref, smix_ref, *, t_s, win_len):
    new0 = LANES - t_s
    knt, vnt = new_rows(0, A_WIDTH), new_rows(A_WIDTH, 2 * A_WIDTH)
    q = srow_ref[:, S_Q:S_U]
    head_of_lane = lax.broadcasted_iota(jnp.int32, (t_s, A_WIDTH), 1) >> HEAD_SHIFT
    q6 = jnp.concatenate([jnp.where(head_of_lane == h, q, 0.0) for h in range(A_HEADS)], axis=0).astype(BF16)
    s_c = _dot(q6, ck_ref[0].astype(BF16))
    s_n = _dot(q6, knt.astype(BF16))

    def weights(shape, key0, lo):
        tq = lax.broadcasted_iota(jnp.int32, shape, 0) & (t_s - 1)
        key = lax.broadcasted_iota(jnp.int32, shape, 1) + key0
        d = win_len + tq - key
        ok = (d >= 0) & (key >= lo)
        w = ((d <= 128).astype(F32) + ((d <= 512) & ((d & 3) == 0)).astype(F32)
             + ((d <= 2048) & ((d & 15) == 0)).astype(F32))
        return jnp.where(ok, w, 0.0)

    w_c = weights(s_c.shape, 0, 0)
    w_n = weights(s_n.shape, win_len - new0, win_len)
    s_c = jnp.where(w_c > 0.0, s_c, NEG)
    s_n = jnp.where(w_n > 0.0, s_n, NEG)
    m = jnp.maximum(jnp.max(s_c, axis=-1, keepdims=True), jnp.max(s_n, axis=-1, keepdims=True))
    p_c = jnp.exp(s_c - m) * w_c
    p_n = jnp.exp(s_n - m) * w_n
    l = jnp.sum(p_c, axis=-1, keepdims=True) + jnp.sum(p_n, axis=-1, keepdims=True)
    o = (_dot_t(p_c.astype(BF16), cv_ref[0].astype(BF16)) + _dot_t(p_n.astype(BF16), vnt.astype(BF16))) * (1.0 / l)
    oa = jnp.zeros((t_s, A_WIDTH), F32)
    for h in range(A_HEADS):
        oa = oa + jnp.where(head_of_lane == h, o[h * t_s:(h + 1) * t_s], 0.0)
    smix_ref[:, S_Q:S_U] = oa


def _sample_mem_conv(srow_ref, st_ref, cmk_ref, cmv_ref, wdw_ref, smix_ref, ns_ref, uc_ref, *, t_s):
    qm = srow_ref[:, S_QM:S_ROW] * SCALE
    mhead = lax.broadcasted_iota(jnp.int32, (t_s, M_WIDTH), 1) >> HEAD_SHIFT
    q4 = jnp.concatenate([jnp.where(mhead == h, qm, 0.0) for h in range(M_HEADS)], axis=0).astype(BF16)
    sm = _dot(q4, cmk_ref[0].astype(BF16))
    mm = jnp.max(sm, axis=-1, keepdims=True)
    pm = jnp.exp(sm - mm)
    lm = jnp.sum(pm, axis=-1, keepdims=True)
    om4 = _dot_t(pm.astype(BF16), cmv_ref[0].astype(BF16)) * (1.0 / lm)
    om = jnp.zeros((t_s, M_WIDTH), F32)
    for h in range(M_HEADS):
        om = om + jnp.where(mhead == h, om4[h * t_s:(h + 1) * t_s], 0.0)
    smix_ref[:, S_QM:S_ROW] = om

    nst = st_ref.shape[1]
    uc_ref[0:nst, :] = st_ref[0]
    uc_ref[nst:nst + t_s, :] = srow_ref[:, S_U:S_QM]
    acc = jnp.zeros((t_s, C_WIDTH), F32)
    for w in range(CONV_W):
        acc = acc + uc_ref[STATE_PAD + w:STATE_PAD + w + t_s, :] * wdw_ref[w:w + 1, :]
    smix_ref[:, S_U:S_QM] = acc
    ns_ref[0] = uc_ref[STATE_PAD + t_s:STATE_PAD + t_s + CONV_W - 1, :]


BACK = 512
NFAR = BACK // QB - 1

def _natural_offset(p):
    p = p & (QB - 1)
    return DIL4 * (p & (CLS4 - 1)) + (p >> CLS4_SHIFT)


def _prompt_attn_kernel(q_ref, kvc_ref, o16_ref, ga_ref, mix_ref, x_ref, wout_ref, npost_ref,
                        srow_ref, kvt_ref, ck_ref, cv_ref, sst_ref, cmk_ref, cmv_ref, wdw_ref,
                        y_ref, smix_ref, wk_ref, ns_ref,
                        mixa_ref, nat_ref, unp_ref, uc_ref, kvp_ref, woutbf_ref, *, tq, t_s, win_len):
    i = pl.program_id(1)
    kv_refs = (kvp_ref, kvc_ref)

    @pl.when((pl.program_id(0) == 0) & (i == 0))
    def _():
        woutbf_ref[...] = wout_ref[...].astype(BF16)

    @pl.when(i == 0)
    def _():
        kvp_ref[...] = jnp.zeros(kvp_ref.shape, BF16)

    for cl in range(DIL16):
        dst = pl.ds(cl, tq // DIL16, stride=DIL16)
        for g in range(NGRP + 1):
            nat_ref[g, dst, :] = o16_ref[0, cl, :, g * LANES:(g + 1) * LANES]

    a_n = _natural_offset(lax.broadcasted_iota(jnp.int32, (QB, 2 * QB), 0))
    c_p = lax.broadcasted_iota(jnp.int32, (QB, 2 * QB), 1)
    d = jnp.where(c_p < QB, QB, 0) + a_n - _natural_offset(c_p)
    near = (d >= 0) & (d <= QB)
    far = (d >= 0) & ((d & (DIL4 - 1)) == 0)
    bias_near0 = jnp.where(near & far, LN2, jnp.where(near | far, 0.0, NEG))
    ia = lax.broadcasted_iota(jnp.int32, (CLS4, QB), 0)
    cf = lax.broadcasted_iota(jnp.int32, (CLS4, QB), 1)
    fblk, ic = cf >> CLS4_SHIFT, cf & (CLS4 - 1)
    bias_far0 = jnp.where((fblk < NFAR) & ((fblk > 0) | (ic >= ia)), 0.0, NEG)
    lane = lax.broadcasted_iota(jnp.int32, (QB, LANES), 1)
    head_a = lane < HEAD_DIM

    def rows_of(w):
        blk, off = divmod(w, tq)
        return kv_refs[blk], off

    def group(x, c):
        return jnp.concatenate([x[c * CLS4:(c + 1) * CLS4], x[QB + c * CLS4:QB + (c + 1) * CLS4]], axis=0)

    def ungroup(parts):
        return jnp.concatenate([p_[:CLS4] for p_ in parts] + [p_[CLS4:] for p_ in parts], axis=0)

    units = [(sub, g) for sub in range(tq // QB) for g in range(NGRP)]
    ctx = {}
    for sub in range(tq // QB):
        first_valid = BACK - i * tq - sub * QB
        bias_near = jnp.where((c_p >= QB) | (NFAR * QB >= first_valid), bias_near0, NEG)
        bias_far = jnp.where(fblk * QB >= first_valid, bias_far0, NEG)
        far_blocks = [rows_of((sub + j) * QB) for j in range(NFAR)]
        ctx[sub] = dict(bias_near=jnp.concatenate([bias_near, bias_near], axis=0),
                        bias_far=jnp.concatenate([bias_far, bias_far], axis=0),
                        prev=rows_of((sub + NFAR) * QB), cur=rows_of((sub + NFAR + 1) * QB),
                        far=far_blocks + far_blocks[:1])

    def far_rows(sub, c, colsl):
        return jnp.concatenate([r[0, off + c * CLS4:off + (c + 1) * CLS4, colsl] for r, off in ctx[sub]["far"]],
                               axis=0)

    def block(sub, which, colsl):
        r, off = ctx[sub][which]
        return r[0, off:off + QB, colsl]

    scores, probs, outs = {}, {}, {}

    def stage_scores(u):
        sub, g = u
        kc = slice(P_K + g * LANES, P_K + (g + 1) * LANES)
        q2 = _pair_split(q_ref[0, sub * QB:(sub + 1) * QB, g * LANES:(g + 1) * LANES].astype(F32)).astype(BF16)
        s_far = ungroup([_dot_t(group(q2, c), far_rows(sub, c, kc)) + ctx[sub]["bias_far"] for c in range(DIL4)])
        s_near = jnp.concatenate([_dot_t(q2, block(sub, "prev", kc)), _dot_t(q2, block(sub, "cur", kc))],
                                 axis=1) + ctx[sub]["bias_near"]
        scores[u] = jnp.concatenate([s_far, s_near], axis=1)

    def stage_softmax(u):
        s = scores.pop(u)
        m = jnp.max(s, axis=-1, keepdims=True)
        p = jnp.exp(s - m)
        probs[u] = (m, jnp.sum(p, axis=-1, keepdims=True), p.astype(BF16))

    def stage_values(u):
        sub, g = u
        vc = slice(P_V + g * LANES, P_V + (g + 1) * LANES)
        pb = probs[u][2]
        outs[u] = (_dot(pb[:, QB:2 * QB], block(sub, "prev", vc)) + _dot(pb[:, 2 * QB:], block(sub, "cur", vc))
                   + ungroup([_dot(group(pb[:, :QB], c), far_rows(sub, c, vc)) for c in range(DIL4)]))

    def stage_merge(u):
        sub, g = u
        rows = slice(sub * QB, (sub + 1) * QB)
        cols = slice(g * LANES, (g + 1) * LANES)
        m, l, _ = probs.pop(u)
        o_s = _pair_join(outs.pop(u))
        m_s = jnp.where(head_a, m[:QB], m[QB:])
        l_s = jnp.where(head_a, l[:QB], l[QB:])
        for c in range(DIL4):
            dst = pl.ds(c, CLS4, stride=DIL4)
            unp_ref[3 * g, dst, :] = o_s[c * CLS4:(c + 1) * CLS4]
            unp_ref[3 * g + 1, dst, :] = m_s[c * CLS4:(c + 1) * CLS4]
            unp_ref[3 * g + 2, dst, :] = l_s[c * CLS4:(c + 1) * CLS4]
        o_n, m_n, l_n = unp_ref[3 * g], unp_ref[3 * g + 1], unp_ref[3 * g + 2]
        st = nat_ref[NGRP, rows, :]
        m_f = jnp.where(head_a, st[:, 2 * g:2 * g + 1], st[:, 2 * g + 1:2 * g + 2])
        l_f = jnp.where(head_a, st[:, STAT_L + 2 * g:STAT_L + 2 * g + 1],
                        st[:, STAT_L + 2 * g + 1:STAT_L + 2 * g + 2])
        mx = jnp.maximum(m_n, m_f)
        w_n = jnp.exp(m_n - mx)
        w_f = jnp.exp(m_f - mx)
        oa = (o_n * w_n + nat_ref[g, rows, :] * w_f) / (l_n * w_n + l_f * w_f)
        mixa_ref[rows, cols] = (oa * ga_ref[0, rows, cols]).astype(BF16)

    def finish(rows):
        z = (_dot(mixa_ref[rows, :], woutbf_ref[0:A_WIDTH, :])
             + _dot(mix_ref[0, rows, :], woutbf_ref[A_WIDTH:, :]))
        y_ref[0, rows, :] = x_ref[0, rows, :] + _rmsnorm(z, npost_ref[...])

    stages = (stage_scores, stage_softmax, stage_values, stage_merge)
    half = len(units) // 2
    for t in range(len(units) + len(stages) - 1):
        for depth, stage in enumerate(stages):
            if 0 <= t - depth < len(units):
                stage(units[t - depth])
        if t - (len(stages) - 1) == half - 1:
            finish(slice(0, tq // 2))
    finish(slice(tq // 2, tq))
    kvp_ref[...] = kvc_ref[...]

    for part in _sample_mix_stages(pl.program_id(0) * pl.num_programs(1) + i, srow_ref, kvt_ref, ck_ref, cv_ref,
                                   sst_ref, cmk_ref, cmv_ref, wdw_ref, smix_ref, wk_ref, ns_ref, uc_ref,
                                   t_s=t_s, win_len=win_len):
        part()


def _prompt_attn(kvq, o16, ga, mix, x, wout, npost, srow, kvt, ck, cv, st_pad, cmk, cmv, wdw, tq, t_s):
    b, s, _ = x.shape
    nsteps = s // tq
    nb, _, win_len = ck.shape
    assert tq == BACK and b * nsteps == nb, "one sample batch per grid step"
    row = lambda w: pl.BlockSpec((1, tq, w), lambda bi, i: (bi, i, 0))
    qspec = pl.BlockSpec((1, tq, A_WIDTH), lambda bi, i: (bi, i, P_Q // A_WIDTH))
    kvcur = pl.BlockSpec((1, tq, 2 * A_WIDTH), lambda bi, i: (bi, i, 0))
    cls = pl.BlockSpec((1, DIL16, tq // DIL16, O16_W), lambda bi, i: (bi, 0, i, 0))
    srows = pl.BlockSpec((t_s, S_ROW), lambda bi, i: (bi * nsteps + i, 0))
    per = lambda a: pl.BlockSpec((1,) + a.shape[1:], lambda bi, i: (bi * nsteps + i, 0, 0))
    return pl.pallas_call(
        functools.partial(_prompt_attn_kernel, tq=tq, t_s=t_s, win_len=win_len),
        grid=(b, nsteps),
        in_specs=[qspec, kvcur, cls, row(A_WIDTH), row(C_WIDTH + M_WIDTH), row(D_MODEL),
                  _resident(wout), _full(npost),
                  srows, _full(kvt), per(ck), per(cv), per(st_pad), per(cmk), per(cmv), _full(wdw)],
        out_specs=[row(D_MODEL), srows, per(ck),
                   pl.BlockSpec((1, CONV_W - 1, C_WIDTH), lambda bi, i: (bi * nsteps + i, 0, 0))],
        out_shape=[jax.ShapeDtypeStruct((b, s, D_MODEL), F32),
                   jax.ShapeDtypeStruct((nb * t_s, S_ROW), F32),
                   jax.ShapeDtypeStruct(ck.shape, F32),
                   jax.ShapeDtypeStruct((nb, CONV_W - 1, C_WIDTH), F32)],
        scratch_shapes=[pltpu.VMEM((tq, A_WIDTH), BF16),
                        pltpu.VMEM((NGRP + 1, tq, LANES), F32),
                        pltpu.VMEM((3 * NGRP, QB, LANES), F32),
                        pltpu.VMEM((STATE_PAD + CONV_W - 1 + t_s, C_WIDTH), F32),
                        pltpu.VMEM((1, tq, 2 * A_WIDTH), BF16),
                        pltpu.VMEM(wout.shape, BF16)],
        compiler_params=pltpu.CompilerParams(dimension_semantics=("arbitrary", "arbitrary"),
                                             vmem_limit_bytes=VMEM_LIMIT),
        name="prompt_attn",
    )(kvq, kvq, o16, ga, mix, x, wout, npost, srow, kvt, ck, cv, st_pad, cmk, cmv, wdw)


def _sample_proj_kernel(x_ref, npre_ref, win_ref, ang_ref, srow_ref, kvt_ref, gate_ref, wbf_ref):
    h = _rmsnorm(x_ref[...], npre_ref[...]).astype(BF16)
    wbf_ref[...] = win_ref[...].astype(BF16)
    proj = _dot(h, wbf_ref[...])
    pats = _rope_patterns(ang_ref[:, 0:LANES], ang_ref[:, LANES:])
    srow_ref[:, S_Q:S_U] = _rope(proj[:, O_QA:O_KA], pats) * SCALE
    srow_ref[:, S_U:S_QM] = proj[:, O_AB:O_BB] * _sigmoid(proj[:, O_BB:O_GB])
    srow_ref[:, S_QM:S_ROW] = proj[:, O_QM:O_GM]
    kv_t = jnp.concatenate([_rope(proj[:, O_KA:O_VA], pats), proj[:, O_VA:O_GA]], axis=1).T
    for slab in range(kvt_ref.shape[0]):
        kvt_ref[slab] = kv_t[:, slab * LANES:(slab + 1) * LANES]
    gate_ref[:, S_Q:S_U] = _silu(proj[:, O_GA:O_AB])
    gate_ref[:, S_U:S_QM] = _silu(proj[:, O_GB:O_QM])
    gate_ref[:, S_QM:S_ROW] = _silu(proj[:, O_GM:D_IN])


def _sample_proj(x, npre, win, ang):
    n = x.shape[0]
    assert n % LANES == 0
    shapes = [(n, S_ROW), (n // LANES, 2 * A_WIDTH, LANES), (n, S_ROW), win.shape]
    dtypes = [F32, F32, F32, BF16]
    return pl.pallas_call(
        _sample_proj_kernel,
        grid=(1,),
        in_specs=[_full(x), _full(npre), _full(win), _full(ang)],
        out_specs=[pl.BlockSpec(sh, lambda i, nd=len(sh): (0,) * nd) for sh in shapes],
        out_shape=[jax.ShapeDtypeStruct(sh, dt) for sh, dt in zip(shapes, dtypes)],
        compiler_params=pltpu.CompilerParams(vmem_limit_bytes=VMEM_LIMIT),
        name="sample_proj",
    )(x, npre, win, ang)


def _sample_out_kernel(x_ref, smix_ref, gate_ref, bdw_ref, lng_ref, lnb_ref, wpw_ref, bpw_ref, wout_ref, npost_ref,
                       y_ref):
    mixed_a = (smix_ref[:, S_Q:S_U] * gate_ref[:, S_Q:S_U]).astype(BF16)
    mixed_b = _conformer_tail(smix_ref[:, S_U:S_QM], gate_ref[:, S_U:S_QM], bdw_ref[...], lng_ref[...], lnb_ref[...],
                              wpw_ref, bpw_ref[...]).astype(BF16)
    mixed_m = (smix_ref[:, S_QM:S_ROW] * gate_ref[:, S_QM:S_ROW]).astype(BF16)
    wout = wout_ref[...].astype(BF16)
    z = (_dot(mixed_a, wout[0:A_WIDTH]) + _dot(mixed_b, wout[A_WIDTH:A_WIDTH + C_WIDTH])
         + _dot(mixed_m, wout[A_WIDTH + C_WIDTH:]))
    y_ref[...] = x_ref[...] + _rmsnorm(z, npost_ref[...])


def _sample_out(x, smix, gate, bdw, lng, lnb, wpw, bpw, wout, npost):
    args = (x, smix, gate, bdw, lng, lnb, wpw, bpw, wout, npost)
    return pl.pallas_call(
        _sample_out_kernel,
        grid=(1,),
        in_specs=[_full(a) for a in args],
        out_specs=_full(x),
        out_shape=jax.ShapeDtypeStruct(x.shape, F32),
        compiler_params=pltpu.CompilerParams(vmem_limit_bytes=VMEM_LIMIT),
        name="sample_out",
    )(*args)


def _rope_cos_sin(pos):
    inv = ROPE_THETA ** (-jnp.arange(0, ROT_DIM, 2, dtype=F32) / ROT_DIM)
    ang = pos.astype(F32)[:, None] * jnp.tile(inv, 2 * LANES // ROT_DIM)[None, :]
    return jnp.concatenate([jnp.cos(ang), jnp.sin(ang)], axis=1)


def _feature_major(cache):
    nb, rows, heads, dim = cache.shape
    return jnp.transpose(cache, (0, 2, 3, 1)).reshape(nb, heads * dim, rows)


def _row_major(cache_t, heads):
    nb, width, rows = cache_t.shape
    return jnp.transpose(cache_t.reshape(nb, heads, width // heads, rows), (0, 3, 1, 2))


PROJ_TILE = 512
ATTN_TILE = BACK


def kernel(x_prompt, x_sample, cache_win_k, cache_win_v, state_conv, cache_mem_k, cache_mem_v, mem_prompt,
           norm_pre, norm_post, w_in, w_out, norm_mem, w_mem_kv, w_dw, b_dw, ln_conv_g, ln_conv_b, w_pw2, b_pw2):
    depth = w_in.shape[0]
    assert depth == 1, "single-layer step"
    b, s, _ = x_prompt.shape
    nb, t_s, _ = x_sample.shape
    win_len = cache_win_k.shape[2]
    assert s % (DIL16 * QB) == 0 and s % PROJ_TILE == 0 and s % ATTN_TILE == 0
    assert win_len == MAX_WINDOW and win_len % LANES == 0
    assert t_s % 8 == 0 and t_s & (t_s - 1) == 0 and t_s < LANES
    l = 0
    row = lambda a: a[l][None, :]
    npre, npost, nmem = row(norm_pre), row(norm_post), row(norm_mem)
    bdw, lng, lnb, bpw = row(b_dw), row(ln_conv_g), row(ln_conv_b), row(b_pw2)
    win, wout, wmem, wpw = w_in[l], w_out[l], w_mem_kv[l], w_pw2[l]
    wdw = w_dw[l]

    pos_s = PAST_LEN + jnp.arange(t_s, dtype=jnp.int32)
    xs = x_sample.reshape(nb * t_s, D_MODEL)
    srow, kvt, gate_s, win_bf = _sample_proj(xs, npre, win, jnp.tile(_rope_cos_sin(pos_s), (nb, 1)))
    st_pad = jnp.pad(state_conv[l], ((0, 0), (STATE_PAD, 0), (0, 0)))

    mk, mv, mkb, mvb = _mem_kv(mem_prompt, nmem, wmem)
    rin = _rope_cos_sin(jnp.arange(PROJ_TILE, dtype=jnp.int32))
    rbase = _rope_cos_sin(jnp.arange(0, s, PROJ_TILE, dtype=jnp.int32))[:, None, :]
    ck, cv = _feature_major(cache_win_k[l]), _feature_major(cache_win_v[l])
    kvq, kvq16, kf, vf, ga, mix, ust, wv = _prompt_proj(x_prompt, npre, win_bf, rin, rbase, mkb, mvb, wdw, bdw,
                                                        lng, lnb, wpw, bpw, kvt, cv, PROJ_TILE, t_s)
    o16 = _dilated16(kvq16)
    y_prompt, smix, wk, nst = _prompt_attn(
        kvq, o16, ga, mix, x_prompt, wout, npost, srow, kvt, ck, cv, st_pad,
        _feature_major(cache_mem_k[l]), _feature_major(cache_mem_v[l]), wdw, ATTN_TILE, t_s)
    y_sample = _sample_out(xs, smix, gate_s, bdw, lng, lnb, wpw, bpw, wout, npost)

    keep_p = kf.shape[1]
    return (y_prompt,
            y_sample.reshape(nb, t_s, D_MODEL),
            kf.reshape(1, b, keep_p, A_HEADS, HEAD_DIM),
            vf.reshape(1, b, keep_p, A_HEADS, HEAD_DIM),
            ust[:, CONV_HIST - (CONV_W - 1):, :][None],
            mk.reshape(1, b, N_MEM, M_HEADS, HEAD_DIM),
            mv.reshape(1, b, N_MEM, M_HEADS, HEAD_DIM),
            _row_major(wk, A_HEADS)[None],
            _row_major(wv, A_HEADS)[None],
            nst[None])
```

```python
import functools
import math

import jax
import jax.numpy as jnp
from jax import lax
from jax.experimental import pallas as pl
from jax.experimental.pallas import tpu as pltpu

F32 = jnp.float32
BF16 = jnp.bfloat16

D_MODEL = 1024
HEAD_DIM = 64
HEAD_SHIFT = 6
A_WIDTH = 384
A_HEADS = 6
M_WIDTH = 256
M_HEADS = 4
C_WIDTH = 384
ROT_DIM = 16
ROPE_THETA = 500000.0
CONV_W = 31
N_MEM = 256
MAX_WINDOW = 2048
PAST_LEN = 16384
EPS = 1e-6
SCALE = HEAD_DIM ** -0.5
NEG = -1e30
LN2 = math.log(2.0)

LANES = 128
NGRP = A_WIDTH // LANES
QB = 128
DIL16 = 16
DIL4 = 4
CLS4 = QB // DIL4
CLS4_SHIFT = 5
O_QA, O_KA, O_VA, O_GA, O_AB, O_BB, O_GB, O_QM, O_GM, D_IN = 0, 384, 768, 1152, 1536, 1920, 2304, 2688, 2944, 3200
P_K, P_V, P_Q, P_W = 0, A_WIDTH, 2 * A_WIDTH, 3 * A_WIDTH

VMEM_LIMIT = 60 * 1024 * 1024


def _sigmoid(x):
    return 1.0 / (1.0 + jnp.exp(-x))


def _silu(x):
    return x * _sigmoid(x)


def _rmsnorm(x, g):
    return x * lax.rsqrt(jnp.mean(x * x, axis=-1, keepdims=True) + EPS) * g


def _dot(a, b):
    return jnp.dot(a, b, preferred_element_type=F32)


def _dot_t(a, b):
    return lax.dot_general(a, b, (((1,), (1,)), ((), ())), preferred_element_type=F32)


def _rope_patterns(cos, sin):
    in_head = lax.broadcasted_iota(jnp.int32, cos.shape, 1) & (HEAD_DIM - 1)
    rot, lo = in_head < ROT_DIM, in_head < ROT_DIM // 2
    return jnp.where(rot, cos, 1.0), jnp.where(lo, -sin, 0.0), jnp.where(rot & ~lo, sin, 0.0)


def _rope(xw, pats):
    cos, s1, s2 = pats
    outs = []
    for g in range(xw.shape[1] // LANES):
        xg = xw[:, g * LANES:(g + 1) * LANES]
        outs.append(xg * cos + pltpu.roll(xg, LANES - 8, 1) * s1 + pltpu.roll(xg, 8, 1) * s2)
    return jnp.concatenate(outs, axis=1)


def _pair_split(q):
    lane = lax.broadcasted_iota(jnp.int32, q.shape, 1)
    qa = jnp.where(lane < HEAD_DIM, q, 0.0)
    qb = jnp.where(lane >= HEAD_DIM, q, 0.0)
    return jnp.concatenate([qa, qb], axis=0)


def _pair_join(x):
    t = x.shape[0] // 2
    lane = lax.broadcasted_iota(jnp.int32, (t, LANES), 1)
    return jnp.where(lane < HEAD_DIM, x[:t], x[t:])


def _mem_attend(qm, mk_ref, mv_ref):
    outs = []
    for g in range(M_WIDTH // LANES):
        cols = slice(g * LANES, (g + 1) * LANES)
        q2 = _pair_split(qm[:, cols] * SCALE).astype(BF16)
        s = _dot_t(q2, mk_ref[0, :, cols])
        m = jnp.max(s, axis=-1, keepdims=True)
        p = jnp.exp(s - m)
        l = jnp.sum(p, axis=-1, keepdims=True)
        o = _dot(p.astype(BF16), mv_ref[0, :, cols]) * (1.0 / l)
        outs.append(_pair_join(o))
    return jnp.concatenate(outs, axis=1)


def _conformer_tail(c, gate_b, bdw, lng, lnb, wpw_ref, bpw):
    cf = c + bdw
    mu = jnp.mean(cf, axis=-1, keepdims=True)
    dev = cf - mu
    var = jnp.mean(dev * dev, axis=-1, keepdims=True)
    cn = dev * lax.rsqrt(var + EPS) * lng + lnb
    ob = _dot(_silu(cn).astype(BF16), wpw_ref[...].astype(BF16)) + bpw
    return ob * gate_b


def _full(a):
    return pl.BlockSpec(a.shape, lambda *_: (0,) * a.ndim)


def _resident(a):
    return pl.BlockSpec(a.shape, lambda *_: (0,) * a.ndim, pipeline_mode=pl.Buffered(1))


def _mem_kv_kernel(mem_ref, g_ref, w_ref, mk_ref, mv_ref, mkb_ref, mvb_ref):
    h = _rmsnorm(mem_ref[0], g_ref[...]).astype(BF16)
    kv = _dot(h, w_ref[...].astype(BF16))
    mk_ref[0] = kv[:, :M_WIDTH]
    mv_ref[0] = kv[:, M_WIDTH:]
    mkb_ref[0] = kv[:, :M_WIDTH].astype(BF16)
    mvb_ref[0] = kv[:, M_WIDTH:].astype(BF16)


def _mem_kv(mem, g, wmem):
    b = mem.shape[0]
    blk = pl.BlockSpec((1, N_MEM, M_WIDTH), lambda i: (i, 0, 0))
    return pl.pallas_call(
        _mem_kv_kernel,
        grid=(b,),
        in_specs=[pl.BlockSpec((1, N_MEM, D_MODEL), lambda i: (i, 0, 0)), _full(g), _full(wmem)],
        out_specs=[blk, blk, blk, blk],
        out_shape=[jax.ShapeDtypeStruct((b, N_MEM, M_WIDTH), F32)] * 2
        + [jax.ShapeDtypeStruct((b, N_MEM, M_WIDTH), BF16)] * 2,
        name="mem_kv",
    )(mem, g, wmem)


CONV_HIST = 32
CONV_CHUNK = 64


def _prompt_proj_kernel(x_ref, npre_ref, win_ref, rin_ref, rbase_ref, mk_ref, mv_ref,
                        wdw_ref, bdw_ref, lng_ref, lnb_ref, wpw_ref, bpw_ref, kvt_ref, cv_ref,
                        kvq_ref, kvq16_ref, kf_ref, vf_ref, ga_ref, mix_ref, ust_ref, wv_ref, cvb_ref,
                        uext_ref, ush_ref, conv_ref, stage_ref, stage4_ref, *, t_s, win_len):
    t = x_ref.shape[1]
    i = pl.program_id(1)

    @pl.when(i == 0)
    def _():
        uext_ref[0:CONV_HIST, :] = jnp.zeros((CONV_HIST, C_WIDTH), F32)

    h = _rmsnorm(x_ref[0], npre_ref[...]).astype(BF16)
    proj = {}

    def project(name, c0, c1):
        def run():
            proj[name] = _dot(h, win_ref[:, c0:c1])
        return run

    def stage(idx, val):
        for g in range(NGRP):
            stage_ref[NGRP * idx + g] = val[:, g * LANES:(g + 1) * LANES]

    pats = []

    def rope_pats():
        if not pats:
            cr, sr = rin_ref[:, 0:LANES], rin_ref[:, LANES:]
            cb, sb = rbase_ref[i, :, 0:LANES], rbase_ref[i, :, LANES:]
            pats.append(_rope_patterns(cb * cr - sb * sr, sb * cr + cb * sr))
        return pats[0]

    def do_k():
        k = _rope(proj.pop("k"), rope_pats())
        kf_ref[0] = k
        stage(0, k)

    def do_q():
        stage(2, _rope(proj.pop("q"), rope_pats()) * SCALE)

    def do_v():
        v = proj.pop("v")
        vf_ref[0] = v
        stage(1, v)

    def do_ga():
        ga_ref[0] = _silu(proj.pop("ga"))

    def do_copies():
        quarter = t // DIL4
        for c4 in range(DIL4):
            for s in range(3 * NGRP):
                stage4_ref[s, c4 * quarter:(c4 + 1) * quarter, :] = stage_ref[s, pl.ds(c4, quarter, stride=DIL4), :]
        for c in range(DIL16):
            c4, c2 = c % DIL4, c // DIL4
            rows_c = [stage4_ref[s, pl.ds(c4 * quarter + c2, t // DIL16, stride=DIL4), :] for s in range(3 * NGRP)]
            kvq16_ref[0, c] = jnp.concatenate(rows_c, axis=1).astype(BF16)
        for blk in range(t // QB):
            for c4 in range(DIL4):
                src = slice(c4 * quarter + blk * CLS4, c4 * quarter + (blk + 1) * CLS4)
                rows_c = [stage4_ref[s, src, :] for s in range(3 * NGRP)]
                kvq_ref[0, blk * QB + c4 * CLS4:blk * QB + (c4 + 1) * CLS4, :] = (
                    jnp.concatenate(rows_c, axis=1).astype(BF16))

    def do_glu():
        uext_ref[CONV_HIST:CONV_HIST + t, :] = proj.pop("ab") * _sigmoid(proj.pop("bb"))
        uext = uext_ref[...]
        for r in range(1, 8):
            ush_ref[r - 1] = pltpu.roll(uext, r, 0)

    def conv_rows(r0):
        def run():
            acc = jnp.zeros((CONV_CHUNK, C_WIDTH), F32)
            for kk in range(CONV_W):
                tap = wdw_ref[CONV_W - 1 - kk:CONV_W - kk, :]
                r, base = kk % 8, CONV_HIST + r0 - (kk - kk % 8)
                if r == 0:
                    slab = uext_ref[base:base + CONV_CHUNK, :]
                else:
                    slab = ush_ref[r - 1, base:base + CONV_CHUNK, :]
                acc = acc + slab * tap
            conv_ref[r0:r0 + CONV_CHUNK, :] = acc
        return run

    def do_history():
        tail = uext_ref[t:t + CONV_HIST, :]
        uext_ref[0:CONV_HIST, :] = tail
        ust_ref[0] = tail

    def do_tail():
        mixed_b = _conformer_tail(conv_ref[...], _silu(proj.pop("gb")), bdw_ref[...], lng_ref[...], lnb_ref[...],
                                  wpw_ref, bpw_ref[...])
        mix_ref[0, :, 0:C_WIDTH] = mixed_b.astype(BF16)

    def do_mem():
        mixed_m = _mem_attend(proj.pop("qm"), mk_ref, mv_ref) * _silu(proj.pop("gm"))
        mix_ref[0, :, C_WIDTH:] = mixed_m.astype(BF16)

    convs = [conv_rows(r0) for r0 in range(0, t, CONV_CHUNK)]
    riders = [([project("k", O_KA, O_VA)], do_k), ([project("q", O_QA, O_KA)], do_q),
              ([project("v", O_VA, O_GA)], do_v), ([project("ga", O_GA, O_AB)], do_ga),
              ([project("qm", O_QM, O_GM), project("gm", O_GM, D_IN)], do_mem),
              ([project("gb", O_GB, O_QM)], do_copies)]
    slide_v = functools.partial(_slide_cache, pl.program_id(0) * pl.num_programs(1) + i, kvt_ref, cv_ref, A_WIDTH,
                                wv_ref, t_s=t_s, win_len=win_len, bf_ref=cvb_ref)
    program = [project("ab", O_AB, O_BB), project("bb", O_BB, O_GB), slide_v, do_glu]
    for n, conv in enumerate(convs):
        if n < len(riders):
            program += riders[n][0]
        program.append(conv)
        if n < len(riders) and riders[n][1] is not None:
            program.append(riders[n][1])
    assert len(convs) >= len(riders)
    program += [do_history, do_tail]
    for piece in program:
        piece()
    assert not proj


def _prompt_proj(x, npre, win, rin, rbase, mkb, mvb, wdw, bdw, lng, lnb, wpw, bpw, kvt, cv, tile, t_s):
    b, s, _ = x.shape
    nt = s // tile
    nb, _, win_len = cv.shape
    assert b * nt == nb, "one sample batch per grid step"
    per_batch = pl.BlockSpec((1,) + cv.shape[1:], lambda bi, i: (bi * nt + i, 0, 0))
    keep = min(MAX_WINDOW, s)
    first_keep = (s - keep) // tile
    row = lambda w: pl.BlockSpec((1, tile, w), lambda bi, i: (bi, i, 0))
    memb = pl.BlockSpec((1, N_MEM, M_WIDTH), lambda bi, i: (bi, 0, 0))
    keepb = pl.BlockSpec((1, tile, A_WIDTH), lambda bi, i: (bi, jnp.maximum(i - first_keep, 0), 0))
    cls = pl.BlockSpec((1, DIL16, tile // DIL16, P_W), lambda bi, i: (bi, 0, i, 0))
    return pl.pallas_call(
        functools.partial(_prompt_proj_kernel, t_s=t_s, win_len=win_len),
        grid=(b, nt),
        in_specs=[row(D_MODEL), _full(npre), _resident(win), _full(rin), _full(rbase),
                  memb, memb, _full(wdw), _full(bdw), _full(lng), _full(lnb), _full(wpw), _full(bpw),
                  _full(kvt), per_batch],
        out_specs=[row(P_W), cls, keepb, keepb, row(A_WIDTH), row(C_WIDTH + M_WIDTH),
                   pl.BlockSpec((1, CONV_HIST, C_WIDTH), lambda bi, i: (bi, 0, 0)), per_batch, per_batch],
        out_shape=[jax.ShapeDtypeStruct((b, s, P_W), BF16),
                   jax.ShapeDtypeStruct((b, DIL16, s // DIL16, P_W), BF16),
                   jax.ShapeDtypeStruct((b, keep, A_WIDTH), F32),
                   jax.ShapeDtypeStruct((b, keep, A_WIDTH), F32),
                   jax.ShapeDtypeStruct((b, s, A_WIDTH), F32),
                   jax.ShapeDtypeStruct((b, s, C_WIDTH + M_WIDTH), BF16),
                   jax.ShapeDtypeStruct((b, CONV_HIST, C_WIDTH), F32),
                   jax.ShapeDtypeStruct(cv.shape, F32), jax.ShapeDtypeStruct(cv.shape, BF16)],
        scratch_shapes=[pltpu.VMEM((CONV_HIST + tile, C_WIDTH), F32),
                        pltpu.VMEM((7, CONV_HIST + tile, C_WIDTH), F32),
                        pltpu.VMEM((tile, C_WIDTH), F32),
                        pltpu.VMEM((3 * NGRP, tile, LANES), F32),
                        pltpu.VMEM((3 * NGRP, tile, LANES), F32)],
        compiler_params=pltpu.CompilerParams(dimension_semantics=("arbitrary", "arbitrary"),
                                             vmem_limit_bytes=VMEM_LIMIT),
        name="prompt_proj",
    )(x, npre, win, rin, rbase, mkb, mvb, wdw, bdw, lng, lnb, wpw, bpw, kvt, cv)


STAT_L = 8
O16_W = A_WIDTH + LANES


CLS_PER_STEP = 4


def _dilated16_kernel(kvq_ref, o_ref):
    a = lax.broadcasted_iota(jnp.int32, (QB, QB), 0)
    c = lax.broadcasted_iota(jnp.int32, (QB, QB), 1)
    tri_prev = jnp.where(c >= a, 0.0, NEG)
    tri_cur = jnp.where(c <= a, 0.0, NEG)
    bias_cur = jnp.concatenate([tri_cur, tri_cur], axis=0)
    both = jnp.concatenate([tri_prev, tri_cur], axis=1)
    bias_both = jnp.concatenate([both, both], axis=0)
    lane = lax.broadcasted_iota(jnp.int32, (QB, LANES), 1)
    ncls, nsub = kvq_ref.shape[1], kvq_ref.shape[2] // QB
    units = [(cl, sub, g) for cl in range(ncls) for sub in range(nsub) for g in range(NGRP)]
    scores, probs = {}, {}
    stats = {(cl, sub): jnp.zeros((QB, LANES), F32) for cl in range(ncls) for sub in range(nsub)}

    def rows_of(sub):
        return slice(sub * QB, (sub + 1) * QB)

    def stage_scores(u):
        cl, sub, g = u
        kc = slice(P_K + g * LANES, P_K + (g + 1) * LANES)
        qc = slice(P_Q + g * LANES, P_Q + (g + 1) * LANES)
        q2 = _pair_split(kvq_ref[0, cl, rows_of(sub), qc].astype(F32)).astype(BF16)
        s_cur = _dot_t(q2, kvq_ref[0, cl, rows_of(sub), kc])
        if sub == 0:
            scores[u] = s_cur + bias_cur
        else:
            scores[u] = jnp.concatenate([_dot_t(q2, kvq_ref[0, cl, rows_of(sub - 1), kc]), s_cur], axis=1) + bias_both

    def stage_softmax(u):
        s = scores.pop(u)
        m = jnp.max(s, axis=-1, keepdims=True)
        p = jnp.exp(s - m)
        probs[u] = (m, jnp.sum(p, axis=-1, keepdims=True), p.astype(BF16))

    def stage_values(u):
        cl, sub, g = u
        vc = slice(P_V + g * LANES, P_V + (g + 1) * LANES)
        m, l, pb = probs.pop(u)
        if sub == 0:
            o = _dot(pb, kvq_ref[0, cl, rows_of(sub), vc])
        else:
            o = (_dot(pb[:, :QB], kvq_ref[0, cl, rows_of(sub - 1), vc])
                 + _dot(pb[:, QB:], kvq_ref[0, cl, rows_of(sub), vc]))
        o_ref[0, cl, rows_of(sub), g * LANES:(g + 1) * LANES] = _pair_join(o)
        st = stats[cl, sub]
        st = jnp.where(lane == 2 * g, m[:QB], st)
        st = jnp.where(lane == 2 * g + 1, m[QB:], st)
        st = jnp.where(lane == STAT_L + 2 * g, l[:QB], st)
        stats[cl, sub] = jnp.where(lane == STAT_L + 2 * g + 1, l[QB:], st)
        if g == NGRP - 1:
            o_ref[0, cl, rows_of(sub), A_WIDTH:] = stats.pop((cl, sub))

    stages = (stage_scores, stage_softmax, stage_values)
    for t in range(len(units) + len(stages) - 1):
        for depth, stage in enumerate(stages):
            if 0 <= t - depth < len(units):
                stage(units[t - depth])


def _dilated16(kvq16):
    b, ncls, nj, w = kvq16.shape
    return pl.pallas_call(
        _dilated16_kernel,
        grid=(b, ncls // CLS_PER_STEP),
        in_specs=[pl.BlockSpec((1, CLS_PER_STEP, nj, w), lambda bi, c: (bi, c, 0, 0))],
        out_specs=pl.BlockSpec((1, CLS_PER_STEP, nj, O16_W), lambda bi, c: (bi, c, 0, 0)),
        out_shape=jax.ShapeDtypeStruct((b, ncls, nj, O16_W), F32),
        compiler_params=pltpu.CompilerParams(dimension_semantics=("arbitrary",) * 2,
                                             vmem_limit_bytes=VMEM_LIMIT),
        name="dilated16",
    )(kvq16)


STATE_PAD = 2
S_Q, S_U, S_QM, S_ROW = 0, A_WIDTH, A_WIDTH + C_WIDTH, A_WIDTH + C_WIDTH + M_WIDTH


def _new_rows(batch, kvt_ref, lo, hi, t_s):
    per_slab = LANES // t_s
    slab, pos = batch // per_slab, batch % per_slab
    return pltpu.roll(kvt_ref[slab, lo:hi, :], LANES - t_s - pos * t_s, 1)


def _slide_cache(batch, kvt_ref, src_ref, lo, dst_ref, *, t_s, win_len, bf_ref=None):
    new0 = LANES - t_s
    keep = lax.broadcasted_iota(jnp.int32, (A_WIDTH, LANES), 1) < new0
    nch = win_len // LANES

    def chunk(c):
        x = src_ref[0, :, c * LANES:(c + 1) * LANES]
        if bf_ref is not None:
            bf_ref[0, :, c * LANES:(c + 1) * LANES] = x.astype(BF16)
        return pltpu.roll(x, new0, 1)

    nxt = chunk(0)
    for c in range(nch):
        cur = nxt
        nxt = chunk(c + 1) if c + 1 < nch else _new_rows(batch, kvt_ref, lo, lo + A_WIDTH, t_s)
        dst_ref[0, :, c * LANES:(c + 1) * LANES] = jnp.where(keep, cur, nxt)


def _sample_mix_stages(batch, srow_ref, kvt_ref, ck_ref, cv_ref, st_ref, cmk_ref, cmv_ref, wdw_ref,
                       smix_ref, wk_ref, ns_ref, uc_ref, *, t_s, win_len):
    new_rows = functools.partial(_new_rows, batch, kvt_ref, t_s=t_s)
    return [functools.partial(_slide_cache, batch, kvt_ref, ck_ref, 0, wk_ref, t_s=t_s, win_len=win_len),
            functools.partial(_sample_attend, new_rows, srow_ref, ck_ref, cv_ref, smix_ref, t_s=t_s, win_len=win_len),
            functools.partial(_sample_mem_conv, srow_ref, st_ref, cmk_ref, cmv_ref, wdw_ref, smix_ref, ns_ref,
                              uc_ref, t_s=t_s)]


def _sample_attend(new_rows, srow_ref, ck_ref, cv_ref, smix_ref, *, t_s, win_len):
    new0 = LANES - t_s
    knt, vnt = new_rows(0, A_WIDTH), new_rows(A_WIDTH, 2 * A_WIDTH)
    q = srow_ref[:, S_Q:S_U]
    head_of_lane = lax.broadcasted_iota(jnp.int32, (t_s, A_WIDTH), 1) >> HEAD_SHIFT
    q6 = jnp.concatenate([jnp.where(head_of_lane == h, q, 0.0) for h in range(A_HEADS)], axis=0).astype(BF16)
    s_c = _dot(q6, ck_ref[0].astype(BF16))
    s_n = _dot(q6, knt.astype(BF16))

    def weights(shape, key0, lo):
        tq = lax.broadcasted_iota(jnp.int32, shape, 0) & (t_s - 1)
        key = lax.broadcasted_iota(jnp.int32, shape, 1) + key0
        d = win_len + tq - key
        ok = (d >= 0) & (key >= lo)
        w = ((d <= 128).astype(F32) + ((d <= 512) & ((d & 3) == 0)).astype(F32)
             + ((d <= 2048) & ((d & 15) == 0)).astype(F32))
        return jnp.where(ok, w, 0.0)

    w_c = weights(s_c.shape, 0, 0)
    w_n = weights(s_n.shape, win_len - new0, win_len)
    s_c = jnp.where(w_c > 0.0, s_c, NEG)
    s_n = jnp.where(w_n > 0.0, s_n, NEG)
    m = jnp.maximum(jnp.max(s_c, axis=-1, keepdims=True), jnp.max(s_n, axis=-1, keepdims=True))
    p_c = jnp.exp(s_c - m) * w_c
    p_n = jnp.exp(s_n - m) * w_n
    l = jnp.sum(p_c, axis=-1, keepdims=True) + jnp.sum(p_n, axis=-1, keepdims=True)
    o = (_dot_t(p_c.astype(BF16), cv_ref[0]) + _dot_t(p_n.astype(BF16), vnt.astype(BF16))) * (1.0 / l)
    oa = jnp.zeros((t_s, A_WIDTH), F32)
    for h in range(A_HEADS):
        oa = oa + jnp.where(head_of_lane == h, o[h * t_s:(h + 1) * t_s], 0.0)
    smix_ref[:, S_Q:S_U] = oa


def _sample_mem_conv(srow_ref, st_ref, cmk_ref, cmv_ref, wdw_ref, smix_ref, ns_ref, uc_ref, *, t_s):
    qm = srow_ref[:, S_QM:S_ROW] * SCALE
    mhead = lax.broadcasted_iota(jnp.int32, (t_s, M_WIDTH), 1) >> HEAD_SHIFT
    q4 = jnp.concatenate([jnp.where(mhead == h, qm, 0.0) for h in range(M_HEADS)], axis=0).astype(BF16)
    sm = _dot(q4, cmk_ref[0].astype(BF16))
    mm = jnp.max(sm, axis=-1, keepdims=True)
    pm = jnp.exp(sm - mm)
    lm = jnp.sum(pm, axis=-1, keepdims=True)
    om4 = _dot_t(pm.astype(BF16), cmv_ref[0].astype(BF16)) * (1.0 / lm)
    om = jnp.zeros((t_s, M_WIDTH), F32)
    for h in range(M_HEADS):
        om = om + jnp.where(mhead == h, om4[h * t_s:(h + 1) * t_s], 0.0)
    smix_ref[:, S_QM:S_ROW] = om

    nst = st_ref.shape[1]
    uc_ref[0:nst, :] = st_ref[0]
    uc_ref[nst:nst + t_s, :] = srow_ref[:, S_U:S_QM]
    acc = jnp.zeros((t_s, C_WIDTH), F32)
    for w in range(CONV_W):
        acc = acc + uc_ref[STATE_PAD + w:STATE_PAD + w + t_s, :] * wdw_ref[w:w + 1, :]
    smix_ref[:, S_U:S_QM] = acc
    ns_ref[0] = uc_ref[STATE_PAD + t_s:STATE_PAD + t_s + CONV_W - 1, :]


BACK = 512
NFAR = BACK // QB - 1

def _natural_offset(p):
    p = p & (QB - 1)
    return DIL4 * (p & (CLS4 - 1)) + (p >> CLS4_SHIFT)


def _prompt_attn_kernel(q_ref, kvc_ref, o16_ref, ga_ref, mix_ref, x_ref, wout_ref, npost_ref,
                        srow_ref, kvt_ref, ck_ref, cv_ref, sst_ref, cmk_ref, cmv_ref, wdw_ref,
                        y_ref, smix_ref, wk_ref, ns_ref,
                        mixa_ref, nat_ref, unp_ref, uc_ref, kvp_ref, woutbf_ref, *, tq, t_s, win_len):
    i = pl.program_id(1)
    kv_refs = (kvp_ref, kvc_ref)

    @pl.when((pl.program_id(0) == 0) & (i == 0))
    def _():
        woutbf_ref[...] = wout_ref[...].astype(BF16)

    @pl.when(i == 0)
    def _():
        kvp_ref[...] = jnp.zeros(kvp_ref.shape, BF16)

    for cl in range(DIL16):
        dst = pl.ds(cl, tq // DIL16, stride=DIL16)
        for g in range(NGRP + 1):
            nat_ref[g, dst, :] = o16_ref[0, cl, :, g * LANES:(g + 1) * LANES]

    a_n = _natural_offset(lax.broadcasted_iota(jnp.int32, (QB, 2 * QB), 0))
    c_p = lax.broadcasted_iota(jnp.int32, (QB, 2 * QB), 1)
    d = jnp.where(c_p < QB, QB, 0) + a_n - _natural_offset(c_p)
    near = (d >= 0) & (d <= QB)
    far = (d >= 0) & ((d & (DIL4 - 1)) == 0)
    bias_near0 = jnp.where(near & far, LN2, jnp.where(near | far, 0.0, NEG))
    ia = lax.broadcasted_iota(jnp.int32, (CLS4, QB), 0)
    cf = lax.broadcasted_iota(jnp.int32, (CLS4, QB), 1)
    fblk, ic = cf >> CLS4_SHIFT, cf & (CLS4 - 1)
    bias_far0 = jnp.where((fblk < NFAR) & ((fblk > 0) | (ic >= ia)), 0.0, NEG)
    lane = lax.broadcasted_iota(jnp.int32, (QB, LANES), 1)
    head_a = lane < HEAD_DIM

    def rows_of(w):
        blk, off = divmod(w, tq)
        return kv_refs[blk], off

    def group(x, c):
        return jnp.concatenate([x[c * CLS4:(c + 1) * CLS4], x[QB + c * CLS4:QB + (c + 1) * CLS4]], axis=0)

    def ungroup(parts):
        return jnp.concatenate([p_[:CLS4] for p_ in parts] + [p_[CLS4:] for p_ in parts], axis=0)

    units = [(sub, g) for sub in range(tq // QB) for g in range(NGRP)]
    ctx = {}
    for sub in range(tq // QB):
        first_valid = BACK - i * tq - sub * QB
        bias_near = jnp.where((c_p >= QB) | (NFAR * QB >= first_valid), bias_near0, NEG)
        bias_far = jnp.where(fblk * QB >= first_valid, bias_far0, NEG)
        far_blocks = [rows_of((sub + j) * QB) for j in range(NFAR)]
        ctx[sub] = dict(bias_near=jnp.concatenate([bias_near, bias_near], axis=0),
                        bias_far=jnp.concatenate([bias_far, bias_far], axis=0),
                        prev=rows_of((sub + NFAR) * QB), cur=rows_of((sub + NFAR + 1) * QB),
                        far=far_blocks + far_blocks[:1])

    def far_rows(sub, c, colsl):
        return jnp.concatenate([r[0, off + c * CLS4:off + (c + 1) * CLS4, colsl] for r, off in ctx[sub]["far"]],
                               axis=0)

    def block(sub, which, colsl):
        r, off = ctx[sub][which]
        return r[0, off:off + QB, colsl]

    scores, probs, outs = {}, {}, {}

    def stage_scores(u):
        sub, g = u
        kc = slice(P_K + g * LANES, P_K + (g + 1) * LANES)
        q2 = _pair_split(q_ref[0, sub * QB:(sub + 1) * QB, g * LANES:(g + 1) * LANES].astype(F32)).astype(BF16)
        s_far = ungroup([_dot_t(group(q2, c), far_rows(sub, c, kc)) + ctx[sub]["bias_far"] for c in range(DIL4)])
        s_near = jnp.concatenate([_dot_t(q2, block(sub, "prev", kc)), _dot_t(q2, block(sub, "cur", kc))],
                                 axis=1) + ctx[sub]["bias_near"]
        scores[u] = jnp.concatenate([s_far, s_near], axis=1)

    def stage_softmax(u):
        s = scores.pop(u)
        m = jnp.max(s, axis=-1, keepdims=True)
        p = jnp.exp(s - m)
        probs[u] = (m, jnp.sum(p, axis=-1, keepdims=True), p.astype(BF16))

    def stage_values(u):
        sub, g = u
        vc = slice(P_V + g * LANES, P_V + (g + 1) * LANES)
        pb = probs[u][2]
        outs[u] = (_dot(pb[:, QB:2 * QB], block(sub, "prev", vc)) + _dot(pb[:, 2 * QB:], block(sub, "cur", vc))
                   + ungroup([_dot(group(pb[:, :QB], c), far_rows(sub, c, vc)) for c in range(DIL4)]))

    def stage_merge(u):
        sub, g = u
        rows = slice(sub * QB, (sub + 1) * QB)
        cols = slice(g * LANES, (g + 1) * LANES)
        m, l, _ = probs.pop(u)
        o_s = _pair_join(outs.pop(u))
        m_s = jnp.where(head_a, m[:QB], m[QB:])
        l_s = jnp.where(head_a, l[:QB], l[QB:])
        for c in range(DIL4):
            dst = pl.ds(c, CLS4, stride=DIL4)
            unp_ref[3 * g, dst, :] = o_s[c * CLS4:(c + 1) * CLS4]
            unp_ref[3 * g + 1, dst, :] = m_s[c * CLS4:(c + 1) * CLS4]
            unp_ref[3 * g + 2, dst, :] = l_s[c * CLS4:(c + 1) * CLS4]
        o_n, m_n, l_n = unp_ref[3 * g], unp_ref[3 * g + 1], unp_ref[3 * g + 2]
        st = nat_ref[NGRP, rows, :]
        m_f = jnp.where(head_a, st[:, 2 * g:2 * g + 1], st[:, 2 * g + 1:2 * g + 2])
        l_f = jnp.where(head_a, st[:, STAT_L + 2 * g:STAT_L + 2 * g + 1],
                        st[:, STAT_L + 2 * g + 1:STAT_L + 2 * g + 2])
        mx = jnp.maximum(m_n, m_f)
        w_n = jnp.exp(m_n - mx)
        w_f = jnp.exp(m_f - mx)
        oa = (o_n * w_n + nat_ref[g, rows, :] * w_f) / (l_n * w_n + l_f * w_f)
        mixa_ref[rows, cols] = (oa * ga_ref[0, rows, cols]).astype(BF16)

    def finish(rows):
        z = (_dot(mixa_ref[rows, :], woutbf_ref[0:A_WIDTH, :])
             + _dot(mix_ref[0, rows, :], woutbf_ref[A_WIDTH:, :]))
        y_ref[0, rows, :] = x_ref[0, rows, :] + _rmsnorm(z, npost_ref[...])

    stages = (stage_scores, stage_softmax, stage_values, stage_merge)
    half = len(units) // 2
    for t in range(len(units) + len(stages) - 1):
        for depth, stage in enumerate(stages):
            if 0 <= t - depth < len(units):
                stage(units[t - depth])
        if t - (len(stages) - 1) == half - 1:
            finish(slice(0, tq // 2))
    finish(slice(tq // 2, tq))
    kvp_ref[...] = kvc_ref[...]

    for part in _sample_mix_stages(pl.program_id(0) * pl.num_programs(1) + i, srow_ref, kvt_ref, ck_ref, cv_ref,
                                   sst_ref, cmk_ref, cmv_ref, wdw_ref, smix_ref, wk_ref, ns_ref, uc_ref,
                                   t_s=t_s, win_len=win_len):
        part()


def _prompt_attn(kvq, o16, ga, mix, x, wout, npost, srow, kvt, ck, cv, st_pad, cmk, cmv, wdw, tq, t_s):
    b, s, _ = x.shape
    nsteps = s // tq
    nb, _, win_len = ck.shape
    assert tq == BACK and b * nsteps == nb, "one sample batch per grid step"
    row = lambda w: pl.BlockSpec((1, tq, w), lambda bi, i: (bi, i, 0))
    qspec = pl.BlockSpec((1, tq, A_WIDTH), lambda bi, i: (bi, i, P_Q // A_WIDTH))
    kvcur = pl.BlockSpec((1, tq, 2 * A_WIDTH), lambda bi, i: (bi, i, 0))
    cls = pl.BlockSpec((1, DIL16, tq // DIL16, O16_W), lambda bi, i: (bi, 0, i, 0))
    srows = pl.BlockSpec((t_s, S_ROW), lambda bi, i: (bi * nsteps + i, 0))
    per = lambda a: pl.BlockSpec((1,) + a.shape[1:], lambda bi, i: (bi * nsteps + i, 0, 0))
    return pl.pallas_call(
        functools.partial(_prompt_attn_kernel, tq=tq, t_s=t_s, win_len=win_len),
        grid=(b, nsteps),
        in_specs=[qspec, kvcur, cls, row(A_WIDTH), row(C_WIDTH + M_WIDTH), row(D_MODEL),
                  _resident(wout), _full(npost),
                  srows, _full(kvt), per(ck), per(cv), per(st_pad), per(cmk), per(cmv), _full(wdw)],
        out_specs=[row(D_MODEL), srows, per(ck),
                   pl.BlockSpec((1, CONV_W - 1, C_WIDTH), lambda bi, i: (bi * nsteps + i, 0, 0))],
        out_shape=[jax.ShapeDtypeStruct((b, s, D_MODEL), F32),
                   jax.ShapeDtypeStruct((nb * t_s, S_ROW), F32),
                   jax.ShapeDtypeStruct(ck.shape, F32),
                   jax.ShapeDtypeStruct((nb, CONV_W - 1, C_WIDTH), F32)],
        scratch_shapes=[pltpu.VMEM((tq, A_WIDTH), BF16),
                        pltpu.VMEM((NGRP + 1, tq, LANES), F32),
                        pltpu.VMEM((3 * NGRP, QB, LANES), F32),
                        pltpu.VMEM((STATE_PAD + CONV_W - 1 + t_s, C_WIDTH), F32),
                        pltpu.VMEM((1, tq, 2 * A_WIDTH), BF16),
                        pltpu.VMEM(wout.shape, BF16)],
        compiler_params=pltpu.CompilerParams(dimension_semantics=("arbitrary", "arbitrary"),
                                             vmem_limit_bytes=VMEM_LIMIT),
        name="prompt_attn",
    )(kvq, kvq, o16, ga, mix, x, wout, npost, srow, kvt, ck, cv, st_pad, cmk, cmv, wdw)


def _sample_proj_kernel(x_ref, npre_ref, win_ref, ang_ref, srow_ref, kvt_ref, gate_ref, wbf_ref):
    h = _rmsnorm(x_ref[...], npre_ref[...]).astype(BF16)
    wbf_ref[...] = win_ref[...].astype(BF16)
    proj = _dot(h, wbf_ref[...])
    pats = _rope_patterns(ang_ref[:, 0:LANES], ang_ref[:, LANES:])
    srow_ref[:, S_Q:S_U] = _rope(proj[:, O_QA:O_KA], pats) * SCALE
    srow_ref[:, S_U:S_QM] = proj[:, O_AB:O_BB] * _sigmoid(proj[:, O_BB:O_GB])
    srow_ref[:, S_QM:S_ROW] = proj[:, O_QM:O_GM]
    kv_t = jnp.concatenate([_rope(proj[:, O_KA:O_VA], pats), proj[:, O_VA:O_GA]], axis=1).T
    for slab in range(kvt_ref.shape[0]):
        kvt_ref[slab] = kv_t[:, slab * LANES:(slab + 1) * LANES]
    gate_ref[:, S_Q:S_U] = _silu(proj[:, O_GA:O_AB])
    gate_ref[:, S_U:S_QM] = _silu(proj[:, O_GB:O_QM])
    gate_ref[:, S_QM:S_ROW] = _silu(proj[:, O_GM:D_IN])


def _sample_proj(x, npre, win, ang):
    n = x.shape[0]
    assert n % LANES == 0
    shapes = [(n, S_ROW), (n // LANES, 2 * A_WIDTH, LANES), (n, S_ROW), win.shape]
    dtypes = [F32, F32, F32, BF16]
    return pl.pallas_call(
        _sample_proj_kernel,
        grid=(1,),
        in_specs=[_full(x), _full(npre), _full(win), _full(ang)],
        out_specs=[pl.BlockSpec(sh, lambda i, nd=len(sh): (0,) * nd) for sh in shapes],
        out_shape=[jax.ShapeDtypeStruct(sh, dt) for sh, dt in zip(shapes, dtypes)],
        compiler_params=pltpu.CompilerParams(vmem_limit_bytes=VMEM_LIMIT),
        name="sample_proj",
    )(x, npre, win, ang)


def _sample_out_kernel(x_ref, smix_ref, gate_ref, bdw_ref, lng_ref, lnb_ref, wpw_ref, bpw_ref, wout_ref, npost_ref,
                       y_ref):
    mixed_a = (smix_ref[:, S_Q:S_U] * gate_ref[:, S_Q:S_U]).astype(BF16)
    mixed_b = _conformer_tail(smix_ref[:, S_U:S_QM], gate_ref[:, S_U:S_QM], bdw_ref[...], lng_ref[...], lnb_ref[...],
                              wpw_ref, bpw_ref[...]).astype(BF16)
    mixed_m = (smix_ref[:, S_QM:S_ROW] * gate_ref[:, S_QM:S_ROW]).astype(BF16)
    wout = wout_ref[...].astype(BF16)
    z = (_dot(mixed_a, wout[0:A_WIDTH]) + _dot(mixed_b, wout[A_WIDTH:A_WIDTH + C_WIDTH])
         + _dot(mixed_m, wout[A_WIDTH + C_WIDTH:]))
    y_ref[...] = x_ref[...] + _rmsnorm(z, npost_ref[...])


def _sample_out(x, smix, gate, bdw, lng, lnb, wpw, bpw, wout, npost):
    args = (x, smix, gate, bdw, lng, lnb, wpw, bpw, wout, npost)
    return pl.pallas_call(
        _sample_out_kernel,
        grid=(1,),
        in_specs=[_full(a) for a in args],
        out_specs=_full(x),
        out_shape=jax.ShapeDtypeStruct(x.shape, F32),
        compiler_params=pltpu.CompilerParams(vmem_limit_bytes=VMEM_LIMIT),
        name="sample_out",
    )(*args)


def _rope_cos_sin(pos):
    inv = ROPE_THETA ** (-jnp.arange(0, ROT_DIM, 2, dtype=F32) / ROT_DIM)
    ang = pos.astype(F32)[:, None] * jnp.tile(inv, 2 * LANES // ROT_DIM)[None, :]
    return jnp.concatenate([jnp.cos(ang), jnp.sin(ang)], axis=1)


def _feature_major(cache):
    nb, rows, heads, dim = cache.shape
    return jnp.transpose(cache, (0, 2, 3, 1)).reshape(nb, heads * dim, rows)


def _row_major(cache_t, heads):
    nb, width, rows = cache_t.shape
    return jnp.transpose(cache_t.reshape(nb, heads, width // heads, rows), (0, 3, 1, 2))


PROJ_TILE = 512
ATTN_TILE = BACK


def kernel(x_prompt, x_sample, cache_win_k, cache_win_v, state_conv, cache_mem_k, cache_mem_v, mem_prompt,
           norm_pre, norm_post, w_in, w_out, norm_mem, w_mem_kv, w_dw, b_dw, ln_conv_g, ln_conv_b, w_pw2, b_pw2):
    depth = w_in.shape[0]
    assert depth == 1, "single-layer step"
    b, s, _ = x_prompt.shape
    nb, t_s, _ = x_sample.shape
    win_len = cache_win_k.shape[2]
    assert s % (DIL16 * QB) == 0 and s % PROJ_TILE == 0 and s % ATTN_TILE == 0
    assert win_len == MAX_WINDOW and win_len % LANES == 0
    assert t_s % 8 == 0 and t_s & (t_s - 1) == 0 and t_s < LANES
    l = 0
    row = lambda a: a[l][None, :]
    npre, npost, nmem = row(norm_pre), row(norm_post), row(norm_mem)
    bdw, lng, lnb, bpw = row(b_dw), row(ln_conv_g), row(ln_conv_b), row(b_pw2)
    win, wout, wmem, wpw = w_in[l], w_out[l], w_mem_kv[l], w_pw2[l]
    wdw = w_dw[l]

    pos_s = PAST_LEN + jnp.arange(t_s, dtype=jnp.int32)
    xs = x_sample.reshape(nb * t_s, D_MODEL)
    srow, kvt, gate_s, win_bf = _sample_proj(xs, npre, win, jnp.tile(_rope_cos_sin(pos_s), (nb, 1)))
    st_pad = jnp.pad(state_conv[l], ((0, 0), (STATE_PAD, 0), (0, 0)))

    mk, mv, mkb, mvb = _mem_kv(mem_prompt, nmem, wmem)
    rin = _rope_cos_sin(jnp.arange(PROJ_TILE, dtype=jnp.int32))
    rbase = _rope_cos_sin(jnp.arange(0, s, PROJ_TILE, dtype=jnp.int32))[:, None, :]
    ck, cv = _feature_major(cache_win_k[l]), _feature_major(cache_win_v[l])
    kvq, kvq16, kf, vf, ga, mix, ust, wv, cvb = _prompt_proj(x_prompt, npre, win_bf, rin, rbase, mkb, mvb, wdw, bdw,
                                                             lng, lnb, wpw, bpw, kvt, cv, PROJ_TILE, t_s)
    o16 = _dilated16(kvq16)
    y_prompt, smix, wk, nst = _prompt_attn(
        kvq, o16, ga, mix, x_prompt, wout, npost, srow, kvt, ck, cvb, st_pad,
        _feature_major(cache_mem_k[l]), _feature_major(cache_mem_v[l]), wdw, ATTN_TILE, t_s)
    y_sample = _sample_out(xs, smix, gate_s, bdw, lng, lnb, wpw, bpw, wout, npost)

    keep_p = kf.shape[1]
    return (y_prompt,
            y_sample.reshape(nb, t_s, D_MODEL),
            kf.reshape(1, b, keep_p, A_HEADS, HEAD_DIM),
            vf.reshape(1, b, keep_p, A_HEADS, HEAD_DIM),
            ust[:, CONV_HIST - (CONV_W - 1):, :][None],
            mk.reshape(1, b, N_MEM, M_HEADS, HEAD_DIM),
            mv.reshape(1, b, N_MEM, M_HEADS, HEAD_DIM),
            _row_major(wk, A_HEADS)[None],
            _row_major(wv, A_HEADS)[None],
            nst[None])
```

```python
import functools
import math

import jax
import jax.numpy as jnp
from jax import lax
from jax.experimental import pallas as pl
from jax.experimental.pallas import tpu as pltpu

F32 = jnp.float32
BF16 = jnp.bfloat16

D_MODEL = 1024
HEAD_DIM = 64
HEAD_SHIFT = 6
A_WIDTH = 384
A_HEADS = 6
M_WIDTH = 256
M_HEADS = 4
C_WIDTH = 384
ROT_DIM = 16
ROPE_THETA = 500000.0
CONV_W = 31
N_MEM = 256
MAX_WINDOW = 2048
PAST_LEN = 16384
EPS = 1e-6
SCALE = HEAD_DIM ** -0.5
NEG = -1e30
LN2 = math.log(2.0)

LANES = 128
NGRP = A_WIDTH // LANES
QB = 128
DIL16 = 16
DIL4 = 4
CLS4 = QB // DIL4
CLS4_SHIFT = 5
O_QA, O_KA, O_VA, O_GA, O_AB, O_BB, O_GB, O_QM, O_GM, D_IN = 0, 384, 768, 1152, 1536, 1920, 2304, 2688, 2944, 3200
P_K, P_V, P_Q, P_W = 0, A_WIDTH, 2 * A_WIDTH, 3 * A_WIDTH

VMEM_LIMIT = 60 * 1024 * 1024


def _sigmoid(x):
    return 1.0 / (1.0 + jnp.exp(-x))


def _silu(x):
    return x * _sigmoid(x)


def _rmsnorm(x, g):
    return x * lax.rsqrt(jnp.mean(x * x, axis=-1, keepdims=True) + EPS) * g


def _dot(a, b):
    return jnp.dot(a, b, preferred_element_type=F32)


def _dot_t(a, b):
    return lax.dot_general(a, b, (((1,), (1,)), ((), ())), preferred_element_type=F32)


def _rope_patterns(cos, sin):
    in_head = lax.broadcasted_iota(jnp.int32, cos.shape, 1) & (HEAD_DIM - 1)
    rot, lo = in_head < ROT_DIM, in_head < ROT_DIM // 2
    return jnp.where(rot, cos, 1.0), jnp.where(lo, -sin, 0.0), jnp.where(rot & ~lo, sin, 0.0)


def _rope(xw, pats):
    cos, s1, s2 = pats
    outs = []
    for g in range(xw.shape[1] // LANES):
        xg = xw[:, g * LANES:(g + 1) * LANES]
        outs.append(xg * cos + pltpu.roll(xg, LANES - 8, 1) * s1 + pltpu.roll(xg, 8, 1) * s2)
    return jnp.concatenate(outs, axis=1)


def _pair_split(q):
    lane = lax.broadcasted_iota(jnp.int32, q.shape, 1)
    qa = jnp.where(lane < HEAD_DIM, q, 0.0)
    qb = jnp.where(lane >= HEAD_DIM, q, 0.0)
    return jnp.concatenate([qa, qb], axis=0)


def _pair_join(x):
    t = x.shape[0] // 2
    lane = lax.broadcasted_iota(jnp.int32, (t, LANES), 1)
    return jnp.where(lane < HEAD_DIM, x[:t], x[t:])


def _mem_attend(qm, mk_ref, mv_ref):
    outs = []
    for g in range(M_WIDTH // LANES):
        cols = slice(g * LANES, (g + 1) * LANES)
        q2 = _pair_split(qm[:, cols] * SCALE).astype(BF16)
        s = _dot_t(q2, mk_ref[0, :, cols])
        m = jnp.max(s, axis=-1, keepdims=True)
        p = jnp.exp(s - m)
        l = jnp.sum(p, axis=-1, keepdims=True)
        o = _dot(p.astype(BF16), mv_ref[0, :, cols]) * (1.0 / l)
        outs.append(_pair_join(o))
    return jnp.concatenate(outs, axis=1)


def _conformer_tail(c, gate_b, bdw, lng, lnb, wpw_ref, bpw):
    cf = c + bdw
    mu = jnp.mean(cf, axis=-1, keepdims=True)
    dev = cf - mu
    var = jnp.mean(dev * dev, axis=-1, keepdims=True)
    cn = dev * lax.rsqrt(var + EPS) * lng + lnb
    ob = _dot(_silu(cn).astype(BF16), wpw_ref[...].astype(BF16)) + bpw
    return ob * gate_b


def _full(a):
    return pl.BlockSpec(a.shape, lambda *_: (0,) * a.ndim)


def _resident(a):
    return pl.BlockSpec(a.shape, lambda *_: (0,) * a.ndim, pipeline_mode=pl.Buffered(1))


def _mem_kv_kernel(mem_ref, g_ref, w_ref, mk_ref, mv_ref, mkb_ref, mvb_ref):
    h = _rmsnorm(mem_ref[0], g_ref[...]).astype(BF16)
    kv = _dot(h, w_ref[...].astype(BF16))
    mk_ref[0] = kv[:, :M_WIDTH]
    mv_ref[0] = kv[:, M_WIDTH:]
    mkb_ref[0] = kv[:, :M_WIDTH].astype(BF16)
    mvb_ref[0] = kv[:, M_WIDTH:].astype(BF16)


def _mem_kv(mem, g, wmem):
    b = mem.shape[0]
    blk = pl.BlockSpec((1, N_MEM, M_WIDTH), lambda i: (i, 0, 0))
    return pl.pallas_call(
        _mem_kv_kernel,
        grid=(b,),
        in_specs=[pl.BlockSpec((1, N_MEM, D_MODEL), lambda i: (i, 0, 0)), _full(g), _full(wmem)],
        out_specs=[blk, blk, blk, blk],
        out_shape=[jax.ShapeDtypeStruct((b, N_MEM, M_WIDTH), F32)] * 2
        + [jax.ShapeDtypeStruct((b, N_MEM, M_WIDTH), BF16)] * 2,
        name="mem_kv",
    )(mem, g, wmem)


CONV_HIST = 32
CONV_CHUNK = 64


def _prompt_proj_kernel(x_ref, npre_ref, win_ref, rin_ref, rbase_ref, mk_ref, mv_ref,
                        wdw_ref, bdw_ref, lng_ref, lnb_ref, wpw_ref, bpw_ref, kvt_ref, cv_ref,
                        kvq_ref, kvq16_ref, kf_ref, vf_ref, ga_ref, mix_ref, ust_ref, wv_ref,
                        uext_ref, ush_ref, conv_ref, stage_ref, stage4_ref, *, t_s, win_len):
    t = x_ref.shape[1]
    i = pl.program_id(1)

    @pl.when(i == 0)
    def _():
        uext_ref[0:CONV_HIST, :] = jnp.zeros((CONV_HIST, C_WIDTH), F32)

    h = _rmsnorm(x_ref[0], npre_ref[...]).astype(BF16)
    proj = {}

    def project(name, c0, c1):
        def run():
            proj[name] = _dot(h, win_ref[:, c0:c1])
        return run

    def stage(idx, val):
        for g in range(NGRP):
            stage_ref[NGRP * idx + g] = val[:, g * LANES:(g + 1) * LANES]

    pats = []

    def rope_pats():
        if not pats:
            cr, sr = rin_ref[:, 0:LANES], rin_ref[:, LANES:]
            cb, sb = rbase_ref[i, :, 0:LANES], rbase_ref[i, :, LANES:]
            pats.append(_rope_patterns(cb * cr - sb * sr, sb * cr + cb * sr))
        return pats[0]

    def do_k():
        k = _rope(proj.pop("k"), rope_pats())
        kf_ref[0] = k
        stage(0, k)

    def do_q():
        stage(2, _rope(proj.pop("q"), rope_pats()) * SCALE)

    def do_v():
        v = proj.pop("v")
        vf_ref[0] = v
        stage(1, v)

    def do_ga():
        ga_ref[0] = _silu(proj.pop("ga"))

    def do_copies():
        quarter = t // DIL4
        for c4 in range(DIL4):
            for s in range(3 * NGRP):
                stage4_ref[s, c4 * quarter:(c4 + 1) * quarter, :] = stage_ref[s, pl.ds(c4, quarter, stride=DIL4), :]
        for c in range(DIL16):
            c4, c2 = c % DIL4, c // DIL4
            rows_c = [stage4_ref[s, pl.ds(c4 * quarter + c2, t // DIL16, stride=DIL4), :] for s in range(3 * NGRP)]
            kvq16_ref[0, c] = jnp.concatenate(rows_c, axis=1).astype(BF16)
        for blk in range(t // QB):
            for c4 in range(DIL4):
                src = slice(c4 * quarter + blk * CLS4, c4 * quarter + (blk + 1) * CLS4)
                rows_c = [stage4_ref[s, src, :] for s in range(3 * NGRP)]
                kvq_ref[0, blk * QB + c4 * CLS4:blk * QB + (c4 + 1) * CLS4, :] = (
                    jnp.concatenate(rows_c, axis=1).astype(BF16))

    def do_glu():
        uext_ref[CONV_HIST:CONV_HIST + t, :] = proj.pop("ab") * _sigmoid(proj.pop("bb"))
        uext = uext_ref[...]
        for r in range(1, 8):
            ush_ref[r - 1] = pltpu.roll(uext, r, 0)

    def conv_rows(r0):
        def run():
            acc = jnp.zeros((CONV_CHUNK, C_WIDTH), F32)
            for kk in range(CONV_W):
                tap = wdw_ref[CONV_W - 1 - kk:CONV_W - kk, :]
                r, base = kk % 8, CONV_HIST + r0 - (kk - kk % 8)
                if r == 0:
                    slab = uext_ref[base:base + CONV_CHUNK, :]
                else:
                    slab = ush_ref[r - 1, base:base + CONV_CHUNK, :]
                acc = acc + slab * tap
            conv_ref[r0:r0 + CONV_CHUNK, :] = acc
        return run

    def do_history():
        tail = uext_ref[t:t + CONV_HIST, :]
        uext_ref[0:CONV_HIST, :] = tail
        ust_ref[0] = tail

    def do_tail():
        mixed_b = _conformer_tail(conv_ref[...], _silu(proj.pop("gb")), bdw_ref[...], lng_ref[...], lnb_ref[...],
                                  wpw_ref, bpw_ref[...])
        mix_ref[0, :, 0:C_WIDTH] = mixed_b.astype(BF16)

    def do_mem():
        mixed_m = _mem_attend(proj.pop("qm"), mk_ref, mv_ref) * _silu(proj.pop("gm"))
        mix_ref[0, :, C_WIDTH:] = mixed_m.astype(BF16)

    convs = [conv_rows(r0) for r0 in range(0, t, CONV_CHUNK)]
    riders = [([project("k", O_KA, O_VA)], do_k), ([project("q", O_QA, O_KA)], do_q),
              ([project("v", O_VA, O_GA)], do_v), ([project("ga", O_GA, O_AB)], do_ga),
              ([project("qm", O_QM, O_GM), project("gm", O_GM, D_IN)], do_mem),
              ([project("gb", O_GB, O_QM)], do_copies)]
    slide_v = functools.partial(_slide_cache, pl.program_id(0) * pl.num_programs(1) + i, kvt_ref, cv_ref, A_WIDTH,
                                wv_ref, t_s=t_s, win_len=win_len)
    program = [project("ab", O_AB, O_BB), project("bb", O_BB, O_GB), slide_v, do_glu]
    for n, conv in enumerate(convs):
        if n < len(riders):
            program += riders[n][0]
        program.append(conv)
        if n < len(riders) and riders[n][1] is not None:
            program.append(riders[n][1])
    assert len(convs) >= len(riders)
    program += [do_history, do_tail]
    for piece in program:
        piece()
    assert not proj


def _prompt_proj(x, npre, win, rin, rbase, mkb, mvb, wdw, bdw, lng, lnb, wpw, bpw, kvt, cv, tile, t_s):
    b, s, _ = x.shape
    nt = s // tile
    nb, _, win_len = cv.shape
    assert b * nt == nb, "one sample batch per grid step"
    per_batch = pl.BlockSpec((1,) + cv.shape[1:], lambda bi, i: (bi * nt + i, 0, 0))
    keep = min(MAX_WINDOW, s)
    first_keep = (s - keep) // tile
    row = lambda w: pl.BlockSpec((1, tile, w), lambda bi, i: (bi, i, 0))
    memb = pl.BlockSpec((1, N_MEM, M_WIDTH), lambda bi, i: (bi, 0, 0))
    keepb = pl.BlockSpec((1, tile, A_WIDTH), lambda bi, i: (bi, jnp.maximum(i - first_keep, 0), 0))
    cls = pl.BlockSpec((1, DIL16, tile // DIL16, P_W), lambda bi, i: (bi, 0, i, 0))
    return pl.pallas_call(
        functools.partial(_prompt_proj_kernel, t_s=t_s, win_len=win_len),
        grid=(b, nt),
        in_specs=[row(D_MODEL), _full(npre), _resident(win), _full(rin), _full(rbase),
                  memb, memb, _full(wdw), _full(bdw), _full(lng), _full(lnb), _full(wpw), _full(bpw),
                  _full(kvt), per_batch],
        out_specs=[row(P_W), cls, keepb, keepb, row(A_WIDTH), row(C_WIDTH + M_WIDTH),
                   pl.BlockSpec((1, CONV_HIST, C_WIDTH), lambda bi, i: (bi, 0, 0)), per_batch],
        out_shape=[jax.ShapeDtypeStruct((b, s, P_W), BF16),
                   jax.ShapeDtypeStruct((b, DIL16, s // DIL16, P_W), BF16),
                   jax.ShapeDtypeStruct((b, keep, A_WIDTH), F32),
                   jax.ShapeDtypeStruct((b, keep, A_WIDTH), F32),
                   jax.ShapeDtypeStruct((b, s, A_WIDTH), F32),
                   jax.ShapeDtypeStruct((b, s, C_WIDTH + M_WIDTH), BF16),
                   jax.ShapeDtypeStruct((b, CONV_HIST, C_WIDTH), F32),
                   jax.ShapeDtypeStruct(cv.shape, F32)],
        scratch_shapes=[pltpu.VMEM((CONV_HIST + tile, C_WIDTH), F32),
                        pltpu.VMEM((7, CONV_HIST + tile, C_WIDTH), F32),
                        pltpu.VMEM((tile, C_WIDTH), F32),
                        pltpu.VMEM((3 * NGRP, tile, LANES), F32),
                        pltpu.VMEM((3 * NGRP, tile, LANES), F32)],
        compiler_params=pltpu.CompilerParams(dimension_semantics=("arbitrary", "arbitrary"),
                                             vmem_limit_bytes=VMEM_LIMIT),
        name="prompt_proj",
    )(x, npre, win, rin, rbase, mkb, mvb, wdw, bdw, lng, lnb, wpw, bpw, kvt, cv)


STAT_L = 8
O16_W = A_WIDTH + LANES


CLS_PER_STEP = 8


def _dilated16_kernel(kvq_ref, o_ref):
    a = lax.broadcasted_iota(jnp.int32, (QB, QB), 0)
    c = lax.broadcasted_iota(jnp.int32, (QB, QB), 1)
    tri_prev = jnp.where(c >= a, 0.0, NEG)
    tri_cur = jnp.where(c <= a, 0.0, NEG)
    bias_cur = jnp.concatenate([tri_cur, tri_cur], axis=0)
    both = jnp.concatenate([tri_prev, tri_cur], axis=1)
    bias_both = jnp.concatenate([both, both], axis=0)
    lane = lax.broadcasted_iota(jnp.int32, (QB, LANES), 1)
    ncls, nsub = kvq_ref.shape[1], kvq_ref.shape[2] // QB
    units = [(cl, sub, g) for cl in range(ncls) for sub in range(nsub) for g in range(NGRP)]
    scores, probs = {}, {}
    stats = {(cl, sub): jnp.zeros((QB, LANES), F32) for cl in range(ncls) for sub in range(nsub)}

    def rows_of(sub):
        return slice(sub * QB, (sub + 1) * QB)

    def stage_scores(u):
        cl, sub, g = u
        kc = slice(P_K + g * LANES, P_K + (g + 1) * LANES)
        qc = slice(P_Q + g * LANES, P_Q + (g + 1) * LANES)
        q2 = _pair_split(kvq_ref[0, cl, rows_of(sub), qc].astype(F32)).astype(BF16)
        s_cur = _dot_t(q2, kvq_ref[0, cl, rows_of(sub), kc])
        if sub == 0:
            scores[u] = s_cur + bias_cur
        else:
            scores[u] = jnp.concatenate([_dot_t(q2, kvq_ref[0, cl, rows_of(sub - 1), kc]), s_cur], axis=1) + bias_both

    def stage_softmax(u):
        s = scores.pop(u)
        m = jnp.max(s, axis=-1, keepdims=True)
        p = jnp.exp(s - m)
        probs[u] = (m, jnp.sum(p, axis=-1, keepdims=True), p.astype(BF16))

    def stage_values(u):
        cl, sub, g = u
        vc = slice(P_V + g * LANES, P_V + (g + 1) * LANES)
        m, l, pb = probs.pop(u)
        if sub == 0:
            o = _dot(pb, kvq_ref[0, cl, rows_of(sub), vc])
        else:
            o = (_dot(pb[:, :QB], kvq_ref[0, cl, rows_of(sub - 1), vc])
                 + _dot(pb[:, QB:], kvq_ref[0, cl, rows_of(sub), vc]))
        o_ref[0, cl, rows_of(sub), g * LANES:(g + 1) * LANES] = _pair_join(o)
        st = stats[cl, sub]
        st = jnp.where(lane == 2 * g, m[:QB], st)
        st = jnp.where(lane == 2 * g + 1, m[QB:], st)
        st = jnp.where(lane == STAT_L + 2 * g, l[:QB], st)
        stats[cl, sub] = jnp.where(lane == STAT_L + 2 * g + 1, l[QB:], st)
        if g == NGRP - 1:
            o_ref[0, cl, rows_of(sub), A_WIDTH:] = stats.pop((cl, sub))

    stages = (stage_scores, stage_softmax, stage_values)
    for t in range(len(units) + len(stages) - 1):
        for depth, stage in enumerate(stages):
            if 0 <= t - depth < len(units):
                stage(units[t - depth])


def _dilated16(kvq16):
    b, ncls, nj, w = kvq16.shape
    return pl.pallas_call(
        _dilated16_kernel,
        grid=(b, ncls // CLS_PER_STEP),
        in_specs=[pl.BlockSpec((1, CLS_PER_STEP, nj, w), lambda bi, c: (bi, c, 0, 0))],
        out_specs=pl.BlockSpec((1, CLS_PER_STEP, nj, O16_W), lambda bi, c: (bi, c, 0, 0)),
        out_shape=jax.ShapeDtypeStruct((b, ncls, nj, O16_W), F32),
        compiler_params=pltpu.CompilerParams(dimension_semantics=("arbitrary",) * 2,
                                             vmem_limit_bytes=VMEM_LIMIT),
        name="dilated16",
    )(kvq16)


STATE_PAD = 2
S_Q, S_U, S_QM, S_ROW = 0, A_WIDTH, A_WIDTH + C_WIDTH, A_WIDTH + C_WIDTH + M_WIDTH


def _new_rows(batch, kvt_ref, lo, hi, t_s):
    per_slab = LANES // t_s
    slab, pos = batch // per_slab, batch % per_slab
    return pltpu.roll(kvt_ref[slab, lo:hi, :], LANES - t_s - pos * t_s, 1)


def _slide_cache(batch, kvt_ref, src_ref, lo, dst_ref, *, t_s, win_len):
    new0 = LANES - t_s
    keep = lax.broadcasted_iota(jnp.int32, (A_WIDTH, LANES), 1) < new0
    nch = win_len // LANES
    nxt = pltpu.roll(src_ref[0, :, 0:LANES], new0, 1)
    for c in range(nch):
        cur = nxt
        nxt = (pltpu.roll(src_ref[0, :, (c + 1) * LANES:(c + 2) * LANES], new0, 1) if c + 1 < nch
               else _new_rows(batch, kvt_ref, lo, lo + A_WIDTH, t_s))
        dst_ref[0, :, c * LANES:(c + 1) * LANES] = jnp.where(keep, cur, nxt)


def _sample_mix_stages(batch, srow_ref, kvt_ref, ck_ref, cv_ref, st_ref, cmk_ref, cmv_ref, wdw_ref,
                       smix_ref, wk_ref, ns_ref, uc_ref, *, t_s, win_len):
    new_rows = functools.partial(_new_rows, batch, kvt_ref, t_s=t_s)
    return [functools.partial(_slide_cache, batch, kvt_ref, ck_ref, 0, wk_ref, t_s=t_s, win_len=win_len),
            functools.partial(_sample_attend, new_rows, srow_ref, ck_ref, cv_ref, smix_ref, t_s=t_s, win_len=win_len),
            functools.partial(_sample_mem_conv, srow_ref, st_ref, cmk_ref, cmv_ref, wdw_ref, smix_ref, ns_ref,
                              uc_ref, t_s=t_s)]


def _sample_attend(new_rows, srow_ref, ck_ref, cv_ref, smix_ref, *, t_s, win_len):
    new0 = LANES - t_s
    knt, vnt = new_rows(0, A_WIDTH), new_rows(A_WIDTH, 2 * A_WIDTH)
    q = srow_ref[:, S_Q:S_U]
    head_of_lane = lax.broadcasted_iota(jnp.int32, (t_s, A_WIDTH), 1) >> HEAD_SHIFT
    q6 = jnp.concatenate([jnp.where(head_of_lane == h, q, 0.0) for h in range(A_HEADS)], axis=0).astype(BF16)
    s_c = _dot(q6, ck_ref[0].astype(BF16))
    s_n = _dot(q6, knt.astype(BF16))

    def weights(shape, key0, lo):
        tq = lax.broadcasted_iota(jnp.int32, shape, 0) & (t_s - 1)
        key = lax.broadcasted_iota(jnp.int32, shape, 1) + key0
        d = win_len + tq - key
        ok = (d >= 0) & (key >= lo)
        w = ((d <= 128).astype(F32) + ((d <= 512) & ((d & 3) == 0)).astype(F32)
             + ((d <= 2048) & ((d & 15) == 0)).astype(F32))
        return jnp.where(ok, w, 0.0)

    w_c = weights(s_c.shape, 0, 0)
    w_n = weights(s_n.shape, win_len - new0, win_len)
    s_c = jnp.where(w_c > 0.0, s_c, NEG)
    s_n = jnp.where(w_n > 0.0, s_n, NEG)
    m = jnp.maximum(jnp.max(s_c, axis=-1, keepdims=True), jnp.max(s_n, axis=-1, keepdims=True))
    p_c = jnp.exp(s_c - m) * w_c
    p_n = jnp.exp(s_n - m) * w_n
    l = jnp.sum(p_c, axis=-1, keepdims=True) + jnp.sum(p_n, axis=-1, keepdims=True)
    o = (_dot_t(p_c.astype(BF16), cv_ref[0].astype(BF16)) + _dot_t(p_n.astype(BF16), vnt.astype(BF16))) * (1.0 / l)
    oa = jnp.zeros((t_s, A_WIDTH), F32)
    for h in range(A_HEADS):
        oa = oa + jnp.where(head_of_lane == h, o[h * t_s:(h + 1) * t_s], 0.0)
    smix_ref[:, S_Q:S_U] = oa


def _sample_mem_conv(srow_ref, st_ref, cmk_ref, cmv_ref, wdw_ref, smix_ref, ns_ref, uc_ref, *, t_s):
    qm = srow_ref[:, S_QM:S_ROW] * SCALE
    mhead = lax.broadcasted_iota(jnp.int32, (t_s, M_WIDTH), 1) >> HEAD_SHIFT
    q4 = jnp.concatenate([jnp.where(mhead == h, qm, 0.0) for h in range(M_HEADS)], axis=0).astype(BF16)
    sm = _dot(q4, cmk_ref[0].astype(BF16))
    mm = jnp.max(sm, axis=-1, keepdims=True)
    pm = jnp.exp(sm - mm)
    lm = jnp.sum(pm, axis=-1, keepdims=True)
    om4 = _dot_t(pm.astype(BF16), cmv_ref[0].astype(BF16)) * (1.0 / lm)
    om = jnp.zeros((t_s, M_WIDTH), F32)
    for h in range(M_HEADS):
        om = om + jnp.where(mhead == h, om4[h * t_s:(h + 1) * t_s], 0.0)
    smix_ref[:, S_QM:S_ROW] = om

    nst = st_ref.shape[1]
    uc_ref[0:nst, :] = st_ref[0]
    uc_ref[nst:nst + t_s, :] = srow_ref[:, S_U:S_QM]
    acc = jnp.zeros((t_s, C_WIDTH), F32)
    for w in range(CONV_W):
        acc = acc + uc_ref[STATE_PAD + w:STATE_PAD + w + t_s, :] * wdw_ref[w:w + 1, :]
    smix_ref[:, S_U:S_QM] = acc
    ns_ref[0] = uc_ref[STATE_PAD + t_s:STATE_PAD + t_s + CONV_W - 1, :]


BACK = 512
NFAR = BACK // QB - 1

def _natural_offset(p):
    p = p & (QB - 1)
    return DIL4 * (p & (CLS4 - 1)) + (p >> CLS4_SHIFT)


def _prompt_attn_kernel(q_ref, kvc_ref, o16_ref, ga_ref, mix_ref, x_ref, wout_ref, npost_ref,
                        srow_ref, kvt_ref, ck_ref, cv_ref, sst_ref, cmk_ref, cmv_ref, wdw_ref,
                        y_ref, smix_ref, wk_ref, ns_ref,
                        mixa_ref, nat_ref, nat4_ref, unp_ref, uc_ref, kvp_ref, woutbf_ref, *, tq, t_s, win_len):
    i = pl.program_id(1)
    kv_refs = (kvp_ref, kvc_ref)

    @pl.when((pl.program_id(0) == 0) & (i == 0))
    def _():
        woutbf_ref[...] = wout_ref[...].astype(BF16)

    @pl.when(i == 0)
    def _():
        kvp_ref[...] = jnp.zeros(kvp_ref.shape, BF16)

    quarter = tq // DIL4
    for cl in range(DIL16):
        c4, c2 = cl % DIL4, cl // DIL4
        dst = pl.ds(c4 * quarter + c2, tq // DIL16, stride=DIL4)
        for g in range(NGRP + 1):
            nat4_ref[g, dst, :] = o16_ref[0, cl, :, g * LANES:(g + 1) * LANES]
    for c4 in range(DIL4):
        for g in range(NGRP + 1):
            nat_ref[g, pl.ds(c4, quarter, stride=DIL4), :] = nat4_ref[g, c4 * quarter:(c4 + 1) * quarter, :]

    a_n = _natural_offset(lax.broadcasted_iota(jnp.int32, (QB, 2 * QB), 0))
    c_p = lax.broadcasted_iota(jnp.int32, (QB, 2 * QB), 1)
    d = jnp.where(c_p < QB, QB, 0) + a_n - _natural_offset(c_p)
    near = (d >= 0) & (d <= QB)
    far = (d >= 0) & ((d & (DIL4 - 1)) == 0)
    bias_near0 = jnp.where(near & far, LN2, jnp.where(near | far, 0.0, NEG))
    ia = lax.broadcasted_iota(jnp.int32, (CLS4, QB), 0)
    cf = lax.broadcasted_iota(jnp.int32, (CLS4, QB), 1)
    fblk, ic = cf >> CLS4_SHIFT, cf & (CLS4 - 1)
    bias_far0 = jnp.where((fblk < NFAR) & ((fblk > 0) | (ic >= ia)), 0.0, NEG)
    lane = lax.broadcasted_iota(jnp.int32, (QB, LANES), 1)
    head_a = lane < HEAD_DIM

    def rows_of(w):
        blk, off = divmod(w, tq)
        return kv_refs[blk], off

    def group(x, c):
        return jnp.concatenate([x[c * CLS4:(c + 1) * CLS4], x[QB + c * CLS4:QB + (c + 1) * CLS4]], axis=0)

    def ungroup(parts):
        return jnp.concatenate([p_[:CLS4] for p_ in parts] + [p_[CLS4:] for p_ in parts], axis=0)

    units = [(sub, g) for sub in range(tq // QB) for g in range(NGRP)]
    ctx = {}
    for sub in range(tq // QB):
        first_valid = BACK - i * tq - sub * QB
        bias_near = jnp.where((c_p >= QB) | (NFAR * QB >= first_valid), bias_near0, NEG)
        bias_far = jnp.where(fblk * QB >= first_valid, bias_far0, NEG)
        far_blocks = [rows_of((sub + j) * QB) for j in range(NFAR)]
        ctx[sub] = dict(bias_near=jnp.concatenate([bias_near, bias_near], axis=0),
                        bias_far=jnp.concatenate([bias_far, bias_far], axis=0),
                        prev=rows_of((sub + NFAR) * QB), cur=rows_of((sub + NFAR + 1) * QB),
                        far=far_blocks + far_blocks[:1])

    def far_rows(sub, c, colsl):
        return jnp.concatenate([r[0, off + c * CLS4:off + (c + 1) * CLS4, colsl] for r, off in ctx[sub]["far"]],
                               axis=0)

    def block(sub, which, colsl):
        r, off = ctx[sub][which]
        return r[0, off:off + QB, colsl]

    scores, probs, outs = {}, {}, {}

    def stage_scores(u):
        sub, g = u
        kc = slice(P_K + g * LANES, P_K + (g + 1) * LANES)
        q2 = _pair_split(q_ref[0, sub * QB:(sub + 1) * QB, g * LANES:(g + 1) * LANES].astype(F32)).astype(BF16)
        s_far = ungroup([_dot_t(group(q2, c), far_rows(sub, c, kc)) + ctx[sub]["bias_far"] for c in range(DIL4)])
        s_near = jnp.concatenate([_dot_t(q2, block(sub, "prev", kc)), _dot_t(q2, block(sub, "cur", kc))],
                                 axis=1) + ctx[sub]["bias_near"]
        scores[u] = jnp.concatenate([s_far, s_near], axis=1)

    def stage_softmax(u):
        s = scores.pop(u)
        m = jnp.max(s, axis=-1, keepdims=True)
        p = jnp.exp(s - m)
        probs[u] = (m, jnp.sum(p, axis=-1, keepdims=True), p.astype(BF16))

    def stage_values(u):
        sub, g = u
        vc = slice(P_V + g * LANES, P_V + (g + 1) * LANES)
        pb = probs[u][2]
        outs[u] = (_dot(pb[:, QB:2 * QB], block(sub, "prev", vc)) + _dot(pb[:, 2 * QB:], block(sub, "cur", vc))
                   + ungroup([_dot(group(pb[:, :QB], c), far_rows(sub, c, vc)) for c in range(DIL4)]))

    def stage_merge(u):
        sub, g = u
        rows = slice(sub * QB, (sub + 1) * QB)
        cols = slice(g * LANES, (g + 1) * LANES)
        m, l, _ = probs.pop(u)
        o_s = _pair_join(outs.pop(u))
        m_s = jnp.where(head_a, m[:QB], m[QB:])
        l_s = jnp.where(head_a, l[:QB], l[QB:])
        for c in range(DIL4):
            dst = pl.ds(c, CLS4, stride=DIL4)
            unp_ref[3 * g, dst, :] = o_s[c * CLS4:(c + 1) * CLS4]
            unp_ref[3 * g + 1, dst, :] = m_s[c * CLS4:(c + 1) * CLS4]
            unp_ref[3 * g + 2, dst, :] = l_s[c * CLS4:(c + 1) * CLS4]
        o_n, m_n, l_n = unp_ref[3 * g], unp_ref[3 * g + 1], unp_ref[3 * g + 2]
        st = nat_ref[NGRP, rows, :]
        m_f = jnp.where(head_a, st[:, 2 * g:2 * g + 1], st[:, 2 * g + 1:2 * g + 2])
        l_f = jnp.where(head_a, st[:, STAT_L + 2 * g:STAT_L + 2 * g + 1],
                        st[:, STAT_L + 2 * g + 1:STAT_L + 2 * g + 2])
        mx = jnp.maximum(m_n, m_f)
        w_n = jnp.exp(m_n - mx)
        w_f = jnp.exp(m_f - mx)
        oa = (o_n * w_n + nat_ref[g, rows, :] * w_f) / (l_n * w_n + l_f * w_f)
        mixa_ref[rows, cols] = (oa * ga_ref[0, rows, cols]).astype(BF16)

    def finish(rows):
        z = (_dot(mixa_ref[rows, :], woutbf_ref[0:A_WIDTH, :])
             + _dot(mix_ref[0, rows, :], woutbf_ref[A_WIDTH:, :]))
        y_ref[0, rows, :] = x_ref[0, rows, :] + _rmsnorm(z, npost_ref[...])

    stages = (stage_scores, stage_softmax, stage_values, stage_merge)
    half = len(units) // 2
    for t in range(len(units) + len(stages) - 1):
        for depth, stage in enumerate(stages):
            if 0 <= t - depth < len(units):
                stage(units[t - depth])
        if t - (len(stages) - 1) == half - 1:
            finish(slice(0, tq // 2))
    finish(slice(tq // 2, tq))
    kvp_ref[...] = kvc_ref[...]

    for part in _sample_mix_stages(pl.program_id(0) * pl.num_programs(1) + i, srow_ref, kvt_ref, ck_ref, cv_ref,
                                   sst_ref, cmk_ref, cmv_ref, wdw_ref, smix_ref, wk_ref, ns_ref, uc_ref,
                                   t_s=t_s, win_len=win_len):
        part()


def _prompt_attn(kvq, o16, ga, mix, x, wout, npost, srow, kvt, ck, cv, st_pad, cmk, cmv, wdw, tq, t_s):
    b, s, _ = x.shape
    nsteps = s // tq
    nb, _, win_len = ck.shape
    assert tq == BACK and b * nsteps == nb, "one sample batch per grid step"
    row = lambda w: pl.BlockSpec((1, tq, w), lambda bi, i: (bi, i, 0))
    qspec = pl.BlockSpec((1, tq, A_WIDTH), lambda bi, i: (bi, i, P_Q // A_WIDTH))
    kvcur = pl.BlockSpec((1, tq, 2 * A_WIDTH), lambda bi, i: (bi, i, 0))
    cls = pl.BlockSpec((1, DIL16, tq // DIL16, O16_W), lambda bi, i: (bi, 0, i, 0))
    srows = pl.BlockSpec((t_s, S_ROW), lambda bi, i: (bi * nsteps + i, 0))
    per = lambda a: pl.BlockSpec((1,) + a.shape[1:], lambda bi, i: (bi * nsteps + i, 0, 0))
    return pl.pallas_call(
        functools.partial(_prompt_attn_kernel, tq=tq, t_s=t_s, win_len=win_len),
        grid=(b, nsteps),
        in_specs=[qspec, kvcur, cls, row(A_WIDTH), row(C_WIDTH + M_WIDTH), row(D_MODEL),
                  _resident(wout), _full(npost),
                  srows, _full(kvt), per(ck), per(cv), per(st_pad), per(cmk), per(cmv), _full(wdw)],
        out_specs=[row(D_MODEL), srows, per(ck),
                   pl.BlockSpec((1, CONV_W - 1, C_WIDTH), lambda bi, i: (bi * nsteps + i, 0, 0))],
        out_shape=[jax.ShapeDtypeStruct((b, s, D_MODEL), F32),
                   jax.ShapeDtypeStruct((nb * t_s, S_ROW), F32),
                   jax.ShapeDtypeStruct(ck.shape, F32),
                   jax.ShapeDtypeStruct((nb, CONV_W - 1, C_WIDTH), F32)],
        scratch_shapes=[pltpu.VMEM((tq, A_WIDTH), BF16),
                        pltpu.VMEM((NGRP + 1, tq, LANES), F32),
                        pltpu.VMEM((NGRP + 1, tq, LANES), F32),
                        pltpu.VMEM((3 * NGRP, QB, LANES), F32),
                        pltpu.VMEM((STATE_PAD + CONV_W - 1 + t_s, C_WIDTH), F32),
                        pltpu.VMEM((1, tq, 2 * A_WIDTH), BF16),
                        pltpu.VMEM(wout.shape, BF16)],
        compiler_params=pltpu.CompilerParams(dimension_semantics=("arbitrary", "arbitrary"),
                                             vmem_limit_bytes=VMEM_LIMIT),
        name="prompt_attn",
    )(kvq, kvq, o16, ga, mix, x, wout, npost, srow, kvt, ck, cv, st_pad, cmk, cmv, wdw)


def _sample_proj_kernel(x_ref, npre_ref, win_ref, ang_ref, srow_ref, kvt_ref, gate_ref, wbf_ref):
    h = _rmsnorm(x_ref[...], npre_ref[...]).astype(BF16)
    wbf_ref[...] = win_ref[...].astype(BF16)
    proj = _dot(h, wbf_ref[...])
    pats = _rope_patterns(ang_ref[:, 0:LANES], ang_ref[:, LANES:])
    srow_ref[:, S_Q:S_U] = _rope(proj[:, O_QA:O_KA], pats) * SCALE
    srow_ref[:, S_U:S_QM] = proj[:, O_AB:O_BB] * _sigmoid(proj[:, O_BB:O_GB])
    srow_ref[:, S_QM:S_ROW] = proj[:, O_QM:O_GM]
    kv_t = jnp.concatenate([_rope(proj[:, O_KA:O_VA], pats), proj[:, O_VA:O_GA]], axis=1).T
    for slab in range(kvt_ref.shape[0]):
        kvt_ref[slab] = kv_t[:, slab * LANES:(slab + 1) * LANES]
    gate_ref[:, S_Q:S_U] = _silu(proj[:, O_GA:O_AB])
    gate_ref[:, S_U:S_QM] = _silu(proj[:, O_GB:O_QM])
    gate_ref[:, S_QM:S_ROW] = _silu(proj[:, O_GM:D_IN])


def _sample_proj(x, npre, win, ang):
    n = x.shape[0]
    assert n % LANES == 0
    shapes = [(n, S_ROW), (n // LANES, 2 * A_WIDTH, LANES), (n, S_ROW), win.shape]
    dtypes = [F32, F32, F32, BF16]
    return pl.pallas_call(
        _sample_proj_kernel,
        grid=(1,),
        in_specs=[_full(x), _full(npre), _full(win), _full(ang)],
        out_specs=[pl.BlockSpec(sh, lambda i, nd=len(sh): (0,) * nd) for sh in shapes],
        out_shape=[jax.ShapeDtypeStruct(sh, dt) for sh, dt in zip(shapes, dtypes)],
        compiler_params=pltpu.CompilerParams(vmem_limit_bytes=VMEM_LIMIT),
        name="sample_proj",
    )(x, npre, win, ang)


def _sample_out_kernel(x_ref, smix_ref, gate_ref, bdw_ref, lng_ref, lnb_ref, wpw_ref, bpw_ref, wout_ref, npost_ref,
                       y_ref):
    mixed_a = (smix_ref[:, S_Q:S_U] * gate_ref[:, S_Q:S_U]).astype(BF16)
    mixed_b = _conformer_tail(smix_ref[:, S_U:S_QM], gate_ref[:, S_U:S_QM], bdw_ref[...], lng_ref[...], lnb_ref[...],
                              wpw_ref, bpw_ref[...]).astype(BF16)
    mixed_m = (smix_ref[:, S_QM:S_ROW] * gate_ref[:, S_QM:S_ROW]).astype(BF16)
    wout = wout_ref[...].astype(BF16)
    z = (_dot(mixed_a, wout[0:A_WIDTH]) + _dot(mixed_b, wout[A_WIDTH:A_WIDTH + C_WIDTH])
         + _dot(mixed_m, wout[A_WIDTH + C_WIDTH:]))
    y_ref[...] = x_ref[...] + _rmsnorm(z, npost_ref[...])


def _sample_out(x, smix, gate, bdw, lng, lnb, wpw, bpw, wout, npost):
    args = (x, smix, gate, bdw, lng, lnb, wpw, bpw, wout, npost)
    return pl.pallas_call(
        _sample_out_kernel,
        grid=(1,),
        in_specs=[_full(a) for a in args],
        out_specs=_full(x),
        out_shape=jax.ShapeDtypeStruct(x.shape, F32),
        compiler_params=pltpu.CompilerParams(vmem_limit_bytes=VMEM_LIMIT),
        name="sample_out",
    )(*args)


def _rope_cos_sin(pos):
    inv = ROPE_THETA ** (-jnp.arange(0, ROT_DIM, 2, dtype=F32) / ROT_DIM)
    ang = pos.astype(F32)[:, None] * jnp.tile(inv, 2 * LANES // ROT_DIM)[None, :]
    return jnp.concatenate([jnp.cos(ang), jnp.sin(ang)], axis=1)


def _feature_major(cache):
    nb, rows, heads, dim = cache.shape
    return jnp.transpose(cache, (0, 2, 3, 1)).reshape(nb, heads * dim, rows)


def _row_major(cache_t, heads):
    nb, width, rows = cache_t.shape
    return jnp.transpose(cache_t.reshape(nb, heads, width // heads, rows), (0, 3, 1, 2))


PROJ_TILE = 512
ATTN_TILE = BACK


def kernel(x_prompt, x_sample, cache_win_k, cache_win_v, state_conv, cache_mem_k, cache_mem_v, mem_prompt,
           norm_pre, norm_post, w_in, w_out, norm_mem, w_mem_kv, w_dw, b_dw, ln_conv_g, ln_conv_b, w_pw2, b_pw2):
    depth = w_in.shape[0]
    assert depth == 1, "single-layer step"
    b, s, _ = x_prompt.shape
    nb, t_s, _ = x_sample.shape
    win_len = cache_win_k.shape[2]
    assert s % (DIL16 * QB) == 0 and s % PROJ_TILE == 0 and s % ATTN_TILE == 0
    assert win_len == MAX_WINDOW and win_len % LANES == 0
    assert t_s % 8 == 0 and t_s & (t_s - 1) == 0 and t_s < LANES
    l = 0
    row = lambda a: a[l][None, :]
    npre, npost, nmem = row(norm_pre), row(norm_post), row(norm_mem)
    bdw, lng, lnb, bpw = row(b_dw), row(ln_conv_g), row(ln_conv_b), row(b_pw2)
    win, wout, wmem, wpw = w_in[l], w_out[l], w_mem_kv[l], w_pw2[l]
    wdw = w_dw[l]

    pos_s = PAST_LEN + jnp.arange(t_s, dtype=jnp.int32)
    xs = x_sample.reshape(nb * t_s, D_MODEL)
    srow, kvt, gate_s, win_bf = _sample_proj(xs, npre, win, jnp.tile(_rope_cos_sin(pos_s), (nb, 1)))
    st_pad = jnp.pad(state_conv[l], ((0, 0), (STATE_PAD, 0), (0, 0)))

    mk, mv, mkb, mvb = _mem_kv(mem_prompt, nmem, wmem)
    rin = _rope_cos_sin(jnp.arange(PROJ_TILE, dtype=jnp.int32))
    rbase = _rope_cos_sin(jnp.arange(0, s, PROJ_TILE, dtype=jnp.int32))[:, None, :]
    ck, cv = _feature_major(cache_win_k[l]), _feature_major(cache_win_v[l])
    kvq, kvq16, kf, vf, ga, mix, ust, wv = _prompt_proj(x_prompt, npre, win_bf, rin, rbase, mkb, mvb, wdw, bdw,
                                                        lng, lnb, wpw, bpw, kvt, cv, PROJ_TILE, t_s)
    o16 = _dilated16(kvq16)
    y_prompt, smix, wk, nst = _prompt_attn(
        kvq, o16, ga, mix, x_prompt, wout, npost, srow, kvt, ck, cv, st_pad,
        _feature_major(cache_mem_k[l]), _feature_major(cache_mem_v[l]), wdw, ATTN_TILE, t_s)
    y_sample = _sample_out(xs, smix, gate_s, bdw, lng, lnb, wpw, bpw, wout, npost)

    keep_p = kf.shape[1]
    return (y_prompt,
            y_sample.reshape(nb, t_s, D_MODEL),
            kf.reshape(1, b, keep_p, A_HEADS, HEAD_DIM),
            vf.reshape(1, b, keep_p, A_HEADS, HEAD_DIM),
            ust[:, CONV_HIST - (CONV_W - 1):, :][None],
            mk.reshape(1, b, N_MEM, M_HEADS, HEAD_DIM),
            mv.reshape(1, b, N_MEM, M_HEADS, HEAD_DIM),
            _row_major(wk, A_HEADS)[None],
            _row_major(wv, A_HEADS)[None],
            nst[None])
```

```python
import functools
import math

import jax
import jax.numpy as jnp
from jax import lax
from jax.experimental import pallas as pl
from jax.experimental.pallas import tpu as pltpu

F32 = jnp.float32
BF16 = jnp.bfloat16

D_MODEL = 1024
HEAD_DIM = 64
HEAD_SHIFT = 6
A_WIDTH = 384
A_HEADS = 6
M_WIDTH = 256
M_HEADS = 4
C_WIDTH = 384
ROT_DIM = 16
ROPE_THETA = 500000.0
CONV_W = 31
N_MEM = 256
MAX_WINDOW = 2048
PAST_LEN = 16384
EPS = 1e-6
SCALE = HEAD_DIM ** -0.5
NEG = -1e30
LN2 = math.log(2.0)

LANES = 128
NGRP = A_WIDTH // LANES
QB = 128
DIL16 = 16
DIL4 = 4
CLS4 = QB // DIL4
CLS4_SHIFT = 5
O_QA, O_KA, O_VA, O_GA, O_AB, O_BB, O_GB, O_QM, O_GM, D_IN = 0, 384, 768, 1152, 1536, 1920, 2304, 2688, 2944, 3200
P_K, P_V, P_Q, P_W = 0, A_WIDTH, 2 * A_WIDTH, 3 * A_WIDTH

VMEM_LIMIT = 60 * 1024 * 1024


def _sigmoid(x):
    return 1.0 / (1.0 + jnp.exp(-x))


def _silu(x):
    return x * _sigmoid(x)


def _rmsnorm(x, g):
    return x * lax.rsqrt(jnp.mean(x * x, axis=-1, keepdims=True) + EPS) * g


def _dot(a, b):
    return jnp.dot(a, b, preferred_element_type=F32)


def _dot_t(a, b):
    return lax.dot_general(a, b, (((1,), (1,)), ((), ())), preferred_element_type=F32)


def _rope_patterns(cos, sin):
    in_head = lax.broadcasted_iota(jnp.int32, cos.shape, 1) & (HEAD_DIM - 1)
    rot, lo = in_head < ROT_DIM, in_head < ROT_DIM // 2
    return jnp.where(rot, cos, 1.0), jnp.where(lo, -sin, 0.0), jnp.where(rot & ~lo, sin, 0.0)


def _rope(xw, pats):
    cos, s1, s2 = pats
    outs = []
    for g in range(xw.shape[1] // LANES):
        xg = xw[:, g * LANES:(g + 1) * LANES]
        outs.append(xg * cos + pltpu.roll(xg, LANES - 8, 1) * s1 + pltpu.roll(xg, 8, 1) * s2)
    return jnp.concatenate(outs, axis=1)


def _pair_split(q):
    lane = lax.broadcasted_iota(jnp.int32, q.shape, 1)
    qa = jnp.where(lane < HEAD_DIM, q, 0.0)
    qb = jnp.where(lane >= HEAD_DIM, q, 0.0)
    return jnp.concatenate([qa, qb], axis=0)


def _pair_join(x):
    t = x.shape[0] // 2
    lane = lax.broadcasted_iota(jnp.int32, (t, LANES), 1)
    return jnp.where(lane < HEAD_DIM, x[:t], x[t:])


def _mem_attend(qm, mk_ref, mv_ref):
    outs = []
    for g in range(M_WIDTH // LANES):
        cols = slice(g * LANES, (g + 1) * LANES)
        q2 = _pair_split(qm[:, cols] * SCALE).astype(BF16)
        s = _dot_t(q2, mk_ref[0, :, cols])
        m = jnp.max(s, axis=-1, keepdims=True)
        p = jnp.exp(s - m)
        l = jnp.sum(p, axis=-1, keepdims=True)
        o = _dot(p.astype(BF16), mv_ref[0, :, cols]) * (1.0 / l)
        outs.append(_pair_join(o))
    return jnp.concatenate(outs, axis=1)


def _conformer_tail(c, gate_b, bdw, lng, lnb, wpw_ref, bpw):
    cf = c + bdw
    mu = jnp.mean(cf, axis=-1, keepdims=True)
    dev = cf - mu
    var = jnp.mean(dev * dev, axis=-1, keepdims=True)
    cn = dev * lax.rsqrt(var + EPS) * lng + lnb
    ob = _dot(_silu(cn).astype(BF16), wpw_ref[...].astype(BF16)) + bpw
    return ob * gate_b


def _full(a):
    return pl.BlockSpec(a.shape, lambda *_: (0,) * a.ndim)


def _resident(a):
    return pl.BlockSpec(a.shape, lambda *_: (0,) * a.ndim, pipeline_mode=pl.Buffered(1))


def _mem_kv_kernel(mem_ref, g_ref, w_ref, mk_ref, mv_ref, mkb_ref, mvb_ref):
    h = _rmsnorm(mem_ref[0], g_ref[...]).astype(BF16)
    kv = _dot(h, w_ref[...].astype(BF16))
    mk_ref[0] = kv[:, :M_WIDTH]
    mv_ref[0] = kv[:, M_WIDTH:]
    mkb_ref[0] = kv[:, :M_WIDTH].astype(BF16)
    mvb_ref[0] = kv[:, M_WIDTH:].astype(BF16)


def _mem_kv(mem, g, wmem):
    b = mem.shape[0]
    blk = pl.BlockSpec((1, N_MEM, M_WIDTH), lambda i: (i, 0, 0))
    return pl.pallas_call(
        _mem_kv_kernel,
        grid=(b,),
        in_specs=[pl.BlockSpec((1, N_MEM, D_MODEL), lambda i: (i, 0, 0)), _full(g), _full(wmem)],
        out_specs=[blk, blk, blk, blk],
        out_shape=[jax.ShapeDtypeStruct((b, N_MEM, M_WIDTH), F32)] * 2
        + [jax.ShapeDtypeStruct((b, N_MEM, M_WIDTH), BF16)] * 2,
        name="mem_kv",
    )(mem, g, wmem)


CONV_HIST = 32
CONV_CHUNK = 64


def _prompt_proj_kernel(x_ref, npre_ref, win_ref, rin_ref, rbase_ref, mk_ref, mv_ref,
                        wdw_ref, bdw_ref, lng_ref, lnb_ref, wpw_ref, bpw_ref, kvt_ref, cv_ref,
                        kvq_ref, kvq16_ref, kf_ref, vf_ref, ga_ref, mix_ref, ust_ref, wv_ref,
                        uext_ref, ush_ref, conv_ref, stage_ref, stage4_ref, *, t_s, win_len):
    t = x_ref.shape[1]
    i = pl.program_id(1)

    @pl.when(i == 0)
    def _():
        uext_ref[0:CONV_HIST, :] = jnp.zeros((CONV_HIST, C_WIDTH), F32)

    h = _rmsnorm(x_ref[0], npre_ref[...]).astype(BF16)
    proj = {}

    def project(name, c0, c1):
        def run():
            proj[name] = _dot(h, win_ref[:, c0:c1])
        return run

    def stage(idx, val):
        for g in range(NGRP):
            stage_ref[NGRP * idx + g] = val[:, g * LANES:(g + 1) * LANES]

    pats = []

    def rope_pats():
        if not pats:
            cr, sr = rin_ref[:, 0:LANES], rin_ref[:, LANES:]
            cb, sb = rbase_ref[i, :, 0:LANES], rbase_ref[i, :, LANES:]
            pats.append(_rope_patterns(cb * cr - sb * sr, sb * cr + cb * sr))
        return pats[0]

    def do_k():
        k = _rope(proj.pop("k"), rope_pats())
        kf_ref[0] = k
        stage(0, k)

    def do_q():
        stage(2, _rope(proj.pop("q"), rope_pats()) * SCALE)

    def do_v():
        v = proj.pop("v")
        vf_ref[0] = v
        stage(1, v)

    def do_ga():
        ga_ref[0] = _silu(proj.pop("ga"))

    def do_copies():
        quarter = t // DIL4
        for c4 in range(DIL4):
            for s in range(3 * NGRP):
                stage4_ref[s, c4 * quarter:(c4 + 1) * quarter, :] = stage_ref[s, pl.ds(c4, quarter, stride=DIL4), :]
        for c in range(DIL16):
            c4, c2 = c % DIL4, c // DIL4
            rows_c = [stage4_ref[s, pl.ds(c4 * quarter + c2, t // DIL16, stride=DIL4), :] for s in range(3 * NGRP)]
            kvq16_ref[0, c] = jnp.concatenate(rows_c, axis=1).astype(BF16)
        for blk in range(t // QB):
            for c4 in range(DIL4):
                src = slice(c4 * quarter + blk * CLS4, c4 * quarter + (blk + 1) * CLS4)
                rows_c = [stage4_ref[s, src, :] for s in range(3 * NGRP)]
                kvq_ref[0, blk * QB + c4 * CLS4:blk * QB + (c4 + 1) * CLS4, :] = (
                    jnp.concatenate(rows_c, axis=1).astype(BF16))

    def do_glu():
        uext_ref[CONV_HIST:CONV_HIST + t, :] = proj.pop("ab") * _sigmoid(proj.pop("bb"))
        uext = uext_ref[...]
        for r in range(1, 8):
            ush_ref[r - 1] = pltpu.roll(uext, r, 0)

    def conv_rows(r0):
        def run():
            acc = jnp.zeros((CONV_CHUNK, C_WIDTH), F32)
            for kk in range(CONV_W):
                tap = wdw_ref[CONV_W - 1 - kk:CONV_W - kk, :]
                r, base = kk % 8, CONV_HIST + r0 - (kk - kk % 8)
                if r == 0:
                    slab = uext_ref[base:base + CONV_CHUNK, :]
                else:
                    slab = ush_ref[r - 1, base:base + CONV_CHUNK, :]
                acc = acc + slab * tap
            conv_ref[r0:r0 + CONV_CHUNK, :] = acc
        return run

    def do_history():
        tail = uext_ref[t:t + CONV_HIST, :]
        uext_ref[0:CONV_HIST, :] = tail
        ust_ref[0] = tail

    def do_tail():
        mixed_b = _conformer_tail(conv_ref[...], _silu(proj.pop("gb")), bdw_ref[...], lng_ref[...], lnb_ref[...],
                                  wpw_ref, bpw_ref[...])
        mix_ref[0, :, 0:C_WIDTH] = mixed_b.astype(BF16)

    def do_mem():
        mixed_m = _mem_attend(proj.pop("qm"), mk_ref, mv_ref) * _silu(proj.pop("gm"))
        mix_ref[0, :, C_WIDTH:] = mixed_m.astype(BF16)

    convs = [conv_rows(r0) for r0 in range(0, t, CONV_CHUNK)]
    riders = [([project("k", O_KA, O_VA)], do_k), ([project("q", O_QA, O_KA)], do_q),
              ([project("v", O_VA, O_GA)], do_v), ([project("ga", O_GA, O_AB)], do_ga),
              ([project("qm", O_QM, O_GM), project("gm", O_GM, D_IN)], do_mem),
              ([project("gb", O_GB, O_QM)], do_copies)]
    slide_v = functools.partial(_slide_cache, pl.program_id(0) * pl.num_programs(1) + i, kvt_ref, cv_ref, A_WIDTH,
                                wv_ref, t_s=t_s, win_len=win_len)
    program = [project("ab", O_AB, O_BB), project("bb", O_BB, O_GB), slide_v, do_glu]
    for n, conv in enumerate(convs):
        if n < len(riders):
            program += riders[n][0]
        program.append(conv)
        if n < len(riders) and riders[n][1] is not None:
            program.append(riders[n][1])
    assert len(convs) >= len(riders)
    program += [do_history, do_tail]
    for piece in program:
        piece()
    assert not proj


def _prompt_proj(x, npre, win, rin, rbase, mkb, mvb, wdw, bdw, lng, lnb, wpw, bpw, kvt, cv, tile, t_s):
    b, s, _ = x.shape
    nt = s // tile
    nb, _, win_len = cv.shape
    assert b * nt == nb, "one sample batch per grid step"
    per_batch = pl.BlockSpec((1,) + cv.shape[1:], lambda bi, i: (bi * nt + i, 0, 0))
    keep = min(MAX_WINDOW, s)
    first_keep = (s - keep) // tile
    row = lambda w: pl.BlockSpec((1, tile, w), lambda bi, i: (bi, i, 0))
    memb = pl.BlockSpec((1, N_MEM, M_WIDTH), lambda bi, i: (bi, 0, 0))
    keepb = pl.BlockSpec((1, tile, A_WIDTH), lambda bi, i: (bi, jnp.maximum(i - first_keep, 0), 0))
    cls = pl.BlockSpec((1, DIL16, tile // DIL16, P_W), lambda bi, i: (bi, 0, i, 0))
    return pl.pallas_call(
        functools.partial(_prompt_proj_kernel, t_s=t_s, win_len=win_len),
        grid=(b, nt),
        in_specs=[row(D_MODEL), _full(npre), _resident(win), _full(rin), _full(rbase),
                  memb, memb, _full(wdw), _full(bdw), _full(lng), _full(lnb), _full(wpw), _full(bpw),
                  _full(kvt), per_batch],
        out_specs=[row(P_W), cls, keepb, keepb, row(A_WIDTH), row(C_WIDTH + M_WIDTH),
                   pl.BlockSpec((1, CONV_HIST, C_WIDTH), lambda bi, i: (bi, 0, 0)), per_batch],
        out_shape=[jax.ShapeDtypeStruct((b, s, P_W), BF16),
                   jax.ShapeDtypeStruct((b, DIL16, s // DIL16, P_W), BF16),
                   jax.ShapeDtypeStruct((b, keep, A_WIDTH), F32),
                   jax.ShapeDtypeStruct((b, keep, A_WIDTH), F32),
                   jax.ShapeDtypeStruct((b, s, A_WIDTH), F32),
                   jax.ShapeDtypeStruct((b, s, C_WIDTH + M_WIDTH), BF16),
                   jax.ShapeDtypeStruct((b, CONV_HIST, C_WIDTH), F32),
                   jax.ShapeDtypeStruct(cv.shape, F32)],
        scratch_shapes=[pltpu.VMEM((CONV_HIST + tile, C_WIDTH), F32),
                        pltpu.VMEM((7, CONV_HIST + tile, C_WIDTH), F32),
                        pltpu.VMEM((tile, C_WIDTH), F32),
                        pltpu.VMEM((3 * NGRP, tile, LANES), F32),
                        pltpu.VMEM((3 * NGRP, tile, LANES), F32)],
        compiler_params=pltpu.CompilerParams(dimension_semantics=("arbitrary", "arbitrary"),
                                             vmem_limit_bytes=VMEM_LIMIT),
        name="prompt_proj",
    )(x, npre, win, rin, rbase, mkb, mvb, wdw, bdw, lng, lnb, wpw, bpw, kvt, cv)


STAT_L = 8
O16_W = A_WIDTH + LANES


CLS_PER_STEP = 8


def _dilated16_kernel(kvq_ref, o_ref):
    a = lax.broadcasted_iota(jnp.int32, (QB, QB), 0)
    c = lax.broadcasted_iota(jnp.int32, (QB, QB), 1)
    tri_prev = jnp.where(c >= a, 0.0, NEG)
    tri_cur = jnp.where(c <= a, 0.0, NEG)
    bias_cur = jnp.concatenate([tri_cur, tri_cur], axis=0)
    both = jnp.concatenate([tri_prev, tri_cur], axis=1)
    bias_both = jnp.concatenate([both, both], axis=0)
    lane = lax.broadcasted_iota(jnp.int32, (QB, LANES), 1)
    ncls, nsub = kvq_ref.shape[1], kvq_ref.shape[2] // QB
    units = [(cl, sub, g) for cl in range(ncls) for sub in range(nsub) for g in range(NGRP)]
    scores, probs = {}, {}
    stats = {(cl, sub): jnp.zeros((QB, LANES), F32) for cl in range(ncls) for sub in range(nsub)}

    def rows_of(sub):
        return slice(sub * QB, (sub + 1) * QB)

    def stage_scores(u):
        cl, sub, g = u
        kc = slice(P_K + g * LANES, P_K + (g + 1) * LANES)
        qc = slice(P_Q + g * LANES, P_Q + (g + 1) * LANES)
        q2 = _pair_split(kvq_ref[0, cl, rows_of(sub), qc].astype(F32)).astype(BF16)
        s_cur = _dot_t(q2, kvq_ref[0, cl, rows_of(sub), kc])
        if sub == 0:
            scores[u] = s_cur + bias_cur
        else:
            scores[u] = jnp.concatenate([_dot_t(q2, kvq_ref[0, cl, rows_of(sub - 1), kc]), s_cur], axis=1) + bias_both

    def stage_softmax(u):
        s = scores.pop(u)
        m = jnp.max(s, axis=-1, keepdims=True)
        p = jnp.exp(s - m)
        probs[u] = (m, jnp.sum(p, axis=-1, keepdims=True), p.astype(BF16))

    def stage_values(u):
        cl, sub, g = u
        vc = slice(P_V + g * LANES, P_V + (g + 1) * LANES)
        m, l, pb = probs.pop(u)
        if sub == 0:
            o = _dot(pb, kvq_ref[0, cl, rows_of(sub), vc])
        else:
            o = (_dot(pb[:, :QB], kvq_ref[0, cl, rows_of(sub - 1), vc])
                 + _dot(pb[:, QB:], kvq_ref[0, cl, rows_of(sub), vc]))
        o_ref[0, cl, rows_of(sub), g * LANES:(g + 1) * LANES] = _pair_join(o)
        st = stats[cl, sub]
        st = jnp.where(lane == 2 * g, m[:QB], st)
        st = jnp.where(lane == 2 * g + 1, m[QB:], st)
        st = jnp.where(lane == STAT_L + 2 * g, l[:QB], st)
        stats[cl, sub] = jnp.where(lane == STAT_L + 2 * g + 1, l[QB:], st)
        if g == NGRP - 1:
            o_ref[0, cl, rows_of(sub), A_WIDTH:] = stats.pop((cl, sub))

    stages = (stage_scores, stage_softmax, stage_values)
    for t in range(len(units) + len(stages) - 1):
        for depth, stage in enumerate(stages):
            if 0 <= t - depth < len(units):
                stage(units[t - depth])


def _dilated16(kvq16):
    b, ncls, nj, w = kvq16.shape
    return pl.pallas_call(
        _dilated16_kernel,
        grid=(b, ncls // CLS_PER_STEP),
        in_specs=[pl.BlockSpec((1, CLS_PER_STEP, nj, w), lambda bi, c: (bi, c, 0, 0))],
        out_specs=pl.BlockSpec((1, CLS_PER_STEP, nj, O16_W), lambda bi, c: (bi, c, 0, 0)),
        out_shape=jax.ShapeDtypeStruct((b, ncls, nj, O16_W), F32),
        compiler_params=pltpu.CompilerParams(dimension_semantics=("arbitrary",) * 2,
                                             vmem_limit_bytes=VMEM_LIMIT),
        name="dilated16",
    )(kvq16)


STATE_PAD = 2
S_Q, S_U, S_QM, S_ROW = 0, A_WIDTH, A_WIDTH + C_WIDTH, A_WIDTH + C_WIDTH + M_WIDTH


def _new_rows(batch, kvt_ref, lo, hi, t_s):
    per_slab = LANES // t_s
    slab, pos = batch // per_slab, batch % per_slab
    return pltpu.roll(kvt_ref[slab, lo:hi, :], LANES - t_s - pos * t_s, 1)


def _slide_cache(batch, kvt_ref, src_ref, lo, dst_ref, *, t_s, win_len):
    new0 = LANES - t_s
    keep = lax.broadcasted_iota(jnp.int32, (A_WIDTH, LANES), 1) < new0
    nch = win_len // LANES
    nxt = pltpu.roll(src_ref[0, :, 0:LANES], new0, 1)
    for c in range(nch):
        cur = nxt
        nxt = (pltpu.roll(src_ref[0, :, (c + 1) * LANES:(c + 2) * LANES], new0, 1) if c + 1 < nch
               else _new_rows(batch, kvt_ref, lo, lo + A_WIDTH, t_s))
        dst_ref[0, :, c * LANES:(c + 1) * LANES] = jnp.where(keep, cur, nxt)


def _sample_mix_stages(batch, srow_ref, kvt_ref, ck_ref, cv_ref, st_ref, cmk_ref, cmv_ref, wdw_ref,
                       smix_ref, wk_ref, ns_ref, uc_ref, *, t_s, win_len):
    new_rows = functools.partial(_new_rows, batch, kvt_ref, t_s=t_s)
    return [functools.partial(_slide_cache, batch, kvt_ref, ck_ref, 0, wk_ref, t_s=t_s, win_len=win_len),
            functools.partial(_sample_attend, new_rows, srow_ref, ck_ref, cv_ref, smix_ref, t_s=t_s, win_len=win_len),
            functools.partial(_sample_mem_conv, srow_ref, st_ref, cmk_ref, cmv_ref, wdw_ref, smix_ref, ns_ref,
                              uc_ref, t_s=t_s)]


def _sample_attend(new_rows, srow_ref, ck_ref, cv_ref, smix_ref, *, t_s, win_len):
    new0 = LANES - t_s
    knt, vnt = new_rows(0, A_WIDTH), new_rows(A_WIDTH, 2 * A_WIDTH)
    q = srow_ref[:, S_Q:S_U]
    head_of_lane = lax.broadcasted_iota(jnp.int32, (t_s, A_WIDTH), 1) >> HEAD_SHIFT
    q6 = jnp.concatenate([jnp.where(head_of_lane == h, q, 0.0) for h in range(A_HEADS)], axis=0).astype(BF16)
    s_c = _dot(q6, ck_ref[0].astype(BF16))
    s_n = _dot(q6, knt.astype(BF16))

    def weights(shape, key0, lo):
        tq = lax.broadcasted_iota(jnp.int32, shape, 0) & (t_s - 1)
        key = lax.broadcasted_iota(jnp.int32, shape, 1) + key0
        d = win_len + tq - key
        ok = (d >= 0) & (key >= lo)
        w = ((d <= 128).astype(F32) + ((d <= 512) & ((d & 3) == 0)).astype(F32)
             + ((d <= 2048) & ((d & 15) == 0)).astype(F32))
        return jnp.where(ok, w, 0.0)

    w_c = weights(s_c.shape, 0, 0)
    w_n = weights(s_n.shape, win_len - new0, win_len)
    s_c = jnp.where(w_c > 0.0, s_c, NEG)
    s_n = jnp.where(w_n > 0.0, s_n, NEG)
    m = jnp.maximum(jnp.max(s_c, axis=-1, keepdims=True), jnp.max(s_n, axis=-1, keepdims=True))
    p_c = jnp.exp(s_c - m) * w_c
    p_n = jnp.exp(s_n - m) * w_n
    l = jnp.sum(p_c, axis=-1, keepdims=True) + jnp.sum(p_n, axis=-1, keepdims=True)
    o = (_dot_t(p_c.astype(BF16), cv_ref[0].astype(BF16)) + _dot_t(p_n.astype(BF16), vnt.astype(BF16))) * (1.0 / l)
    oa = jnp.zeros((t_s, A_WIDTH), F32)
    for h in range(A_HEADS):
        oa = oa + jnp.where(head_of_lane == h, o[h * t_s:(h + 1) * t_s], 0.0)
    smix_ref[:, S_Q:S_U] = oa


def _sample_mem_conv(srow_ref, st_ref, cmk_ref, cmv_ref, wdw_ref, smix_ref, ns_ref, uc_ref, *, t_s):
    qm = srow_ref[:, S_QM:S_ROW] * SCALE
    mhead = lax.broadcasted_iota(jnp.int32, (t_s, M_WIDTH), 1) >> HEAD_SHIFT
    q4 = jnp.concatenate([jnp.where(mhead == h, qm, 0.0) for h in range(M_HEADS)], axis=0).astype(BF16)
    sm = _dot(q4, cmk_ref[0].astype(BF16))
    mm = jnp.max(sm, axis=-1, keepdims=True)
    pm = jnp.exp(sm - mm)
    lm = jnp.sum(pm, axis=-1, keepdims=True)
    om4 = _dot_t(pm.astype(BF16), cmv_ref[0].astype(BF16)) * (1.0 / lm)
    om = jnp.zeros((t_s, M_WIDTH), F32)
    for h in range(M_HEADS):
        om = om + jnp.where(mhead == h, om4[h * t_s:(h + 1) * t_s], 0.0)
    smix_ref[:, S_QM:S_ROW] = om

    nst = st_ref.shape[1]
    uc_ref[0:nst, :] = st_ref[0]
    uc_ref[nst:nst + t_s, :] = srow_ref[:, S_U:S_QM]
    acc = jnp.zeros((t_s, C_WIDTH), F32)
    for w in range(CONV_W):
        acc = acc + uc_ref[STATE_PAD + w:STATE_PAD + w + t_s, :] * wdw_ref[w:w + 1, :]
    smix_ref[:, S_U:S_QM] = acc
    ns_ref[0] = uc_ref[STATE_PAD + t_s:STATE_PAD + t_s + CONV_W - 1, :]


BACK = 512
NFAR = BACK // QB - 1

def _natural_offset(p):
    p = p & (QB - 1)
    return DIL4 * (p & (CLS4 - 1)) + (p >> CLS4_SHIFT)


def _prompt_attn_kernel(kvq_ref, o16_ref, ga_ref, mix_ref, x_ref, wout_ref, npost_ref,
                        srow_ref, kvt_ref, ck_ref, cv_ref, sst_ref, cmk_ref, cmv_ref, wdw_ref,
                        y_ref, smix_ref, wk_ref, ns_ref,
                        mixa_ref, nat_ref, nat4_ref, unp_ref, uc_ref, kvp_ref, woutbf_ref, *, tq, t_s, win_len):
    i = pl.program_id(1)
    kv_refs = (kvp_ref, kvq_ref)

    @pl.when((pl.program_id(0) == 0) & (i == 0))
    def _():
        woutbf_ref[...] = wout_ref[...].astype(BF16)

    @pl.when(i == 0)
    def _():
        kvp_ref[...] = jnp.zeros(kvp_ref.shape, BF16)

    quarter = tq // DIL4
    for cl in range(DIL16):
        c4, c2 = cl % DIL4, cl // DIL4
        dst = pl.ds(c4 * quarter + c2, tq // DIL16, stride=DIL4)
        for g in range(NGRP + 1):
            nat4_ref[g, dst, :] = o16_ref[0, cl, :, g * LANES:(g + 1) * LANES]
    for c4 in range(DIL4):
        for g in range(NGRP + 1):
            nat_ref[g, pl.ds(c4, quarter, stride=DIL4), :] = nat4_ref[g, c4 * quarter:(c4 + 1) * quarter, :]

    a_n = _natural_offset(lax.broadcasted_iota(jnp.int32, (QB, 2 * QB), 0))
    c_p = lax.broadcasted_iota(jnp.int32, (QB, 2 * QB), 1)
    d = jnp.where(c_p < QB, QB, 0) + a_n - _natural_offset(c_p)
    near = (d >= 0) & (d <= QB)
    far = (d >= 0) & ((d & (DIL4 - 1)) == 0)
    bias_near0 = jnp.where(near & far, LN2, jnp.where(near | far, 0.0, NEG))
    ia = lax.broadcasted_iota(jnp.int32, (CLS4, QB), 0)
    cf = lax.broadcasted_iota(jnp.int32, (CLS4, QB), 1)
    fblk, ic = cf >> CLS4_SHIFT, cf & (CLS4 - 1)
    bias_far0 = jnp.where((fblk < NFAR) & ((fblk > 0) | (ic >= ia)), 0.0, NEG)
    lane = lax.broadcasted_iota(jnp.int32, (QB, LANES), 1)
    head_a = lane < HEAD_DIM

    def rows_of(w):
        blk, off = divmod(w, tq)
        return kv_refs[blk], off

    def group(x, c):
        return jnp.concatenate([x[c * CLS4:(c + 1) * CLS4], x[QB + c * CLS4:QB + (c + 1) * CLS4]], axis=0)

    def ungroup(parts):
        return jnp.concatenate([p_[:CLS4] for p_ in parts] + [p_[CLS4:] for p_ in parts], axis=0)

    units = [(sub, g) for sub in range(tq // QB) for g in range(NGRP)]
    ctx = {}
    for sub in range(tq // QB):
        first_valid = BACK - i * tq - sub * QB
        bias_near = jnp.where((c_p >= QB) | (NFAR * QB >= first_valid), bias_near0, NEG)
        bias_far = jnp.where(fblk * QB >= first_valid, bias_far0, NEG)
        far_blocks = [rows_of((sub + j) * QB) for j in range(NFAR)]
        ctx[sub] = dict(bias_near=jnp.concatenate([bias_near, bias_near], axis=0),
                        bias_far=jnp.concatenate([bias_far, bias_far], axis=0),
                        prev=rows_of((sub + NFAR) * QB), cur=rows_of((sub + NFAR + 1) * QB),
                        far=far_blocks + far_blocks[:1])

    def far_rows(sub, c, colsl):
        return jnp.concatenate([r[0, off + c * CLS4:off + (c + 1) * CLS4, colsl] for r, off in ctx[sub]["far"]],
                               axis=0)

    def block(sub, which, colsl):
        r, off = ctx[sub][which]
        return r[0, off:off + QB, colsl]

    scores, probs, outs = {}, {}, {}

    def stage_scores(u):
        sub, g = u
        kc = slice(P_K + g * LANES, P_K + (g + 1) * LANES)
        qc = slice(P_Q + g * LANES, P_Q + (g + 1) * LANES)
        q2 = _pair_split(kvq_ref[0, sub * QB:(sub + 1) * QB, qc].astype(F32)).astype(BF16)
        s_far = ungroup([_dot_t(group(q2, c), far_rows(sub, c, kc)) + ctx[sub]["bias_far"] for c in range(DIL4)])
        s_near = jnp.concatenate([_dot_t(q2, block(sub, "prev", kc)), _dot_t(q2, block(sub, "cur", kc))],
                                 axis=1) + ctx[sub]["bias_near"]
        scores[u] = jnp.concatenate([s_far, s_near], axis=1)

    def stage_softmax(u):
        s = scores.pop(u)
        m = jnp.max(s, axis=-1, keepdims=True)
        p = jnp.exp(s - m)
        probs[u] = (m, jnp.sum(p, axis=-1, keepdims=True), p.astype(BF16))

    def stage_values(u):
        sub, g = u
        vc = slice(P_V + g * LANES, P_V + (g + 1) * LANES)
        pb = probs[u][2]
        outs[u] = (_dot(pb[:, QB:2 * QB], block(sub, "prev", vc)) + _dot(pb[:, 2 * QB:], block(sub, "cur", vc))
                   + ungroup([_dot(group(pb[:, :QB], c), far_rows(sub, c, vc)) for c in range(DIL4)]))

    def stage_merge(u):
        sub, g = u
        rows = slice(sub * QB, (sub + 1) * QB)
        cols = slice(g * LANES, (g + 1) * LANES)
        m, l, _ = probs.pop(u)
        o_s = _pair_join(outs.pop(u))
        m_s = jnp.where(head_a, m[:QB], m[QB:])
        l_s = jnp.where(head_a, l[:QB], l[QB:])
        for c in range(DIL4):
            dst = pl.ds(c, CLS4, stride=DIL4)
            unp_ref[3 * g, dst, :] = o_s[c * CLS4:(c + 1) * CLS4]
            unp_ref[3 * g + 1, dst, :] = m_s[c * CLS4:(c + 1) * CLS4]
            unp_ref[3 * g + 2, dst, :] = l_s[c * CLS4:(c + 1) * CLS4]
        o_n, m_n, l_n = unp_ref[3 * g], unp_ref[3 * g + 1], unp_ref[3 * g + 2]
        st = nat_ref[NGRP, rows, :]
        m_f = jnp.where(head_a, st[:, 2 * g:2 * g + 1], st[:, 2 * g + 1:2 * g + 2])
        l_f = jnp.where(head_a, st[:, STAT_L + 2 * g:STAT_L + 2 * g + 1],
                        st[:, STAT_L + 2 * g + 1:STAT_L + 2 * g + 2])
        mx = jnp.maximum(m_n, m_f)
        w_n = jnp.exp(m_n - mx)
        w_f = jnp.exp(m_f - mx)
        oa = (o_n * w_n + nat_ref[g, rows, :] * w_f) / (l_n * w_n + l_f * w_f)
        mixa_ref[rows, cols] = (oa * ga_ref[0, rows, cols]).astype(BF16)

    def finish(rows):
        z = (_dot(mixa_ref[rows, :], woutbf_ref[0:A_WIDTH, :])
             + _dot(mix_ref[0, rows, :], woutbf_ref[A_WIDTH:, :]))
        y_ref[0, rows, :] = x_ref[0, rows, :] + _rmsnorm(z, npost_ref[...])

    stages = (stage_scores, stage_softmax, stage_values, stage_merge)
    half = len(units) // 2
    for t in range(len(units) + len(stages) - 1):
        for depth, stage in enumerate(stages):
            if 0 <= t - depth < len(units):
                stage(units[t - depth])
        if t - (len(stages) - 1) == half - 1:
            finish(slice(0, tq // 2))
    finish(slice(tq // 2, tq))
    kvp_ref[...] = kvq_ref[:, :, 0:P_Q]

    for part in _sample_mix_stages(pl.program_id(0) * pl.num_programs(1) + i, srow_ref, kvt_ref, ck_ref, cv_ref,
                                   sst_ref, cmk_ref, cmv_ref, wdw_ref, smix_ref, wk_ref, ns_ref, uc_ref,
                                   t_s=t_s, win_len=win_len):
        part()


def _prompt_attn(kvq, o16, ga, mix, x, wout, npost, srow, kvt, ck, cv, st_pad, cmk, cmv, wdw, tq, t_s):
    b, s, _ = x.shape
    nsteps = s // tq
    nb, _, win_len = ck.shape
    assert tq == BACK and b * nsteps == nb, "one sample batch per grid step"
    row = lambda w: pl.BlockSpec((1, tq, w), lambda bi, i: (bi, i, 0))
    cls = pl.BlockSpec((1, DIL16, tq // DIL16, O16_W), lambda bi, i: (bi, 0, i, 0))
    srows = pl.BlockSpec((t_s, S_ROW), lambda bi, i: (bi * nsteps + i, 0))
    per = lambda a: pl.BlockSpec((1,) + a.shape[1:], lambda bi, i: (bi * nsteps + i, 0, 0))
    return pl.pallas_call(
        functools.partial(_prompt_attn_kernel, tq=tq, t_s=t_s, win_len=win_len),
        grid=(b, nsteps),
        in_specs=[row(kvq.shape[2]), cls, row(A_WIDTH), row(C_WIDTH + M_WIDTH), row(D_MODEL),
                  _resident(wout), _full(npost),
                  srows, _full(kvt), per(ck), per(cv), per(st_pad), per(cmk), per(cmv), _full(wdw)],
        out_specs=[row(D_MODEL), srows, per(ck),
                   pl.BlockSpec((1, CONV_W - 1, C_WIDTH), lambda bi, i: (bi * nsteps + i, 0, 0))],
        out_shape=[jax.ShapeDtypeStruct((b, s, D_MODEL), F32),
                   jax.ShapeDtypeStruct((nb * t_s, S_ROW), F32),
                   jax.ShapeDtypeStruct(ck.shape, F32),
                   jax.ShapeDtypeStruct((nb, CONV_W - 1, C_WIDTH), F32)],
        scratch_shapes=[pltpu.VMEM((tq, A_WIDTH), BF16),
                        pltpu.VMEM((NGRP + 1, tq, LANES), F32),
                        pltpu.VMEM((NGRP + 1, tq, LANES), F32),
                        pltpu.VMEM((3 * NGRP, QB, LANES), F32),
                        pltpu.VMEM((STATE_PAD + CONV_W - 1 + t_s, C_WIDTH), F32),
                        pltpu.VMEM((1, tq, 2 * A_WIDTH), BF16),
                        pltpu.VMEM(wout.shape, BF16)],
        compiler_params=pltpu.CompilerParams(dimension_semantics=("arbitrary", "arbitrary"),
                                             vmem_limit_bytes=VMEM_LIMIT),
        name="prompt_attn",
    )(kvq, o16, ga, mix, x, wout, npost, srow, kvt, ck, cv, st_pad, cmk, cmv, wdw)


def _sample_proj_kernel(x_ref, npre_ref, win_ref, ang_ref, srow_ref, kvt_ref, gate_ref, wbf_ref):
    h = _rmsnorm(x_ref[...], npre_ref[...]).astype(BF16)
    wbf_ref[...] = win_ref[...].astype(BF16)
    proj = _dot(h, wbf_ref[...])
    pats = _rope_patterns(ang_ref[:, 0:LANES], ang_ref[:, LANES:])
    srow_ref[:, S_Q:S_U] = _rope(proj[:, O_QA:O_KA], pats) * SCALE
    srow_ref[:, S_U:S_QM] = proj[:, O_AB:O_BB] * _sigmoid(proj[:, O_BB:O_GB])
    srow_ref[:, S_QM:S_ROW] = proj[:, O_QM:O_GM]
    kv_t = jnp.concatenate([_rope(proj[:, O_KA:O_VA], pats), proj[:, O_VA:O_GA]], axis=1).T
    for slab in range(kvt_ref.shape[0]):
        kvt_ref[slab] = kv_t[:, slab * LANES:(slab + 1) * LANES]
    gate_ref[:, S_Q:S_U] = _silu(proj[:, O_GA:O_AB])
    gate_ref[:, S_U:S_QM] = _silu(proj[:, O_GB:O_QM])
    gate_ref[:, S_QM:S_ROW] = _silu(proj[:, O_GM:D_IN])


def _sample_proj(x, npre, win, ang):
    n = x.shape[0]
    assert n % LANES == 0
    shapes = [(n, S_ROW), (n // LANES, 2 * A_WIDTH, LANES), (n, S_ROW), win.shape]
    dtypes = [F32, F32, F32, BF16]
    return pl.pallas_call(
        _sample_proj_kernel,
        grid=(1,),
        in_specs=[_full(x), _full(npre), _full(win), _full(ang)],
        out_specs=[pl.BlockSpec(sh, lambda i, nd=len(sh): (0,) * nd) for sh in shapes],
        out_shape=[jax.ShapeDtypeStruct(sh, dt) for sh, dt in zip(shapes, dtypes)],
        compiler_params=pltpu.CompilerParams(vmem_limit_bytes=VMEM_LIMIT),
        name="sample_proj",
    )(x, npre, win, ang)


def _sample_out_kernel(x_ref, smix_ref, gate_ref, bdw_ref, lng_ref, lnb_ref, wpw_ref, bpw_ref, wout_ref, npost_ref,
                       y_ref):
    mixed_a = (smix_ref[:, S_Q:S_U] * gate_ref[:, S_Q:S_U]).astype(BF16)
    mixed_b = _conformer_tail(smix_ref[:, S_U:S_QM], gate_ref[:, S_U:S_QM], bdw_ref[...], lng_ref[...], lnb_ref[...],
                              wpw_ref, bpw_ref[...]).astype(BF16)
    mixed_m = (smix_ref[:, S_QM:S_ROW] * gate_ref[:, S_QM:S_ROW]).astype(BF16)
    wout = wout_ref[...].astype(BF16)
    z = (_dot(mixed_a, wout[0:A_WIDTH]) + _dot(mixed_b, wout[A_WIDTH:A_WIDTH + C_WIDTH])
         + _dot(mixed_m, wout[A_WIDTH + C_WIDTH:]))
    y_ref[...] = x_ref[...] + _rmsnorm(z, npost_ref[...])


def _sample_out(x, smix, gate, bdw, lng, lnb, wpw, bpw, wout, npost):
    args = (x, smix, gate, bdw, lng, lnb, wpw, bpw, wout, npost)
    return pl.pallas_call(
        _sample_out_kernel,
        grid=(1,),
        in_specs=[_full(a) for a in args],
        out_specs=_full(x),
        out_shape=jax.ShapeDtypeStruct(x.shape, F32),
        compiler_params=pltpu.CompilerParams(vmem_limit_bytes=VMEM_LIMIT),
        name="sample_out",
    )(*args)


def _rope_cos_sin(pos):
    inv = ROPE_THETA ** (-jnp.arange(0, ROT_DIM, 2, dtype=F32) / ROT_DIM)
    ang = pos.astype(F32)[:, None] * jnp.tile(inv, 2 * LANES // ROT_DIM)[None, :]
    return jnp.concatenate([jnp.cos(ang), jnp.sin(ang)], axis=1)


def _feature_major(cache):
    nb, rows, heads, dim = cache.shape
    return jnp.transpose(cache, (0, 2, 3, 1)).reshape(nb, heads * dim, rows)


def _row_major(cache_t, heads):
    nb, width, rows = cache_t.shape
    return jnp.transpose(cache_t.reshape(nb, heads, width // heads, rows), (0, 3, 1, 2))


PROJ_TILE = 512
ATTN_TILE = BACK


def kernel(x_prompt, x_sample, cache_win_k, cache_win_v, state_conv, cache_mem_k, cache_mem_v, mem_prompt,
           norm_pre, norm_post, w_in, w_out, norm_mem, w_mem_kv, w_dw, b_dw, ln_conv_g, ln_conv_b, w_pw2, b_pw2):
    depth = w_in.shape[0]
    assert depth == 1, "single-layer step"
    b, s, _ = x_prompt.shape
    nb, t_s, _ = x_sample.shape
    win_len = cache_win_k.shape[2]
    assert s % (DIL16 * QB) == 0 and s % PROJ_TILE == 0 and s % ATTN_TILE == 0
    assert win_len == MAX_WINDOW and win_len % LANES == 0
    assert t_s % 8 == 0 and t_s & (t_s - 1) == 0 and t_s < LANES
    l = 0
    row = lambda a: a[l][None, :]
    npre, npost, nmem = row(norm_pre), row(norm_post), row(norm_mem)
    bdw, lng, lnb, bpw = row(b_dw), row(ln_conv_g), row(ln_conv_b), row(b_pw2)
    win, wout, wmem, wpw = w_in[l], w_out[l], w_mem_kv[l], w_pw2[l]
    wdw = w_dw[l]

    pos_s = PAST_LEN + jnp.arange(t_s, dtype=jnp.int32)
    xs = x_sample.reshape(nb * t_s, D_MODEL)
    srow, kvt, gate_s, win_bf = _sample_proj(xs, npre, win, jnp.tile(_rope_cos_sin(pos_s), (nb, 1)))
    st_pad = jnp.pad(state_conv[l], ((0, 0), (STATE_PAD, 0), (0, 0)))

    mk, mv, mkb, mvb = _mem_kv(mem_prompt, nmem, wmem)
    rin = _rope_cos_sin(jnp.arange(PROJ_TILE, dtype=jnp.int32))
    rbase = _rope_cos_sin(jnp.arange(0, s, PROJ_TILE, dtype=jnp.int32))[:, None, :]
    ck, cv = _feature_major(cache_win_k[l]), _feature_major(cache_win_v[l])
    kvq, kvq16, kf, vf, ga, mix, ust, wv = _prompt_proj(x_prompt, npre, win_bf, rin, rbase, mkb, mvb, wdw, bdw,
                                                        lng, lnb, wpw, bpw, kvt, cv, PROJ_TILE, t_s)
    o16 = _dilated16(kvq16)
    y_prompt, smix, wk, nst = _prompt_attn(
        kvq, o16, ga, mix, x_prompt, wout, npost, srow, kvt, ck, cv, st_pad,
        _feature_major(cache_mem_k[l]), _feature_major(cache_mem_v[l]), wdw, ATTN_TILE, t_s)
    y_sample = _sample_out(xs, smix, gate_s, bdw, lng, lnb, wpw, bpw, wout, npost)

    keep_p = kf.shape[1]
    return (y_prompt,
            y_sample.reshape(nb, t_s, D_MODEL),
            kf.reshape(1, b, keep_p, A_HEADS, HEAD_DIM),
            vf.reshape(1, b, keep_p, A_HEADS, HEAD_DIM),
            ust[:, CONV_HIST - (CONV_W - 1):, :][None],
            mk.reshape(1, b, N_MEM, M_HEADS, HEAD_DIM),
            mv.reshape(1, b, N_MEM, M_HEADS, HEAD_DIM),
            _row_major(wk, A_HEADS)[None],
            _row_major(wv, A_HEADS)[None],
            nst[None])
```

```python
import functools
import math

import jax
import jax.numpy as jnp
from jax import lax
from jax.experimental import pallas as pl
from jax.experimental.pallas import tpu as pltpu

F32 = jnp.float32
BF16 = jnp.bfloat16

D_MODEL = 1024
HEAD_DIM = 64
HEAD_SHIFT = 6
A_WIDTH = 384
A_HEADS = 6
M_WIDTH = 256
M_HEADS = 4
C_WIDTH = 384
ROT_DIM = 16
ROPE_THETA = 500000.0
CONV_W = 31
N_MEM = 256
MAX_WINDOW = 2048
PAST_LEN = 16384
EPS = 1e-6
SCALE = HEAD_DIM ** -0.5
NEG = -1e30
LN2 = math.log(2.0)

LANES = 128
NGRP = A_WIDTH // LANES
QB = 128
DIL16 = 16
DIL4 = 4
CLS4 = QB // DIL4
CLS4_SHIFT = 5
O_QA, O_KA, O_VA, O_GA, O_AB, O_BB, O_GB, O_QM, O_GM, D_IN = 0, 384, 768, 1152, 1536, 1920, 2304, 2688, 2944, 3200
P_K, P_V, P_Q, P_W = 0, A_WIDTH, 2 * A_WIDTH, 3 * A_WIDTH

VMEM_LIMIT = 60 * 1024 * 1024


def _sigmoid(x):
    return 1.0 / (1.0 + jnp.exp(-x))


def _silu(x):
    return x * _sigmoid(x)


def _rmsnorm(x, g):
    return x * lax.rsqrt(jnp.mean(x * x, axis=-1, keepdims=True) + EPS) * g


def _dot(a, b):
    return jnp.dot(a, b, preferred_element_type=F32)


def _dot_t(a, b):
    return lax.dot_general(a, b, (((1,), (1,)), ((), ())), preferred_element_type=F32)


def _rope_patterns(cos, sin):
    in_head = lax.broadcasted_iota(jnp.int32, cos.shape, 1) & (HEAD_DIM - 1)
    rot, lo = in_head < ROT_DIM, in_head < ROT_DIM // 2
    return jnp.where(rot, cos, 1.0), jnp.where(lo, -sin, 0.0), jnp.where(rot & ~lo, sin, 0.0)


def _rope(xw, pats):
    cos, s1, s2 = pats
    outs = []
    for g in range(xw.shape[1] // LANES):
        xg = xw[:, g * LANES:(g + 1) * LANES]
        outs.append(xg * cos + pltpu.roll(xg, LANES - 8, 1) * s1 + pltpu.roll(xg, 8, 1) * s2)
    return jnp.concatenate(outs, axis=1)


def _pair_split(q):
    lane = lax.broadcasted_iota(jnp.int32, q.shape, 1)
    qa = jnp.where(lane < HEAD_DIM, q, 0.0)
    qb = jnp.where(lane >= HEAD_DIM, q, 0.0)
    return jnp.concatenate([qa, qb], axis=0)


def _pair_join(x):
    t = x.shape[0] // 2
    lane = lax.broadcasted_iota(jnp.int32, (t, LANES), 1)
    return jnp.where(lane < HEAD_DIM, x[:t], x[t:])


def _mem_attend(qm, mk_ref, mv_ref):
    outs = []
    for g in range(M_WIDTH // LANES):
        cols = slice(g * LANES, (g + 1) * LANES)
        q2 = _pair_split(qm[:, cols] * SCALE).astype(BF16)
        s = _dot_t(q2, mk_ref[0, :, cols])
        m = jnp.max(s, axis=-1, keepdims=True)
        p = jnp.exp(s - m)
        l = jnp.sum(p, axis=-1, keepdims=True)
        o = _dot(p.astype(BF16), mv_ref[0, :, cols]) * (1.0 / l)
        outs.append(_pair_join(o))
    return jnp.concatenate(outs, axis=1)


def _conformer_tail(c, gate_b, bdw, lng, lnb, wpw_ref, bpw):
    cf = c + bdw
    mu = jnp.mean(cf, axis=-1, keepdims=True)
    dev = cf - mu
    var = jnp.mean(dev * dev, axis=-1, keepdims=True)
    cn = dev * lax.rsqrt(var + EPS) * lng + lnb
    ob = _dot(_silu(cn).astype(BF16), wpw_ref[...].astype(BF16)) + bpw
    return ob * gate_b


def _full(a):
    return pl.BlockSpec(a.shape, lambda *_: (0,) * a.ndim)


def _resident(a):
    return pl.BlockSpec(a.shape, lambda *_: (0,) * a.ndim, pipeline_mode=pl.Buffered(1))


def _mem_kv(mem_ref, g_ref, w_ref, mk_ref, mv_ref, mkb_ref, mvb_ref):
    w = w_ref[...].astype(BF16)
    for bi in range(mem_ref.shape[0]):
        kv = _dot(_rmsnorm(mem_ref[bi], g_ref[...]).astype(BF16), w)
        mk_ref[bi] = kv[:, :M_WIDTH]
        mv_ref[bi] = kv[:, M_WIDTH:]
        mkb_ref[bi] = kv[:, :M_WIDTH].astype(BF16)
        mvb_ref[bi] = kv[:, M_WIDTH:].astype(BF16)


CONV_HIST = 32
CONV_CHUNK = 64


def _prompt_proj_kernel(x_ref, npre_ref, win_ref, rin_ref, rbase_ref, mk_ref, mv_ref,
                        wdw_ref, bdw_ref, lng_ref, lnb_ref, wpw_ref, bpw_ref, kvt_ref, cv_ref,
                        kvq_ref, kvq16_ref, kf_ref, vf_ref, ga_ref, mix_ref, ust_ref, wv_ref,
                        uext_ref, ush_ref, conv_ref, stage_ref, stage4_ref, *, t_s, win_len):
    t = x_ref.shape[1]
    i = pl.program_id(1)

    @pl.when(i == 0)
    def _():
        uext_ref[0:CONV_HIST, :] = jnp.zeros((CONV_HIST, C_WIDTH), F32)

    h = _rmsnorm(x_ref[0], npre_ref[...]).astype(BF16)
    proj = {}

    def project(name, c0, c1):
        def run():
            proj[name] = _dot(h, win_ref[:, c0:c1])
        return run

    def stage(idx, val):
        for g in range(NGRP):
            stage_ref[NGRP * idx + g] = val[:, g * LANES:(g + 1) * LANES]

    pats = []

    def rope_pats():
        if not pats:
            cr, sr = rin_ref[:, 0:LANES], rin_ref[:, LANES:]
            cb, sb = rbase_ref[i, :, 0:LANES], rbase_ref[i, :, LANES:]
            pats.append(_rope_patterns(cb * cr - sb * sr, sb * cr + cb * sr))
        return pats[0]

    def do_k():
        k = _rope(proj.pop("k"), rope_pats())
        kf_ref[0] = k
        stage(0, k)

    def do_q():
        stage(2, _rope(proj.pop("q"), rope_pats()) * SCALE)

    def do_v():
        v = proj.pop("v")
        vf_ref[0] = v
        stage(1, v)

    def do_ga():
        ga_ref[0] = _silu(proj.pop("ga"))

    def do_copies():
        quarter = t // DIL4
        for c4 in range(DIL4):
            for s in range(3 * NGRP):
                stage4_ref[s, c4 * quarter:(c4 + 1) * quarter, :] = stage_ref[s, pl.ds(c4, quarter, stride=DIL4), :]
        for c in range(DIL16):
            c4, c2 = c % DIL4, c // DIL4
            rows_c = [stage4_ref[s, pl.ds(c4 * quarter + c2, t // DIL16, stride=DIL4), :] for s in range(3 * NGRP)]
            kvq16_ref[0, c] = jnp.concatenate(rows_c, axis=1).astype(BF16)
        for blk in range(t // QB):
            for c4 in range(DIL4):
                src = slice(c4 * quarter + blk * CLS4, c4 * quarter + (blk + 1) * CLS4)
                rows_c = [stage4_ref[s, src, :] for s in range(3 * NGRP)]
                kvq_ref[0, blk * QB + c4 * CLS4:blk * QB + (c4 + 1) * CLS4, :] = (
                    jnp.concatenate(rows_c, axis=1).astype(BF16))

    def do_glu():
        uext_ref[CONV_HIST:CONV_HIST + t, :] = proj.pop("ab") * _sigmoid(proj.pop("bb"))
        uext = uext_ref[...]
        for r in range(1, 8):
            ush_ref[r - 1] = pltpu.roll(uext, r, 0)

    def conv_rows(r0):
        def run():
            acc = jnp.zeros((CONV_CHUNK, C_WIDTH), F32)
            for kk in range(CONV_W):
                tap = wdw_ref[CONV_W - 1 - kk:CONV_W - kk, :]
                r, base = kk % 8, CONV_HIST + r0 - (kk - kk % 8)
                if r == 0:
                    slab = uext_ref[base:base + CONV_CHUNK, :]
                else:
                    slab = ush_ref[r - 1, base:base + CONV_CHUNK, :]
                acc = acc + slab * tap
            conv_ref[r0:r0 + CONV_CHUNK, :] = acc
        return run

    def do_history():
        tail = uext_ref[t:t + CONV_HIST, :]
        uext_ref[0:CONV_HIST, :] = tail
        ust_ref[0] = tail

    def do_tail():
        mixed_b = _conformer_tail(conv_ref[...], _silu(proj.pop("gb")), bdw_ref[...], lng_ref[...], lnb_ref[...],
                                  wpw_ref, bpw_ref[...])
        mix_ref[0, :, 0:C_WIDTH] = mixed_b.astype(BF16)

    def do_mem():
        mixed_m = _mem_attend(proj.pop("qm"), mk_ref, mv_ref) * _silu(proj.pop("gm"))
        mix_ref[0, :, C_WIDTH:] = mixed_m.astype(BF16)

    convs = [conv_rows(r0) for r0 in range(0, t, CONV_CHUNK)]
    riders = [([project("k", O_KA, O_VA)], do_k), ([project("q", O_QA, O_KA)], do_q),
              ([project("v", O_VA, O_GA)], do_v), ([project("ga", O_GA, O_AB)], do_ga),
              ([project("qm", O_QM, O_GM), project("gm", O_GM, D_IN)], do_mem),
              ([project("gb", O_GB, O_QM)], do_copies)]
    slide_v = functools.partial(_slide_cache, pl.program_id(0) * pl.num_programs(1) + i, kvt_ref, cv_ref, A_WIDTH,
                                wv_ref, t_s=t_s, win_len=win_len)
    program = [project("ab", O_AB, O_BB), project("bb", O_BB, O_GB), slide_v, do_glu]
    for n, conv in enumerate(convs):
        if n < len(riders):
            program += riders[n][0]
        program.append(conv)
        if n < len(riders) and riders[n][1] is not None:
            program.append(riders[n][1])
    assert len(convs) >= len(riders)
    program += [do_history, do_tail]
    for piece in program:
        piece()
    assert not proj


def _prompt_proj(x, npre, win, rin, rbase, mkb, mvb, wdw, bdw, lng, lnb, wpw, bpw, kvt, cv, tile, t_s):
    b, s, _ = x.shape
    nt = s // tile
    nb, _, win_len = cv.shape
    assert b * nt == nb, "one sample batch per grid step"
    per_batch = pl.BlockSpec((1,) + cv.shape[1:], lambda bi, i: (bi * nt + i, 0, 0))
    keep = min(MAX_WINDOW, s)
    first_keep = (s - keep) // tile
    row = lambda w: pl.BlockSpec((1, tile, w), lambda bi, i: (bi, i, 0))
    memb = pl.BlockSpec((1, N_MEM, M_WIDTH), lambda bi, i: (bi, 0, 0))
    keepb = pl.BlockSpec((1, tile, A_WIDTH), lambda bi, i: (bi, jnp.maximum(i - first_keep, 0), 0))
    cls = pl.BlockSpec((1, DIL16, tile // DIL16, P_W), lambda bi, i: (bi, 0, i, 0))
    return pl.pallas_call(
        functools.partial(_prompt_proj_kernel, t_s=t_s, win_len=win_len),
        grid=(b, nt),
        in_specs=[row(D_MODEL), _full(npre), _resident(win), _full(rin), _full(rbase),
                  memb, memb, _full(wdw), _full(bdw), _full(lng), _full(lnb), _full(wpw), _full(bpw),
                  _full(kvt), per_batch],
        out_specs=[row(P_W), cls, keepb, keepb, row(A_WIDTH), row(C_WIDTH + M_WIDTH),
                   pl.BlockSpec((1, CONV_HIST, C_WIDTH), lambda bi, i: (bi, 0, 0)), per_batch],
        out_shape=[jax.ShapeDtypeStruct((b, s, P_W), BF16),
                   jax.ShapeDtypeStruct((b, DIL16, s // DIL16, P_W), BF16),
                   jax.ShapeDtypeStruct((b, keep, A_WIDTH), F32),
                   jax.ShapeDtypeStruct((b, keep, A_WIDTH), F32),
                   jax.ShapeDtypeStruct((b, s, A_WIDTH), F32),
                   jax.ShapeDtypeStruct((b, s, C_WIDTH + M_WIDTH), BF16),
                   jax.ShapeDtypeStruct((b, CONV_HIST, C_WIDTH), F32),
                   jax.ShapeDtypeStruct(cv.shape, F32)],
        scratch_shapes=[pltpu.VMEM((CONV_HIST + tile, C_WIDTH), F32),
                        pltpu.VMEM((7, CONV_HIST + tile, C_WIDTH), F32),
                        pltpu.VMEM((tile, C_WIDTH), F32),
                        pltpu.VMEM((3 * NGRP, tile, LANES), F32),
                        pltpu.VMEM((3 * NGRP, tile, LANES), F32)],
        compiler_params=pltpu.CompilerParams(dimension_semantics=("arbitrary", "arbitrary"),
                                             vmem_limit_bytes=VMEM_LIMIT),
        name="prompt_proj",
    )(x, npre, win, rin, rbase, mkb, mvb, wdw, bdw, lng, lnb, wpw, bpw, kvt, cv)


STAT_L = 8
O16_W = A_WIDTH + LANES


CLS_PER_STEP = 8


def _dilated16_kernel(kvq_ref, o_ref):
    a = lax.broadcasted_iota(jnp.int32, (QB, QB), 0)
    c = lax.broadcasted_iota(jnp.int32, (QB, QB), 1)
    tri_prev = jnp.where(c >= a, 0.0, NEG)
    tri_cur = jnp.where(c <= a, 0.0, NEG)
    bias_cur = jnp.concatenate([tri_cur, tri_cur], axis=0)
    both = jnp.concatenate([tri_prev, tri_cur], axis=1)
    bias_both = jnp.concatenate([both, both], axis=0)
    lane = lax.broadcasted_iota(jnp.int32, (QB, LANES), 1)
    ncls, nsub = kvq_ref.shape[1], kvq_ref.shape[2] // QB
    units = [(cl, sub, g) for cl in range(ncls) for sub in range(nsub) for g in range(NGRP)]
    scores, probs = {}, {}
    stats = {(cl, sub): jnp.zeros((QB, LANES), F32) for cl in range(ncls) for sub in range(nsub)}

    def rows_of(sub):
        return slice(sub * QB, (sub + 1) * QB)

    def stage_scores(u):
        cl, sub, g = u
        kc = slice(P_K + g * LANES, P_K + (g + 1) * LANES)
        qc = slice(P_Q + g * LANES, P_Q + (g + 1) * LANES)
        q2 = _pair_split(kvq_ref[0, cl, rows_of(sub), qc].astype(F32)).astype(BF16)
        s_cur = _dot_t(q2, kvq_ref[0, cl, rows_of(sub), kc])
        if sub == 0:
            scores[u] = s_cur + bias_cur
        else:
            scores[u] = jnp.concatenate([_dot_t(q2, kvq_ref[0, cl, rows_of(sub - 1), kc]), s_cur], axis=1) + bias_both

    def stage_softmax(u):
        s = scores.pop(u)
        m = jnp.max(s, axis=-1, keepdims=True)
        p = jnp.exp(s - m)
        probs[u] = (m, jnp.sum(p, axis=-1, keepdims=True), p.astype(BF16))

    def stage_values(u):
        cl, sub, g = u
        vc = slice(P_V + g * LANES, P_V + (g + 1) * LANES)
        m, l, pb = probs.pop(u)
        if sub == 0:
            o = _dot(pb, kvq_ref[0, cl, rows_of(sub), vc])
        else:
            o = (_dot(pb[:, :QB], kvq_ref[0, cl, rows_of(sub - 1), vc])
                 + _dot(pb[:, QB:], kvq_ref[0, cl, rows_of(sub), vc]))
        o_ref[0, cl, rows_of(sub), g * LANES:(g + 1) * LANES] = _pair_join(o)
        st = stats[cl, sub]
        st = jnp.where(lane == 2 * g, m[:QB], st)
        st = jnp.where(lane == 2 * g + 1, m[QB:], st)
        st = jnp.where(lane == STAT_L + 2 * g, l[:QB], st)
        stats[cl, sub] = jnp.where(lane == STAT_L + 2 * g + 1, l[QB:], st)
        if g == NGRP - 1:
            o_ref[0, cl, rows_of(sub), A_WIDTH:] = stats.pop((cl, sub))

    stages = (stage_scores, stage_softmax, stage_values)
    for t in range(len(units) + len(stages) - 1):
        for depth, stage in enumerate(stages):
            if 0 <= t - depth < len(units):
                stage(units[t - depth])


def _dilated16(kvq16):
    b, ncls, nj, w = kvq16.shape
    return pl.pallas_call(
        _dilated16_kernel,
        grid=(b, ncls // CLS_PER_STEP),
        in_specs=[pl.BlockSpec((1, CLS_PER_STEP, nj, w), lambda bi, c: (bi, c, 0, 0))],
        out_specs=pl.BlockSpec((1, CLS_PER_STEP, nj, O16_W), lambda bi, c: (bi, c, 0, 0)),
        out_shape=jax.ShapeDtypeStruct((b, ncls, nj, O16_W), F32),
        compiler_params=pltpu.CompilerParams(dimension_semantics=("arbitrary",) * 2,
                                             vmem_limit_bytes=VMEM_LIMIT),
        name="dilated16",
    )(kvq16)


STATE_PAD = 2
S_Q, S_U, S_QM, S_ROW = 0, A_WIDTH, A_WIDTH + C_WIDTH, A_WIDTH + C_WIDTH + M_WIDTH


def _new_rows(batch, kvt_ref, lo, hi, t_s):
    per_slab = LANES // t_s
    slab, pos = batch // per_slab, batch % per_slab
    return pltpu.roll(kvt_ref[slab, lo:hi, :], LANES - t_s - pos * t_s, 1)


def _slide_cache(batch, kvt_ref, src_ref, lo, dst_ref, *, t_s, win_len):
    new0 = LANES - t_s
    keep = lax.broadcasted_iota(jnp.int32, (A_WIDTH, LANES), 1) < new0
    nch = win_len // LANES
    nxt = pltpu.roll(src_ref[0, :, 0:LANES], new0, 1)
    for c in range(nch):
        cur = nxt
        nxt = (pltpu.roll(src_ref[0, :, (c + 1) * LANES:(c + 2) * LANES], new0, 1) if c + 1 < nch
               else _new_rows(batch, kvt_ref, lo, lo + A_WIDTH, t_s))
        dst_ref[0, :, c * LANES:(c + 1) * LANES] = jnp.where(keep, cur, nxt)


def _sample_mix_stages(batch, srow_ref, kvt_ref, ck_ref, cv_ref, st_ref, cmk_ref, cmv_ref, wdw_ref,
                       smix_ref, wk_ref, ns_ref, uc_ref, *, t_s, win_len):
    new_rows = functools.partial(_new_rows, batch, kvt_ref, t_s=t_s)
    return [functools.partial(_slide_cache, batch, kvt_ref, ck_ref, 0, wk_ref, t_s=t_s, win_len=win_len),
            functools.partial(_sample_attend, new_rows, srow_ref, ck_ref, cv_ref, smix_ref, t_s=t_s, win_len=win_len),
            functools.partial(_sample_mem_conv, srow_ref, st_ref, cmk_ref, cmv_ref, wdw_ref, smix_ref, ns_ref,
                              uc_ref, t_s=t_s)]


def _sample_attend(new_rows, srow_ref, ck_ref, cv_ref, smix_ref, *, t_s, win_len):
    new0 = LANES - t_s
    knt, vnt = new_rows(0, A_WIDTH), new_rows(A_WIDTH, 2 * A_WIDTH)
    q = srow_ref[:, S_Q:S_U]
    head_of_lane = lax.broadcasted_iota(jnp.int32, (t_s, A_WIDTH), 1) >> HEAD_SHIFT
    q6 = jnp.concatenate([jnp.where(head_of_lane == h, q, 0.0) for h in range(A_HEADS)], axis=0).astype(BF16)
    s_c = _dot(q6, ck_ref[0].astype(BF16))
    s_n = _dot(q6, knt.astype(BF16))

    def weights(shape, key0, lo):
        tq = lax.broadcasted_iota(jnp.int32, shape, 0) & (t_s - 1)
        key = lax.broadcasted_iota(jnp.int32, shape, 1) + key0
        d = win_len + tq - key
        ok = (d >= 0) & (key >= lo)
        w = ((d <= 128).astype(F32) + ((d <= 512) & ((d & 3) == 0)).astype(F32)
             + ((d <= 2048) & ((d & 15) == 0)).astype(F32))
        return jnp.where(ok, w, 0.0)

    w_c = weights(s_c.shape, 0, 0)
    w_n = weights(s_n.shape, win_len - new0, win_len)
    s_c = jnp.where(w_c > 0.0, s_c, NEG)
    s_n = jnp.where(w_n > 0.0, s_n, NEG)
    m = jnp.maximum(jnp.max(s_c, axis=-1, keepdims=True), jnp.max(s_n, axis=-1, keepdims=True))
    p_c = jnp.exp(s_c - m) * w_c
    p_n = jnp.exp(s_n - m) * w_n
    l = jnp.sum(p_c, axis=-1, keepdims=True) + jnp.sum(p_n, axis=-1, keepdims=True)
    o = (_dot_t(p_c.astype(BF16), cv_ref[0].astype(BF16)) + _dot_t(p_n.astype(BF16), vnt.astype(BF16))) * (1.0 / l)
    oa = jnp.zeros((t_s, A_WIDTH), F32)
    for h in range(A_HEADS):
        oa = oa + jnp.where(head_of_lane == h, o[h * t_s:(h + 1) * t_s], 0.0)
    smix_ref[:, S_Q:S_U] = oa


def _sample_mem_conv(srow_ref, st_ref, cmk_ref, cmv_ref, wdw_ref, smix_ref, ns_ref, uc_ref, *, t_s):
    qm = srow_ref[:, S_QM:S_ROW] * SCALE
    mhead = lax.broadcasted_iota(jnp.int32, (t_s, M_WIDTH), 1) >> HEAD_SHIFT
    q4 = jnp.concatenate([jnp.where(mhead == h, qm, 0.0) for h in range(M_HEADS)], axis=0).astype(BF16)
    sm = _dot(q4, cmk_ref[0].astype(BF16))
    mm = jnp.max(sm, axis=-1, keepdims=True)
    pm = jnp.exp(sm - mm)
    lm = jnp.sum(pm, axis=-1, keepdims=True)
    om4 = _dot_t(pm.astype(BF16), cmv_ref[0].astype(BF16)) * (1.0 / lm)
    om = jnp.zeros((t_s, M_WIDTH), F32)
    for h in range(M_HEADS):
        om = om + jnp.where(mhead == h, om4[h * t_s:(h + 1) * t_s], 0.0)
    smix_ref[:, S_QM:S_ROW] = om

    nst = STATE_PAD + st_ref.shape[1]
    uc_ref[STATE_PAD:nst, :] = st_ref[0]
    uc_ref[nst:nst + t_s, :] = srow_ref[:, S_U:S_QM]
    acc = jnp.zeros((t_s, C_WIDTH), F32)
    for w in range(CONV_W):
        acc = acc + uc_ref[STATE_PAD + w:STATE_PAD + w + t_s, :] * wdw_ref[w:w + 1, :]
    smix_ref[:, S_U:S_QM] = acc
    ns_ref[0] = uc_ref[STATE_PAD + t_s:STATE_PAD + t_s + CONV_W - 1, :]


BACK = 512
NFAR = BACK // QB - 1

def _natural_offset(p):
    p = p & (QB - 1)
    return DIL4 * (p & (CLS4 - 1)) + (p >> CLS4_SHIFT)


def _prompt_attn_kernel(kvq_ref, o16_ref, ga_ref, mix_ref, x_ref, wout_ref, npost_ref,
                        srow_ref, kvt_ref, ck_ref, cv_ref, sst_ref, cmk_ref, cmv_ref, wdw_ref,
                        y_ref, smix_ref, wk_ref, ns_ref,
                        mixa_ref, nat_ref, nat4_ref, unp_ref, uc_ref, kvp_ref, woutbf_ref, *, tq, t_s, win_len):
    i = pl.program_id(1)
    kv_refs = (kvp_ref, kvq_ref)

    @pl.when((pl.program_id(0) == 0) & (i == 0))
    def _():
        woutbf_ref[...] = wout_ref[...].astype(BF16)

    @pl.when(i == 0)
    def _():
        kvp_ref[...] = jnp.zeros(kvp_ref.shape, BF16)

    quarter = tq // DIL4
    for cl in range(DIL16):
        c4, c2 = cl % DIL4, cl // DIL4
        dst = pl.ds(c4 * quarter + c2, tq // DIL16, stride=DIL4)
        for g in range(NGRP + 1):
            nat4_ref[g, dst, :] = o16_ref[0, cl, :, g * LANES:(g + 1) * LANES]
    for c4 in range(DIL4):
        for g in range(NGRP + 1):
            nat_ref[g, pl.ds(c4, quarter, stride=DIL4), :] = nat4_ref[g, c4 * quarter:(c4 + 1) * quarter, :]

    a_n = _natural_offset(lax.broadcasted_iota(jnp.int32, (QB, 2 * QB), 0))
    c_p = lax.broadcasted_iota(jnp.int32, (QB, 2 * QB), 1)
    d = jnp.where(c_p < QB, QB, 0) + a_n - _natural_offset(c_p)
    near = (d >= 0) & (d <= QB)
    far = (d >= 0) & ((d & (DIL4 - 1)) == 0)
    bias_near0 = jnp.where(near & far, LN2, jnp.where(near | far, 0.0, NEG))
    ia = lax.broadcasted_iota(jnp.int32, (CLS4, QB), 0)
    cf = lax.broadcasted_iota(jnp.int32, (CLS4, QB), 1)
    fblk, ic = cf >> CLS4_SHIFT, cf & (CLS4 - 1)
    bias_far0 = jnp.where((fblk < NFAR) & ((fblk > 0) | (ic >= ia)), 0.0, NEG)
    lane = lax.broadcasted_iota(jnp.int32, (QB, LANES), 1)
    head_a = lane < HEAD_DIM

    def rows_of(w):
        blk, off = divmod(w, tq)
        return kv_refs[blk], off

    def group(x, c):
        return jnp.concatenate([x[c * CLS4:(c + 1) * CLS4], x[QB + c * CLS4:QB + (c + 1) * CLS4]], axis=0)

    def ungroup(parts):
        return jnp.concatenate([p_[:CLS4] for p_ in parts] + [p_[CLS4:] for p_ in parts], axis=0)

    units = [(sub, g) for sub in range(tq // QB) for g in range(NGRP)]
    ctx = {}
    for sub in range(tq // QB):
        first_valid = BACK - i * tq - sub * QB
        bias_near = jnp.where((c_p >= QB) | (NFAR * QB >= first_valid), bias_near0, NEG)
        bias_far = jnp.where(fblk * QB >= first_valid, bias_far0, NEG)
        far_blocks = [rows_of((sub + j) * QB) for j in range(NFAR)]
        ctx[sub] = dict(bias_near=jnp.concatenate([bias_near, bias_near], axis=0),
                        bias_far=jnp.concatenate([bias_far, bias_far], axis=0),
                        prev=rows_of((sub + NFAR) * QB), cur=rows_of((sub + NFAR + 1) * QB),
                        far=far_blocks + far_blocks[:1])

    def far_rows(sub, c, colsl):
        return jnp.concatenate([r[0, off + c * CLS4:off + (c + 1) * CLS4, colsl] for r, off in ctx[sub]["far"]],
                               axis=0)

    def block(sub, which, colsl):
        r, off = ctx[sub][which]
        return r[0, off:off + QB, colsl]

    scores, probs, outs = {}, {}, {}

    def stage_scores(u):
        sub, g = u
        kc = slice(P_K + g * LANES, P_K + (g + 1) * LANES)
        qc = slice(P_Q + g * LANES, P_Q + (g + 1) * LANES)
        q2 = _pair_split(kvq_ref[0, sub * QB:(sub + 1) * QB, qc].astype(F32)).astype(BF16)
        s_far = ungroup([_dot_t(group(q2, c), far_rows(sub, c, kc)) + ctx[sub]["bias_far"] for c in range(DIL4)])
        s_near = jnp.concatenate([_dot_t(q2, block(sub, "prev", kc)), _dot_t(q2, block(sub, "cur", kc))],
                                 axis=1) + ctx[sub]["bias_near"]
        scores[u] = jnp.concatenate([s_far, s_near], axis=1)

    def stage_softmax(u):
        s = scores.pop(u)
        m = jnp.max(s, axis=-1, keepdims=True)
        p = jnp.exp(s - m)
        probs[u] = (m, jnp.sum(p, axis=-1, keepdims=True), p.astype(BF16))

    def stage_values(u):
        sub, g = u
        vc = slice(P_V + g * LANES, P_V + (g + 1) * LANES)
        pb = probs[u][2]
        outs[u] = (_dot(pb[:, QB:2 * QB], block(sub, "prev", vc)) + _dot(pb[:, 2 * QB:], block(sub, "cur", vc))
                   + ungroup([_dot(group(pb[:, :QB], c), far_rows(sub, c, vc)) for c in range(DIL4)]))

    def stage_merge(u):
        sub, g = u
        rows = slice(sub * QB, (sub + 1) * QB)
        cols = slice(g * LANES, (g + 1) * LANES)
        m, l, _ = probs.pop(u)
        o_s = _pair_join(outs.pop(u))
        m_s = jnp.where(head_a, m[:QB], m[QB:])
        l_s = jnp.where(head_a, l[:QB], l[QB:])
        for c in range(DIL4):
            dst = pl.ds(c, CLS4, stride=DIL4)
            unp_ref[3 * g, dst, :] = o_s[c * CLS4:(c + 1) * CLS4]
            unp_ref[3 * g + 1, dst, :] = m_s[c * CLS4:(c + 1) * CLS4]
            unp_ref[3 * g + 2, dst, :] = l_s[c * CLS4:(c + 1) * CLS4]
        o_n, m_n, l_n = unp_ref[3 * g], unp_ref[3 * g + 1], unp_ref[3 * g + 2]
        st = nat_ref[NGRP, rows, :]
        m_f = jnp.where(head_a, st[:, 2 * g:2 * g + 1], st[:, 2 * g + 1:2 * g + 2])
        l_f = jnp.where(head_a, st[:, STAT_L + 2 * g:STAT_L + 2 * g + 1],
                        st[:, STAT_L + 2 * g + 1:STAT_L + 2 * g + 2])
        mx = jnp.maximum(m_n, m_f)
        w_n = jnp.exp(m_n - mx)
        w_f = jnp.exp(m_f - mx)
        oa = (o_n * w_n + nat_ref[g, rows, :] * w_f) / (l_n * w_n + l_f * w_f)
        mixa_ref[rows, cols] = (oa * ga_ref[0, rows, cols]).astype(BF16)

    def finish(rows):
        z = (_dot(mixa_ref[rows, :], woutbf_ref[0:A_WIDTH, :])
             + _dot(mix_ref[0, rows, :], woutbf_ref[A_WIDTH:, :]))
        y_ref[0, rows, :] = x_ref[0, rows, :] + _rmsnorm(z, npost_ref[...])

    stages = (stage_scores, stage_softmax, stage_values, stage_merge)
    half = len(units) // 2
    for t in range(len(units) + len(stages) - 1):
        for depth, stage in enumerate(stages):
            if 0 <= t - depth < len(units):
                stage(units[t - depth])
        if t - (len(stages) - 1) == half - 1:
            finish(slice(0, tq // 2))
    finish(slice(tq // 2, tq))
    kvp_ref[...] = kvq_ref[:, :, 0:P_Q]

    for part in _sample_mix_stages(pl.program_id(0) * pl.num_programs(1) + i, srow_ref, kvt_ref, ck_ref, cv_ref,
                                   sst_ref, cmk_ref, cmv_ref, wdw_ref, smix_ref, wk_ref, ns_ref, uc_ref,
                                   t_s=t_s, win_len=win_len):
        part()


def _prompt_attn(kvq, o16, ga, mix, x, wout, npost, srow, kvt, ck, cv, st, cmk, cmv, wdw, tq, t_s):
    b, s, _ = x.shape
    nsteps = s // tq
    nb, _, win_len = ck.shape
    assert tq == BACK and b * nsteps == nb, "one sample batch per grid step"
    row = lambda w: pl.BlockSpec((1, tq, w), lambda bi, i: (bi, i, 0))
    cls = pl.BlockSpec((1, DIL16, tq // DIL16, O16_W), lambda bi, i: (bi, 0, i, 0))
    srows = pl.BlockSpec((t_s, S_ROW), lambda bi, i: (bi * nsteps + i, 0))
    per = lambda a: pl.BlockSpec((1,) + a.shape[1:], lambda bi, i: (bi * nsteps + i, 0, 0))
    return pl.pallas_call(
        functools.partial(_prompt_attn_kernel, tq=tq, t_s=t_s, win_len=win_len),
        grid=(b, nsteps),
        in_specs=[row(kvq.shape[2]), cls, row(A_WIDTH), row(C_WIDTH + M_WIDTH), row(D_MODEL),
                  _resident(wout), _full(npost),
                  srows, _full(kvt), per(ck), per(cv), per(st), per(cmk), per(cmv), _full(wdw)],
        out_specs=[row(D_MODEL), srows, per(ck),
                   pl.BlockSpec((1, CONV_W - 1, C_WIDTH), lambda bi, i: (bi * nsteps + i, 0, 0))],
        out_shape=[jax.ShapeDtypeStruct((b, s, D_MODEL), F32),
                   jax.ShapeDtypeStruct((nb * t_s, S_ROW), F32),
                   jax.ShapeDtypeStruct(ck.shape, F32),
                   jax.ShapeDtypeStruct((nb, CONV_W - 1, C_WIDTH), F32)],
        scratch_shapes=[pltpu.VMEM((tq, A_WIDTH), BF16),
                        pltpu.VMEM((NGRP + 1, tq, LANES), F32),
                        pltpu.VMEM((NGRP + 1, tq, LANES), F32),
                        pltpu.VMEM((3 * NGRP, QB, LANES), F32),
                        pltpu.VMEM((STATE_PAD + CONV_W - 1 + t_s, C_WIDTH), F32),
                        pltpu.VMEM((1, tq, 2 * A_WIDTH), BF16),
                        pltpu.VMEM(wout.shape, BF16)],
        compiler_params=pltpu.CompilerParams(dimension_semantics=("arbitrary", "arbitrary"),
                                             vmem_limit_bytes=VMEM_LIMIT),
        name="prompt_attn",
    )(kvq, o16, ga, mix, x, wout, npost, srow, kvt, ck, cv, st, cmk, cmv, wdw)


def _sample_proj_kernel(x_ref, npre_ref, win_ref, ang_ref, mem_ref, nmem_ref, wmem_ref,
                        srow_ref, kvt_ref, gate_ref, wbf_ref, mk_ref, mv_ref, mkb_ref, mvb_ref):
    _mem_kv(mem_ref, nmem_ref, wmem_ref, mk_ref, mv_ref, mkb_ref, mvb_ref)
    h = _rmsnorm(x_ref[...], npre_ref[...]).astype(BF16)
    wbf_ref[...] = win_ref[...].astype(BF16)
    proj = _dot(h, wbf_ref[...])
    pats = _rope_patterns(ang_ref[:, 0:LANES], ang_ref[:, LANES:])
    srow_ref[:, S_Q:S_U] = _rope(proj[:, O_QA:O_KA], pats) * SCALE
    srow_ref[:, S_U:S_QM] = proj[:, O_AB:O_BB] * _sigmoid(proj[:, O_BB:O_GB])
    srow_ref[:, S_QM:S_ROW] = proj[:, O_QM:O_GM]
    kv_t = jnp.concatenate([_rope(proj[:, O_KA:O_VA], pats), proj[:, O_VA:O_GA]], axis=1).T
    for slab in range(kvt_ref.shape[0]):
        kvt_ref[slab] = kv_t[:, slab * LANES:(slab + 1) * LANES]
    gate_ref[:, S_Q:S_U] = _silu(proj[:, O_GA:O_AB])
    gate_ref[:, S_U:S_QM] = _silu(proj[:, O_GB:O_QM])
    gate_ref[:, S_QM:S_ROW] = _silu(proj[:, O_GM:D_IN])


def _sample_proj(x, npre, win, ang, mem, nmem, wmem):
    n = x.shape[0]
    assert n % LANES == 0
    mshape = mem.shape[:2] + (M_WIDTH,)
    shapes = [(n, S_ROW), (n // LANES, 2 * A_WIDTH, LANES), (n, S_ROW), win.shape] + [mshape] * 4
    dtypes = [F32, F32, F32, BF16, F32, F32, BF16, BF16]
    args = (x, npre, win, ang, mem, nmem, wmem)
    return pl.pallas_call(
        _sample_proj_kernel,
        grid=(1,),
        in_specs=[_full(a) for a in args],
        out_specs=[pl.BlockSpec(sh, lambda i, nd=len(sh): (0,) * nd) for sh in shapes],
        out_shape=[jax.ShapeDtypeStruct(sh, dt) for sh, dt in zip(shapes, dtypes)],
        compiler_params=pltpu.CompilerParams(vmem_limit_bytes=VMEM_LIMIT),
        name="sample_proj",
    )(*args)


def _sample_out_kernel(x_ref, smix_ref, gate_ref, bdw_ref, lng_ref, lnb_ref, wpw_ref, bpw_ref, wout_ref, npost_ref,
                       y_ref):
    mixed_a = (smix_ref[:, S_Q:S_U] * gate_ref[:, S_Q:S_U]).astype(BF16)
    mixed_b = _conformer_tail(smix_ref[:, S_U:S_QM], gate_ref[:, S_U:S_QM], bdw_ref[...], lng_ref[...], lnb_ref[...],
                              wpw_ref, bpw_ref[...]).astype(BF16)
    mixed_m = (smix_ref[:, S_QM:S_ROW] * gate_ref[:, S_QM:S_ROW]).astype(BF16)
    wout = wout_ref[...].astype(BF16)
    z = (_dot(mixed_a, wout[0:A_WIDTH]) + _dot(mixed_b, wout[A_WIDTH:A_WIDTH + C_WIDTH])
         + _dot(mixed_m, wout[A_WIDTH + C_WIDTH:]))
    y_ref[...] = x_ref[...] + _rmsnorm(z, npost_ref[...])


def _sample_out(x, smix, gate, bdw, lng, lnb, wpw, bpw, wout, npost):
    args = (x, smix, gate, bdw, lng, lnb, wpw, bpw, wout, npost)
    return pl.pallas_call(
        _sample_out_kernel,
        grid=(1,),
        in_specs=[_full(a) for a in args],
        out_specs=_full(x),
        out_shape=jax.ShapeDtypeStruct(x.shape, F32),
        compiler_params=pltpu.CompilerParams(vmem_limit_bytes=VMEM_LIMIT),
        name="sample_out",
    )(*args)


def _rope_cos_sin(pos):
    inv = ROPE_THETA ** (-jnp.arange(0, ROT_DIM, 2, dtype=F32) / ROT_DIM)
    ang = pos.astype(F32)[:, None] * jnp.tile(inv, 2 * LANES // ROT_DIM)[None, :]
    return jnp.concatenate([jnp.cos(ang), jnp.sin(ang)], axis=1)


def _feature_major(cache):
    nb, rows, heads, dim = cache.shape
    return jnp.transpose(cache, (0, 2, 3, 1)).reshape(nb, heads * dim, rows)


def _row_major(cache_t, heads):
    nb, width, rows = cache_t.shape
    return jnp.transpose(cache_t.reshape(nb, heads, width // heads, rows), (0, 3, 1, 2))


PROJ_TILE = 512
ATTN_TILE = BACK


def kernel(x_prompt, x_sample, cache_win_k, cache_win_v, state_conv, cache_mem_k, cache_mem_v, mem_prompt,
           norm_pre, norm_post, w_in, w_out, norm_mem, w_mem_kv, w_dw, b_dw, ln_conv_g, ln_conv_b, w_pw2, b_pw2):
    depth = w_in.shape[0]
    assert depth == 1, "single-layer step"
    b, s, _ = x_prompt.shape
    nb, t_s, _ = x_sample.shape
    win_len = cache_win_k.shape[2]
    assert s % (DIL16 * QB) == 0 and s % PROJ_TILE == 0 and s % ATTN_TILE == 0
    assert win_len == MAX_WINDOW and win_len % LANES == 0
    assert t_s % 8 == 0 and t_s & (t_s - 1) == 0 and t_s < LANES
    l = 0
    row = lambda a: a[l][None, :]
    npre, npost, nmem = row(norm_pre), row(norm_post), row(norm_mem)
    bdw, lng, lnb, bpw = row(b_dw), row(ln_conv_g), row(ln_conv_b), row(b_pw2)
    win, wout, wmem, wpw = w_in[l], w_out[l], w_mem_kv[l], w_pw2[l]
    wdw = w_dw[l]

    pos_s = PAST_LEN + jnp.arange(t_s, dtype=jnp.int32)
    xs = x_sample.reshape(nb * t_s, D_MODEL)
    srow, kvt, gate_s, win_bf, mk, mv, mkb, mvb = _sample_proj(
        xs, npre, win, jnp.tile(_rope_cos_sin(pos_s), (nb, 1)), mem_prompt, nmem, wmem)

    rin = _rope_cos_sin(jnp.arange(PROJ_TILE, dtype=jnp.int32))
    rbase = _rope_cos_sin(jnp.arange(0, s, PROJ_TILE, dtype=jnp.int32))[:, None, :]
    ck, cv = _feature_major(cache_win_k[l]), _feature_major(cache_win_v[l])
    kvq, kvq16, kf, vf, ga, mix, ust, wv = _prompt_proj(x_prompt, npre, win_bf, rin, rbase, mkb, mvb, wdw, bdw,
                                                        lng, lnb, wpw, bpw, kvt, cv, PROJ_TILE, t_s)
    o16 = _dilated16(kvq16)
    y_prompt, smix, wk, nst = _prompt_attn(
        kvq, o16, ga, mix, x_prompt, wout, npost, srow, kvt, ck, cv, state_conv[l],
        _feature_major(cache_mem_k[l]), _feature_major(cache_mem_v[l]), wdw, ATTN_TILE, t_s)
    y_sample = _sample_out(xs, smix, gate_s, bdw, lng, lnb, wpw, bpw, wout, npost)

    keep_p = kf.shape[1]
    return (y_prompt,
            y_sample.reshape(nb, t_s, D_MODEL),
            kf.reshape(1, b, keep_p, A_HEADS, HEAD_DIM),
            vf.reshape(1, b, keep_p, A_HEADS, HEAD_DIM),
            ust[:, CONV_HIST - (CONV_W - 1):, :][None],
            mk.reshape(1, b, N_MEM, M_HEADS, HEAD_DIM),
            mv.reshape(1, b, N_MEM, M_HEADS, HEAD_DIM),
            _row_major(wk, A_HEADS)[None],
            _row_major(wv, A_HEADS)[None],
            nst[None])
```

```python
import functools
import math

import jax
import jax.numpy as jnp
from jax import lax
from jax.experimental import pallas as pl
from jax.experimental.pallas import tpu as pltpu

F32 = jnp.float32
BF16 = jnp.bfloat16

D_MODEL = 1024
HEAD_DIM = 64
HEAD_SHIFT = 6
A_WIDTH = 384
A_HEADS = 6
M_WIDTH = 256
M_HEADS = 4
C_WIDTH = 384
ROT_DIM = 16
ROPE_THETA = 500000.0
CONV_W = 31
N_MEM = 256
MAX_WINDOW = 2048
PAST_LEN = 16384
EPS = 1e-6
SCALE = HEAD_DIM ** -0.5
NEG = -1e30
LN2 = math.log(2.0)

LANES = 128
NGRP = A_WIDTH // LANES
QB = 128
DIL16 = 16
DIL4 = 4
CLS4 = QB // DIL4
CLS4_SHIFT = 5
O_QA, O_KA, O_VA, O_GA, O_AB, O_BB, O_GB, O_QM, O_GM, D_IN = 0, 384, 768, 1152, 1536, 1920, 2304, 2688, 2944, 3200
P_K, P_V, P_Q, P_W = 0, A_WIDTH, 2 * A_WIDTH, 3 * A_WIDTH

VMEM_LIMIT = 60 * 1024 * 1024


def _sigmoid(x):
    return 1.0 / (1.0 + jnp.exp(-x))


def _silu(x):
    return x * _sigmoid(x)


def _rmsnorm(x, g):
    return x * lax.rsqrt(jnp.mean(x * x, axis=-1, keepdims=True) + EPS) * g


def _dot(a, b):
    return jnp.dot(a, b, preferred_element_type=F32)


def _dot_t(a, b):
    return lax.dot_general(a, b, (((1,), (1,)), ((), ())), preferred_element_type=F32)


def _rope_patterns(cos, sin):
    in_head = lax.broadcasted_iota(jnp.int32, cos.shape, 1) & (HEAD_DIM - 1)
    rot, lo = in_head < ROT_DIM, in_head < ROT_DIM // 2
    return jnp.where(rot, cos, 1.0), jnp.where(lo, -sin, 0.0), jnp.where(rot & ~lo, sin, 0.0)


def _rope(xw, pats):
    cos, s1, s2 = pats
    outs = []
    for g in range(xw.shape[1] // LANES):
        xg = xw[:, g * LANES:(g + 1) * LANES]
        outs.append(xg * cos + pltpu.roll(xg, LANES - 8, 1) * s1 + pltpu.roll(xg, 8, 1) * s2)
    return jnp.concatenate(outs, axis=1)


def _pair_split(q):
    lane = lax.broadcasted_iota(jnp.int32, q.shape, 1)
    qa = jnp.where(lane < HEAD_DIM, q, 0.0)
    qb = jnp.where(lane >= HEAD_DIM, q, 0.0)
    return jnp.concatenate([qa, qb], axis=0)


def _pair_join(x):
    t = x.shape[0] // 2
    lane = lax.broadcasted_iota(jnp.int32, (t, LANES), 1)
    return jnp.where(lane < HEAD_DIM, x[:t], x[t:])


def _mem_attend(qm, mk_ref, mv_ref):
    outs = []
    for g in range(M_WIDTH // LANES):
        cols = slice(g * LANES, (g + 1) * LANES)
        q2 = _pair_split(qm[:, cols] * SCALE).astype(BF16)
        s = _dot_t(q2, mk_ref[0, :, cols])
        m = jnp.max(s, axis=-1, keepdims=True)
        p = jnp.exp(s - m)
        l = jnp.sum(p, axis=-1, keepdims=True)
        o = _dot(p.astype(BF16), mv_ref[0, :, cols]) * (1.0 / l)
        outs.append(_pair_join(o))
    return jnp.concatenate(outs, axis=1)


def _conformer_tail(c, gate_b, bdw, lng, lnb, wpw_ref, bpw):
    cf = c + bdw
    mu = jnp.mean(cf, axis=-1, keepdims=True)
    dev = cf - mu
    var = jnp.mean(dev * dev, axis=-1, keepdims=True)
    cn = dev * lax.rsqrt(var + EPS) * lng + lnb
    ob = _dot(_silu(cn).astype(BF16), wpw_ref[...].astype(BF16)) + bpw
    return ob * gate_b


def _full(a):
    return pl.BlockSpec(a.shape, lambda *_: (0,) * a.ndim)


def _resident(a):
    return pl.BlockSpec(a.shape, lambda *_: (0,) * a.ndim, pipeline_mode=pl.Buffered(1))


def _mem_kv_kernel(mem_ref, g_ref, w_ref, mk_ref, mv_ref, mkb_ref, mvb_ref):
    h = _rmsnorm(mem_ref[0], g_ref[...]).astype(BF16)
    kv = _dot(h, w_ref[...].astype(BF16))
    mk_ref[0] = kv[:, :M_WIDTH]
    mv_ref[0] = kv[:, M_WIDTH:]
    mkb_ref[0] = kv[:, :M_WIDTH].astype(BF16)
    mvb_ref[0] = kv[:, M_WIDTH:].astype(BF16)


def _mem_kv(mem, g, wmem):
    b = mem.shape[0]
    blk = pl.BlockSpec((1, N_MEM, M_WIDTH), lambda i: (i, 0, 0))
    return pl.pallas_call(
        _mem_kv_kernel,
        grid=(b,),
        in_specs=[pl.BlockSpec((1, N_MEM, D_MODEL), lambda i: (i, 0, 0)), _full(g), _full(wmem)],
        out_specs=[blk, blk, blk, blk],
        out_shape=[jax.ShapeDtypeStruct((b, N_MEM, M_WIDTH), F32)] * 2
        + [jax.ShapeDtypeStruct((b, N_MEM, M_WIDTH), BF16)] * 2,
        name="mem_kv",
    )(mem, g, wmem)


CONV_HIST = 32
CONV_CHUNK = 64


def _prompt_proj_kernel(x_ref, npre_ref, win_ref, rin_ref, rbase_ref, mk_ref, mv_ref,
                        wdw_ref, bdw_ref, lng_ref, lnb_ref, wpw_ref, bpw_ref, kvt_ref, cv_ref,
                        kvq_ref, kvq16_ref, kf_ref, vf_ref, ga_ref, mix_ref, ust_ref, wv_ref,
                        uext_ref, ush_ref, conv_ref, stage_ref, stage4_ref, *, t_s, win_len):
    t = x_ref.shape[1]
    i = pl.program_id(1)

    @pl.when(i == 0)
    def _():
        uext_ref[0:CONV_HIST, :] = jnp.zeros((CONV_HIST, C_WIDTH), F32)

    h = _rmsnorm(x_ref[0], npre_ref[...]).astype(BF16)
    proj = {}

    def project(name, c0, c1):
        def run():
            proj[name] = _dot(h, win_ref[:, c0:c1])
        return run

    def stage(idx, val):
        for g in range(NGRP):
            stage_ref[NGRP * idx + g] = val[:, g * LANES:(g + 1) * LANES]

    pats = []

    def rope_pats():
        if not pats:
            cr, sr = rin_ref[:, 0:LANES], rin_ref[:, LANES:]
            cb, sb = rbase_ref[i, :, 0:LANES], rbase_ref[i, :, LANES:]
            pats.append(_rope_patterns(cb * cr - sb * sr, sb * cr + cb * sr))
        return pats[0]

    def do_k():
        k = _rope(proj.pop("k"), rope_pats())
        kf_ref[0] = k
        stage(0, k)

    def do_q():
        stage(2, _rope(proj.pop("q"), rope_pats()) * SCALE)

    def do_v():
        v = proj.pop("v")
        vf_ref[0] = v
        stage(1, v)

    def do_ga():
        ga_ref[0] = _silu(proj.pop("ga"))

    def do_copies():
        quarter = t // DIL4
        for c4 in range(DIL4):
            for s in range(3 * NGRP):
                stage4_ref[s, c4 * quarter:(c4 + 1) * quarter, :] = stage_ref[s, pl.ds(c4, quarter, stride=DIL4), :]
        for c in range(DIL16):
            c4, c2 = c % DIL4, c // DIL4
            rows_c = [stage4_ref[s, pl.ds(c4 * quarter + c2, t // DIL16, stride=DIL4), :] for s in range(3 * NGRP)]
            kvq16_ref[0, c] = jnp.concatenate(rows_c, axis=1).astype(BF16)
        for blk in range(t // QB):
            for c4 in range(DIL4):
                src = slice(c4 * quarter + blk * CLS4, c4 * quarter + (blk + 1) * CLS4)
                rows_c = [stage4_ref[s, src, :] for s in range(3 * NGRP)]
                kvq_ref[0, blk * QB + c4 * CLS4:blk * QB + (c4 + 1) * CLS4, :] = (
                    jnp.concatenate(rows_c, axis=1).astype(BF16))

    def do_glu():
        uext_ref[CONV_HIST:CONV_HIST + t, :] = proj.pop("ab") * _sigmoid(proj.pop("bb"))
        uext = uext_ref[...]
        for r in range(1, 8):
            ush_ref[r - 1] = pltpu.roll(uext, r, 0)

    def conv_rows(r0):
        def run():
            acc = jnp.zeros((CONV_CHUNK, C_WIDTH), F32)
            for kk in range(CONV_W):
                tap = wdw_ref[CONV_W - 1 - kk:CONV_W - kk, :]
                r, base = kk % 8, CONV_HIST + r0 - (kk - kk % 8)
                if r == 0:
                    slab = uext_ref[base:base + CONV_CHUNK, :]
                else:
                    slab = ush_ref[r - 1, base:base + CONV_CHUNK, :]
                acc = acc + slab * tap
            conv_ref[r0:r0 + CONV_CHUNK, :] = acc
        return run

    def do_history():
        tail = uext_ref[t:t + CONV_HIST, :]
        uext_ref[0:CONV_HIST, :] = tail
        ust_ref[0] = tail

    def do_tail():
        mixed_b = _conformer_tail(conv_ref[...], _silu(proj.pop("gb")), bdw_ref[...], lng_ref[...], lnb_ref[...],
                                  wpw_ref, bpw_ref[...])
        mix_ref[0, :, 0:C_WIDTH] = mixed_b.astype(BF16)

    def do_mem():
        mixed_m = _mem_attend(proj.pop("qm"), mk_ref, mv_ref) * _silu(proj.pop("gm"))
        mix_ref[0, :, C_WIDTH:] = mixed_m.astype(BF16)

    convs = [conv_rows(r0) for r0 in range(0, t, CONV_CHUNK)]
    riders = [([project("k", O_KA, O_VA)], do_k), ([project("q", O_QA, O_KA)], do_q),
              ([project("v", O_VA, O_GA)], do_v), ([project("ga", O_GA, O_AB)], do_ga),
              ([project("qm", O_QM, O_GM), project("gm", O_GM, D_IN)], do_mem),
              ([project("gb", O_GB, O_QM)], do_copies)]
    slide_v = functools.partial(_slide_cache, pl.program_id(0) * pl.num_programs(1) + i, kvt_ref, cv_ref, A_WIDTH,
                                wv_ref, t_s=t_s, win_len=win_len)
    program = [project("ab", O_AB, O_BB), project("bb", O_BB, O_GB), slide_v, do_glu]
    for n, conv in enumerate(convs):
        if n < len(riders):
            program += riders[n][0]
        program.append(conv)
        if n < len(riders) and riders[n][1] is not None:
            program.append(riders[n][1])
    assert len(convs) >= len(riders)
    program += [do_history, do_tail]
    for piece in program:
        piece()
    assert not proj


def _prompt_proj(x, npre, win, rin, rbase, mkb, mvb, wdw, bdw, lng, lnb, wpw, bpw, kvt, cv, tile, t_s):
    b, s, _ = x.shape
    nt = s // tile
    nb, _, win_len = cv.shape
    assert b * nt == nb, "one sample batch per grid step"
    per_batch = pl.BlockSpec((1,) + cv.shape[1:], lambda bi, i: (bi * nt + i, 0, 0))
    keep = min(MAX_WINDOW, s)
    first_keep = (s - keep) // tile
    row = lambda w: pl.BlockSpec((1, tile, w), lambda bi, i: (bi, i, 0))
    memb = pl.BlockSpec((1, N_MEM, M_WIDTH), lambda bi, i: (bi, 0, 0))
    keepb = pl.BlockSpec((1, tile, A_WIDTH), lambda bi, i: (bi, jnp.maximum(i - first_keep, 0), 0))
    cls = pl.BlockSpec((1, DIL16, tile // DIL16, P_W), lambda bi, i: (bi, 0, i, 0))
    return pl.pallas_call(
        functools.partial(_prompt_proj_kernel, t_s=t_s, win_len=win_len),
        grid=(b, nt),
        in_specs=[row(D_MODEL), _full(npre), _resident(win), _full(rin), _full(rbase),
                  memb, memb, _full(wdw), _full(bdw), _full(lng), _full(lnb), _full(wpw), _full(bpw),
                  _full(kvt), per_batch],
        out_specs=[row(P_W), cls, keepb, keepb, row(A_WIDTH), row(C_WIDTH + M_WIDTH),
                   pl.BlockSpec((1, CONV_HIST, C_WIDTH), lambda bi, i: (bi, 0, 0)), per_batch],
        out_shape=[jax.ShapeDtypeStruct((b, s, P_W), BF16),
                   jax.ShapeDtypeStruct((b, DIL16, s // DIL16, P_W), BF16),
                   jax.ShapeDtypeStruct((b, keep, A_WIDTH), F32),
                   jax.ShapeDtypeStruct((b, keep, A_WIDTH), F32),
                   jax.ShapeDtypeStruct((b, s, A_WIDTH), F32),
                   jax.ShapeDtypeStruct((b, s, C_WIDTH + M_WIDTH), BF16),
                   jax.ShapeDtypeStruct((b, CONV_HIST, C_WIDTH), F32),
                   jax.ShapeDtypeStruct(cv.shape, F32)],
        scratch_shapes=[pltpu.VMEM((CONV_HIST + tile, C_WIDTH), F32),
                        pltpu.VMEM((7, CONV_HIST + tile, C_WIDTH), F32),
                        pltpu.VMEM((tile, C_WIDTH), F32),
                        pltpu.VMEM((3 * NGRP, tile, LANES), F32),
                        pltpu.VMEM((3 * NGRP, tile, LANES), F32)],
        compiler_params=pltpu.CompilerParams(dimension_semantics=("arbitrary", "arbitrary"),
                                             vmem_limit_bytes=VMEM_LIMIT),
        name="prompt_proj",
    )(x, npre, win, rin, rbase, mkb, mvb, wdw, bdw, lng, lnb, wpw, bpw, kvt, cv)


STAT_L = 8
O16_W = A_WIDTH + LANES


CLS_PER_STEP = 8


def _dilated16_kernel(kvq_ref, o_ref):
    a = lax.broadcasted_iota(jnp.int32, (QB, QB), 0)
    c = lax.broadcasted_iota(jnp.int32, (QB, QB), 1)
    tri_prev = jnp.where(c >= a, 0.0, NEG)
    tri_cur = jnp.where(c <= a, 0.0, NEG)
    bias_cur = jnp.concatenate([tri_cur, tri_cur], axis=0)
    both = jnp.concatenate([tri_prev, tri_cur], axis=1)
    bias_both = jnp.concatenate([both, both], axis=0)
    lane = lax.broadcasted_iota(jnp.int32, (QB, LANES), 1)
    ncls, nsub = kvq_ref.shape[1], kvq_ref.shape[2] // QB
    units = [(cl, sub, g) for cl in range(ncls) for sub in range(nsub) for g in range(NGRP)]
    scores, probs = {}, {}
    stats = {(cl, sub): jnp.zeros((QB, LANES), F32) for cl in range(ncls) for sub in range(nsub)}

    def rows_of(sub):
        return slice(sub * QB, (sub + 1) * QB)

    def stage_scores(u):
        cl, sub, g = u
        kc = slice(P_K + g * LANES, P_K + (g + 1) * LANES)
        qc = slice(P_Q + g * LANES, P_Q + (g + 1) * LANES)
        q2 = _pair_split(kvq_ref[0, cl, rows_of(sub), qc].astype(F32)).astype(BF16)
        s_cur = _dot_t(q2, kvq_ref[0, cl, rows_of(sub), kc])
        if sub == 0:
            scores[u] = s_cur + bias_cur
        else:
            scores[u] = jnp.concatenate([_dot_t(q2, kvq_ref[0, cl, rows_of(sub - 1), kc]), s_cur], axis=1) + bias_both

    def stage_softmax(u):
        s = scores.pop(u)
        m = jnp.max(s, axis=-1, keepdims=True)
        p = jnp.exp(s - m)
        probs[u] = (m, jnp.sum(p, axis=-1, keepdims=True), p.astype(BF16))

    def stage_values(u):
        cl, sub, g = u
        vc = slice(P_V + g * LANES, P_V + (g + 1) * LANES)
        m, l, pb = probs.pop(u)
        if sub == 0:
            o = _dot(pb, kvq_ref[0, cl, rows_of(sub), vc])
        else:
            o = (_dot(pb[:, :QB], kvq_ref[0, cl, rows_of(sub - 1), vc])
                 + _dot(pb[:, QB:], kvq_ref[0, cl, rows_of(sub), vc]))
        o_ref[0, cl, rows_of(sub), g * LANES:(g + 1) * LANES] = _pair_join(o)
        st = stats[cl, sub]
        st = jnp.where(lane == 2 * g, m[:QB], st)
        st = jnp.where(lane == 2 * g + 1, m[QB:], st)
        st = jnp.where(lane == STAT_L + 2 * g, l[:QB], st)
        stats[cl, sub] = jnp.where(lane == STAT_L + 2 * g + 1, l[QB:], st)
        if g == NGRP - 1:
            o_ref[0, cl, rows_of(sub), A_WIDTH:] = stats.pop((cl, sub))

    stages = (stage_scores, stage_softmax, stage_values)
    for t in range(len(units) + len(stages) - 1):
        for depth, stage in enumerate(stages):
            if 0 <= t - depth < len(units):
                stage(units[t - depth])


def _dilated16(kvq16):
    b, ncls, nj, w = kvq16.shape
    return pl.pallas_call(
        _dilated16_kernel,
        grid=(b, ncls // CLS_PER_STEP),
        in_specs=[pl.BlockSpec((1, CLS_PER_STEP, nj, w), lambda bi, c: (bi, c, 0, 0))],
        out_specs=pl.BlockSpec((1, CLS_PER_STEP, nj, O16_W), lambda bi, c: (bi, c, 0, 0)),
        out_shape=jax.ShapeDtypeStruct((b, ncls, nj, O16_W), F32),
        compiler_params=pltpu.CompilerParams(dimension_semantics=("arbitrary",) * 2,
                                             vmem_limit_bytes=VMEM_LIMIT),
        name="dilated16",
    )(kvq16)


STATE_PAD = 2
S_Q, S_U, S_QM, S_ROW = 0, A_WIDTH, A_WIDTH + C_WIDTH, A_WIDTH + C_WIDTH + M_WIDTH


def _new_rows(batch, kvt_ref, lo, hi, t_s):
    per_slab = LANES // t_s
    slab, pos = batch // per_slab, batch % per_slab
    return pltpu.roll(kvt_ref[slab, lo:hi, :], LANES - t_s - pos * t_s, 1)


def _slide_cache(batch, kvt_ref, src_ref, lo, dst_ref, *, t_s, win_len):
    new0 = LANES - t_s
    keep = lax.broadcasted_iota(jnp.int32, (A_WIDTH, LANES), 1) < new0
    nch = win_len // LANES
    nxt = pltpu.roll(src_ref[0, :, 0:LANES], new0, 1)
    for c in range(nch):
        cur = nxt
        nxt = (pltpu.roll(src_ref[0, :, (c + 1) * LANES:(c + 2) * LANES], new0, 1) if c + 1 < nch
               else _new_rows(batch, kvt_ref, lo, lo + A_WIDTH, t_s))
        dst_ref[0, :, c * LANES:(c + 1) * LANES] = jnp.where(keep, cur, nxt)


def _sample_mix_stages(batch, srow_ref, kvt_ref, ck_ref, cv_ref, st_ref, cmk_ref, cmv_ref, wdw_ref,
                       smix_ref, wk_ref, ns_ref, uc_ref, *, t_s, win_len):
    new_rows = functools.partial(_new_rows, batch, kvt_ref, t_s=t_s)
    return [functools.partial(_slide_cache, batch, kvt_ref, ck_ref, 0, wk_ref, t_s=t_s, win_len=win_len),
            functools.partial(_sample_attend, new_rows, srow_ref, ck_ref, cv_ref, smix_ref, t_s=t_s, win_len=win_len),
            functools.partial(_sample_mem_conv, srow_ref, st_ref, cmk_ref, cmv_ref, wdw_ref, smix_ref, ns_ref,
                              uc_ref, t_s=t_s)]


def _sample_attend(new_rows, srow_ref, ck_ref, cv_ref, smix_ref, *, t_s, win_len):
    new0 = LANES - t_s
    knt, vnt = new_rows(0, A_WIDTH), new_rows(A_WIDTH, 2 * A_WIDTH)
    q = srow_ref[:, S_Q:S_U]
    head_of_lane = lax.broadcasted_iota(jnp.int32, (t_s, A_WIDTH), 1) >> HEAD_SHIFT
    q6 = jnp.concatenate([jnp.where(head_of_lane == h, q, 0.0) for h in range(A_HEADS)], axis=0).astype(BF16)
    s_c = _dot(q6, ck_ref[0].astype(BF16))
    s_n = _dot(q6, knt.astype(BF16))

    def weights(shape, key0, lo):
        tq = lax.broadcasted_iota(jnp.int32, shape, 0) & (t_s - 1)
        key = lax.broadcasted_iota(jnp.int32, shape, 1) + key0
        d = win_len + tq - key
        ok = (d >= 0) & (key >= lo)
        w = ((d <= 128).astype(F32) + ((d <= 512) & ((d & 3) == 0)).astype(F32)
             + ((d <= 2048) & ((d & 15) == 0)).astype(F32))
        return jnp.where(ok, w, 0.0)

    w_c = weights(s_c.shape, 0, 0)
    w_n = weights(s_n.shape, win_len - new0, win_len)
    s_c = jnp.where(w_c > 0.0, s_c, NEG)
    s_n = jnp.where(w_n > 0.0, s_n, NEG)
    m = jnp.maximum(jnp.max(s_c, axis=-1, keepdims=True), jnp.max(s_n, axis=-1, keepdims=True))
    p_c = jnp.exp(s_c - m) * w_c
    p_n = jnp.exp(s_n - m) * w_n
    l = jnp.sum(p_c, axis=-1, keepdims=True) + jnp.sum(p_n, axis=-1, keepdims=True)
    o = (_dot_t(p_c.astype(BF16), cv_ref[0].astype(BF16)) + _dot_t(p_n.astype(BF16), vnt.astype(BF16))) * (1.0 / l)
    oa = jnp.zeros((t_s, A_WIDTH), F32)
    for h in range(A_HEADS):
        oa = oa + jnp.where(head_of_lane == h, o[h * t_s:(h + 1) * t_s], 0.0)
    smix_ref[:, S_Q:S_U] = oa


def _sample_mem_conv(srow_ref, st_ref, cmk_ref, cmv_ref, wdw_ref, smix_ref, ns_ref, uc_ref, *, t_s):
    qm = srow_ref[:, S_QM:S_ROW] * SCALE
    mhead = lax.broadcasted_iota(jnp.int32, (t_s, M_WIDTH), 1) >> HEAD_SHIFT
    q4 = jnp.concatenate([jnp.where(mhead == h, qm, 0.0) for h in range(M_HEADS)], axis=0).astype(BF16)
    sm = _dot(q4, cmk_ref[0].astype(BF16))
    mm = jnp.max(sm, axis=-1, keepdims=True)
    pm = jnp.exp(sm - mm)
    lm = jnp.sum(pm, axis=-1, keepdims=True)
    om4 = _dot_t(pm.astype(BF16), cmv_ref[0].astype(BF16)) * (1.0 / lm)
    om = jnp.zeros((t_s, M_WIDTH), F32)
    for h in range(M_HEADS):
        om = om + jnp.where(mhead == h, om4[h * t_s:(h + 1) * t_s], 0.0)
    smix_ref[:, S_QM:S_ROW] = om

    nst = st_ref.shape[1]
    uc_ref[0:nst, :] = st_ref[0]
    uc_ref[nst:nst + t_s, :] = srow_ref[:, S_U:S_QM]
    acc = jnp.zeros((t_s, C_WIDTH), F32)
    for w in range(CONV_W):
        acc = acc + uc_ref[STATE_PAD + w:STATE_PAD + w + t_s, :] * wdw_ref[w:w + 1, :]
    smix_ref[:, S_U:S_QM] = acc
    ns_ref[0] = uc_ref[STATE_PAD + t_s:STATE_PAD + t_s + CONV_W - 1, :]


BACK = 512
NFAR = BACK // QB - 1

def _natural_offset(p):
    p = p & (QB - 1)
    return DIL4 * (p & (CLS4 - 1)) + (p >> CLS4_SHIFT)


def _prompt_attn_kernel(kvq_ref, kvp_ref, o16_ref, ga_ref, mix_ref, x_ref, wout_ref, npost_ref,
                        srow_ref, kvt_ref, ck_ref, cv_ref, sst_ref, cmk_ref, cmv_ref, wdw_ref,
                        y_ref, smix_ref, wk_ref, ns_ref,
                        mixa_ref, nat_ref, nat4_ref, unp_ref, uc_ref, woutbf_ref, *, tq, t_s, win_len):
    i = pl.program_id(1)
    kv_refs = (kvp_ref, kvq_ref)

    @pl.when((pl.program_id(0) == 0) & (i == 0))
    def _():
        woutbf_ref[...] = wout_ref[...].astype(BF16)

    quarter = tq // DIL4
    for cl in range(DIL16):
        c4, c2 = cl % DIL4, cl // DIL4
        dst = pl.ds(c4 * quarter + c2, tq // DIL16, stride=DIL4)
        for g in range(NGRP + 1):
            nat4_ref[g, dst, :] = o16_ref[0, cl, :, g * LANES:(g + 1) * LANES]
    for c4 in range(DIL4):
        for g in range(NGRP + 1):
            nat_ref[g, pl.ds(c4, quarter, stride=DIL4), :] = nat4_ref[g, c4 * quarter:(c4 + 1) * quarter, :]

    a_n = _natural_offset(lax.broadcasted_iota(jnp.int32, (QB, 2 * QB), 0))
    c_p = lax.broadcasted_iota(jnp.int32, (QB, 2 * QB), 1)
    d = jnp.where(c_p < QB, QB, 0) + a_n - _natural_offset(c_p)
    near = (d >= 0) & (d <= QB)
    far = (d >= 0) & ((d & (DIL4 - 1)) == 0)
    bias_near0 = jnp.where(near & far, LN2, jnp.where(near | far, 0.0, NEG))
    ia = lax.broadcasted_iota(jnp.int32, (CLS4, QB), 0)
    cf = lax.broadcasted_iota(jnp.int32, (CLS4, QB), 1)
    fblk, ic = cf >> CLS4_SHIFT, cf & (CLS4 - 1)
    bias_far0 = jnp.where((fblk < NFAR) & ((fblk > 0) | (ic >= ia)), 0.0, NEG)
    lane = lax.broadcasted_iota(jnp.int32, (QB, LANES), 1)
    head_a = lane < HEAD_DIM

    def rows_of(w):
        blk, off = divmod(w, tq)
        return kv_refs[blk], off

    def group(x, c):
        return jnp.concatenate([x[c * CLS4:(c + 1) * CLS4], x[QB + c * CLS4:QB + (c + 1) * CLS4]], axis=0)

    def ungroup(parts):
        return jnp.concatenate([p_[:CLS4] for p_ in parts] + [p_[CLS4:] for p_ in parts], axis=0)

    units = [(sub, g) for sub in range(tq // QB) for g in range(NGRP)]
    ctx = {}
    for sub in range(tq // QB):
        first_valid = BACK - i * tq - sub * QB
        bias_near = jnp.where((c_p >= QB) | (NFAR * QB >= first_valid), bias_near0, NEG)
        bias_far = jnp.where(fblk * QB >= first_valid, bias_far0, NEG)
        far_blocks = [rows_of((sub + j) * QB) for j in range(NFAR)]
        ctx[sub] = dict(bias_near=jnp.concatenate([bias_near, bias_near], axis=0),
                        bias_far=jnp.concatenate([bias_far, bias_far], axis=0),
                        prev=rows_of((sub + NFAR) * QB), cur=rows_of((sub + NFAR + 1) * QB),
                        far=far_blocks + far_blocks[:1])

    def far_rows(sub, c, colsl):
        return jnp.concatenate([r[0, off + c * CLS4:off + (c + 1) * CLS4, colsl] for r, off in ctx[sub]["far"]],
                               axis=0)

    def block(sub, which, colsl):
        r, off = ctx[sub][which]
        return r[0, off:off + QB, colsl]

    scores, probs, outs = {}, {}, {}

    def stage_scores(u):
        sub, g = u
        kc = slice(P_K + g * LANES, P_K + (g + 1) * LANES)
        qc = slice(P_Q + g * LANES, P_Q + (g + 1) * LANES)
        q2 = _pair_split(kvq_ref[0, sub * QB:(sub + 1) * QB, qc].astype(F32)).astype(BF16)
        s_far = ungroup([_dot_t(group(q2, c), far_rows(sub, c, kc)) + ctx[sub]["bias_far"] for c in range(DIL4)])
        s_near = jnp.concatenate([_dot_t(q2, block(sub, "prev", kc)), _dot_t(q2, block(sub, "cur", kc))],
                                 axis=1) + ctx[sub]["bias_near"]
        scores[u] = jnp.concatenate([s_far, s_near], axis=1)

    def stage_softmax(u):
        s = scores.pop(u)
        m = jnp.max(s, axis=-1, keepdims=True)
        p = jnp.exp(s - m)
        probs[u] = (m, jnp.sum(p, axis=-1, keepdims=True), p.astype(BF16))

    def stage_values(u):
        sub, g = u
        vc = slice(P_V + g * LANES, P_V + (g + 1) * LANES)
        pb = probs[u][2]
        outs[u] = (_dot(pb[:, QB:2 * QB], block(sub, "prev", vc)) + _dot(pb[:, 2 * QB:], block(sub, "cur", vc))
                   + ungroup([_dot(group(pb[:, :QB], c), far_rows(sub, c, vc)) for c in range(DIL4)]))

    def stage_merge(u):
        sub, g = u
        rows = slice(sub * QB, (sub + 1) * QB)
        cols = slice(g * LANES, (g + 1) * LANES)
        m, l, _ = probs.pop(u)
        o_s = _pair_join(outs.pop(u))
        m_s = jnp.where(head_a, m[:QB], m[QB:])
        l_s = jnp.where(head_a, l[:QB], l[QB:])
        for c in range(DIL4):
            dst = pl.ds(c, CLS4, stride=DIL4)
            unp_ref[3 * g, dst, :] = o_s[c * CLS4:(c + 1) * CLS4]
            unp_ref[3 * g + 1, dst, :] = m_s[c * CLS4:(c + 1) * CLS4]
            unp_ref[3 * g + 2, dst, :] = l_s[c * CLS4:(c + 1) * CLS4]
        o_n, m_n, l_n = unp_ref[3 * g], unp_ref[3 * g + 1], unp_ref[3 * g + 2]
        st = nat_ref[NGRP, rows, :]
        m_f = jnp.where(head_a, st[:, 2 * g:2 * g + 1], st[:, 2 * g + 1:2 * g + 2])
        l_f = jnp.where(head_a, st[:, STAT_L + 2 * g:STAT_L + 2 * g + 1],
                        st[:, STAT_L + 2 * g + 1:STAT_L + 2 * g + 2])
        mx = jnp.maximum(m_n, m_f)
        w_n = jnp.exp(m_n - mx)
        w_f = jnp.exp(m_f - mx)
        oa = (o_n * w_n + nat_ref[g, rows, :] * w_f) / (l_n * w_n + l_f * w_f)
        mixa_ref[rows, cols] = (oa * ga_ref[0, rows, cols]).astype(BF16)

    def finish(rows):
        z = (_dot(mixa_ref[rows, :], woutbf_ref[0:A_WIDTH, :])
             + _dot(mix_ref[0, rows, :], woutbf_ref[A_WIDTH:, :]))
        y_ref[0, rows, :] = x_ref[0, rows, :] + _rmsnorm(z, npost_ref[...])

    stages = (stage_scores, stage_softmax, stage_values, stage_merge)
    half = len(units) // 2
    for t in range(len(units) + len(stages) - 1):
        for depth, stage in enumerate(stages):
            if 0 <= t - depth < len(units):
                stage(units[t - depth])
        if t - (len(stages) - 1) == half - 1:
            finish(slice(0, tq // 2))
    finish(slice(tq // 2, tq))

    for part in _sample_mix_stages(pl.program_id(0) * pl.num_programs(1) + i, srow_ref, kvt_ref, ck_ref, cv_ref,
                                   sst_ref, cmk_ref, cmv_ref, wdw_ref, smix_ref, wk_ref, ns_ref, uc_ref,
                                   t_s=t_s, win_len=win_len):
        part()


def _prompt_attn(kvq, o16, ga, mix, x, wout, npost, srow, kvt, ck, cv, st_pad, cmk, cmv, wdw, tq, t_s):
    b, s, _ = x.shape
    nsteps = s // tq
    nb, _, win_len = ck.shape
    assert tq == BACK and b * nsteps == nb, "one sample batch per grid step"
    row = lambda w: pl.BlockSpec((1, tq, w), lambda bi, i: (bi, i, 0))
    cls = pl.BlockSpec((1, DIL16, tq // DIL16, O16_W), lambda bi, i: (bi, 0, i, 0))
    srows = pl.BlockSpec((t_s, S_ROW), lambda bi, i: (bi * nsteps + i, 0))
    per = lambda a: pl.BlockSpec((1,) + a.shape[1:], lambda bi, i: (bi * nsteps + i, 0, 0))
    return pl.pallas_call(
        functools.partial(_prompt_attn_kernel, tq=tq, t_s=t_s, win_len=win_len),
        grid=(b, nsteps),
        in_specs=[row(kvq.shape[2]), pl.BlockSpec((1, tq, P_Q), lambda bi, i: (bi, jnp.maximum(i - 1, 0), 0)),
                  cls, row(A_WIDTH), row(C_WIDTH + M_WIDTH), row(D_MODEL),
                  _resident(wout), _full(npost),
                  srows, _full(kvt), per(ck), per(cv), per(st_pad), per(cmk), per(cmv), _full(wdw)],
        out_specs=[row(D_MODEL), srows, per(ck),
                   pl.BlockSpec((1, CONV_W - 1, C_WIDTH), lambda bi, i: (bi * nsteps + i, 0, 0))],
        out_shape=[jax.ShapeDtypeStruct((b, s, D_MODEL), F32),
                   jax.ShapeDtypeStruct((nb * t_s, S_ROW), F32),
                   jax.ShapeDtypeStruct(ck.shape, F32),
                   jax.ShapeDtypeStruct((nb, CONV_W - 1, C_WIDTH), F32)],
        scratch_shapes=[pltpu.VMEM((tq, A_WIDTH), BF16),
                        pltpu.VMEM((NGRP + 1, tq, LANES), F32),
                        pltpu.VMEM((NGRP + 1, tq, LANES), F32),
                        pltpu.VMEM((3 * NGRP, QB, LANES), F32),
                        pltpu.VMEM((STATE_PAD + CONV_W - 1 + t_s, C_WIDTH), F32),
                        pltpu.VMEM(wout.shape, BF16)],
        compiler_params=pltpu.CompilerParams(dimension_semantics=("arbitrary", "arbitrary"),
                                             vmem_limit_bytes=VMEM_LIMIT),
        name="prompt_attn",
    )(kvq, kvq, o16, ga, mix, x, wout, npost, srow, kvt, ck, cv, st_pad, cmk, cmv, wdw)


def _sample_proj_kernel(x_ref, npre_ref, win_ref, ang_ref, srow_ref, kvt_ref, gate_ref, wbf_ref):
    h = _rmsnorm(x_ref[...], npre_ref[...]).astype(BF16)
    wbf_ref[...] = win_ref[...].astype(BF16)
    proj = _dot(h, wbf_ref[...])
    pats = _rope_patterns(ang_ref[:, 0:LANES], ang_ref[:, LANES:])
    srow_ref[:, S_Q:S_U] = _rope(proj[:, O_QA:O_KA], pats) * SCALE
    srow_ref[:, S_U:S_QM] = proj[:, O_AB:O_BB] * _sigmoid(proj[:, O_BB:O_GB])
    srow_ref[:, S_QM:S_ROW] = proj[:, O_QM:O_GM]
    kv_t = jnp.concatenate([_rope(proj[:, O_KA:O_VA], pats), proj[:, O_VA:O_GA]], axis=1).T
    for slab in range(kvt_ref.shape[0]):
        kvt_ref[slab] = kv_t[:, slab * LANES:(slab + 1) * LANES]
    gate_ref[:, S_Q:S_U] = _silu(proj[:, O_GA:O_AB])
    gate_ref[:, S_U:S_QM] = _silu(proj[:, O_GB:O_QM])
    gate_ref[:, S_QM:S_ROW] = _silu(proj[:, O_GM:D_IN])


def _sample_proj(x, npre, win, ang):
    n = x.shape[0]
    assert n % LANES == 0
    shapes = [(n, S_ROW), (n // LANES, 2 * A_WIDTH, LANES), (n, S_ROW), win.shape]
    dtypes = [F32, F32, F32, BF16]
    return pl.pallas_call(
        _sample_proj_kernel,
        grid=(1,),
        in_specs=[_full(x), _full(npre), _full(win), _full(ang)],
        out_specs=[pl.BlockSpec(sh, lambda i, nd=len(sh): (0,) * nd) for sh in shapes],
        out_shape=[jax.ShapeDtypeStruct(sh, dt) for sh, dt in zip(shapes, dtypes)],
        compiler_params=pltpu.CompilerParams(vmem_limit_bytes=VMEM_LIMIT),
        name="sample_proj",
    )(x, npre, win, ang)


def _sample_out_kernel(x_ref, smix_ref, gate_ref, bdw_ref, lng_ref, lnb_ref, wpw_ref, bpw_ref, wout_ref, npost_ref,
                       y_ref):
    mixed_a = (smix_ref[:, S_Q:S_U] * gate_ref[:, S_Q:S_U]).astype(BF16)
    mixed_b = _conformer_tail(smix_ref[:, S_U:S_QM], gate_ref[:, S_U:S_QM], bdw_ref[...], lng_ref[...], lnb_ref[...],
                              wpw_ref, bpw_ref[...]).astype(BF16)
    mixed_m = (smix_ref[:, S_QM:S_ROW] * gate_ref[:, S_QM:S_ROW]).astype(BF16)
    wout = wout_ref[...].astype(BF16)
    z = (_dot(mixed_a, wout[0:A_WIDTH]) + _dot(mixed_b, wout[A_WIDTH:A_WIDTH + C_WIDTH])
         + _dot(mixed_m, wout[A_WIDTH + C_WIDTH:]))
    y_ref[...] = x_ref[...] + _rmsnorm(z, npost_ref[...])


def _sample_out(x, smix, gate, bdw, lng, lnb, wpw, bpw, wout, npost):
    args = (x, smix, gate, bdw, lng, lnb, wpw, bpw, wout, npost)
    return pl.pallas_call(
        _sample_out_kernel,
        grid=(1,),
        in_specs=[_full(a) for a in args],
        out_specs=_full(x),
        out_shape=jax.ShapeDtypeStruct(x.shape, F32),
        compiler_params=pltpu.CompilerParams(vmem_limit_bytes=VMEM_LIMIT),
        name="sample_out",
    )(*args)


def _rope_cos_sin(pos):
    inv = ROPE_THETA ** (-jnp.arange(0, ROT_DIM, 2, dtype=F32) / ROT_DIM)
    ang = pos.astype(F32)[:, None] * jnp.tile(inv, 2 * LANES // ROT_DIM)[None, :]
    return jnp.concatenate([jnp.cos(ang), jnp.sin(ang)], axis=1)


def _feature_major(cache):
    nb, rows, heads, dim = cache.shape
    return jnp.transpose(cache, (0, 2, 3, 1)).reshape(nb, heads * dim, rows)


def _row_major(cache_t, heads):
    nb, width, rows = cache_t.shape
    return jnp.transpose(cache_t.reshape(nb, heads, width // heads, rows), (0, 3, 1, 2))


PROJ_TILE = 512
ATTN_TILE = BACK


def kernel(x_prompt, x_sample, cache_win_k, cache_win_v, state_conv, cache_mem_k, cache_mem_v, mem_prompt,
           norm_pre, norm_post, w_in, w_out, norm_mem, w_mem_kv, w_dw, b_dw, ln_conv_g, ln_conv_b, w_pw2, b_pw2):
    depth = w_in.shape[0]
    assert depth == 1, "single-layer step"
    b, s, _ = x_prompt.shape
    nb, t_s, _ = x_sample.shape
    win_len = cache_win_k.shape[2]
    assert s % (DIL16 * QB) == 0 and s % PROJ_TILE == 0 and s % ATTN_TILE == 0
    assert win_len == MAX_WINDOW and win_len % LANES == 0
    assert t_s % 8 == 0 and t_s & (t_s - 1) == 0 and t_s < LANES
    l = 0
    row = lambda a: a[l][None, :]
    npre, npost, nmem = row(norm_pre), row(norm_post), row(norm_mem)
    bdw, lng, lnb, bpw = row(b_dw), row(ln_conv_g), row(ln_conv_b), row(b_pw2)
    win, wout, wmem, wpw = w_in[l], w_out[l], w_mem_kv[l], w_pw2[l]
    wdw = w_dw[l]

    pos_s = PAST_LEN + jnp.arange(t_s, dtype=jnp.int32)
    xs = x_sample.reshape(nb * t_s, D_MODEL)
    srow, kvt, gate_s, win_bf = _sample_proj(xs, npre, win, jnp.tile(_rope_cos_sin(pos_s), (nb, 1)))
    st_pad = jnp.pad(state_conv[l], ((0, 0), (STATE_PAD, 0), (0, 0)))

    mk, mv, mkb, mvb = _mem_kv(mem_prompt, nmem, wmem)
    rin = _rope_cos_sin(jnp.arange(PROJ_TILE, dtype=jnp.int32))
    rbase = _rope_cos_sin(jnp.arange(0, s, PROJ_TILE, dtype=jnp.int32))[:, None, :]
    ck, cv = _feature_major(cache_win_k[l]), _feature_major(cache_win_v[l])
    kvq, kvq16, kf, vf, ga, mix, ust, wv = _prompt_proj(x_prompt, npre, win_bf, rin, rbase, mkb, mvb, wdw, bdw,
                                                        lng, lnb, wpw, bpw, kvt, cv, PROJ_TILE, t_s)
    o16 = _dilated16(kvq16)
    y_prompt, smix, wk, nst = _prompt_attn(
        kvq, o16, ga, mix, x_prompt, wout, npost, srow, kvt, ck, cv, st_pad,
        _feature_major(cache_mem_k[l]), _feature_major(cache_mem_v[l]), wdw, ATTN_TILE, t_s)
    y_sample = _sample_out(xs, smix, gate_s, bdw, lng, lnb, wpw, bpw, wout, npost)

    keep_p = kf.shape[1]
    return (y_prompt,
            y_sample.reshape(nb, t_s, D_MODEL),
            kf.reshape(1, b, keep_p, A_HEADS, HEAD_DIM),
            vf.reshape(1, b, keep_p, A_HEADS, HEAD_DIM),
            ust[:, CONV_HIST - (CONV_W - 1):, :][None],
            mk.reshape(1, b, N_MEM, M_HEADS, HEAD_DIM),
            mv.reshape(1, b, N_MEM, M_HEADS, HEAD_DIM),
            _row_major(wk, A_HEADS)[None],
            _row_major(wv, A_HEADS)[None],
            nst[None])
```

```python
import functools
import math

import jax
import jax.numpy as jnp
from jax import lax
from jax.experimental import pallas as pl
from jax.experimental.pallas import tpu as pltpu

F32 = jnp.float32
BF16 = jnp.bfloat16

D_MODEL = 1024
HEAD_DIM = 64
HEAD_SHIFT = 6
A_WIDTH = 384
A_HEADS = 6
M_WIDTH = 256
M_HEADS = 4
C_WIDTH = 384
ROT_DIM = 16
ROPE_THETA = 500000.0
CONV_W = 31
N_MEM = 256
MAX_WINDOW = 2048
PAST_LEN = 16384
EPS = 1e-6
SCALE = HEAD_DIM ** -0.5
NEG = -1e30
LN2 = math.log(2.0)

LANES = 128
NGRP = A_WIDTH // LANES
QB = 128
DIL16 = 16
DIL4 = 4
CLS4 = QB // DIL4
CLS4_SHIFT = 5
O_QA, O_KA, O_VA, O_GA, O_AB, O_BB, O_GB, O_QM, O_GM, D_IN = 0, 384, 768, 1152, 1536, 1920, 2304, 2688, 2944, 3200
P_K, P_V, P_Q, P_W = 0, A_WIDTH, 2 * A_WIDTH, 3 * A_WIDTH

VMEM_LIMIT = 60 * 1024 * 1024


def _sigmoid(x):
    return 1.0 / (1.0 + jnp.exp(-x))


def _silu(x):
    return x * _sigmoid(x)


def _rmsnorm(x, g):
    return x * lax.rsqrt(jnp.mean(x * x, axis=-1, keepdims=True) + EPS) * g


def _dot(a, b):
    return jnp.dot(a, b, preferred_element_type=F32)


def _dot_t(a, b):
    return lax.dot_general(a, b, (((1,), (1,)), ((), ())), preferred_element_type=F32)


def _rope_patterns(cos, sin):
    in_head = lax.broadcasted_iota(jnp.int32, cos.shape, 1) & (HEAD_DIM - 1)
    rot, lo = in_head < ROT_DIM, in_head < ROT_DIM // 2
    return jnp.where(rot, cos, 1.0), jnp.where(lo, -sin, 0.0), jnp.where(rot & ~lo, sin, 0.0)


def _rope(xw, pats):
    cos, s1, s2 = pats
    outs = []
    for g in range(xw.shape[1] // LANES):
        xg = xw[:, g * LANES:(g + 1) * LANES]
        outs.append(xg * cos + pltpu.roll(xg, LANES - 8, 1) * s1 + pltpu.roll(xg, 8, 1) * s2)
    return jnp.concatenate(outs, axis=1)


def _pair_split(q):
    lane = lax.broadcasted_iota(jnp.int32, q.shape, 1)
    qa = jnp.where(lane < HEAD_DIM, q, 0.0)
    qb = jnp.where(lane >= HEAD_DIM, q, 0.0)
    return jnp.concatenate([qa, qb], axis=0)


def _pair_join(x):
    t = x.shape[0] // 2
    lane = lax.broadcasted_iota(jnp.int32, (t, LANES), 1)
    return jnp.where(lane < HEAD_DIM, x[:t], x[t:])


def _mem_attend(qm, mk_ref, mv_ref):
    outs = []
    for g in range(M_WIDTH // LANES):
        cols = slice(g * LANES, (g + 1) * LANES)
        q2 = _pair_split(qm[:, cols] * SCALE).astype(BF16)
        s = _dot_t(q2, mk_ref[0, :, cols])
        m = jnp.max(s, axis=-1, keepdims=True)
        p = jnp.exp(s - m)
        l = jnp.sum(p, axis=-1, keepdims=True)
        o = _dot(p.astype(BF16), mv_ref[0, :, cols]) * (1.0 / l)
        outs.append(_pair_join(o))
    return jnp.concatenate(outs, axis=1)


def _conformer_tail(c, gate_b, bdw, lng, lnb, wpw_ref, bpw):
    cf = c + bdw
    mu = jnp.mean(cf, axis=-1, keepdims=True)
    dev = cf - mu
    var = jnp.mean(dev * dev, axis=-1, keepdims=True)
    cn = dev * lax.rsqrt(var + EPS) * lng + lnb
    ob = _dot(_silu(cn).astype(BF16), wpw_ref[...].astype(BF16)) + bpw
    return ob * gate_b


def _full(a):
    return pl.BlockSpec(a.shape, lambda *_: (0,) * a.ndim)


def _resident(a):
    return pl.BlockSpec(a.shape, lambda *_: (0,) * a.ndim, pipeline_mode=pl.Buffered(1))


def _mem_kv(mem_ref, g_ref, w_ref, mk_ref, mv_ref, mkb_ref, mvb_ref):
    w = w_ref[...].astype(BF16)
    for bi in range(mem_ref.shape[0]):
        kv = _dot(_rmsnorm(mem_ref[bi], g_ref[...]).astype(BF16), w)
        mk_ref[bi] = kv[:, :M_WIDTH]
        mv_ref[bi] = kv[:, M_WIDTH:]
        mkb_ref[bi] = kv[:, :M_WIDTH].astype(BF16)
        mvb_ref[bi] = kv[:, M_WIDTH:].astype(BF16)


CONV_HIST = 32
CONV_CHUNK = 64


def _prompt_proj_kernel(x_ref, npre_ref, win_ref, rin_ref, rbase_ref, mk_ref, mv_ref,
                        wdw_ref, bdw_ref, lng_ref, lnb_ref, wpw_ref, bpw_ref, kvt_ref, cv_ref,
                        kvq_ref, kvq16_ref, kf_ref, vf_ref, ga_ref, mix_ref, ust_ref, wv_ref,
                        uext_ref, ush_ref, conv_ref, stage_ref, stage4_ref, *, t_s, win_len):
    t = x_ref.shape[1]
    i = pl.program_id(1)

    @pl.when(i == 0)
    def _():
        uext_ref[0:CONV_HIST, :] = jnp.zeros((CONV_HIST, C_WIDTH), F32)

    h = _rmsnorm(x_ref[0], npre_ref[...]).astype(BF16)
    proj = {}

    def project(name, c0, c1):
        def run():
            proj[name] = _dot(h, win_ref[:, c0:c1])
        return run

    def stage(idx, val):
        for g in range(NGRP):
            stage_ref[NGRP * idx + g] = val[:, g * LANES:(g + 1) * LANES]

    pats = []

    def rope_pats():
        if not pats:
            cr, sr = rin_ref[:, 0:LANES], rin_ref[:, LANES:]
            cb, sb = rbase_ref[i, :, 0:LANES], rbase_ref[i, :, LANES:]
            pats.append(_rope_patterns(cb * cr - sb * sr, sb * cr + cb * sr))
        return pats[0]

    def do_k():
        k = _rope(proj.pop("k"), rope_pats())
        kf_ref[0] = k
        stage(0, k)

    def do_q():
        stage(2, _rope(proj.pop("q"), rope_pats()) * SCALE)

    def do_v():
        v = proj.pop("v")
        vf_ref[0] = v
        stage(1, v)

    def do_ga():
        ga_ref[0] = _silu(proj.pop("ga"))

    def do_copies():
        quarter = t // DIL4
        for c4 in range(DIL4):
            for s in range(3 * NGRP):
                stage4_ref[s, c4 * quarter:(c4 + 1) * quarter, :] = stage_ref[s, pl.ds(c4, quarter, stride=DIL4), :]
        for c in range(DIL16):
            c4, c2 = c % DIL4, c // DIL4
            rows_c = [stage4_ref[s, pl.ds(c4 * quarter + c2, t // DIL16, stride=DIL4), :] for s in range(3 * NGRP)]
            kvq16_ref[0, c] = jnp.concatenate(rows_c, axis=1).astype(BF16)
        for blk in range(t // QB):
            for c4 in range(DIL4):
                src = slice(c4 * quarter + blk * CLS4, c4 * quarter + (blk + 1) * CLS4)
                rows_c = [stage4_ref[s, src, :] for s in range(3 * NGRP)]
                kvq_ref[0, blk * QB + c4 * CLS4:blk * QB + (c4 + 1) * CLS4, :] = (
                    jnp.concatenate(rows_c, axis=1).astype(BF16))

    def do_glu():
        uext_ref[CONV_HIST:CONV_HIST + t, :] = proj.pop("ab") * _sigmoid(proj.pop("bb"))
        uext = uext_ref[...]
        for r in range(1, 8):
            ush_ref[r - 1] = pltpu.roll(uext, r, 0)

    def conv_rows(r0):
        def run():
            acc = jnp.zeros((CONV_CHUNK, C_WIDTH), F32)
            for kk in range(CONV_W):
                tap = wdw_ref[CONV_W - 1 - kk:CONV_W - kk, :]
                r, base = kk % 8, CONV_HIST + r0 - (kk - kk % 8)
                if r == 0:
                    slab = uext_ref[base:base + CONV_CHUNK, :]
                else:
                    slab = ush_ref[r - 1, base:base + CONV_CHUNK, :]
                acc = acc + slab * tap
            conv_ref[r0:r0 + CONV_CHUNK, :] = acc
        return run

    def do_history():
        tail = uext_ref[t:t + CONV_HIST, :]
        uext_ref[0:CONV_HIST, :] = tail
        ust_ref[0] = tail

    def do_tail():
        mixed_b = _conformer_tail(conv_ref[...], _silu(proj.pop("gb")), bdw_ref[...], lng_ref[...], lnb_ref[...],
                                  wpw_ref, bpw_ref[...])
        mix_ref[0, :, 0:C_WIDTH] = mixed_b.astype(BF16)

    def do_mem():
        mixed_m = _mem_attend(proj.pop("qm"), mk_ref, mv_ref) * _silu(proj.pop("gm"))
        mix_ref[0, :, C_WIDTH:] = mixed_m.astype(BF16)

    convs = [conv_rows(r0) for r0 in range(0, t, CONV_CHUNK)]
    riders = [([project("q", O_QA, O_KA)], do_q),
              ([project("v", O_VA, O_GA)], do_v), ([project("ga", O_GA, O_AB)], do_ga),
              ([project("qm", O_QM, O_GM), project("gm", O_GM, D_IN)], do_mem),
              ([project("gb", O_GB, O_QM)], do_copies)]
    slide_v = functools.partial(_slide_cache, pl.program_id(0) * pl.num_programs(1) + i, kvt_ref, cv_ref, A_WIDTH,
                                wv_ref, t_s=t_s, win_len=win_len)
    program = [project("ab", O_AB, O_BB), project("k", O_KA, O_VA), project("bb", O_BB, O_GB), do_k, slide_v, do_glu]
    for n, conv in enumerate(convs):
        if n < len(riders):
            program += riders[n][0]
        program.append(conv)
        if n < len(riders) and riders[n][1] is not None:
            program.append(riders[n][1])
    assert len(convs) >= len(riders)
    program += [do_history, do_tail]
    for piece in program:
        piece()
    assert not proj


def _prompt_proj(x, npre, win, rin, rbase, mkb, mvb, wdw, bdw, lng, lnb, wpw, bpw, kvt, cv, tile, t_s):
    b, s, _ = x.shape
    nt = s // tile
    nb, _, win_len = cv.shape
    assert b * nt == nb, "one sample batch per grid step"
    per_batch = pl.BlockSpec((1,) + cv.shape[1:], lambda bi, i: (bi * nt + i, 0, 0))
    keep = min(MAX_WINDOW, s)
    first_keep = (s - keep) // tile
    row = lambda w: pl.BlockSpec((1, tile, w), lambda bi, i: (bi, i, 0))
    memb = pl.BlockSpec((1, N_MEM, M_WIDTH), lambda bi, i: (bi, 0, 0))
    keepb = pl.BlockSpec((1, tile, A_WIDTH), lambda bi, i: (bi, jnp.maximum(i - first_keep, 0), 0))
    cls = pl.BlockSpec((1, DIL16, tile // DIL16, P_W), lambda bi, i: (bi, 0, i, 0))
    return pl.pallas_call(
        functools.partial(_prompt_proj_kernel, t_s=t_s, win_len=win_len),
        grid=(b, nt),
        in_specs=[row(D_MODEL), _full(npre), _resident(win), _full(rin), _full(rbase),
                  memb, memb, _full(wdw), _full(bdw), _full(lng), _full(lnb), _full(wpw), _full(bpw),
                  _full(kvt), per_batch],
        out_specs=[row(P_W), cls, keepb, keepb, row(A_WIDTH), row(C_WIDTH + M_WIDTH),
                   pl.BlockSpec((1, CONV_HIST, C_WIDTH), lambda bi, i: (bi, 0, 0)), per_batch],
        out_shape=[jax.ShapeDtypeStruct((b, s, P_W), BF16),
                   jax.ShapeDtypeStruct((b, DIL16, s // DIL16, P_W), BF16),
                   jax.ShapeDtypeStruct((b, keep, A_WIDTH), F32),
                   jax.ShapeDtypeStruct((b, keep, A_WIDTH), F32),
                   jax.ShapeDtypeStruct((b, s, A_WIDTH), F32),
                   jax.ShapeDtypeStruct((b, s, C_WIDTH + M_WIDTH), BF16),
                   jax.ShapeDtypeStruct((b, CONV_HIST, C_WIDTH), F32),
                   jax.ShapeDtypeStruct(cv.shape, F32)],
        scratch_shapes=[pltpu.VMEM((CONV_HIST + tile, C_WIDTH), F32),
                        pltpu.VMEM((7, CONV_HIST + tile, C_WIDTH), F32),
                        pltpu.VMEM((tile, C_WIDTH), F32),
                        pltpu.VMEM((3 * NGRP, tile, LANES), F32),
                        pltpu.VMEM((3 * NGRP, tile, LANES), F32)],
        compiler_params=pltpu.CompilerParams(dimension_semantics=("arbitrary", "arbitrary"),
                                             vmem_limit_bytes=VMEM_LIMIT),
        name="prompt_proj",
    )(x, npre, win, rin, rbase, mkb, mvb, wdw, bdw, lng, lnb, wpw, bpw, kvt, cv)


STAT_L = 8
O16_W = A_WIDTH + LANES


CLS_PER_STEP = 8


def _dilated16_kernel(kvq_ref, o_ref):
    a = lax.broadcasted_iota(jnp.int32, (QB, QB), 0)
    c = lax.broadcasted_iota(jnp.int32, (QB, QB), 1)
    tri_prev = jnp.where(c >= a, 0.0, NEG)
    tri_cur = jnp.where(c <= a, 0.0, NEG)
    bias_cur = jnp.concatenate([tri_cur, tri_cur], axis=0)
    both = jnp.concatenate([tri_prev, tri_cur], axis=1)
    bias_both = jnp.concatenate([both, both], axis=0)
    lane = lax.broadcasted_iota(jnp.int32, (QB, LANES), 1)
    ncls, nsub = kvq_ref.shape[1], kvq_ref.shape[2] // QB
    units = [(cl, sub, g) for cl in range(ncls) for sub in range(nsub) for g in range(NGRP)]
    scores, probs = {}, {}
    stats = {(cl, sub): jnp.zeros((QB, LANES), F32) for cl in range(ncls) for sub in range(nsub)}

    def rows_of(sub):
        return slice(sub * QB, (sub + 1) * QB)

    def stage_scores(u):
        cl, sub, g = u
        kc = slice(P_K + g * LANES, P_K + (g + 1) * LANES)
        qc = slice(P_Q + g * LANES, P_Q + (g + 1) * LANES)
        q2 = _pair_split(kvq_ref[0, cl, rows_of(sub), qc].astype(F32)).astype(BF16)
        s_cur = _dot_t(q2, kvq_ref[0, cl, rows_of(sub), kc])
        if sub == 0:
            scores[u] = s_cur + bias_cur
        else:
            scores[u] = jnp.concatenate([_dot_t(q2, kvq_ref[0, cl, rows_of(sub - 1), kc]), s_cur], axis=1) + bias_both

    def stage_softmax(u):
        s = scores.pop(u)
        m = jnp.max(s, axis=-1, keepdims=True)
        p = jnp.exp(s - m)
        probs[u] = (m, jnp.sum(p, axis=-1, keepdims=True), p.astype(BF16))

    def stage_values(u):
        cl, sub, g = u
        vc = slice(P_V + g * LANES, P_V + (g + 1) * LANES)
        m, l, pb = probs.pop(u)
        if sub == 0:
            o = _dot(pb, kvq_ref[0, cl, rows_of(sub), vc])
        else:
            o = (_dot(pb[:, :QB], kvq_ref[0, cl, rows_of(sub - 1), vc])
                 + _dot(pb[:, QB:], kvq_ref[0, cl, rows_of(sub), vc]))
        o_ref[0, cl, rows_of(sub), g * LANES:(g + 1) * LANES] = _pair_join(o)
        st = stats[cl, sub]
        st = jnp.where(lane == 2 * g, m[:QB], st)
        st = jnp.where(lane == 2 * g + 1, m[QB:], st)
        st = jnp.where(lane == STAT_L + 2 * g, l[:QB], st)
        stats[cl, sub] = jnp.where(lane == STAT_L + 2 * g + 1, l[QB:], st)
        if g == NGRP - 1:
            o_ref[0, cl, rows_of(sub), A_WIDTH:] = stats.pop((cl, sub))

    stages = (stage_scores, stage_softmax, stage_values)
    for t in range(len(units) + len(stages) - 1):
        for depth, stage in enumerate(stages):
            if 0 <= t - depth < len(units):
                stage(units[t - depth])


def _dilated16(kvq16):
    b, ncls, nj, w = kvq16.shape
    return pl.pallas_call(
        _dilated16_kernel,
        grid=(b, ncls // CLS_PER_STEP),
        in_specs=[pl.BlockSpec((1, CLS_PER_STEP, nj, w), lambda bi, c: (bi, c, 0, 0))],
        out_specs=pl.BlockSpec((1, CLS_PER_STEP, nj, O16_W), lambda bi, c: (bi, c, 0, 0)),
        out_shape=jax.ShapeDtypeStruct((b, ncls, nj, O16_W), F32),
        compiler_params=pltpu.CompilerParams(dimension_semantics=("arbitrary",) * 2,
                                             vmem_limit_bytes=VMEM_LIMIT),
        name="dilated16",
    )(kvq16)


STATE_PAD = 2
S_Q, S_U, S_QM, S_ROW = 0, A_WIDTH, A_WIDTH + C_WIDTH, A_WIDTH + C_WIDTH + M_WIDTH


def _new_rows(batch, kvt_ref, lo, hi, t_s):
    per_slab = LANES // t_s
    slab, pos = batch // per_slab, batch % per_slab
    return pltpu.roll(kvt_ref[slab, lo:hi, :], LANES - t_s - pos * t_s, 1)


def _slide_cache(batch, kvt_ref, src_ref, lo, dst_ref, *, t_s, win_len):
    new0 = LANES - t_s
    keep = lax.broadcasted_iota(jnp.int32, (A_WIDTH, LANES), 1) < new0
    nch = win_len // LANES
    nxt = pltpu.roll(src_ref[0, :, 0:LANES], new0, 1)
    for c in range(nch):
        cur = nxt
        nxt = (pltpu.roll(src_ref[0, :, (c + 1) * LANES:(c + 2) * LANES], new0, 1) if c + 1 < nch
               else _new_rows(batch, kvt_ref, lo, lo + A_WIDTH, t_s))
        dst_ref[0, :, c * LANES:(c + 1) * LANES] = jnp.where(keep, cur, nxt)


def _sample_mix_stages(batch, srow_ref, kvt_ref, ck_ref, cv_ref, st_ref, cmk_ref, cmv_ref, wdw_ref,
                       smix_ref, wk_ref, ns_ref, uc_ref, *, t_s, win_len):
    new_rows = functools.partial(_new_rows, batch, kvt_ref, t_s=t_s)
    return [functools.partial(_slide_cache, batch, kvt_ref, ck_ref, 0, wk_ref, t_s=t_s, win_len=win_len),
            functools.partial(_sample_attend, new_rows, srow_ref, ck_ref, cv_ref, smix_ref, t_s=t_s, win_len=win_len),
            functools.partial(_sample_mem_conv, srow_ref, st_ref, cmk_ref, cmv_ref, wdw_ref, smix_ref, ns_ref,
                              uc_ref, t_s=t_s)]


def _sample_attend(new_rows, srow_ref, ck_ref, cv_ref, smix_ref, *, t_s, win_len):
    new0 = LANES - t_s
    knt, vnt = new_rows(0, A_WIDTH), new_rows(A_WIDTH, 2 * A_WIDTH)
    q = srow_ref[:, S_Q:S_U]
    head_of_lane = lax.broadcasted_iota(jnp.int32, (t_s, A_WIDTH), 1) >> HEAD_SHIFT
    q6 = jnp.concatenate([jnp.where(head_of_lane == h, q, 0.0) for h in range(A_HEADS)], axis=0).astype(BF16)
    s_c = _dot(q6, ck_ref[0].astype(BF16))
    s_n = _dot(q6, knt.astype(BF16))

    def weights(shape, key0, lo):
        tq = lax.broadcasted_iota(jnp.int32, shape, 0) & (t_s - 1)
        key = lax.broadcasted_iota(jnp.int32, shape, 1) + key0
        d = win_len + tq - key
        ok = (d >= 0) & (key >= lo)
        w = ((d <= 128).astype(F32) + ((d <= 512) & ((d & 3) == 0)).astype(F32)
             + ((d <= 2048) & ((d & 15) == 0)).astype(F32))
        return jnp.where(ok, w, 0.0)

    w_c = weights(s_c.shape, 0, 0)
    w_n = weights(s_n.shape, win_len - new0, win_len)
    s_c = jnp.where(w_c > 0.0, s_c, NEG)
    s_n = jnp.where(w_n > 0.0, s_n, NEG)
    m = jnp.maximum(jnp.max(s_c, axis=-1, keepdims=True), jnp.max(s_n, axis=-1, keepdims=True))
    p_c = jnp.exp(s_c - m) * w_c
    p_n = jnp.exp(s_n - m) * w_n
    l = jnp.sum(p_c, axis=-1, keepdims=True) + jnp.sum(p_n, axis=-1, keepdims=True)
    o = (_dot_t(p_c.astype(BF16), cv_ref[0].astype(BF16)) + _dot_t(p_n.astype(BF16), vnt.astype(BF16))) * (1.0 / l)
    oa = jnp.zeros((t_s, A_WIDTH), F32)
    for h in range(A_HEADS):
        oa = oa + jnp.where(head_of_lane == h, o[h * t_s:(h + 1) * t_s], 0.0)
    smix_ref[:, S_Q:S_U] = oa


def _sample_mem_conv(srow_ref, st_ref, cmk_ref, cmv_ref, wdw_ref, smix_ref, ns_ref, uc_ref, *, t_s):
    qm = srow_ref[:, S_QM:S_ROW] * SCALE
    mhead = lax.broadcasted_iota(jnp.int32, (t_s, M_WIDTH), 1) >> HEAD_SHIFT
    q4 = jnp.concatenate([jnp.where(mhead == h, qm, 0.0) for h in range(M_HEADS)], axis=0).astype(BF16)
    sm = _dot(q4, cmk_ref[0].astype(BF16))
    mm = jnp.max(sm, axis=-1, keepdims=True)
    pm = jnp.exp(sm - mm)
    lm = jnp.sum(pm, axis=-1, keepdims=True)
    om4 = _dot_t(pm.astype(BF16), cmv_ref[0].astype(BF16)) * (1.0 / lm)
    om = jnp.zeros((t_s, M_WIDTH), F32)
    for h in range(M_HEADS):
        om = om + jnp.where(mhead == h, om4[h * t_s:(h + 1) * t_s], 0.0)
    smix_ref[:, S_QM:S_ROW] = om

    nst = st_ref.shape[1]
    uc_ref[0:nst, :] = st_ref[0]
    uc_ref[nst:nst + t_s, :] = srow_ref[:, S_U:S_QM]
    acc = jnp.zeros((t_s, C_WIDTH), F32)
    for w in range(CONV_W):
        acc = acc + uc_ref[STATE_PAD + w:STATE_PAD + w + t_s, :] * wdw_ref[w:w + 1, :]
    smix_ref[:, S_U:S_QM] = acc
    ns_ref[0] = uc_ref[STATE_PAD + t_s:STATE_PAD + t_s + CONV_W - 1, :]


BACK = 512
NFAR = BACK // QB - 1

def _natural_offset(p):
    p = p & (QB - 1)
    return DIL4 * (p & (CLS4 - 1)) + (p >> CLS4_SHIFT)


def _prompt_attn_kernel(kvq_ref, o16_ref, ga_ref, mix_ref, x_ref, wout_ref, npost_ref,
                        srow_ref, kvt_ref, ck_ref, cv_ref, sst_ref, cmk_ref, cmv_ref, wdw_ref,
                        y_ref, smix_ref, wk_ref, ns_ref,
                        mixa_ref, nat_ref, nat4_ref, unp_ref, uc_ref, kvp_ref, woutbf_ref, *, tq, t_s, win_len):
    i = pl.program_id(1)
    kv_refs = (kvp_ref, kvq_ref)

    @pl.when((pl.program_id(0) == 0) & (i == 0))
    def _():
        woutbf_ref[...] = wout_ref[...].astype(BF16)

    @pl.when(i == 0)
    def _():
        kvp_ref[...] = jnp.zeros(kvp_ref.shape, BF16)

    quarter = tq // DIL4
    for cl in range(DIL16):
        c4, c2 = cl % DIL4, cl // DIL4
        dst = pl.ds(c4 * quarter + c2, tq // DIL16, stride=DIL4)
        for g in range(NGRP + 1):
            nat4_ref[g, dst, :] = o16_ref[0, cl, :, g * LANES:(g + 1) * LANES]
    for c4 in range(DIL4):
        for g in range(NGRP + 1):
            nat_ref[g, pl.ds(c4, quarter, stride=DIL4), :] = nat4_ref[g, c4 * quarter:(c4 + 1) * quarter, :]

    a_n = _natural_offset(lax.broadcasted_iota(jnp.int32, (QB, 2 * QB), 0))
    c_p = lax.broadcasted_iota(jnp.int32, (QB, 2 * QB), 1)
    d = jnp.where(c_p < QB, QB, 0) + a_n - _natural_offset(c_p)
    near = (d >= 0) & (d <= QB)
    far = (d >= 0) & ((d & (DIL4 - 1)) == 0)
    bias_near0 = jnp.where(near & far, LN2, jnp.where(near | far, 0.0, NEG))
    ia = lax.broadcasted_iota(jnp.int32, (CLS4, QB), 0)
    cf = lax.broadcasted_iota(jnp.int32, (CLS4, QB), 1)
    fblk, ic = cf >> CLS4_SHIFT, cf & (CLS4 - 1)
    bias_far0 = jnp.where((fblk < NFAR) & ((fblk > 0) | (ic >= ia)), 0.0, NEG)
    lane = lax.broadcasted_iota(jnp.int32, (QB, LANES), 1)
    head_a = lane < HEAD_DIM

    def rows_of(w):
        blk, off = divmod(w, tq)
        return kv_refs[blk], off

    def group(x, c):
        return jnp.concatenate([x[c * CLS4:(c + 1) * CLS4], x[QB + c * CLS4:QB + (c + 1) * CLS4]], axis=0)

    def ungroup(parts):
        return jnp.concatenate([p_[:CLS4] for p_ in parts] + [p_[CLS4:] for p_ in parts], axis=0)

    units = [(sub, g) for sub in range(tq // QB) for g in range(NGRP)]
    ctx = {}
    for sub in range(tq // QB):
        first_valid = BACK - i * tq - sub * QB
        bias_near = jnp.where((c_p >= QB) | (NFAR * QB >= first_valid), bias_near0, NEG)
        bias_far = jnp.where(fblk * QB >= first_valid, bias_far0, NEG)
        far_blocks = [rows_of((sub + j) * QB) for j in range(NFAR)]
        ctx[sub] = dict(bias_near=jnp.concatenate([bias_near, bias_near], axis=0),
                        bias_far=jnp.concatenate([bias_far, bias_far], axis=0),
                        prev=rows_of((sub + NFAR) * QB), cur=rows_of((sub + NFAR + 1) * QB),
                        far=far_blocks + far_blocks[:1])

    def far_rows(sub, c, colsl):
        return jnp.concatenate([r[0, off + c * CLS4:off + (c + 1) * CLS4, colsl] for r, off in ctx[sub]["far"]],
                               axis=0)

    def block(sub, which, colsl):
        r, off = ctx[sub][which]
        return r[0, off:off + QB, colsl]

    scores, probs, outs = {}, {}, {}

    def stage_scores(u):
        sub, g = u
        kc = slice(P_K + g * LANES, P_K + (g + 1) * LANES)
        qc = slice(P_Q + g * LANES, P_Q + (g + 1) * LANES)
        q2 = _pair_split(kvq_ref[0, sub * QB:(sub + 1) * QB, qc].astype(F32)).astype(BF16)
        s_far = ungroup([_dot_t(group(q2, c), far_rows(sub, c, kc)) + ctx[sub]["bias_far"] for c in range(DIL4)])
        s_near = jnp.concatenate([_dot_t(q2, block(sub, "prev", kc)), _dot_t(q2, block(sub, "cur", kc))],
                                 axis=1) + ctx[sub]["bias_near"]
        scores[u] = jnp.concatenate([s_far, s_near], axis=1)

    def stage_softmax(u):
        s = scores.pop(u)
        m = jnp.max(s, axis=-1, keepdims=True)
        p = jnp.exp(s - m)
        probs[u] = (m, jnp.sum(p, axis=-1, keepdims=True), p.astype(BF16))

    def stage_values(u):
        sub, g = u
        vc = slice(P_V + g * LANES, P_V + (g + 1) * LANES)
        pb = probs[u][2]
        outs[u] = (_dot(pb[:, QB:2 * QB], block(sub, "prev", vc)) + _dot(pb[:, 2 * QB:], block(sub, "cur", vc))
                   + ungroup([_dot(group(pb[:, :QB], c), far_rows(sub, c, vc)) for c in range(DIL4)]))

    def stage_merge(u):
        sub, g = u
        rows = slice(sub * QB, (sub + 1) * QB)
        cols = slice(g * LANES, (g + 1) * LANES)
        m, l, _ = probs.pop(u)
        o_s = _pair_join(outs.pop(u))
        m_s = jnp.where(head_a, m[:QB], m[QB:])
        l_s = jnp.where(head_a, l[:QB], l[QB:])
        for c in range(DIL4):
            dst = pl.ds(c, CLS4, stride=DIL4)
            unp_ref[3 * g, dst, :] = o_s[c * CLS4:(c + 1) * CLS4]
            unp_ref[3 * g + 1, dst, :] = m_s[c * CLS4:(c + 1) * CLS4]
            unp_ref[3 * g + 2, dst, :] = l_s[c * CLS4:(c + 1) * CLS4]
        o_n, m_n, l_n = unp_ref[3 * g], unp_ref[3 * g + 1], unp_ref[3 * g + 2]
        st = nat_ref[NGRP, rows, :]
        m_f = jnp.where(head_a, st[:, 2 * g:2 * g + 1], st[:, 2 * g + 1:2 * g + 2])
        l_f = jnp.where(head_a, st[:, STAT_L + 2 * g:STAT_L + 2 * g + 1],
                        st[:, STAT_L + 2 * g + 1:STAT_L + 2 * g + 2])
        mx = jnp.maximum(m_n, m_f)
        w_n = jnp.exp(m_n - mx)
        w_f = jnp.exp(m_f - mx)
        oa = (o_n * w_n + nat_ref[g, rows, :] * w_f) / (l_n * w_n + l_f * w_f)
        mixa_ref[rows, cols] = (oa * ga_ref[0, rows, cols]).astype(BF16)

    def finish(rows):
        z = (_dot(mixa_ref[rows, :], woutbf_ref[0:A_WIDTH, :])
             + _dot(mix_ref[0, rows, :], woutbf_ref[A_WIDTH:, :]))
        y_ref[0, rows, :] = x_ref[0, rows, :] + _rmsnorm(z, npost_ref[...])

    stages = (stage_scores, stage_softmax, stage_values, stage_merge)
    half = len(units) // 2
    for t in range(len(units) + len(stages) - 1):
        for depth, stage in enumerate(stages):
            if 0 <= t - depth < len(units):
                stage(units[t - depth])
        if t - (len(stages) - 1) == half - 1:
            finish(slice(0, tq // 2))
    finish(slice(tq // 2, tq))
    kvp_ref[...] = kvq_ref[:, :, 0:P_Q]

    for part in _sample_mix_stages(pl.program_id(0) * pl.num_programs(1) + i, srow_ref, kvt_ref, ck_ref, cv_ref,
                                   sst_ref, cmk_ref, cmv_ref, wdw_ref, smix_ref, wk_ref, ns_ref, uc_ref,
                                   t_s=t_s, win_len=win_len):
        part()


def _prompt_attn(kvq, o16, ga, mix, x, wout, npost, srow, kvt, ck, cv, st_pad, cmk, cmv, wdw, tq, t_s):
    b, s, _ = x.shape
    nsteps = s // tq
    nb, _, win_len = ck.shape
    assert tq == BACK and b * nsteps == nb, "one sample batch per grid step"
    row = lambda w: pl.BlockSpec((1, tq, w), lambda bi, i: (bi, i, 0))
    cls = pl.BlockSpec((1, DIL16, tq // DIL16, O16_W), lambda bi, i: (bi, 0, i, 0))
    srows = pl.BlockSpec((t_s, S_ROW), lambda bi, i: (bi * nsteps + i, 0))
    per = lambda a: pl.BlockSpec((1,) + a.shape[1:], lambda bi, i: (bi * nsteps + i, 0, 0))
    return pl.pallas_call(
        functools.partial(_prompt_attn_kernel, tq=tq, t_s=t_s, win_len=win_len),
        grid=(b, nsteps),
        in_specs=[row(kvq.shape[2]), cls, row(A_WIDTH), row(C_WIDTH + M_WIDTH), row(D_MODEL),
                  _resident(wout), _full(npost),
                  srows, _full(kvt), per(ck), per(cv), per(st_pad), per(cmk), per(cmv), _full(wdw)],
        out_specs=[row(D_MODEL), srows, per(ck),
                   pl.BlockSpec((1, CONV_W - 1, C_WIDTH), lambda bi, i: (bi * nsteps + i, 0, 0))],
        out_shape=[jax.ShapeDtypeStruct((b, s, D_MODEL), F32),
                   jax.ShapeDtypeStruct((nb * t_s, S_ROW), F32),
                   jax.ShapeDtypeStruct(ck.shape, F32),
                   jax.ShapeDtypeStruct((nb, CONV_W - 1, C_WIDTH), F32)],
        scratch_shapes=[pltpu.VMEM((tq, A_WIDTH), BF16),
                        pltpu.VMEM((NGRP + 1, tq, LANES), F32),
                        pltpu.VMEM((NGRP + 1, tq, LANES), F32),
                        pltpu.VMEM((3 * NGRP, QB, LANES), F32),
                        pltpu.VMEM((STATE_PAD + CONV_W - 1 + t_s, C_WIDTH), F32),
                        pltpu.VMEM((1, tq, 2 * A_WIDTH), BF16),
                        pltpu.VMEM(wout.shape, BF16)],
        compiler_params=pltpu.CompilerParams(dimension_semantics=("arbitrary", "arbitrary"),
                                             vmem_limit_bytes=VMEM_LIMIT),
        name="prompt_attn",
    )(kvq, o16, ga, mix, x, wout, npost, srow, kvt, ck, cv, st_pad, cmk, cmv, wdw)


def _sample_proj_kernel(x_ref, npre_ref, win_ref, ang_ref, mem_ref, nmem_ref, wmem_ref,
                        srow_ref, kvt_ref, gate_ref, wbf_ref, mk_ref, mv_ref, mkb_ref, mvb_ref):
    _mem_kv(mem_ref, nmem_ref, wmem_ref, mk_ref, mv_ref, mkb_ref, mvb_ref)
    h =_rmsnorm(x_ref[...], npre_ref[...]).astype(BF16)
    wbf_ref[...] = win_ref[...].astype(BF16)
    proj = _dot(h, wbf_ref[...])
    pats = _rope_patterns(ang_ref[:, 0:LANES], ang_ref[:, LANES:])
    srow_ref[:, S_Q:S_U] = _rope(proj[:, O_QA:O_KA], pats) * SCALE
    srow_ref[:, S_U:S_QM] = proj[:, O_AB:O_BB] * _sigmoid(proj[:, O_BB:O_GB])
    srow_ref[:, S_QM:S_ROW] = proj[:, O_QM:O_GM]
    kv_t = jnp.concatenate([_rope(proj[:, O_KA:O_VA], pats), proj[:, O_VA:O_GA]], axis=1).T
    for slab in range(kvt_ref.shape[0]):
        kvt_ref[slab] = kv_t[:, slab * LANES:(slab + 1) * LANES]
    gate_ref[:, S_Q:S_U] = _silu(proj[:, O_GA:O_AB])
    gate_ref[:, S_U:S_QM] = _silu(proj[:, O_GB:O_QM])
    gate_ref[:, S_QM:S_ROW] = _silu(proj[:, O_GM:D_IN])


def _sample_proj(x, npre, win, ang, mem, nmem, wmem):
    n = x.shape[0]
    assert n % LANES == 0
    mshape = mem.shape[:2] + (M_WIDTH,)
    shapes = [(n, S_ROW), (n // LANES, 2 * A_WIDTH, LANES), (n, S_ROW), win.shape] + [mshape] * 4
    dtypes = [F32, F32, F32, BF16, F32, F32, BF16, BF16]
    args = (x, npre, win, ang, mem, nmem, wmem)
    return pl.pallas_call(
        _sample_proj_kernel,
        grid=(1,),
        in_specs=[_full(a) for a in args],
        out_specs=[pl.BlockSpec(sh, lambda i, nd=len(sh): (0,) * nd) for sh in shapes],
        out_shape=[jax.ShapeDtypeStruct(sh, dt) for sh, dt in zip(shapes, dtypes)],
        compiler_params=pltpu.CompilerParams(vmem_limit_bytes=VMEM_LIMIT),
        name="sample_proj",
    )(*args)


def _sample_out_kernel(x_ref, smix_ref, gate_ref, bdw_ref, lng_ref, lnb_ref, wpw_ref, bpw_ref, wout_ref, npost_ref,
                       y_ref):
    mixed_a = (smix_ref[:, S_Q:S_U] * gate_ref[:, S_Q:S_U]).astype(BF16)
    mixed_b = _conformer_tail(smix_ref[:, S_U:S_QM], gate_ref[:, S_U:S_QM], bdw_ref[...], lng_ref[...], lnb_ref[...],
                              wpw_ref, bpw_ref[...]).astype(BF16)
    mixed_m = (smix_ref[:, S_QM:S_ROW] * gate_ref[:, S_QM:S_ROW]).astype(BF16)
    wout = wout_ref[...].astype(BF16)
    z = (_dot(mixed_a, wout[0:A_WIDTH]) + _dot(mixed_b, wout[A_WIDTH:A_WIDTH + C_WIDTH])
         + _dot(mixed_m, wout[A_WIDTH + C_WIDTH:]))
    y_ref[...] = x_ref[...] + _rmsnorm(z, npost_ref[...])


def _sample_out(x, smix, gate, bdw, lng, lnb, wpw, bpw, wout, npost):
    args = (x, smix, gate, bdw, lng, lnb, wpw, bpw, wout, npost)
    return pl.pallas_call(
        _sample_out_kernel,
        grid=(1,),
        in_specs=[_full(a) for a in args],
        out_specs=_full(x),
        out_shape=jax.ShapeDtypeStruct(x.shape, F32),
        compiler_params=pltpu.CompilerParams(vmem_limit_bytes=VMEM_LIMIT),
        name="sample_out",
    )(*args)


def _rope_cos_sin(pos):
    inv = ROPE_THETA ** (-jnp.arange(0, ROT_DIM, 2, dtype=F32) / ROT_DIM)
    ang = pos.astype(F32)[:, None] * jnp.tile(inv, 2 * LANES // ROT_DIM)[None, :]
    return jnp.concatenate([jnp.cos(ang), jnp.sin(ang)], axis=1)


def _feature_major(cache):
    nb, rows, heads, dim = cache.shape
    return jnp.transpose(cache, (0, 2, 3, 1)).reshape(nb, heads * dim, rows)


def _row_major(cache_t, heads):
    nb, width, rows = cache_t.shape
    return jnp.transpose(cache_t.reshape(nb, heads, width // heads, rows), (0, 3, 1, 2))


PROJ_TILE = 512
ATTN_TILE = BACK


def kernel(x_prompt, x_sample, cache_win_k, cache_win_v, state_conv, cache_mem_k, cache_mem_v, mem_prompt,
           norm_pre, norm_post, w_in, w_out, norm_mem, w_mem_kv, w_dw, b_dw, ln_conv_g, ln_conv_b, w_pw2, b_pw2):
    depth = w_in.shape[0]
    assert depth == 1, "single-layer step"
    b, s, _ = x_prompt.shape
    nb, t_s, _ = x_sample.shape
    win_len = cache_win_k.shape[2]
    assert s % (DIL16 * QB) == 0 and s % PROJ_TILE == 0 and s % ATTN_TILE == 0
    assert win_len == MAX_WINDOW and win_len % LANES == 0
    assert t_s % 8 == 0 and t_s & (t_s - 1) == 0 and t_s < LANES
    l = 0
    row = lambda a: a[l][None, :]
    npre, npost, nmem = row(norm_pre), row(norm_post), row(norm_mem)
    bdw, lng, lnb, bpw = row(b_dw), row(ln_conv_g), row(ln_conv_b), row(b_pw2)
    win, wout, wmem, wpw = w_in[l], w_out[l], w_mem_kv[l], w_pw2[l]
    wdw = w_dw[l]

    pos_s = PAST_LEN + jnp.arange(t_s, dtype=jnp.int32)
    xs = x_sample.reshape(nb * t_s, D_MODEL)
    srow, kvt, gate_s, win_bf, mk, mv, mkb, mvb = _sample_proj(
        xs, npre, win, jnp.tile(_rope_cos_sin(pos_s), (nb, 1)), mem_prompt, nmem, wmem)
    st_pad = jnp.pad(state_conv[l], ((0, 0), (STATE_PAD, 0), (0, 0)))

    rin = _rope_cos_sin(jnp.arange(PROJ_TILE, dtype=jnp.int32))
    rbase = _rope_cos_sin(jnp.arange(0, s, PROJ_TILE, dtype=jnp.int32))[:, None, :]
    ck, cv = _feature_major(cache_win_k[l]), _feature_major(cache_win_v[l])
    kvq, kvq16, kf, vf, ga, mix, ust, wv = _prompt_proj(x_prompt, npre, win_bf, rin, rbase, mkb, mvb, wdw, bdw,
                                                        lng, lnb, wpw, bpw, kvt, cv, PROJ_TILE, t_s)
    o16 = _dilated16(kvq16)
    y_prompt, smix, wk, nst = _prompt_attn(
        kvq, o16, ga, mix, x_prompt, wout, npost, srow, kvt, ck, cv, st_pad,
        _feature_major(cache_mem_k[l]), _feature_major(cache_mem_v[l]), wdw, ATTN_TILE, t_s)
    y_sample = _sample_out(xs, smix, gate_s, bdw, lng, lnb, wpw, bpw, wout, npost)

    keep_p = kf.shape[1]
    return (y_prompt,
            y_sample.reshape(nb, t_s, D_MODEL),
            kf.reshape(1, b, keep_p, A_HEADS, HEAD_DIM),
            vf.reshape(1, b, keep_p, A_HEADS, HEAD_DIM),
            ust[:, CONV_HIST - (CONV_W - 1):, :][None],
            mk.reshape(1, b, N_MEM, M_HEADS, HEAD_DIM),
            mv.reshape(1, b, N_MEM, M_HEADS, HEAD_DIM),
            _row_major(wk, A_HEADS)[None],
            _row_major(wv, A_HEADS)[None],
            nst[None])
```
